```python
import math
import jax, jax.numpy as jnp
from jax import lax
import numpy as np

D_MODEL = 1024
BATCH = 8
SEQ = 4096
DEPTH = 1
DEC_BATCH = 8
DEC_SEQ = 64
PAST_LEN = 2048

CHUNK = 64
D_MIX = D_MODEL
D_SSM = D_MIX // 2
D_CONV = D_MIX - D_SSM
SSM_GROUP = 16
N_SSM_GROUPS = D_SSM // SSM_GROUP
SSM_STATE = 64
CONV_WIDTH = 3
CONV_HEADS = 8
D_PROJ = D_SSM + 3 * D_CONV
N_EXPERTS = 64
TOP_K = 6
N_EXPERT_GROUPS = 8
TOPK_GROUPS = 4
D_EXPERT = 256
D_SHARED = D_EXPERT
ROUTED_SCALE = 2.5
MOE_BLOCK = 128
DEEPNORM_ALPHA = (2.0 * DEPTH) ** 0.25
DEEPNORM_BETA = (8.0 * DEPTH) ** -0.25
LN_EPS = 1e-5
RMS_EPS = 1e-6

kernel_name = 'hybrid_s5_shortconv_moe_stream'


def layer_norm(x, g, b):
    xf = x.astype(jnp.float32)
    mu = xf.mean(-1, keepdims=True)
    var = jnp.square(xf - mu).mean(-1, keepdims=True)
    return ((xf - mu) * lax.rsqrt(var + LN_EPS) * g.astype(jnp.float32) + b.astype(jnp.float32)).astype(x.dtype)


def rms_norm(x, g):
    xf = x.astype(jnp.float32)
    return (xf * lax.rsqrt(jnp.mean(xf * xf, -1, keepdims=True) + RMS_EPS) * g.astype(jnp.float32)).astype(x.dtype)


def _complex_affine_combine(e1, e2):
    a1r, a1i, b1r, b1i = e1
    a2r, a2i, b2r, b2i = e2
    return (a2r * a1r - a2i * a1i,
            a2r * a1i + a2i * a1r,
            a2r * b1r - a2i * b1i + b2r,
            a2r * b1i + a2i * b1r + b2i)


def s5_ssm(u, h0_re, h0_im, lam_re, lam_im, log_dt, b_re, b_im, c_re, c_im, d_skip):
    bsz, l, _ = u.shape
    uf = u.astype(jnp.float32).reshape(bsz, l, N_SSM_GROUPS, SSM_GROUP)
    lr = lam_re.astype(jnp.float32)
    li = lam_im.astype(jnp.float32)
    dt = jnp.exp(log_dt.astype(jnp.float32))[:, None]
    mag = jnp.exp(lr * dt)
    ar = mag * jnp.cos(li * dt)
    ai = mag * jnp.sin(li * dt)
    den = lr * lr + li * li
    qr = ((ar - 1.0) * lr + ai * li) / den
    qi = (ai * lr - (ar - 1.0) * li) / den
    br_ = b_re.astype(jnp.float32)
    bi_ = b_im.astype(jnp.float32)
    bbar_r = qr[..., None] * br_ - qi[..., None] * bi_
    bbar_i = qr[..., None] * bi_ + qi[..., None] * br_
    bu_r = jnp.einsum('blgc,gpc->blgp', uf, bbar_r)
    bu_i = jnp.einsum('blgc,gpc->blgp', uf, bbar_i)
    h0r = h0_re.astype(jnp.float32)
    h0i = h0_im.astype(jnp.float32)
    bu_r = bu_r.at[:, 0].add(ar * h0r - ai * h0i)
    bu_i = bu_i.at[:, 0].add(ar * h0i + ai * h0r)
    a_r = jnp.broadcast_to(ar, bu_r.shape)
    a_i = jnp.broadcast_to(ai, bu_i.shape)
    _, _, hr, hi = lax.associative_scan(_complex_affine_combine, (a_r, a_i, bu_r, bu_i), axis=1)
    y = (jnp.einsum('blgp,gcp->blgc', hr, c_re.astype(jnp.float32))
         - jnp.einsum('blgp,gcp->blgc', hi, c_im.astype(jnp.float32))
         + d_skip.astype(jnp.float32).reshape(N_SSM_GROUPS, SSM_GROUP) * uf)
    return y.reshape(bsz, l, D_SSM), hr[:, -1], hi[:, -1]


def gated_short_conv(gate_b, gate_c, v, buf, conv_w):
    cv = gate_c * v
    full = jnp.concatenate([buf.astype(cv.dtype), cv], axis=1)
    l = v.shape[1]
    y = full[:, 0:l] * conv_w[0]
    for k in range(1, CONV_WIDTH):
        y = y + full[:, k:k + l] * conv_w[k]
    return gate_b * y, full[:, full.shape[1] - (CONV_WIDTH - 1):]


def token_mixing(h, h0_re, h0_im, conv_buf, w_in, lam_re, lam_im, log_dt, ssm_b_re, ssm_b_im,
                 ssm_c_re, ssm_c_im, ssm_d, w_glu, b_glu, conv_w, beta_ssm, beta_conv, w_out):
    p = h @ w_in
    u, gb, gc, v = jnp.split(p, [D_SSM, D_SSM + D_CONV, D_SSM + 2 * D_CONV], axis=-1)
    y_ssm, hT_re, hT_im = s5_ssm(u, h0_re, h0_im, lam_re, lam_im, log_dt, ssm_b_re, ssm_b_im,
                                 ssm_c_re, ssm_c_im, ssm_d)
    g = jax.nn.gelu(y_ssm)
    y_ssm = (g * jax.nn.sigmoid(g @ w_glu.astype(jnp.float32) + b_glu.astype(jnp.float32))).astype(h.dtype)
    y_conv, new_buf = gated_short_conv(gb, gc, v, conv_buf, conv_w)
    mixed = jnp.concatenate([rms_norm(y_ssm, beta_ssm), rms_norm(y_conv, beta_conv)], axis=-1)
    return mixed @ w_out, hT_re, hT_im, new_buf


def route(h, w_router, router_bias):
    t = h.shape[0]
    scores = jax.nn.sigmoid((h @ w_router).astype(jnp.float32))
    sel = scores + router_bias.astype(jnp.float32)
    per_group = N_EXPERTS // N_EXPERT_GROUPS
    grp_score = lax.top_k(sel.reshape(t, N_EXPERT_GROUPS, per_group), 2)[0].sum(-1)
    _, top_g = lax.top_k(grp_score, TOPK_GROUPS)
    gmask = jnp.any(top_g[..., None] == jnp.arange(N_EXPERT_GROUPS), axis=-2)
    masked = jnp.where(jnp.repeat(gmask, per_group, axis=-1), sel, -jnp.inf)
    _, top_e = lax.top_k(masked, TOP_K)
    w = jnp.take_along_axis(scores, top_e, axis=-1)
    w = w / jnp.sum(w, -1, keepdims=True) * ROUTED_SCALE
    return top_e, w


def routed_experts(h, top_e, top_w, w_gate, w_up, w_down):
    t, d = h.shape
    tk = t * TOP_K
    n_blocks = tk // MOE_BLOCK + N_EXPERTS
    rows = n_blocks * MOE_BLOCK
    flat_e = top_e.reshape(-1)
    flat_tok = jnp.repeat(jnp.arange(t, dtype=jnp.int32), TOP_K)
    flat_w = top_w.reshape(-1)
    order = jnp.argsort(flat_e)
    e_sorted = flat_e[order]
    counts = jnp.bincount(flat_e, length=N_EXPERTS)
    padded = (counts + MOE_BLOCK - 1) // MOE_BLOCK * MOE_BLOCK
    pad_end = jnp.cumsum(padded)
    pad_start = pad_end - padded
    start = jnp.cumsum(counts) - counts
    dest = pad_start[e_sorted] + jnp.arange(tk) - start[e_sorted]
    row_tok = jnp.full((rows,), t, jnp.int32).at[dest].set(flat_tok[order])
    row_w = jnp.zeros((rows,), flat_w.dtype).at[dest].set(flat_w[order])
    block_e = jnp.minimum(jnp.searchsorted(pad_end, jnp.arange(n_blocks) * MOE_BLOCK, side='right'),
                          N_EXPERTS - 1)
    h_pad = jnp.concatenate([h, jnp.zeros((1, d), h.dtype)], axis=0)

    def expert_block(args):
        tok, e = args
        xb = h_pad[tok]
        hid = jax.nn.silu(xb @ w_gate[e]) * (xb @ w_up[e])
        return hid @ w_down[e]

    yb = lax.map(expert_block, (row_tok.reshape(n_blocks, MOE_BLOCK), block_e))
    yb = yb.reshape(rows, d) * row_w[:, None].astype(yb.dtype)
    return jax.ops.segment_sum(yb, row_tok, num_segments=t + 1)[:t]


def moe_ffn(h, w_router, router_bias, w_gate, w_up, w_down, ws_gate, ws_up, ws_down):
    top_e, top_w = route(h, w_router, router_bias)
    routed = routed_experts(h, top_e, top_w, w_gate, w_up, w_down)
    shared = (jax.nn.silu(h @ ws_gate) * (h @ ws_up)) @ ws_down
    return routed.astype(h.dtype) + shared


def setup_inputs(seed: int = 0) -> dict:
    key = jax.random.key(seed)
    ks = list(jax.random.split(key, 40))
    nrm = lambda k, s: jax.random.normal(k, s, jnp.float32)
    L, G, P, E = DEPTH, N_SSM_GROUPS, SSM_STATE, N_EXPERTS
    inp = {}
    inp['x_prompt'] = nrm(ks[0], (BATCH, SEQ, D_MODEL))
    inp['x_sample'] = nrm(ks[1], (DEC_BATCH, DEC_SEQ, D_MODEL))
    inp['state_ssm_re'] = 0.3 * nrm(ks[2], (L, DEC_BATCH, G, P))
    inp['state_ssm_im'] = 0.3 * nrm(ks[3], (L, DEC_BATCH, G, P))
    inp['cache_conv'] = nrm(ks[4], (L, DEC_BATCH, CONV_WIDTH - 1, D_CONV))
    inp['ln_in_g'] = 1.0 + 0.02 * nrm(ks[5], (D_MODEL,))
    inp['ln_in_b'] = 0.02 * nrm(ks[6], (D_MODEL,))
    inp['w_in'] = nrm(ks[7], (L, D_MODEL, D_PROJ)) * D_MODEL ** -0.5
    inp['lam_re'] = -0.5 + 0.01 * nrm(ks[8], (L, G, P))
    inp['lam_im'] = jnp.pi * jnp.arange(P, dtype=jnp.float32) + 1e-3 * nrm(ks[9], (L, G, P))
    inp['log_dt'] = jax.random.uniform(ks[10], (L, G), jnp.float32, math.log(1e-3), math.log(1e-1))
    inp['ssm_b_re'] = nrm(ks[11], (L, G, P, SSM_GROUP)) * (2 * SSM_GROUP) ** -0.5
    inp['ssm_b_im'] = nrm(ks[12], (L, G, P, SSM_GROUP)) * (2 * SSM_GROUP) ** -0.5
    inp['ssm_c_re'] = nrm(ks[13], (L, G, SSM_GROUP, P)) * P ** -0.5
    inp['ssm_c_im'] = nrm(ks[14], (L, G, SSM_GROUP, P)) * P ** -0.5
    inp['ssm_d'] = nrm(ks[15], (L, D_SSM))
    inp['w_glu'] = nrm(ks[16], (L, D_SSM, D_SSM)) * D_SSM ** -0.5
    inp['b_glu'] = 0.02 * nrm(ks[17], (L, D_SSM))
    inp['conv_w'] = nrm(ks[18], (L, CONV_WIDTH, D_CONV)) * CONV_WIDTH ** -0.5
    inp['beta_ssm'] = 1.0 + 0.02 * nrm(ks[19], (L, D_SSM))
    inp['beta_conv'] = 1.0 + 0.02 * nrm(ks[20], (L, D_CONV))
    inp['w_out'] = nrm(ks[21], (L, D_MIX, D_MODEL)) * D_MIX ** -0.5 * DEEPNORM_BETA
    inp['ln1_g'] = 1.0 + 0.02 * nrm(ks[22], (L, D_MODEL))
    inp['ln1_b'] = 0.02 * nrm(ks[23], (L, D_MODEL))
    inp['w_router'] = nrm(ks[24], (L, D_MODEL, E)) * D_MODEL ** -0.5
    inp['router_bias'] = 0.01 * nrm(ks[25], (L, E))
    inp['w_gate'] = nrm(ks[26], (L, E, D_MODEL, D_EXPERT)) * D_MODEL ** -0.5
    inp['w_up'] = nrm(ks[27], (L, E, D_MODEL, D_EXPERT)) * D_MODEL ** -0.5
    inp['w_down'] = nrm(ks[28], (L, E, D_EXPERT, D_MODEL)) * D_EXPERT ** -0.5 * DEEPNORM_BETA
    inp['ws_gate'] = nrm(ks[29], (L, D_MODEL, D_SHARED)) * D_MODEL ** -0.5
    inp['ws_up'] = nrm(ks[30], (L, D_MODEL, D_SHARED)) * D_MODEL ** -0.5
    inp['ws_down'] = nrm(ks[31], (L, D_SHARED, D_MODEL)) * D_SHARED ** -0.5 * DEEPNORM_BETA
    inp['ln2_g'] = 1.0 + 0.02 * nrm(ks[32], (L, D_MODEL))
    inp['ln2_b'] = 0.02 * nrm(ks[33], (L, D_MODEL))
    return inp


def reference(x_prompt, x_sample, state_ssm_re, state_ssm_im, cache_conv, ln_in_g, ln_in_b,
              w_in, lam_re, lam_im, log_dt, ssm_b_re, ssm_b_im, ssm_c_re, ssm_c_im, ssm_d,
              w_glu, b_glu, conv_w, beta_ssm, beta_conv, w_out, ln1_g, ln1_b,
              w_router, router_bias, w_gate, w_up, w_down, ws_gate, ws_up, ws_down,
              ln2_g, ln2_b):
    hp = layer_norm(x_prompt, ln_in_g, ln_in_b)
    hs = layer_norm(x_sample, ln_in_g, ln_in_b)
    bp = hp.shape[0]
    n_prompt = hp.shape[0] * hp.shape[1]
    p_re, p_im, p_conv, s_re, s_im, s_conv = [], [], [], [], [], []
    for i in range(DEPTH):
        mix_w = (w_in[i], lam_re[i], lam_im[i], log_dt[i], ssm_b_re[i], ssm_b_im[i],
                 ssm_c_re[i], ssm_c_im[i], ssm_d[i], w_glu[i], b_glu[i], conv_w[i],
                 beta_ssm[i], beta_conv[i], w_out[i])
        zero_h = jnp.zeros((bp, N_SSM_GROUPS, SSM_STATE), jnp.float32)
        zero_buf = jnp.zeros((bp, CONV_WIDTH - 1, D_CONV), hp.dtype)
        mp, pr, pi, pc = token_mixing(hp, zero_h, zero_h, zero_buf, *mix_w)
        ms, sr, si, sc = token_mixing(hs, state_ssm_re[i], state_ssm_im[i], cache_conv[i], *mix_w)
        hp = layer_norm(DEEPNORM_ALPHA * hp + mp, ln1_g[i], ln1_b[i])
        hs = layer_norm(DEEPNORM_ALPHA * hs + ms, ln1_g[i], ln1_b[i])
        tokens = jnp.concatenate([hp.reshape(-1, D_MODEL), hs.reshape(-1, D_MODEL)], axis=0)
        f = moe_ffn(tokens, w_router[i], router_bias[i], w_gate[i], w_up[i], w_down[i],
                    ws_gate[i], ws_up[i], ws_down[i])
        hp = layer_norm(DEEPNORM_ALPHA * hp + f[:n_prompt].reshape(hp.shape), ln2_g[i], ln2_b[i])
        hs = layer_norm(DEEPNORM_ALPHA * hs + f[n_prompt:].reshape(hs.shape), ln2_g[i], ln2_b[i])
        p_re.append(pr); p_im.append(pi); p_conv.append(pc)
        s_re.append(sr); s_im.append(si); s_conv.append(sc)
    return (hp, hs, jnp.stack(p_re), jnp.stack(p_im), jnp.stack(p_conv),
            jnp.stack(s_re), jnp.stack(s_im), jnp.stack(s_conv))
```

```python
import functools
import math

import jax
import jax.numpy as jnp
from jax import lax
from jax.experimental import pallas as pl
from jax.experimental.pallas import tpu as pltpu

F32 = jnp.float32
BF16 = jnp.bfloat16
I32 = jnp.int32

D_MODEL = 1024
D_SSM = 512
D_CONV = 512
SSM_GROUP = 16
N_GROUPS = 32
SSM_STATE = 64
D_STATE = N_GROUPS * SSM_STATE
N_EXPERTS = 64
TOP_K = 6
N_EXPERT_GROUPS = 8
GROUP_SIZE = N_EXPERTS // N_EXPERT_GROUPS
TOPK_GROUPS = 4
D_EXPERT = 256
ROUTED_SCALE = 2.5
DEPTH = 1
DEEPNORM_ALPHA = (2.0 * DEPTH) ** 0.25
LN_EPS = 1e-5
RMS_EPS = 1e-6

BATCH = 8
CHUNK_T = 64
CHUNK_ROWS = CHUNK_T * BATCH
SCAN_COLS = 512
ROUTE_TILE = 512
SCATTER_TILE = 256
COMBINE_TILE = 128
EXPERT_BLOCK = 256
V7X_VMEM_LIMIT = 56 * 1024 * 1024
NEG_INF = float("-inf")


def _layer_norm(x, g, b):
    mu = jnp.mean(x, axis=-1, keepdims=True)
    xc = x - mu
    var = jnp.mean(xc * xc, axis=-1, keepdims=True)
    return xc * lax.rsqrt(var + LN_EPS) * g + b


def _rms_norm(x, g):
    return x * lax.rsqrt(jnp.mean(x * x, axis=-1, keepdims=True) + RMS_EPS) * g


def _dot(a, b):
    return jnp.dot(a, b, preferred_element_type=F32)


def _discretise(lr, li, log_dt):
    dt = jnp.exp(log_dt)
    mag = jnp.exp(lr * dt)
    ar = mag * jnp.cos(li * dt)
    ai = mag * jnp.sin(li * dt)
    den = lr * lr + li * li
    qr = ((ar - 1.0) * lr + ai * li) / den
    qi = (ai * lr - (ar - 1.0) * li) / den
    return ar, ai, qr, qi


def _prep_kernel(lr_ref, li_ref, ldt_ref, lrc_ref, lic_ref, ldtc_ref, br_ref, bi_ref,
                 ar_ref, ai_ref, bbr_ref, bbi_ref):
    ar, ai, _, _ = _discretise(lr_ref[...], li_ref[...], ldt_ref[...])
    ar_ref[...] = ar
    ai_ref[...] = ai
    _, _, qr, qi = _discretise(lrc_ref[...], lic_ref[...], ldtc_ref[...])
    br = br_ref[...]
    bi = bi_ref[...]
    bbr_ref[...] = qr * br - qi * bi
    bbi_ref[...] = qr * bi + qi * br


def _prep(lam_re, lam_im, log_dt, b_re, b_im):
    g, p = lam_re.shape
    per_channel = lambda a: jnp.repeat(a, SSM_GROUP, axis=0)
    rows = lambda b: jnp.transpose(b, (0, 2, 1)).reshape(g * SSM_GROUP, p)
    ldt = log_dt.reshape(g, 1)
    ar, ai, bbr, bbi = pl.pallas_call(
        _prep_kernel,
        out_shape=(jax.ShapeDtypeStruct((g, p), F32), jax.ShapeDtypeStruct((g, p), F32),
                   jax.ShapeDtypeStruct((g * SSM_GROUP, p), F32), jax.ShapeDtypeStruct((g * SSM_GROUP, p), F32)),
        name="prep",
    )(lam_re, lam_im, ldt, per_channel(lam_re), per_channel(lam_im), per_channel(ldt), rows(b_re), rows(b_im))
    return ar, ai, bbr.reshape(g, SSM_GROUP, p), bbi.reshape(g, SSM_GROUP, p)


def _mix_kernel(n_prompt_chunks,
                x_ref, h0r_ref, h0i_ref, cbuf_ref, lng_ref, lnb_ref, win_ref, ar_ref, ai_ref,
                bbr_ref, bbi_ref, cbr_ref, cbi_ref, dsk_ref, wglu_ref, bglu_ref, cw_ref,
                bs_ref, bc_ref, wout_ref, g1_ref, b1_ref,
                h1_ref, pr_ref, pi_ref, pc_ref, sr_ref, si_ref, sc_ref,
                str_ref, sti_ref, hr_ref, hi_ref, cv_ref):
    c = pl.program_id(0)
    n = CHUNK_ROWS
    carry_rows = 2 * BATCH

    @pl.when(c == 0)
    def _():
        hr_ref[...] = jnp.zeros_like(hr_ref)
        hi_ref[...] = jnp.zeros_like(hi_ref)
        cv_ref[0:carry_rows, :] = jnp.zeros((carry_rows, D_CONV), F32)

    @pl.when(c == n_prompt_chunks)
    def _():
        hr_ref[...] = h0r_ref[...]
        hi_ref[...] = h0i_ref[...]
        cv_ref[0:carry_rows, :] = cbuf_ref[...]

    h = _layer_norm(x_ref[...], lng_ref[...], lnb_ref[...])
    hb = h.astype(BF16)

    u = _dot(hb, win_ref[:, 0:D_SSM])
    ub = u.astype(BF16)
    u_tile = 128
    s_tile = u_tile // SSM_GROUP * SSM_STATE
    for j in range(D_SSM // u_tile):
        uj = ub[:, j * u_tile:(j + 1) * u_tile]
        str_ref[:, j * s_tile:(j + 1) * s_tile] = _dot(uj, bbr_ref[j])
        sti_ref[:, j * s_tile:(j + 1) * s_tile] = _dot(uj, bbi_ref[j])

    for cb in range(D_STATE // SCAN_COLS):
        cols = slice(cb * SCAN_COLS, (cb + 1) * SCAN_COLS)
        ar = jnp.broadcast_to(ar_ref[:, cols], (BATCH, SCAN_COLS))
        ai = jnp.broadcast_to(ai_ref[:, cols], (BATCH, SCAN_COLS))

        def step(t, carry, cols=cols, ar=ar, ai=ai):
            sr, si = carry
            r0 = pl.multiple_of(t * BATCH, BATCH)
            nr = ar * sr - ai * si + str_ref[pl.ds(r0, BATCH), cols]
            ni = ar * si + ai * sr + sti_ref[pl.ds(r0, BATCH), cols]
            str_ref[pl.ds(r0, BATCH), cols] = nr
            sti_ref[pl.ds(r0, BATCH), cols] = ni
            return nr, ni

        sr, si = lax.fori_loop(0, CHUNK_T, step, (hr_ref[:, cols], hi_ref[:, cols]), unroll=8)
        hr_ref[:, cols] = sr
        hi_ref[:, cols] = si

    k_tile = 1024
    ys = []
    for j in range(D_STATE // k_tile):
        sl = slice(j * k_tile, (j + 1) * k_tile)
        ys.append(_dot(str_ref[:, sl].astype(BF16), cbr_ref[j]) + _dot(sti_ref[:, sl].astype(BF16), cbi_ref[j]))
    y_ssm = jnp.concatenate(ys, axis=-1) + dsk_ref[...] * u
    g = jax.nn.gelu(y_ssm)
    y_ssm = g * jax.nn.sigmoid(_dot(g.astype(BF16), wglu_ref[...]) + bglu_ref[...])
    mix = _dot(_rms_norm(y_ssm, bs_ref[...]).astype(BF16), wout_ref[0:D_SSM, :])

    gate_b = _dot(hb, win_ref[:, D_SSM:D_SSM + D_CONV])
    gate_c = _dot(hb, win_ref[:, D_SSM + D_CONV:D_SSM + 2 * D_CONV])
    v = _dot(hb, win_ref[:, D_SSM + 2 * D_CONV:D_SSM + 3 * D_CONV])
    cv_ref[carry_rows:carry_rows + n, :] = gate_c * v
    y_conv = gate_b * (cv_ref[0:n, :] * cw_ref[0:1, :]
                       + cv_ref[BATCH:BATCH + n, :] * cw_ref[1:2, :]
                       + cv_ref[carry_rows:carry_rows + n, :] * cw_ref[2:3, :])
    tail = cv_ref[n:n + carry_rows, :]
    cv_ref[0:carry_rows, :] = tail
    mix = mix + _dot(_rms_norm(y_conv, bc_ref[...]).astype(BF16), wout_ref[D_SSM:D_SSM + D_CONV, :])

    h1_ref[...] = _layer_norm(DEEPNORM_ALPHA * h + mix, g1_ref[...], b1_ref[...])

    @pl.when(c == n_prompt_chunks - 1)
    def _():
        pr_ref[...] = hr_ref[...]
        pi_ref[...] = hi_ref[...]
        pc_ref[...] = tail

    @pl.when(c == n_prompt_chunks)
    def _():
        sr_ref[...] = hr_ref[...]
        si_ref[...] = hi_ref[...]
        sc_ref[...] = tail


def _mix(x_all, h0r, h0i, cbuf, n_prompt_chunks, weights):
    t_rows = x_all.shape[0]
    n_chunks = t_rows // CHUNK_ROWS
    const2 = lambda c: (0, 0)
    const3 = lambda c: (0, 0, 0)
    w_specs = [pl.BlockSpec(w.shape, const3 if w.ndim == 3 else const2) for w in weights]
    state = jax.ShapeDtypeStruct((BATCH, D_STATE), F32)
    tail = jax.ShapeDtypeStruct((2 * BATCH, D_CONV), F32)
    state_spec = pl.BlockSpec((BATCH, D_STATE), const2)
    tail_spec = pl.BlockSpec((2 * BATCH, D_CONV), const2)
    return pl.pallas_call(
        functools.partial(_mix_kernel, n_prompt_chunks),
        grid=(n_chunks,),
        in_specs=[pl.BlockSpec((CHUNK_ROWS, D_MODEL), lambda c: (c, 0)), state_spec, state_spec, tail_spec] + w_specs,
        out_specs=(pl.BlockSpec((CHUNK_ROWS, D_MODEL), lambda c: (c, 0)),
                   state_spec, state_spec, tail_spec, state_spec, state_spec, tail_spec),
        out_shape=(jax.ShapeDtypeStruct((t_rows, D_MODEL), F32), state, state, tail, state, state, tail),
        scratch_shapes=[pltpu.VMEM((CHUNK_ROWS, D_STATE), F32), pltpu.VMEM((CHUNK_ROWS, D_STATE), F32),
                        pltpu.VMEM((BATCH, D_STATE), F32), pltpu.VMEM((BATCH, D_STATE), F32),
                        pltpu.VMEM((CHUNK_ROWS + 2 * BATCH, D_CONV), F32)],
        compiler_params=pltpu.CompilerParams(dimension_semantics=("arbitrary",), vmem_limit_bytes=V7X_VMEM_LIMIT),
        name="mix",
    )(x_all, h0r, h0i, cbuf, *weights)


def _route_kernel(h_ref, wrt_ref, bias_ref, tri_ref, e_ref, w_ref, rank_ref, cnt_ref, run_ref):
    i = pl.program_id(0)
    tr = ROUTE_TILE

    @pl.when(i == 0)
    def _():
        run_ref[...] = jnp.zeros_like(run_ref)

    logits = lax.dot_general(wrt_ref[...], h_ref[...], (((1,), (1,)), ((), ())),
                             precision=lax.Precision.HIGHEST, preferred_element_type=F32)
    scores = jax.nn.sigmoid(logits)
    sel = scores + bias_ref[...]

    sub = lax.broadcasted_iota(I32, (GROUP_SIZE, tr), 0).astype(F32)
    blocks, gscore = [], []
    for g in range(N_EXPERT_GROUPS):
        blk = sel[g * GROUP_SIZE:(g + 1) * GROUP_SIZE, :]
        m1 = jnp.max(blk, axis=0, keepdims=True)
        first = jnp.min(jnp.where(blk == m1, sub, float(GROUP_SIZE)), axis=0, keepdims=True)
        m2 = jnp.max(jnp.where(sub == first, NEG_INF, blk), axis=0, keepdims=True)
        blocks.append(blk)
        gscore.append(m1 + m2)
    masked = []
    for g in range(N_EXPERT_GROUPS):
        beaten = jnp.zeros((1, tr), F32)
        for o in range(N_EXPERT_GROUPS):
            if o == g:
                continue
            wins = gscore[o] >= gscore[g] if o < g else gscore[o] > gscore[g]
            beaten = beaten + wins.astype(F32)
        masked.append(jnp.where(beaten < float(TOPK_GROUPS), blocks[g], NEG_INF))
    masked = jnp.concatenate(masked, axis=0)

    row = lax.broadcasted_iota(I32, (N_EXPERTS, tr), 0).astype(F32)
    picked = jnp.zeros((N_EXPERTS, tr), F32)
    hots, idxs, gates = [], [], []
    for _ in range(TOP_K):
        m = jnp.max(masked, axis=0, keepdims=True)
        idx = jnp.min(jnp.where(masked == m, row, float(N_EXPERTS)), axis=0, keepdims=True)
        hot = row == idx
        hots.append(hot)
        idxs.append(idx)
        gates.append(jnp.sum(jnp.where(hot, scores, 0.0), axis=0, keepdims=True))
        masked = jnp.where(hot, NEG_INF, masked)
        picked = picked + hot.astype(F32)
    total = gates[0]
    for k in range(1, TOP_K):
        total = total + gates[k]

    before = _dot(picked.astype(BF16), tri_ref[...]) + run_ref[...]
    for k in range(TOP_K):
        e_ref[k:k + 1, :] = idxs[k].astype(I32)
        w_ref[k:k + 1, :] = gates[k] / total * ROUTED_SCALE
        rank_ref[k:k + 1, :] = jnp.sum(jnp.where(hots[k], before, 0.0), axis=0, keepdims=True).astype(I32)
    e_ref[TOP_K:, :] = jnp.zeros((8 - TOP_K, tr), I32)
    w_ref[TOP_K:, :] = jnp.zeros((8 - TOP_K, tr), F32)
    rank_ref[TOP_K:, :] = jnp.zeros((8 - TOP_K, tr), I32)
    run_ref[...] = run_ref[...] + jnp.sum(picked, axis=1, keepdims=True)
    cnt_ref[...] = jnp.broadcast_to(run_ref[...], cnt_ref.shape)


def _route(h1, w_router, router_bias):
    t_rows = h1.shape[0]
    tr = ROUTE_TILE
    tri = jnp.triu(jnp.ones((tr, tr), BF16), k=1)
    tok = lambda i: (0, i)
    const2 = lambda i: (0, 0)
    return pl.pallas_call(
        _route_kernel,
        grid=(t_rows // tr,),
        in_specs=[pl.BlockSpec((tr, D_MODEL), lambda i: (i, 0)),
                  pl.BlockSpec((N_EXPERTS, D_MODEL), const2),
                  pl.BlockSpec((N_EXPERTS, 1), const2),
                  pl.BlockSpec((tr, tr), const2)],
        out_specs=(pl.BlockSpec((8, tr), tok), pl.BlockSpec((8, tr), tok), pl.BlockSpec((8, tr), tok),
                   pl.BlockSpec((N_EXPERTS, 128), const2)),
        out_shape=(jax.ShapeDtypeStruct((8, t_rows), I32), jax.ShapeDtypeStruct((8, t_rows), F32),
                   jax.ShapeDtypeStruct((8, t_rows), I32), jax.ShapeDtypeStruct((N_EXPERTS, 128), F32)),
        scratch_shapes=[pltpu.VMEM((N_EXPERTS, 1), F32)],
        compiler_params=pltpu.CompilerParams(dimension_semantics=("arbitrary",)),
        name="route",
    )(h1, w_router.T, router_bias.reshape(N_EXPERTS, 1), tri)


def _pos_kernel(n_blocks_pad, e_ref, rank_ref, ccol_ref, crow_ref, pos_ref, be_ref, nb_ref):
    tr = ROUTE_TILE
    blk = float(EXPERT_BLOCK)
    pad = lambda cnt: jnp.floor((cnt + (blk - 1.0)) * (1.0 / blk)) * blk
    padded_row = pad(crow_ref[0:1, :])
    lane = lax.broadcasted_iota(I32, (N_EXPERTS, 128), 1)
    sub = lax.broadcasted_iota(I32, (N_EXPERTS, 128), 0)
    start = jnp.sum(jnp.where(lane < sub, padded_row, 0.0), axis=1, keepdims=True)
    end = start + pad(ccol_ref[:, 0:1])

    row = lax.broadcasted_iota(I32, (N_EXPERTS, tr), 0)
    for k in range(TOP_K):
        base = jnp.sum(jnp.where(row == e_ref[k:k + 1, :], start, 0.0), axis=0, keepdims=True)
        pos_ref[k:k + 1, :] = rank_ref[k:k + 1, :] + base.astype(I32)
    pos_ref[TOP_K:, :] = jnp.zeros((8 - TOP_K, tr), I32)

    first_row = lax.broadcasted_iota(I32, (N_EXPERTS, n_blocks_pad), 1).astype(F32) * blk
    owner = jnp.sum((end <= first_row).astype(F32), axis=0, keepdims=True)
    be_ref[...] = jnp.broadcast_to(jnp.minimum(owner, float(N_EXPERTS - 1)).astype(I32), be_ref.shape)
    used = jnp.max(end, axis=0, keepdims=True) * (1.0 / blk)
    nb_ref[...] = jnp.broadcast_to(used.astype(I32), nb_ref.shape)


def _pos(e, rank, cnt, n_blocks_pad):
    t_rows = e.shape[1]
    tr = ROUTE_TILE
    counts = cnt[:, 0]
    ccol = jnp.broadcast_to(counts[:, None], (N_EXPERTS, 128))
    crow = jnp.broadcast_to(jnp.pad(counts, (0, 128 - N_EXPERTS))[None, :], (8, 128))
    tok = lambda i: (0, i)
    const2 = lambda i: (0, 0)
    return pl.pallas_call(
        functools.partial(_pos_kernel, n_blocks_pad),
        grid=(t_rows // tr,),
        in_specs=[pl.BlockSpec((8, tr), tok), pl.BlockSpec((8, tr), tok),
                  pl.BlockSpec((N_EXPERTS, 128), const2), pl.BlockSpec((8, 128), const2)],
        out_specs=(pl.BlockSpec((8, tr), tok), pl.BlockSpec((8, n_blocks_pad), const2), pl.BlockSpec((8, 128), const2)),
        out_shape=(jax.ShapeDtypeStruct((8, t_rows), I32), jax.ShapeDtypeStruct((8, n_blocks_pad), I32),
                   jax.ShapeDtypeStruct((8, 128), I32)),
        compiler_params=pltpu.CompilerParams(dimension_semantics=("arbitrary",)),
        name="pos",
    )(e, rank, ccol, crow)


def _row_copy(src_ref, src_row, dst_ref, dst_row, sem):
    return pltpu.make_async_copy(src_ref.at[pl.ds(src_row, 1), :], dst_ref.at[pl.ds(dst_row, 1), :], sem)


def _scatter_kernel(pos_hbm, h_ref, xs_in, xs_out, pos_smem, psem, sem):
    del xs_in
    i = pl.program_id(0)
    ts = SCATTER_TILE
    load = pltpu.make_async_copy(pos_hbm.at[i], pos_smem, psem)
    load.start()
    load.wait()

    def issue(t, _):
        for k in range(TOP_K):
            _row_copy(h_ref, t, xs_out, pos_smem[k, t], sem).start()
        return 0

    lax.fori_loop(0, ts, issue, 0)

    def drain(t, _):
        for k in range(TOP_K):
            _row_copy(h_ref, 0, xs_out, 0, sem).wait()
        return 0

    lax.fori_loop(0, ts, drain, 0)


def _tile_major(a, tile):
    return jnp.transpose(a.reshape(8, a.shape[1] // tile, tile), (1, 0, 2))


def _scatter(h1, pos, n_rows):
    t_rows = h1.shape[0]
    ts = SCATTER_TILE
    xs0 = jnp.zeros((n_rows, D_MODEL), F32)
    return pl.pallas_call(
        _scatter_kernel,
        grid=(t_rows // ts,),
        in_specs=[pl.BlockSpec(memory_space=pl.ANY),
                  pl.BlockSpec((ts, D_MODEL), lambda i: (i, 0)),
                  pl.BlockSpec(memory_space=pl.ANY)],
        out_specs=pl.BlockSpec(memory_space=pl.ANY),
        out_shape=jax.ShapeDtypeStruct((n_rows, D_MODEL), F32),
        scratch_shapes=[pltpu.SMEM((8, ts), I32), pltpu.SemaphoreType.DMA(()), pltpu.SemaphoreType.DMA(())],
        input_output_aliases={2: 0},
        compiler_params=pltpu.CompilerParams(dimension_semantics=("arbitrary",)),
        name="scatter",
    )(_tile_major(pos, ts), h1, xs0)


def _experts_kernel(be_ref, nb_ref, x_ref, wg_ref, wu_ref, wd_ref, y_ref):
    del be_ref
    b = pl.program_id(0)

    @pl.when(b < nb_ref[0])
    def _():
        xb = x_ref[...].astype(BF16)
        gate = _dot(xb, wg_ref[...].astype(BF16))
        up = _dot(xb, wu_ref[...].astype(BF16))
        hid = (gate * jax.nn.sigmoid(gate) * up).astype(BF16)
        y_ref[...] = _dot(hid, wd_ref[...].astype(BF16))

    @pl.when(b >= nb_ref[0])
    def _():
        y_ref[...] = jnp.zeros_like(y_ref)


def _experts(xs, block_e, n_used, w_gate, w_up, w_down):
    n_rows = xs.shape[0]
    blk = EXPERT_BLOCK
    return pl.pallas_call(
        _experts_kernel,
        grid_spec=pltpu.PrefetchScalarGridSpec(
            num_scalar_prefetch=2,
            grid=(n_rows // blk,),
            in_specs=[pl.BlockSpec((blk, D_MODEL), lambda b, be, nb: (b, 0)),
                      pl.BlockSpec((None, D_MODEL, D_EXPERT), lambda b, be, nb: (be[b], 0, 0)),
                      pl.BlockSpec((None, D_MODEL, D_EXPERT), lambda b, be, nb: (be[b], 0, 0)),
                      pl.BlockSpec((None, D_EXPERT, D_MODEL), lambda b, be, nb: (be[b], 0, 0))],
            out_specs=pl.BlockSpec((blk, D_MODEL), lambda b, be, nb: (b, 0)),
        ),
        out_shape=jax.ShapeDtypeStruct((n_rows, D_MODEL), F32),
        compiler_params=pltpu.CompilerParams(dimension_semantics=("arbitrary",)),
        name="experts",
    )(block_e, n_used, xs, w_gate, w_up, w_down)


def _combine_kernel(pos_hbm, h_ref, w_ref, ys_hbm, wsg_ref, wsu_ref, wsd_ref, g2_ref, b2_ref,
                    out_ref, pos_smem, ybuf, psem, sem):
    i = pl.program_id(0)
    tc = COMBINE_TILE
    load = pltpu.make_async_copy(pos_hbm.at[i], pos_smem, psem)
    load.start()
    load.wait()

    def issue(t, _):
        for k in range(TOP_K):
            _row_copy(ys_hbm, pos_smem[k, t], ybuf.at[k], t, sem).start()
        return 0

    lax.fori_loop(0, tc, issue, 0)

    h = h_ref[...]
    hb = h.astype(BF16)
    gate = _dot(hb, wsg_ref[...])
    up = _dot(hb, wsu_ref[...])
    acc = _dot((gate * jax.nn.sigmoid(gate) * up).astype(BF16), wsd_ref[...])

    eye = (lax.broadcasted_iota(I32, (tc, tc), 0) == lax.broadcasted_iota(I32, (tc, tc), 1)).astype(F32)
    w_cols = lax.dot_general(eye, w_ref[...], (((1,), (1,)), ((), ())),
                             precision=lax.Precision.HIGHEST, preferred_element_type=F32)

    def drain(t, _):
        for k in range(TOP_K):
            _row_copy(ys_hbm, 0, ybuf.at[k], 0, sem).wait()
        return 0

    lax.fori_loop(0, tc, drain, 0)
    for k in range(TOP_K):
        acc = acc + ybuf[k] * w_cols[:, k:k + 1]
    out_ref[...] = _layer_norm(DEEPNORM_ALPHA * h + acc, g2_ref[...], b2_ref[...])


def _combine(h1, pos, gate_w, ys, ws_gate, ws_up, ws_down, ln2_g, ln2_b):
    t_rows = h1.shape[0]
    tc = COMBINE_TILE
    const2 = lambda i: (0, 0)
    return pl.pallas_call(
        _combine_kernel,
        grid=(t_rows // tc,),
        in_specs=[pl.BlockSpec(memory_space=pl.ANY),
                  pl.BlockSpec((tc, D_MODEL), lambda i: (i, 0)),
                  pl.BlockSpec((8, tc), lambda i: (0, i)),
                  pl.BlockSpec(memory_space=pl.ANY),
                  pl.BlockSpec((D_MODEL, D_EXPERT), const2),
                  pl.BlockSpec((D_MODEL, D_EXPERT), const2),
                  pl.BlockSpec((D_EXPERT, D_MODEL), const2),
                  pl.BlockSpec((1, D_MODEL), const2),
                  pl.BlockSpec((1, D_MODEL), const2)],
        out_specs=pl.BlockSpec((tc, D_MODEL), lambda i: (i, 0)),
        out_shape=jax.ShapeDtypeStruct((t_rows, D_MODEL), F32),
        scratch_shapes=[pltpu.SMEM((8, tc), I32), pltpu.VMEM((TOP_K, tc, D_MODEL), F32),
                        pltpu.SemaphoreType.DMA(()), pltpu.SemaphoreType.DMA(())],
        compiler_params=pltpu.CompilerParams(dimension_semantics=("arbitrary",)),
        name="combine",
    )(_tile_major(pos, tc), h1, gate_w, ys, ws_gate.astype(BF16), ws_up.astype(BF16), ws_down.astype(BF16),
      ln2_g.reshape(1, D_MODEL), ln2_b.reshape(1, D_MODEL))


def _time_major(x):
    b, l, d = x.shape
    return jnp.transpose(x, (1, 0, 2)).reshape(l * b, d)


def _batch_major(x, b):
    return jnp.transpose(x.reshape(x.shape[0] // b, b, x.shape[1]), (1, 0, 2))


def _block_diag(blocks):
    n, r, c = blocks.shape
    eye = jnp.eye(n, dtype=blocks.dtype)
    return (blocks[:, :, None, :] * eye[:, None, :, None]).reshape(n * r, n * c)


def kernel(x_prompt, x_sample, state_ssm_re, state_ssm_im, cache_conv, ln_in_g, ln_in_b, w_in, lam_re, lam_im, log_dt, ssm_b_re, ssm_b_im, ssm_c_re, ssm_c_im, ssm_d, w_glu, b_glu, conv_w, beta_ssm, beta_conv, w_out, ln1_g, ln1_b, w_router, router_bias, w_gate, w_up, w_down, ws_gate, ws_up, ws_down, ln2_g, ln2_b):
    bp, lp, _ = x_prompt.shape
    bs, ls, _ = x_sample.shape
    assert bp == BATCH and bs == BATCH and ls == CHUNK_T and lp % CHUNK_T == 0
    assert w_in.shape[0] == 1, "single-layer model"
    n_prompt = bp * lp
    row = lambda a: a.reshape(1, -1)

    a_re, a_im, bb_re, bb_im = _prep(lam_re[0], lam_im[0], log_dt[0], ssm_b_re[0], ssm_b_im[0])
    groups_in = 128 // SSM_GROUP
    groups_out = 256 // SSM_GROUP
    bbd = lambda bb: jnp.stack([_block_diag(bb[j * groups_in:(j + 1) * groups_in])
                                for j in range(N_GROUPS // groups_in)]).astype(BF16)
    ct = lambda cc: jnp.transpose(cc, (0, 2, 1))
    cbd = lambda cc: jnp.stack([_block_diag(ct(cc)[j * groups_out:(j + 1) * groups_out])
                                for j in range(N_GROUPS // groups_out)]).astype(BF16)
    mix_weights = (row(ln_in_g), row(ln_in_b), w_in[0].astype(BF16), row(a_re), row(a_im),
                   bbd(bb_re), bbd(bb_im), cbd(ssm_c_re[0]), cbd(-ssm_c_im[0]), row(ssm_d[0]),
                   w_glu[0].astype(BF16), row(b_glu[0]), conv_w[0], row(beta_ssm[0]), row(beta_conv[0]),
                   w_out[0].astype(BF16), row(ln1_g[0]), row(ln1_b[0]))

    x_all = jnp.concatenate([_time_major(x_prompt), _time_major(x_sample)], axis=0)
    h0r = state_ssm_re[0].reshape(BATCH, D_STATE)
    h0i = state_ssm_im[0].reshape(BATCH, D_STATE)
    cbuf = _time_major(cache_conv[0])
    h1, p_re, p_im, p_conv, s_re, s_im, s_conv = _mix(x_all, h0r, h0i, cbuf, n_prompt // CHUNK_ROWS, mix_weights)

    t_rows = h1.shape[0]
    n_rows = t_rows * TOP_K + N_EXPERTS * EXPERT_BLOCK
    n_blocks = n_rows // EXPERT_BLOCK
    n_blocks_pad = -(-n_blocks // 128) * 128
    e, gate_w, rank, cnt = _route(h1, w_router[0], router_bias[0])
    pos, block_e, n_used = _pos(e, rank, cnt, n_blocks_pad)
    xs = _scatter(h1, pos, n_rows)
    ys = _experts(xs, block_e[0, :n_blocks], n_used[0, :1], w_gate[0], w_up[0], w_down[0])
    out = _combine(h1, pos, gate_w, ys, ws_gate[0], ws_up[0], ws_down[0], ln2_g[0], ln2_b[0])

    st = lambda s: s.reshape(1, BATCH, N_GROUPS, SSM_STATE)
    cv = lambda t: _batch_major(t, BATCH)[None]
    return (_batch_major(out[:n_prompt], BATCH), _batch_major(out[n_prompt:], BATCH),
            st(p_re), st(p_im), cv(p_conv), st(s_re), st(s_im), cv(s_conv))
```

```python
import functools
import math

import jax
import jax.numpy as jnp
from jax import lax
from jax.experimental import pallas as pl
from jax.experimental.pallas import tpu as pltpu

F32 = jnp.float32
BF16 = jnp.bfloat16
I32 = jnp.int32

D_MODEL = 1024
D_SSM = 512
D_CONV = 512
SSM_GROUP = 16
N_GROUPS = 32
SSM_STATE = 64
D_STATE = N_GROUPS * SSM_STATE
N_EXPERTS = 64
TOP_K = 6
N_EXPERT_GROUPS = 8
GROUP_SIZE = N_EXPERTS // N_EXPERT_GROUPS
TOPK_GROUPS = 4
D_EXPERT = 256
ROUTED_SCALE = 2.5
DEPTH = 1
DEEPNORM_ALPHA = (2.0 * DEPTH) ** 0.25
LN_EPS = 1e-5
RMS_EPS = 1e-6

BATCH = 8
CHUNK_T = 64
CHUNK_ROWS = CHUNK_T * BATCH
SCAN_COLS = 512
ROUTE_TILE = 512
SCATTER_TILE = 256
COMBINE_TILE = 128
COMBINE_TILES_PER_STEP = 10
EXPERT_BLOCK = 256
LANES = 128
TOKEN_ROWS = D_MODEL // LANES
V7X_VMEM_LIMIT = 56 * 1024 * 1024
NEG_INF = float("-inf")


def _layer_norm(x, g, b):
    mu = jnp.mean(x, axis=-1, keepdims=True)
    xc = x - mu
    var = jnp.mean(xc * xc, axis=-1, keepdims=True)
    return xc * lax.rsqrt(var + LN_EPS) * g + b


def _rms_norm(x, g):
    return x * lax.rsqrt(jnp.mean(x * x, axis=-1, keepdims=True) + RMS_EPS) * g


def _dot(a, b):
    return jnp.dot(a, b, preferred_element_type=F32)


def _discretise(lr, li, log_dt):
    dt = jnp.exp(log_dt)
    mag = jnp.exp(lr * dt)
    ar = mag * jnp.cos(li * dt)
    ai = mag * jnp.sin(li * dt)
    den = lr * lr + li * li
    qr = ((ar - 1.0) * lr + ai * li) / den
    qi = (ai * lr - (ar - 1.0) * li) / den
    return ar, ai, qr, qi


def _prep_kernel(lr_ref, li_ref, ldt_ref, lrc_ref, lic_ref, ldtc_ref, br_ref, bi_ref,
                 ar_ref, ai_ref, bbr_ref, bbi_ref):
    ar, ai, _, _ = _discretise(lr_ref[...], li_ref[...], ldt_ref[...])
    ar_ref[...] = ar
    ai_ref[...] = ai
    _, _, qr, qi = _discretise(lrc_ref[...], lic_ref[...], ldtc_ref[...])
    br = br_ref[...]
    bi = bi_ref[...]
    bbr_ref[...] = qr * br - qi * bi
    bbi_ref[...] = qr * bi + qi * br


def _prep(lam_re, lam_im, log_dt, b_re, b_im):
    g, p = lam_re.shape
    per_channel = lambda a: jnp.repeat(a, SSM_GROUP, axis=0)
    rows = lambda b: jnp.transpose(b, (0, 2, 1)).reshape(g * SSM_GROUP, p)
    ldt = log_dt.reshape(g, 1)
    ar, ai, bbr, bbi = pl.pallas_call(
        _prep_kernel,
        out_shape=(jax.ShapeDtypeStruct((g, p), F32), jax.ShapeDtypeStruct((g, p), F32),
                   jax.ShapeDtypeStruct((g * SSM_GROUP, p), F32), jax.ShapeDtypeStruct((g * SSM_GROUP, p), F32)),
        name="prep",
    )(lam_re, lam_im, ldt, per_channel(lam_re), per_channel(lam_im), per_channel(ldt), rows(b_re), rows(b_im))
    return ar, ai, bbr.reshape(g, SSM_GROUP, p), bbi.reshape(g, SSM_GROUP, p)


def _chunk_copies(src_hbm, t0, xbuf, slot, sems):
    return [pltpu.make_async_copy(src_hbm.at[b, pl.ds(t0, CHUNK_T), :], xbuf.at[slot, :, b, :], sems.at[slot])
            for b in range(BATCH)]


def _mix_kernel(n_prompt_chunks,
                xp_hbm, xs_hbm, h0r_ref, h0i_ref, cbuf_ref, lng_ref, lnb_ref, win_ref, ar_ref, ai_ref,
                bbr_ref, bbi_ref, cbr_ref, cbi_ref, dsk_ref, wglu_ref, bglu_ref, cw_ref,
                bs_ref, bc_ref, wout_ref, g1_ref, b1_ref,
                h1_ref, pr_ref, pi_ref, pc_ref, sr_ref, si_ref, sc_ref,
                str_ref, sti_ref, hr_ref, hi_ref, cv_ref, xbuf, xsems):
    c = pl.program_id(0)
    n = CHUNK_ROWS
    carry_rows = 2 * BATCH
    slot = c % 2

    @pl.when(c == 0)
    def _():
        for cp in _chunk_copies(xp_hbm, 0, xbuf, 0, xsems):
            cp.start()

    @pl.when(c + 1 < n_prompt_chunks)
    def _():
        for cp in _chunk_copies(xp_hbm, pl.multiple_of((c + 1) * CHUNK_T, CHUNK_T), xbuf, 1 - slot, xsems):
            cp.start()

    @pl.when(c + 1 == n_prompt_chunks)
    def _():
        for cp in _chunk_copies(xs_hbm, 0, xbuf, 1 - slot, xsems):
            cp.start()

    @pl.when(c == 0)
    def _():
        hr_ref[...] = jnp.zeros_like(hr_ref)
        hi_ref[...] = jnp.zeros_like(hi_ref)
        cv_ref[0:carry_rows, :] = jnp.zeros((carry_rows, D_CONV), F32)

    @pl.when(c == n_prompt_chunks)
    def _():
        hr_ref[...] = h0r_ref[...]
        hi_ref[...] = h0i_ref[...]
        cv_ref[0:carry_rows, :] = cbuf_ref[...]

    for cp in _chunk_copies(xp_hbm, 0, xbuf, slot, xsems):
        cp.wait()
    h = _layer_norm(xbuf[slot].reshape(n, D_MODEL), lng_ref[...], lnb_ref[...])
    hb = h.astype(BF16)

    u = _dot(hb, win_ref[:, 0:D_SSM])
    ub = u.astype(BF16)
    u_tile = 128
    s_tile = u_tile // SSM_GROUP * SSM_STATE
    for j in range(D_SSM // u_tile):
        uj = ub[:, j * u_tile:(j + 1) * u_tile]
        str_ref[:, j * s_tile:(j + 1) * s_tile] = _dot(uj, bbr_ref[j])
        sti_ref[:, j * s_tile:(j + 1) * s_tile] = _dot(uj, bbi_ref[j])

    for cb in range(D_STATE // SCAN_COLS):
        cols = slice(cb * SCAN_COLS, (cb + 1) * SCAN_COLS)
        ar = jnp.broadcast_to(ar_ref[:, cols], (BATCH, SCAN_COLS))
        ai = jnp.broadcast_to(ai_ref[:, cols], (BATCH, SCAN_COLS))

        def step(t, carry, cols=cols, ar=ar, ai=ai):
            sr, si = carry
            r0 = pl.multiple_of(t * BATCH, BATCH)
            nr = ar * sr - ai * si + str_ref[pl.ds(r0, BATCH), cols]
            ni = ar * si + ai * sr + sti_ref[pl.ds(r0, BATCH), cols]
            str_ref[pl.ds(r0, BATCH), cols] = nr
            sti_ref[pl.ds(r0, BATCH), cols] = ni
            return nr, ni

        sr, si = lax.fori_loop(0, CHUNK_T, step, (hr_ref[:, cols], hi_ref[:, cols]), unroll=8)
        hr_ref[:, cols] = sr
        hi_ref[:, cols] = si

    k_tile = 1024
    ys = []
    for j in range(D_STATE // k_tile):
        sl = slice(j * k_tile, (j + 1) * k_tile)
        ys.append(_dot(str_ref[:, sl].astype(BF16), cbr_ref[j]) + _dot(sti_ref[:, sl].astype(BF16), cbi_ref[j]))
    y_ssm = jnp.concatenate(ys, axis=-1) + dsk_ref[...] * u
    g = jax.nn.gelu(y_ssm)
    y_ssm = g * jax.nn.sigmoid(_dot(g.astype(BF16), wglu_ref[...]) + bglu_ref[...])
    mix = _dot(_rms_norm(y_ssm, bs_ref[...]).astype(BF16), wout_ref[0:D_SSM, :])

    gate_b = _dot(hb, win_ref[:, D_SSM:D_SSM + D_CONV])
    gate_c = _dot(hb, win_ref[:, D_SSM + D_CONV:D_SSM + 2 * D_CONV])
    v = _dot(hb, win_ref[:, D_SSM + 2 * D_CONV:D_SSM + 3 * D_CONV])
    cv_ref[carry_rows:carry_rows + n, :] = gate_c * v
    y_conv = gate_b * (cv_ref[0:n, :] * cw_ref[0:1, :]
                       + cv_ref[BATCH:BATCH + n, :] * cw_ref[1:2, :]
                       + cv_ref[carry_rows:carry_rows + n, :] * cw_ref[2:3, :])
    tail = cv_ref[n:n + carry_rows, :]
    cv_ref[0:carry_rows, :] = tail
    mix = mix + _dot(_rms_norm(y_conv, bc_ref[...]).astype(BF16), wout_ref[D_SSM:D_SSM + D_CONV, :])

    h1_ref[...] = _layer_norm(DEEPNORM_ALPHA * h + mix, g1_ref[...], b1_ref[...])

    @pl.when(c == n_prompt_chunks - 1)
    def _():
        pr_ref[...] = hr_ref[...]
        pi_ref[...] = hi_ref[...]
        pc_ref[...] = tail

    @pl.when(c == n_prompt_chunks)
    def _():
        sr_ref[...] = hr_ref[...]
        si_ref[...] = hi_ref[...]
        sc_ref[...] = tail


def _mix(x_prompt, x_sample, h0r, h0i, cbuf, weights):
    n_prompt_chunks = x_prompt.shape[1] // CHUNK_T
    n_chunks = n_prompt_chunks + 1
    t_rows = n_chunks * CHUNK_ROWS
    any_spec = pl.BlockSpec(memory_space=pl.ANY)
    const2 = lambda c: (0, 0)
    const3 = lambda c: (0, 0, 0)
    w_specs = [pl.BlockSpec(w.shape, const3 if w.ndim == 3 else const2) for w in weights]
    state = jax.ShapeDtypeStruct((BATCH, D_STATE), F32)
    tail = jax.ShapeDtypeStruct((2 * BATCH, D_CONV), F32)
    state_spec = pl.BlockSpec((BATCH, D_STATE), const2)
    tail_spec = pl.BlockSpec((2 * BATCH, D_CONV), const2)
    return pl.pallas_call(
        functools.partial(_mix_kernel, n_prompt_chunks),
        grid=(n_chunks,),
        in_specs=[any_spec, any_spec, state_spec, state_spec, tail_spec] + w_specs,
        out_specs=(pl.BlockSpec((CHUNK_ROWS, D_MODEL), lambda c: (c, 0)),
                   state_spec, state_spec, tail_spec, state_spec, state_spec, tail_spec),
        out_shape=(jax.ShapeDtypeStruct((t_rows, D_MODEL), F32), state, state, tail, state, state, tail),
        scratch_shapes=[pltpu.VMEM((CHUNK_ROWS, D_STATE), F32), pltpu.VMEM((CHUNK_ROWS, D_STATE), F32),
                        pltpu.VMEM((BATCH, D_STATE), F32), pltpu.VMEM((BATCH, D_STATE), F32),
                        pltpu.VMEM((CHUNK_ROWS + 2 * BATCH, D_CONV), F32),
                        pltpu.VMEM((2, CHUNK_T, BATCH, D_MODEL), F32), pltpu.SemaphoreType.DMA((2,))],
        compiler_params=pltpu.CompilerParams(dimension_semantics=("arbitrary",), vmem_limit_bytes=V7X_VMEM_LIMIT),
        name="mix",
    )(x_prompt, x_sample, h0r, h0i, cbuf, *weights)


def _route_kernel(h_ref, wrt_ref, bias_ref, tri_ref, e_ref, w_ref, rank_ref, cnt_ref, run_ref):
    i = pl.program_id(0)
    tr = ROUTE_TILE

    @pl.when(i == 0)
    def _():
        run_ref[...] = jnp.zeros_like(run_ref)

    logits = lax.dot_general(wrt_ref[...], h_ref[...], (((1,), (1,)), ((), ())),
                             precision=lax.Precision.HIGHEST, preferred_element_type=F32)
    scores = jax.nn.sigmoid(logits)
    sel = scores + bias_ref[...]

    sub = lax.broadcasted_iota(I32, (GROUP_SIZE, tr), 0).astype(F32)
    blocks, gscore = [], []
    for g in range(N_EXPERT_GROUPS):
        blk = sel[g * GROUP_SIZE:(g + 1) * GROUP_SIZE, :]
        m1 = jnp.max(blk, axis=0, keepdims=True)
        first = jnp.min(jnp.where(blk == m1, sub, float(GROUP_SIZE)), axis=0, keepdims=True)
        m2 = jnp.max(jnp.where(sub == first, NEG_INF, blk), axis=0, keepdims=True)
        blocks.append(blk)
        gscore.append(m1 + m2)
    masked = []
    for g in range(N_EXPERT_GROUPS):
        beaten = jnp.zeros((1, tr), F32)
        for o in range(N_EXPERT_GROUPS):
            if o == g:
                continue
            wins = gscore[o] >= gscore[g] if o < g else gscore[o] > gscore[g]
            beaten = beaten + wins.astype(F32)
        masked.append(jnp.where(beaten < float(TOPK_GROUPS), blocks[g], NEG_INF))
    masked = jnp.concatenate(masked, axis=0)

    row = lax.broadcasted_iota(I32, (N_EXPERTS, tr), 0).astype(F32)
    picked = jnp.zeros((N_EXPERTS, tr), F32)
    hots, idxs, gates = [], [], []
    for _ in range(TOP_K):
        m = jnp.max(masked, axis=0, keepdims=True)
        idx = jnp.min(jnp.where(masked == m, row, float(N_EXPERTS)), axis=0, keepdims=True)
        hot = row == idx
        hots.append(hot)
        idxs.append(idx)
        gates.append(jnp.sum(jnp.where(hot, scores, 0.0), axis=0, keepdims=True))
        masked = jnp.where(hot, NEG_INF, masked)
        picked = picked + hot.astype(F32)
    total = gates[0]
    for k in range(1, TOP_K):
        total = total + gates[k]

    before = _dot(picked.astype(BF16), tri_ref[...]) + run_ref[...]
    for k in range(TOP_K):
        e_ref[k:k + 1, :] = idxs[k].astype(I32)
        w_ref[k:k + 1, :] = gates[k] / total * ROUTED_SCALE
        rank_ref[k:k + 1, :] = jnp.sum(jnp.where(hots[k], before, 0.0), axis=0, keepdims=True).astype(I32)
    e_ref[TOP_K:, :] = jnp.zeros((8 - TOP_K, tr), I32)
    w_ref[TOP_K:, :] = jnp.zeros((8 - TOP_K, tr), F32)
    rank_ref[TOP_K:, :] = jnp.zeros((8 - TOP_K, tr), I32)
    run_ref[...] = run_ref[...] + jnp.sum(picked, axis=1, keepdims=True)
    cnt_ref[...] = jnp.broadcast_to(run_ref[...], cnt_ref.shape)


def _route(h1, w_router, router_bias):
    t_rows = h1.shape[0]
    tr = ROUTE_TILE
    tri = jnp.triu(jnp.ones((tr, tr), BF16), k=1)
    tok = lambda i: (0, i)
    const2 = lambda i: (0, 0)
    return pl.pallas_call(
        _route_kernel,
        grid=(t_rows // tr,),
        in_specs=[pl.BlockSpec((tr, D_MODEL), lambda i: (i, 0)),
                  pl.BlockSpec((N_EXPERTS, D_MODEL), const2),
                  pl.BlockSpec((N_EXPERTS, 1), const2),
                  pl.BlockSpec((tr, tr), const2)],
        out_specs=(pl.BlockSpec((8, tr), tok), pl.BlockSpec((8, tr), tok), pl.BlockSpec((8, tr), tok),
                   pl.BlockSpec((N_EXPERTS, 128), const2)),
        out_shape=(jax.ShapeDtypeStruct((8, t_rows), I32), jax.ShapeDtypeStruct((8, t_rows), F32),
                   jax.ShapeDtypeStruct((8, t_rows), I32), jax.ShapeDtypeStruct((N_EXPERTS, 128), F32)),
        scratch_shapes=[pltpu.VMEM((N_EXPERTS, 1), F32)],
        compiler_params=pltpu.CompilerParams(dimension_semantics=("arbitrary",)),
        name="route",
    )(h1, w_router.T, router_bias.reshape(N_EXPERTS, 1), tri)


def _pos_kernel(n_blocks_pad, e_ref, rank_ref, ccol_ref, crow_ref, pos_ref, be_ref, nb_ref, start_ref):
    tr = ROUTE_TILE
    blk = float(EXPERT_BLOCK)
    pad = lambda cnt: jnp.floor((cnt + (blk - 1.0)) * (1.0 / blk)) * blk
    padded_row = pad(crow_ref[0:1, :])
    lane = lax.broadcasted_iota(I32, (N_EXPERTS, 128), 1)
    sub = lax.broadcasted_iota(I32, (N_EXPERTS, 128), 0)
    start = jnp.sum(jnp.where(lane < sub, padded_row, 0.0), axis=1, keepdims=True)
    end = start + pad(ccol_ref[:, 0:1])

    row = lax.broadcasted_iota(I32, (N_EXPERTS, tr), 0)
    for k in range(TOP_K):
        base = jnp.sum(jnp.where(row == e_ref[k:k + 1, :], start, 0.0), axis=0, keepdims=True)
        pos_ref[k:k + 1, :] = rank_ref[k:k + 1, :] + base.astype(I32)
    pos_ref[TOP_K:, :] = jnp.zeros((8 - TOP_K, tr), I32)

    first_row = lax.broadcasted_iota(I32, (N_EXPERTS, n_blocks_pad), 1).astype(F32) * blk
    owner = jnp.sum((end <= first_row).astype(F32), axis=0, keepdims=True)
    be_ref[...] = jnp.broadcast_to(jnp.minimum(owner, float(N_EXPERTS - 1)).astype(I32), be_ref.shape)
    used = jnp.max(end, axis=0, keepdims=True) * (1.0 / blk)
    nb_ref[...] = jnp.broadcast_to(used.astype(I32), nb_ref.shape)
    start_ref[...] = jnp.broadcast_to(start.astype(I32), start_ref.shape)


def _pos(e, rank, cnt, n_blocks_pad):
    t_rows = e.shape[1]
    tr = ROUTE_TILE
    counts = cnt[:, 0]
    ccol = jnp.broadcast_to(counts[:, None], (N_EXPERTS, 128))
    crow = jnp.broadcast_to(jnp.pad(counts, (0, 128 - N_EXPERTS))[None, :], (8, 128))
    tok = lambda i: (0, i)
    const2 = lambda i: (0, 0)
    return pl.pallas_call(
        functools.partial(_pos_kernel, n_blocks_pad),
        grid=(t_rows // tr,),
        in_specs=[pl.BlockSpec((8, tr), tok), pl.BlockSpec((8, tr), tok),
                  pl.BlockSpec((N_EXPERTS, 128), const2), pl.BlockSpec((8, 128), const2)],
        out_specs=(pl.BlockSpec((8, tr), tok), pl.BlockSpec((8, n_blocks_pad), const2), pl.BlockSpec((8, 128), const2),
                   pl.BlockSpec((N_EXPERTS, 128), const2)),
        out_shape=(jax.ShapeDtypeStruct((8, t_rows), I32), jax.ShapeDtypeStruct((8, n_blocks_pad), I32),
                   jax.ShapeDtypeStruct((8, 128), I32), jax.ShapeDtypeStruct((N_EXPERTS, 128), I32)),
        compiler_params=pltpu.CompilerParams(dimension_semantics=("arbitrary",)),
        name="pos",
    )(e, rank, ccol, crow)


def _token_copy(src_ref, src_tok, dst_ref, dst_tok, sem):
    src = pl.ds(pl.multiple_of(src_tok * TOKEN_ROWS, TOKEN_ROWS), TOKEN_ROWS)
    dst = pl.ds(pl.multiple_of(dst_tok * TOKEN_ROWS, TOKEN_ROWS), TOKEN_ROWS)
    return pltpu.make_async_copy(src_ref.at[src, :], dst_ref.at[dst, :], sem)


def _store_token_tiles(dst_ref, x):
    n = x.shape[0]
    for s in range(TOKEN_ROWS):
        dst_ref[pl.ds(s, n, stride=TOKEN_ROWS), :] = x[:, s * LANES:(s + 1) * LANES]


def _load_token_tiles(src_ref, n):
    return jnp.concatenate([src_ref[pl.ds(s, n, stride=TOKEN_ROWS), :] for s in range(TOKEN_ROWS)], axis=-1)


def _scatter_kernel(n_blocks, start_ref, count_ref, nb_ref, pos_hbm, h_ref, xs_out,
                    pos_smem, tiles, zeros, psem, sems, zsem):
    i = pl.program_id(0)
    ts = SCATTER_TILE
    blk_rows = EXPERT_BLOCK * TOKEN_ROWS
    load = pltpu.make_async_copy(pos_hbm.at[i], pos_smem, psem)
    load.start()

    @pl.when(i == 0)
    def _():
        zeros[...] = jnp.zeros_like(zeros)
        zero_block = lambda b: pltpu.make_async_copy(
            zeros, xs_out.at[pl.ds(pl.multiple_of(b * blk_rows, blk_rows), blk_rows), :], zsem)

        def pad_rows(e, n_pad):
            lo = start_ref[e] + count_ref[e]
            hi = start_ref[e] + ((count_ref[e] + (EXPERT_BLOCK - 1)) & (-EXPERT_BLOCK))

            def zero_row(r, _):
                _token_copy(zeros, 0, xs_out, r, zsem).start()
                return 0

            lax.fori_loop(lo, hi, zero_row, 0)
            return n_pad + (hi - lo)

        n_pad = lax.fori_loop(0, N_EXPERTS, pad_rows, 0)

        def start_block(b, _):
            zero_block(b).start()
            return 0

        lax.fori_loop(nb_ref[0], n_blocks, start_block, 0)

        def wait_row(r, _):
            _token_copy(zeros, 0, xs_out, 0, zsem).wait()
            return 0

        lax.fori_loop(0, n_pad, wait_row, 0)

        def wait_block(b, _):
            zero_block(0).wait()
            return 0

        lax.fori_loop(nb_ref[0], n_blocks, wait_block, 0)

    def drain(half):
        for _ in range(TOP_K):
            pltpu.make_async_copy(tiles.at[half], xs_out.at[pl.ds(0, ts * TOKEN_ROWS), :], sems.at[half]).wait()

    load.wait()
    for half in range(2):
        @pl.when(i > 0)
        def _():
            drain(half)

        _store_token_tiles(tiles.at[half], h_ref[half * ts:(half + 1) * ts, :])

        def issue(t, _, half=half):
            for k in range(TOP_K):
                _token_copy(tiles.at[half], t, xs_out, pos_smem[k, half * ts + t], sems.at[half]).start()
            return 0

        lax.fori_loop(0, ts, issue, 0, unroll=2)

    @pl.when(i == pl.num_programs(0) - 1)
    def _():
        drain(0)
        drain(1)


def _tile_major(a, tile):
    return jnp.transpose(a.reshape(8, a.shape[1] // tile, tile), (1, 0, 2))


def _scatter(h1, pos, starts, counts, n_used, n_rows):
    t_rows = h1.shape[0]
    step = 2 * SCATTER_TILE
    n_blocks = n_rows // EXPERT_BLOCK
    return pl.pallas_call(
        functools.partial(_scatter_kernel, n_blocks),
        grid_spec=pltpu.PrefetchScalarGridSpec(
            num_scalar_prefetch=3,
            grid=(t_rows // step,),
            in_specs=[pl.BlockSpec(memory_space=pl.ANY),
                      pl.BlockSpec((step, D_MODEL), lambda i, *_: (i, 0))],
            out_specs=pl.BlockSpec(memory_space=pl.ANY),
            scratch_shapes=[pltpu.SMEM((8, step), I32),
                            pltpu.VMEM((2, SCATTER_TILE * TOKEN_ROWS, LANES), F32),
                            pltpu.VMEM((EXPERT_BLOCK * TOKEN_ROWS, LANES), F32),
                            pltpu.SemaphoreType.DMA(()), pltpu.SemaphoreType.DMA((2,)), pltpu.SemaphoreType.DMA(())],
        ),
        out_shape=jax.ShapeDtypeStruct((n_rows * TOKEN_ROWS, LANES), F32),
        compiler_params=pltpu.CompilerParams(dimension_semantics=("arbitrary",)),
        name="scatter",
    )(starts, counts, n_used, _tile_major(pos, step), h1)


def _experts_kernel(be_ref, nb_ref, x_ref, wg_ref, wu_ref, wd_ref, y_ref, wg_bf, wu_bf, wd_bf):
    b = pl.program_id(0)

    @pl.when((b == 0) | (be_ref[b] != be_ref[jnp.maximum(b - 1, 0)]))
    def _():
        wg_bf[...] = wg_ref[...].astype(BF16)
        wu_bf[...] = wu_ref[...].astype(BF16)
        wd_bf[...] = wd_ref[...].astype(BF16)

    @pl.when(b < nb_ref[0])
    def _():
        xb = _load_token_tiles(x_ref, EXPERT_BLOCK).astype(BF16)
        gate = _dot(xb, wg_bf[...])
        up = _dot(xb, wu_bf[...])
        hid = (gate * jax.nn.sigmoid(gate) * up).astype(BF16)
        _store_token_tiles(y_ref, _dot(hid, wd_bf[...]))

    @pl.when(b >= nb_ref[0])
    def _():
        y_ref[...] = jnp.zeros_like(y_ref)


def _experts(xs, block_e, n_used, w_gate, w_up, w_down):
    blk_rows = EXPERT_BLOCK * TOKEN_ROWS
    return pl.pallas_call(
        _experts_kernel,
        grid_spec=pltpu.PrefetchScalarGridSpec(
            num_scalar_prefetch=2,
            grid=(xs.shape[0] // blk_rows,),
            in_specs=[pl.BlockSpec((blk_rows, LANES), lambda b, be, nb: (b, 0)),
                      pl.BlockSpec((None, D_MODEL, D_EXPERT), lambda b, be, nb: (be[b], 0, 0)),
                      pl.BlockSpec((None, D_MODEL, D_EXPERT), lambda b, be, nb: (be[b], 0, 0)),
                      pl.BlockSpec((None, D_EXPERT, D_MODEL), lambda b, be, nb: (be[b], 0, 0))],
            out_specs=pl.BlockSpec((blk_rows, LANES), lambda b, be, nb: (b, 0)),
            scratch_shapes=[pltpu.VMEM((D_MODEL, D_EXPERT), BF16), pltpu.VMEM((D_MODEL, D_EXPERT), BF16),
                            pltpu.VMEM((D_EXPERT, D_MODEL), BF16)],
        ),
        out_shape=jax.ShapeDtypeStruct(xs.shape, F32),
        compiler_params=pltpu.CompilerParams(dimension_semantics=("arbitrary",)),
        name="experts",
    )(block_e, n_used, xs, w_gate, w_up, w_down)


def _combine_kernel(n_prompt_tiles, pos_hbm, h_ref, w_ref, ys_hbm, wsg_ref, wsu_ref, wsd_ref, g2_ref, b2_ref,
                    yp_hbm, ysm_hbm, pos_smem, ybuf, obuf, psem, gsems, osems):
    i = pl.program_id(0)
    tc = COMBINE_TILE
    tile_t = tc // BATCH
    load = pltpu.make_async_copy(pos_hbm.at[i], pos_smem, psem)
    load.start()
    load.wait()

    def gather(tile, slot):
        def issue(t, _):
            for k in range(TOP_K):
                _token_copy(ys_hbm, pos_smem[k, tile * tc + t], ybuf.at[slot, k], t, gsems.at[slot]).start()
            return 0

        lax.fori_loop(0, tc, issue, 0, unroll=2)

    def gather_wait(slot):
        for k in range(TOP_K):
            pltpu.make_async_copy(ys_hbm.at[pl.ds(0, tc * TOKEN_ROWS), :], ybuf.at[slot, k], gsems.at[slot]).wait()

    def out_copies(dst_hbm, t0, slot):
        return [pltpu.make_async_copy(obuf.at[slot, :, b, :], dst_hbm.at[b, pl.ds(t0, tile_t), :], osems.at[slot])
                for b in range(BATCH)]

    def out_wait(slot):
        for cp in out_copies(yp_hbm, 0, slot):
            cp.wait()

    def compute(tile, slot):
        r0 = pl.multiple_of(tile * tc, tc)
        h = h_ref[pl.ds(r0, tc), :]
        hb = h.astype(BF16)
        gate = _dot(hb, wsg_ref[...])
        up = _dot(hb, wsu_ref[...])
        shared = _dot((gate * jax.nn.sigmoid(gate) * up).astype(BF16), wsd_ref[...])

        eye = (lax.broadcasted_iota(I32, (tc, tc), 0) == lax.broadcasted_iota(I32, (tc, tc), 1)).astype(F32)
        w_cols = lax.dot_general(eye, w_ref[:, pl.ds(r0, tc)], (((1,), (1,)), ((), ())),
                                 precision=lax.Precision.HIGHEST, preferred_element_type=F32)
        gather_wait(slot)
        pieces = []
        for s in range(TOKEN_ROWS):
            acc = shared[:, s * LANES:(s + 1) * LANES]
            for k in range(TOP_K):
                acc = acc + ybuf[slot, k, pl.ds(s, tc, stride=TOKEN_ROWS), :] * w_cols[:, k:k + 1]
            pieces.append(acc)
        out = _layer_norm(DEEPNORM_ALPHA * h + jnp.concatenate(pieces, axis=-1), g2_ref[...], b2_ref[...])
        obuf[slot] = out.reshape(tile_t, BATCH, D_MODEL)

        g = i * COMBINE_TILES_PER_STEP + tile

        @pl.when(g < n_prompt_tiles)
        def _():
            for cp in out_copies(yp_hbm, pl.multiple_of(g * tile_t, tile_t), slot):
                cp.start()

        @pl.when(g >= n_prompt_tiles)
        def _():
            for cp in out_copies(ysm_hbm, pl.multiple_of((g - n_prompt_tiles) * tile_t, tile_t), slot):
                cp.start()

    n_pairs = COMBINE_TILES_PER_STEP // 2
    gather(0, 0)

    def pair(q, _):
        gather(2 * q + 1, 1)

        @pl.when(q > 0)
        def _():
            out_wait(0)

        compute(2 * q, 0)

        @pl.when(q + 1 < n_pairs)
        def _():
            gather(2 * q + 2, 0)

        @pl.when(q > 0)
        def _():
            out_wait(1)

        compute(2 * q + 1, 1)
        return 0

    lax.fori_loop(0, n_pairs, pair, 0)
    out_wait(0)
    out_wait(1)


def _combine(h1, pos, gate_w, ys, ws_gate, ws_up, ws_down, ln2_g, ln2_b, prompt_shape, sample_shape):
    t_rows = h1.shape[0]
    tc = COMBINE_TILE
    step = tc * COMBINE_TILES_PER_STEP
    n_prompt_tiles = prompt_shape[0] * prompt_shape[1] // tc
    const2 = lambda i: (0, 0)
    any_spec = pl.BlockSpec(memory_space=pl.ANY)
    return pl.pallas_call(
        functools.partial(_combine_kernel, n_prompt_tiles),
        grid=(t_rows // step,),
        in_specs=[any_spec,
                  pl.BlockSpec((step, D_MODEL), lambda i: (i, 0)),
                  pl.BlockSpec((8, step), lambda i: (0, i)),
                  any_spec,
                  pl.BlockSpec((D_MODEL, D_EXPERT), const2),
                  pl.BlockSpec((D_MODEL, D_EXPERT), const2),
                  pl.BlockSpec((D_EXPERT, D_MODEL), const2),
                  pl.BlockSpec((1, D_MODEL), const2),
                  pl.BlockSpec((1, D_MODEL), const2)],
        out_specs=(any_spec, any_spec),
        out_shape=(jax.ShapeDtypeStruct(prompt_shape, F32), jax.ShapeDtypeStruct(sample_shape, F32)),
        scratch_shapes=[pltpu.SMEM((8, step), I32),
                        pltpu.VMEM((2, TOP_K, tc * TOKEN_ROWS, LANES), F32),
                        pltpu.VMEM((2, tc // BATCH, BATCH, D_MODEL), F32),
                        pltpu.SemaphoreType.DMA(()), pltpu.SemaphoreType.DMA((2,)), pltpu.SemaphoreType.DMA((2,))],
        compiler_params=pltpu.CompilerParams(dimension_semantics=("arbitrary",), vmem_limit_bytes=V7X_VMEM_LIMIT),
        name="combine",
    )(_tile_major(pos, step), h1, gate_w, ys, ws_gate.astype(BF16), ws_up.astype(BF16), ws_down.astype(BF16),
      ln2_g.reshape(1, D_MODEL), ln2_b.reshape(1, D_MODEL))


def _time_major(x):
    b, l, d = x.shape
    return jnp.transpose(x, (1, 0, 2)).reshape(l * b, d)


def _batch_major(x, b):
    return jnp.transpose(x.reshape(x.shape[0] // b, b, x.shape[1]), (1, 0, 2))


def _block_diag(blocks):
    n, r, c = blocks.shape
    eye = jnp.eye(n, dtype=blocks.dtype)
    return (blocks[:, :, None, :] * eye[:, None, :, None]).reshape(n * r, n * c)


def kernel(x_prompt, x_sample, state_ssm_re, state_ssm_im, cache_conv, ln_in_g, ln_in_b, w_in, lam_re, lam_im, log_dt, ssm_b_re, ssm_b_im, ssm_c_re, ssm_c_im, ssm_d, w_glu, b_glu, conv_w, beta_ssm, beta_conv, w_out, ln1_g, ln1_b, w_router, router_bias, w_gate, w_up, w_down, ws_gate, ws_up, ws_down, ln2_g, ln2_b):
    bp, lp, _ = x_prompt.shape
    bs, ls, _ = x_sample.shape
    assert bp == BATCH and bs == BATCH and ls == CHUNK_T and lp % CHUNK_T == 0
    assert w_in.shape[0] == 1, "single-layer model"
    n_prompt = bp * lp
    row = lambda a: a.reshape(1, -1)

    a_re, a_im, bb_re, bb_im = _prep(lam_re[0], lam_im[0], log_dt[0], ssm_b_re[0], ssm_b_im[0])
    groups_in = 128 // SSM_GROUP
    groups_out = 256 // SSM_GROUP
    bbd = lambda bb: jnp.stack([_block_diag(bb[j * groups_in:(j + 1) * groups_in])
                                for j in range(N_GROUPS // groups_in)]).astype(BF16)
    ct = lambda cc: jnp.transpose(cc, (0, 2, 1))
    cbd = lambda cc: jnp.stack([_block_diag(ct(cc)[j * groups_out:(j + 1) * groups_out])
                                for j in range(N_GROUPS // groups_out)]).astype(BF16)
    mix_weights = (row(ln_in_g), row(ln_in_b), w_in[0].astype(BF16), row(a_re), row(a_im),
                   bbd(bb_re), bbd(bb_im), cbd(ssm_c_re[0]), cbd(-ssm_c_im[0]), row(ssm_d[0]),
                   w_glu[0].astype(BF16), row(b_glu[0]), conv_w[0], row(beta_ssm[0]), row(beta_conv[0]),
                   w_out[0].astype(BF16), row(ln1_g[0]), row(ln1_b[0]))

    h0r = state_ssm_re[0].reshape(BATCH, D_STATE)
    h0i = state_ssm_im[0].reshape(BATCH, D_STATE)
    cbuf = _time_major(cache_conv[0])
    h1, p_re, p_im, p_conv, s_re, s_im, s_conv = _mix(x_prompt, x_sample, h0r, h0i, cbuf, mix_weights)

    t_rows = h1.shape[0]
    assert t_rows % (COMBINE_TILE * COMBINE_TILES_PER_STEP) == 0 and n_prompt % COMBINE_TILE == 0
    n_rows = t_rows * TOP_K + N_EXPERTS * EXPERT_BLOCK
    n_blocks = n_rows // EXPERT_BLOCK
    n_blocks_pad = -(-n_blocks // 128) * 128
    e, gate_w, rank, cnt = _route(h1, w_router[0], router_bias[0])
    pos, block_e, n_used, starts = _pos(e, rank, cnt, n_blocks_pad)
    n_used = n_used[0, :1]
    xs = _scatter(h1, pos, starts[:, 0], cnt[:, 0].astype(I32), n_used, n_rows)
    ys = _experts(xs, block_e[0, :n_blocks], n_used, w_gate[0], w_up[0], w_down[0])
    y_prompt, y_sample = _combine(h1, pos, gate_w, ys, ws_gate[0], ws_up[0], ws_down[0], ln2_g[0], ln2_b[0],
                                  x_prompt.shape, x_sample.shape)

    st = lambda s: s.reshape(1, BATCH, N_GROUPS, SSM_STATE)
    cv = lambda t: _batch_major(t, BATCH)[None]
    return (y_prompt, y_sample, st(p_re), st(p_im), cv(p_conv), st(s_re), st(s_im), cv(s_conv))
```

```python
import functools
import math

import jax
import jax.numpy as jnp
from jax import lax
from jax.experimental import pallas as pl
from jax.experimental.pallas import tpu as pltpu

F32 = jnp.float32
BF16 = jnp.bfloat16
I32 = jnp.int32
U32 = jnp.uint32

D_MODEL = 1024
D_SSM = 512
D_CONV = 512
SSM_GROUP = 16
N_GROUPS = 32
SSM_STATE = 64
D_STATE = N_GROUPS * SSM_STATE
N_EXPERTS = 64
TOP_K = 6
N_EXPERT_GROUPS = 8
GROUP_SIZE = N_EXPERTS // N_EXPERT_GROUPS
TOPK_GROUPS = 4
D_EXPERT = 256
ROUTED_SCALE = 2.5
DEPTH = 1
DEEPNORM_ALPHA = (2.0 * DEPTH) ** 0.25
LN_EPS = 1e-5
RMS_EPS = 1e-6

BATCH = 8
CHUNK_T = 64
CHUNK_ROWS = CHUNK_T * BATCH
SCAN_COLS = 512
ROUTE_TILE = 512
SCATTER_TILE = 256
COMBINE_TILE = 128
COMBINE_TILES_PER_STEP = 10
EXPERT_BLOCK = 512
LANES = 128
TOKEN_ROWS = D_MODEL // (2 * LANES)
V7X_VMEM_LIMIT = 56 * 1024 * 1024
NEG_INF = float("-inf")


def _layer_norm(x, g, b):
    mu = jnp.mean(x, axis=-1, keepdims=True)
    xc = x - mu
    var = jnp.mean(xc * xc, axis=-1, keepdims=True)
    return xc * lax.rsqrt(var + LN_EPS) * g + b


def _rms_norm(x, g):
    return x * lax.rsqrt(jnp.mean(x * x, axis=-1, keepdims=True) + RMS_EPS) * g


def _dot(a, b):
    return jnp.dot(a, b, preferred_element_type=F32)


def _discretise(lr, li, log_dt):
    dt = jnp.exp(log_dt)
    mag = jnp.exp(lr * dt)
    ar = mag * jnp.cos(li * dt)
    ai = mag * jnp.sin(li * dt)
    den = lr * lr + li * li
    qr = ((ar - 1.0) * lr + ai * li) / den
    qi = (ai * lr - (ar - 1.0) * li) / den
    return ar, ai, qr, qi


def _prep_kernel(lr_ref, li_ref, ldt_ref, lrc_ref, lic_ref, ldtc_ref, br_ref, bi_ref,
                 ar_ref, ai_ref, bbr_ref, bbi_ref):
    ar, ai, _, _ = _discretise(lr_ref[...], li_ref[...], ldt_ref[...])
    ar_ref[...] = ar
    ai_ref[...] = ai
    _, _, qr, qi = _discretise(lrc_ref[...], lic_ref[...], ldtc_ref[...])
    br = br_ref[...]
    bi = bi_ref[...]
    bbr_ref[...] = qr * br - qi * bi
    bbi_ref[...] = qr * bi + qi * br


def _prep(lam_re, lam_im, log_dt, b_re, b_im):
    g, p = lam_re.shape
    per_channel = lambda a: jnp.repeat(a, SSM_GROUP, axis=0)
    rows = lambda b: jnp.transpose(b, (0, 2, 1)).reshape(g * SSM_GROUP, p)
    ldt = log_dt.reshape(g, 1)
    ar, ai, bbr, bbi = pl.pallas_call(
        _prep_kernel,
        out_shape=(jax.ShapeDtypeStruct((g, p), F32), jax.ShapeDtypeStruct((g, p), F32),
                   jax.ShapeDtypeStruct((g * SSM_GROUP, p), F32), jax.ShapeDtypeStruct((g * SSM_GROUP, p), F32)),
        name="prep",
    )(lam_re, lam_im, ldt, per_channel(lam_re), per_channel(lam_im), per_channel(ldt), rows(b_re), rows(b_im))
    return ar, ai, bbr.reshape(g, SSM_GROUP, p), bbi.reshape(g, SSM_GROUP, p)


def _chunk_copies(src_hbm, t0, xbuf, slot, sems):
    return [pltpu.make_async_copy(src_hbm.at[b, pl.ds(t0, CHUNK_T), :], xbuf.at[slot, :, b, :], sems.at[slot])
            for b in range(BATCH)]


def _mix_kernel(n_prompt_chunks,
                xp_hbm, xs_hbm, h0r_ref, h0i_ref, cbuf_ref, lng_ref, lnb_ref, win_ref, ar_ref, ai_ref,
                bbr_ref, bbi_ref, cbr_ref, cbi_ref, dsk_ref, wglu_ref, bglu_ref, cw_ref,
                bs_ref, bc_ref, wout_ref, g1_ref, b1_ref,
                h1_ref, pr_ref, pi_ref, pc_ref, sr_ref, si_ref, sc_ref,
                str_ref, sti_ref, hr_ref, hi_ref, cv_ref, xbuf, xsems):
    c = pl.program_id(0)
    n = CHUNK_ROWS
    carry_rows = 2 * BATCH
    slot = c % 2

    @pl.when(c == 0)
    def _():
        for cp in _chunk_copies(xp_hbm, 0, xbuf, 0, xsems):
            cp.start()

    @pl.when(c + 1 < n_prompt_chunks)
    def _():
        for cp in _chunk_copies(xp_hbm, pl.multiple_of((c + 1) * CHUNK_T, CHUNK_T), xbuf, 1 - slot, xsems):
            cp.start()

    @pl.when(c + 1 == n_prompt_chunks)
    def _():
        for cp in _chunk_copies(xs_hbm, 0, xbuf, 1 - slot, xsems):
            cp.start()

    @pl.when(c == 0)
    def _():
        hr_ref[...] = jnp.zeros_like(hr_ref)
        hi_ref[...] = jnp.zeros_like(hi_ref)
        cv_ref[0:carry_rows, :] = jnp.zeros((carry_rows, D_CONV), F32)

    @pl.when(c == n_prompt_chunks)
    def _():
        hr_ref[...] = h0r_ref[...]
        hi_ref[...] = h0i_ref[...]
        cv_ref[0:carry_rows, :] = cbuf_ref[...]

    for cp in _chunk_copies(xp_hbm, 0, xbuf, slot, xsems):
        cp.wait()
    h = _layer_norm(xbuf[slot].reshape(n, D_MODEL), lng_ref[...], lnb_ref[...])
    hb = h.astype(BF16)

    u = _dot(hb, win_ref[:, 0:D_SSM])
    ub = u.astype(BF16)
    u_tile = 128
    s_tile = u_tile // SSM_GROUP * SSM_STATE
    for j in range(D_SSM // u_tile):
        uj = ub[:, j * u_tile:(j + 1) * u_tile]
        str_ref[:, j * s_tile:(j + 1) * s_tile] = _dot(uj, bbr_ref[j])
        sti_ref[:, j * s_tile:(j + 1) * s_tile] = _dot(uj, bbi_ref[j])

    for cb in range(D_STATE // SCAN_COLS):
        cols = slice(cb * SCAN_COLS, (cb + 1) * SCAN_COLS)
        ar = jnp.broadcast_to(ar_ref[:, cols], (BATCH, SCAN_COLS))
        ai = jnp.broadcast_to(ai_ref[:, cols], (BATCH, SCAN_COLS))

        def step(t, carry, cols=cols, ar=ar, ai=ai):
            sr, si = carry
            r0 = pl.multiple_of(t * BATCH, BATCH)
            nr = ar * sr - ai * si + str_ref[pl.ds(r0, BATCH), cols]
            ni = ar * si + ai * sr + sti_ref[pl.ds(r0, BATCH), cols]
            str_ref[pl.ds(r0, BATCH), cols] = nr
            sti_ref[pl.ds(r0, BATCH), cols] = ni
            return nr, ni

        sr, si = lax.fori_loop(0, CHUNK_T, step, (hr_ref[:, cols], hi_ref[:, cols]), unroll=8)
        hr_ref[:, cols] = sr
        hi_ref[:, cols] = si

    k_tile = 1024
    ys = []
    for j in range(D_STATE // k_tile):
        sl = slice(j * k_tile, (j + 1) * k_tile)
        ys.append(_dot(str_ref[:, sl].astype(BF16), cbr_ref[j]) + _dot(sti_ref[:, sl].astype(BF16), cbi_ref[j]))
    y_ssm = jnp.concatenate(ys, axis=-1) + dsk_ref[...] * u
    g = jax.nn.gelu(y_ssm)
    y_ssm = g * jax.nn.sigmoid(_dot(g.astype(BF16), wglu_ref[...]) + bglu_ref[...])
    mix = _dot(_rms_norm(y_ssm, bs_ref[...]).astype(BF16), wout_ref[0:D_SSM, :])

    gate_b = _dot(hb, win_ref[:, D_SSM:D_SSM + D_CONV])
    gate_c = _dot(hb, win_ref[:, D_SSM + D_CONV:D_SSM + 2 * D_CONV])
    v = _dot(hb, win_ref[:, D_SSM + 2 * D_CONV:D_SSM + 3 * D_CONV])
    cv_ref[carry_rows:carry_rows + n, :] = gate_c * v
    y_conv = gate_b * (cv_ref[0:n, :] * cw_ref[0:1, :]
                       + cv_ref[BATCH:BATCH + n, :] * cw_ref[1:2, :]
                       + cv_ref[carry_rows:carry_rows + n, :] * cw_ref[2:3, :])
    tail = cv_ref[n:n + carry_rows, :]
    cv_ref[0:carry_rows, :] = tail
    mix = mix + _dot(_rms_norm(y_conv, bc_ref[...]).astype(BF16), wout_ref[D_SSM:D_SSM + D_CONV, :])

    h1_ref[...] = _layer_norm(DEEPNORM_ALPHA * h + mix, g1_ref[...], b1_ref[...])

    @pl.when(c == n_prompt_chunks - 1)
    def _():
        pr_ref[...] = hr_ref[...]
        pi_ref[...] = hi_ref[...]
        pc_ref[...] = tail

    @pl.when(c == n_prompt_chunks)
    def _():
        sr_ref[...] = hr_ref[...]
        si_ref[...] = hi_ref[...]
        sc_ref[...] = tail


def _mix(x_prompt, x_sample, h0r, h0i, cbuf, weights):
    n_prompt_chunks = x_prompt.shape[1] // CHUNK_T
    n_chunks = n_prompt_chunks + 1
    t_rows = n_chunks * CHUNK_ROWS
    any_spec = pl.BlockSpec(memory_space=pl.ANY)
    const2 = lambda c: (0, 0)
    const3 = lambda c: (0, 0, 0)
    w_specs = [pl.BlockSpec(w.shape, const3 if w.ndim == 3 else const2) for w in weights]
    state = jax.ShapeDtypeStruct((BATCH, D_STATE), F32)
    tail = jax.ShapeDtypeStruct((2 * BATCH, D_CONV), F32)
    state_spec = pl.BlockSpec((BATCH, D_STATE), const2)
    tail_spec = pl.BlockSpec((2 * BATCH, D_CONV), const2)
    return pl.pallas_call(
        functools.partial(_mix_kernel, n_prompt_chunks),
        grid=(n_chunks,),
        in_specs=[any_spec, any_spec, state_spec, state_spec, tail_spec] + w_specs,
        out_specs=(pl.BlockSpec((CHUNK_ROWS, D_MODEL), lambda c: (c, 0)),
                   state_spec, state_spec, tail_spec, state_spec, state_spec, tail_spec),
        out_shape=(jax.ShapeDtypeStruct((t_rows, D_MODEL), F32), state, state, tail, state, state, tail),
        scratch_shapes=[pltpu.VMEM((CHUNK_ROWS, D_STATE), F32), pltpu.VMEM((CHUNK_ROWS, D_STATE), F32),
                        pltpu.VMEM((BATCH, D_STATE), F32), pltpu.VMEM((BATCH, D_STATE), F32),
                        pltpu.VMEM((CHUNK_ROWS + 2 * BATCH, D_CONV), F32),
                        pltpu.VMEM((2, CHUNK_T, BATCH, D_MODEL), F32), pltpu.SemaphoreType.DMA((2,))],
        compiler_params=pltpu.CompilerParams(dimension_semantics=("arbitrary",), vmem_limit_bytes=V7X_VMEM_LIMIT),
        name="mix",
    )(x_prompt, x_sample, h0r, h0i, cbuf, *weights)


def _route_kernel(h_ref, wrt_ref, bias_ref, tri_ref, e_ref, w_ref, rank_ref, cnt_ref, run_ref):
    i = pl.program_id(0)
    tr = ROUTE_TILE

    @pl.when(i == 0)
    def _():
        run_ref[...] = jnp.zeros_like(run_ref)

    def split(a):
        hi = a.astype(BF16)
        return hi, (a - hi.astype(F32)).astype(BF16)

    nt_dot = lambda a, b: lax.dot_general(a, b, (((1,), (1,)), ((), ())), preferred_element_type=F32)
    w_hi, w_lo = split(wrt_ref[...])
    h_hi, h_lo = split(h_ref[...])
    logits = nt_dot(w_hi, h_hi) + (nt_dot(w_hi, h_lo) + nt_dot(w_lo, h_hi))
    scores = jax.nn.sigmoid(logits)
    sel = scores + bias_ref[...]

    sub = lax.broadcasted_iota(I32, (GROUP_SIZE, tr), 0).astype(F32)
    blocks, gscore = [], []
    for g in range(N_EXPERT_GROUPS):
        blk = sel[g * GROUP_SIZE:(g + 1) * GROUP_SIZE, :]
        m1 = jnp.max(blk, axis=0, keepdims=True)
        first = jnp.min(jnp.where(blk == m1, sub, float(GROUP_SIZE)), axis=0, keepdims=True)
        m2 = jnp.max(jnp.where(sub == first, NEG_INF, blk), axis=0, keepdims=True)
        blocks.append(blk)
        gscore.append(m1 + m2)
    masked = []
    for g in range(N_EXPERT_GROUPS):
        beaten = jnp.zeros((1, tr), F32)
        for o in range(N_EXPERT_GROUPS):
            if o == g:
                continue
            wins = gscore[o] >= gscore[g] if o < g else gscore[o] > gscore[g]
            beaten = beaten + wins.astype(F32)
        masked.append(jnp.where(beaten < float(TOPK_GROUPS), blocks[g], NEG_INF))
    masked = jnp.concatenate(masked, axis=0)

    row = lax.broadcasted_iota(I32, (N_EXPERTS, tr), 0).astype(F32)
    picked = jnp.zeros((N_EXPERTS, tr), F32)
    hots, idxs, gates = [], [], []
    for _ in range(TOP_K):
        m = jnp.max(masked, axis=0, keepdims=True)
        idx = jnp.min(jnp.where(masked == m, row, float(N_EXPERTS)), axis=0, keepdims=True)
        hot = row == idx
        hots.append(hot)
        idxs.append(idx)
        gates.append(jnp.sum(jnp.where(hot, scores, 0.0), axis=0, keepdims=True))
        masked = jnp.where(hot, NEG_INF, masked)
        picked = picked + hot.astype(F32)
    total = gates[0]
    for k in range(1, TOP_K):
        total = total + gates[k]

    before = _dot(picked.astype(BF16), tri_ref[...]) + run_ref[...]
    for k in range(TOP_K):
        e_ref[k:k + 1, :] = idxs[k].astype(I32)
        w_ref[k:k + 1, :] = gates[k] / total * ROUTED_SCALE
        rank_ref[k:k + 1, :] = jnp.sum(jnp.where(hots[k], before, 0.0), axis=0, keepdims=True).astype(I32)
    e_ref[TOP_K:, :] = jnp.zeros((8 - TOP_K, tr), I32)
    w_ref[TOP_K:, :] = jnp.zeros((8 - TOP_K, tr), F32)
    rank_ref[TOP_K:, :] = jnp.zeros((8 - TOP_K, tr), I32)
    run_ref[...] = run_ref[...] + jnp.sum(picked, axis=1, keepdims=True)
    cnt_ref[...] = jnp.broadcast_to(run_ref[...], cnt_ref.shape)


def _route(h1, w_router, router_bias):
    t_rows = h1.shape[0]
    tr = ROUTE_TILE
    tri = jnp.triu(jnp.ones((tr, tr), BF16), k=1)
    tok = lambda i: (0, i)
    const2 = lambda i: (0, 0)
    return pl.pallas_call(
        _route_kernel,
        grid=(t_rows // tr,),
        in_specs=[pl.BlockSpec((tr, D_MODEL), lambda i: (i, 0)),
                  pl.BlockSpec((N_EXPERTS, D_MODEL), const2),
                  pl.BlockSpec((N_EXPERTS, 1), const2),
                  pl.BlockSpec((tr, tr), const2)],
        out_specs=(pl.BlockSpec((8, tr), tok), pl.BlockSpec((8, tr), tok), pl.BlockSpec((8, tr), tok),
                   pl.BlockSpec((N_EXPERTS, 128), const2)),
        out_shape=(jax.ShapeDtypeStruct((8, t_rows), I32), jax.ShapeDtypeStruct((8, t_rows), F32),
                   jax.ShapeDtypeStruct((8, t_rows), I32), jax.ShapeDtypeStruct((N_EXPERTS, 128), F32)),
        scratch_shapes=[pltpu.VMEM((N_EXPERTS, 1), F32)],
        compiler_params=pltpu.CompilerParams(dimension_semantics=("arbitrary",)),
        name="route",
    )(h1, w_router.T, router_bias.reshape(N_EXPERTS, 1), tri)


def _pos_kernel(n_blocks_pad, e_ref, rank_ref, ccol_ref, crow_ref, pos_ref, be_ref, nb_ref, start_ref):
    tr = ROUTE_TILE
    blk = float(EXPERT_BLOCK)
    pad = lambda cnt: jnp.floor((cnt + (blk - 1.0)) * (1.0 / blk)) * blk
    padded_row = pad(crow_ref[0:1, :])
    lane = lax.broadcasted_iota(I32, (N_EXPERTS, 128), 1)
    sub = lax.broadcasted_iota(I32, (N_EXPERTS, 128), 0)
    start = jnp.sum(jnp.where(lane < sub, padded_row, 0.0), axis=1, keepdims=True)
    end = start + pad(ccol_ref[:, 0:1])

    row = lax.broadcasted_iota(I32, (N_EXPERTS, tr), 0)
    for k in range(TOP_K):
        base = jnp.sum(jnp.where(row == e_ref[k:k + 1, :], start, 0.0), axis=0, keepdims=True)
        pos_ref[k:k + 1, :] = rank_ref[k:k + 1, :] + base.astype(I32)
    pos_ref[TOP_K:, :] = jnp.zeros((8 - TOP_K, tr), I32)

    first_row = lax.broadcasted_iota(I32, (N_EXPERTS, n_blocks_pad), 1).astype(F32) * blk
    owner = jnp.sum((end <= first_row).astype(F32), axis=0, keepdims=True)
    be_ref[...] = jnp.broadcast_to(jnp.minimum(owner, float(N_EXPERTS - 1)).astype(I32), be_ref.shape)
    used = jnp.max(end, axis=0, keepdims=True) * (1.0 / blk)
    nb_ref[...] = jnp.broadcast_to(used.astype(I32), nb_ref.shape)
    start_ref[...] = jnp.broadcast_to(start.astype(I32), start_ref.shape)


def _pos(e, rank, cnt, n_blocks_pad):
    t_rows = e.shape[1]
    tr = ROUTE_TILE
    counts = cnt[:, 0]
    ccol = jnp.broadcast_to(counts[:, None], (N_EXPERTS, 128))
    crow = jnp.broadcast_to(jnp.pad(counts, (0, 128 - N_EXPERTS))[None, :], (8, 128))
    tok = lambda i: (0, i)
    const2 = lambda i: (0, 0)
    return pl.pallas_call(
        functools.partial(_pos_kernel, n_blocks_pad),
        grid=(t_rows // tr,),
        in_specs=[pl.BlockSpec((8, tr), tok), pl.BlockSpec((8, tr), tok),
                  pl.BlockSpec((N_EXPERTS, 128), const2), pl.BlockSpec((8, 128), const2)],
        out_specs=(pl.BlockSpec((8, tr), tok), pl.BlockSpec((8, n_blocks_pad), const2), pl.BlockSpec((8, 128), const2),
                   pl.BlockSpec((N_EXPERTS, 128), const2)),
        out_shape=(jax.ShapeDtypeStruct((8, t_rows), I32), jax.ShapeDtypeStruct((8, n_blocks_pad), I32),
                   jax.ShapeDtypeStruct((8, 128), I32), jax.ShapeDtypeStruct((N_EXPERTS, 128), I32)),
        compiler_params=pltpu.CompilerParams(dimension_semantics=("arbitrary",)),
        name="pos",
    )(e, rank, ccol, crow)


def _token_copy(src_ref, src_tok, dst_ref, dst_tok, sem):
    src = pl.ds(pl.multiple_of(src_tok * TOKEN_ROWS, TOKEN_ROWS), TOKEN_ROWS)
    dst = pl.ds(pl.multiple_of(dst_tok * TOKEN_ROWS, TOKEN_ROWS), TOKEN_ROWS)
    return pltpu.make_async_copy(src_ref.at[src, :], dst_ref.at[dst, :], sem)


def _store_token_tiles(dst_ref, x):
    n = x.shape[0]
    half = TOKEN_ROWS * LANES
    bits = lambda v: lax.bitcast_convert_type(v.astype(BF16).astype(F32), U32)
    for s in range(TOKEN_ROWS):
        hi = bits(x[:, s * LANES:(s + 1) * LANES])
        lo = bits(x[:, half + s * LANES:half + (s + 1) * LANES])
        dst_ref[pl.ds(s, n, stride=TOKEN_ROWS), :] = hi | (lo >> 16)


def _load_token_tiles(src_ref, n):
    his, los = [], []
    for s in range(TOKEN_ROWS):
        words = src_ref[pl.ds(s, n, stride=TOKEN_ROWS), :]
        his.append(lax.bitcast_convert_type(words & jnp.uint32(0xFFFF0000), F32))
        los.append(lax.bitcast_convert_type(words << 16, F32))
    return his, los


def _scatter_kernel(n_blocks, start_ref, count_ref, nb_ref, pos_hbm, h_ref, xs_out,
                    pos_smem, tiles, zeros, psem, sems, zsem):
    i = pl.program_id(0)
    ts = SCATTER_TILE
    blk_rows = EXPERT_BLOCK * TOKEN_ROWS
    load = pltpu.make_async_copy(pos_hbm.at[i], pos_smem, psem)
    load.start()

    @pl.when(i == 0)
    def _():
        zeros[...] = jnp.zeros_like(zeros)
        zero_block = lambda b: pltpu.make_async_copy(
            zeros, xs_out.at[pl.ds(pl.multiple_of(b * blk_rows, blk_rows), blk_rows), :], zsem)

        def pad_rows(e, n_pad):
            lo = start_ref[e] + count_ref[e]
            hi = start_ref[e] + ((count_ref[e] + (EXPERT_BLOCK - 1)) & (-EXPERT_BLOCK))

            def zero_row(r, _):
                _token_copy(zeros, 0, xs_out, r, zsem).start()
                return 0

            lax.fori_loop(lo, hi, zero_row, 0)
            return n_pad + (hi - lo)

        n_pad = lax.fori_loop(0, N_EXPERTS, pad_rows, 0)

        def start_block(b, _):
            zero_block(b).start()
            return 0

        lax.fori_loop(nb_ref[0], n_blocks, start_block, 0)

        def wait_row(r, _):
            _token_copy(zeros, 0, xs_out, 0, zsem).wait()
            return 0

        lax.fori_loop(0, n_pad, wait_row, 0)

        def wait_block(b, _):
            zero_block(0).wait()
            return 0

        lax.fori_loop(nb_ref[0], n_blocks, wait_block, 0)

    def drain(half):
        for _ in range(TOP_K):
            pltpu.make_async_copy(tiles.at[half], xs_out.at[pl.ds(0, ts * TOKEN_ROWS), :], sems.at[half]).wait()

    load.wait()
    for half in range(2):
        @pl.when(i > 0)
        def _():
            drain(half)

        _store_token_tiles(tiles.at[half], h_ref[half * ts:(half + 1) * ts, :])

        def issue(t, _, half=half):
            for k in range(TOP_K):
                _token_copy(tiles.at[half], t, xs_out, pos_smem[k, half * ts + t], sems.at[half]).start()
            return 0

        lax.fori_loop(0, ts, issue, 0, unroll=2)

    @pl.when(i == pl.num_programs(0) - 1)
    def _():
        drain(0)
        drain(1)


def _tile_major(a, tile):
    return jnp.transpose(a.reshape(8, a.shape[1] // tile, tile), (1, 0, 2))


def _scatter(h1, pos, starts, counts, n_used, n_rows):
    t_rows = h1.shape[0]
    step = 2 * SCATTER_TILE
    n_blocks = n_rows // EXPERT_BLOCK
    return pl.pallas_call(
        functools.partial(_scatter_kernel, n_blocks),
        grid_spec=pltpu.PrefetchScalarGridSpec(
            num_scalar_prefetch=3,
            grid=(t_rows // step,),
            in_specs=[pl.BlockSpec(memory_space=pl.ANY),
                      pl.BlockSpec((step, D_MODEL), lambda i, *_: (i, 0))],
            out_specs=pl.BlockSpec(memory_space=pl.ANY),
            scratch_shapes=[pltpu.SMEM((8, step), I32),
                            pltpu.VMEM((2, SCATTER_TILE * TOKEN_ROWS, LANES), U32),
                            pltpu.VMEM((EXPERT_BLOCK * TOKEN_ROWS, LANES), U32),
                            pltpu.SemaphoreType.DMA(()), pltpu.SemaphoreType.DMA((2,)), pltpu.SemaphoreType.DMA(())],
        ),
        out_shape=jax.ShapeDtypeStruct((n_rows * TOKEN_ROWS, LANES), U32),
        compiler_params=pltpu.CompilerParams(dimension_semantics=("arbitrary",)),
        name="scatter",
    )(starts, counts, n_used, _tile_major(pos, step), h1)


def _experts_kernel(be_ref, nb_ref, x_ref, wg_ref, wu_ref, wd_ref, y_ref, wg_bf, wu_bf, wd_bf):
    b = pl.program_id(0)

    @pl.when((b == 0) | (be_ref[b] != be_ref[jnp.maximum(b - 1, 0)]))
    def _():
        wg_bf[...] = wg_ref[...].astype(BF16)
        wu_bf[...] = wu_ref[...].astype(BF16)
        wd_bf[...] = wd_ref[...].astype(BF16)

    @pl.when(b < nb_ref[0])
    def _():
        his, los = _load_token_tiles(x_ref, EXPERT_BLOCK)
        xb = jnp.concatenate(his + los, axis=-1).astype(BF16)
        gate = _dot(xb, wg_bf[...])
        up = _dot(xb, wu_bf[...])
        hid = (gate * jax.nn.sigmoid(gate) * up).astype(BF16)
        _store_token_tiles(y_ref, _dot(hid, wd_bf[...]))

    @pl.when(b >= nb_ref[0])
    def _():
        y_ref[...] = jnp.zeros_like(y_ref)


def _experts(xs, block_e, n_used, w_gate, w_up, w_down):
    blk_rows = EXPERT_BLOCK * TOKEN_ROWS
    return pl.pallas_call(
        _experts_kernel,
        grid_spec=pltpu.PrefetchScalarGridSpec(
            num_scalar_prefetch=2,
            grid=(xs.shape[0] // blk_rows,),
            in_specs=[pl.BlockSpec((blk_rows, LANES), lambda b, be, nb: (b, 0)),
                      pl.BlockSpec((None, D_MODEL, D_EXPERT), lambda b, be, nb: (be[b], 0, 0)),
                      pl.BlockSpec((None, D_MODEL, D_EXPERT), lambda b, be, nb: (be[b], 0, 0)),
                      pl.BlockSpec((None, D_EXPERT, D_MODEL), lambda b, be, nb: (be[b], 0, 0))],
            out_specs=pl.BlockSpec((blk_rows, LANES), lambda b, be, nb: (b, 0)),
            scratch_shapes=[pltpu.VMEM((D_MODEL, D_EXPERT), BF16), pltpu.VMEM((D_MODEL, D_EXPERT), BF16),
                            pltpu.VMEM((D_EXPERT, D_MODEL), BF16)],
        ),
        out_shape=jax.ShapeDtypeStruct(xs.shape, U32),
        compiler_params=pltpu.CompilerParams(dimension_semantics=("arbitrary",)),
        name="experts",
    )(block_e, n_used, xs, w_gate, w_up, w_down)


def _combine_kernel(n_prompt_tiles, pos_hbm, h_ref, w_ref, ys_hbm, wsg_ref, wsu_ref, wsd_ref, g2_ref, b2_ref,
                    yp_hbm, ysm_hbm, pos_smem, ybuf, obuf, psem, gsems, osems):
    i = pl.program_id(0)
    tc = COMBINE_TILE
    tile_t = tc // BATCH
    load = pltpu.make_async_copy(pos_hbm.at[i], pos_smem, psem)
    load.start()
    load.wait()

    def gather(tile, slot):
        def issue(t, _):
            for k in range(TOP_K):
                _token_copy(ys_hbm, pos_smem[k, tile * tc + t], ybuf.at[slot, k], t, gsems.at[slot]).start()
            return 0

        lax.fori_loop(0, tc, issue, 0, unroll=2)

    def gather_wait(slot):
        for k in range(TOP_K):
            pltpu.make_async_copy(ys_hbm.at[pl.ds(0, tc * TOKEN_ROWS), :], ybuf.at[slot, k], gsems.at[slot]).wait()

    def out_copies(dst_hbm, t0, slot):
        return [pltpu.make_async_copy(obuf.at[slot, :, b, :], dst_hbm.at[b, pl.ds(t0, tile_t), :], osems.at[slot])
                for b in range(BATCH)]

    def out_wait(slot):
        for cp in out_copies(yp_hbm, 0, slot):
            cp.wait()

    def compute(tile, slot):
        r0 = pl.multiple_of(tile * tc, tc)
        h = h_ref[pl.ds(r0, tc), :]
        hb = h.astype(BF16)
        gate = _dot(hb, wsg_ref[...])
        up = _dot(hb, wsu_ref[...])
        shared = _dot((gate * jax.nn.sigmoid(gate) * up).astype(BF16), wsd_ref[...])

        eye = (lax.broadcasted_iota(I32, (tc, tc), 0) == lax.broadcasted_iota(I32, (tc, tc), 1)).astype(F32)
        w_cols = lax.dot_general(eye, w_ref[:, pl.ds(r0, tc)], (((1,), (1,)), ((), ())),
                                 precision=lax.Precision.HIGHEST, preferred_element_type=F32)
        gather_wait(slot)
        n_pieces = 2 * TOKEN_ROWS
        pieces = [shared[:, p * LANES:(p + 1) * LANES] for p in range(n_pieces)]
        for k in range(TOP_K):
            his, los = _load_token_tiles(ybuf.at[slot, k], tc)
            for p, y in enumerate(his + los):
                pieces[p] = pieces[p] + y * w_cols[:, k:k + 1]
        out = _layer_norm(DEEPNORM_ALPHA * h + jnp.concatenate(pieces, axis=-1), g2_ref[...], b2_ref[...])
        obuf[slot] = out.reshape(tile_t, BATCH, D_MODEL)

        g = i * COMBINE_TILES_PER_STEP + tile

        @pl.when(g < n_prompt_tiles)
        def _():
            for cp in out_copies(yp_hbm, pl.multiple_of(g * tile_t, tile_t), slot):
                cp.start()

        @pl.when(g >= n_prompt_tiles)
        def _():
            for cp in out_copies(ysm_hbm, pl.multiple_of((g - n_prompt_tiles) * tile_t, tile_t), slot):
                cp.start()

    n_pairs = COMBINE_TILES_PER_STEP // 2
    gather(0, 0)

    def pair(q, _):
        gather(2 * q + 1, 1)

        @pl.when(q > 0)
        def _():
            out_wait(0)

        compute(2 * q, 0)

        @pl.when(q + 1 < n_pairs)
        def _():
            gather(2 * q + 2, 0)

        @pl.when(q > 0)
        def _():
            out_wait(1)

        compute(2 * q + 1, 1)
        return 0

    lax.fori_loop(0, n_pairs, pair, 0)
    out_wait(0)
    out_wait(1)


def _combine(h1, pos, gate_w, ys, ws_gate, ws_up, ws_down, ln2_g, ln2_b, prompt_shape, sample_shape):
    t_rows = h1.shape[0]
    tc = COMBINE_TILE
    step = tc * COMBINE_TILES_PER_STEP
    n_prompt_tiles = prompt_shape[0] * prompt_shape[1] // tc
    const2 = lambda i: (0, 0)
    any_spec = pl.BlockSpec(memory_space=pl.ANY)
    return pl.pallas_call(
        functools.partial(_combine_kernel, n_prompt_tiles),
        grid=(t_rows // step,),
        in_specs=[any_spec,
                  pl.BlockSpec((step, D_MODEL), lambda i: (i, 0)),
                  pl.BlockSpec((8, step), lambda i: (0, i)),
                  any_spec,
                  pl.BlockSpec((D_MODEL, D_EXPERT), const2),
                  pl.BlockSpec((D_MODEL, D_EXPERT), const2),
                  pl.BlockSpec((D_EXPERT, D_MODEL), const2),
                  pl.BlockSpec((1, D_MODEL), const2),
                  pl.BlockSpec((1, D_MODEL), const2)],
        out_specs=(any_spec, any_spec),
        out_shape=(jax.ShapeDtypeStruct(prompt_shape, F32), jax.ShapeDtypeStruct(sample_shape, F32)),
        scratch_shapes=[pltpu.SMEM((8, step), I32),
                        pltpu.VMEM((2, TOP_K, tc * TOKEN_ROWS, LANES), U32),
                        pltpu.VMEM((2, tc // BATCH, BATCH, D_MODEL), F32),
                        pltpu.SemaphoreType.DMA(()), pltpu.SemaphoreType.DMA((2,)), pltpu.SemaphoreType.DMA((2,))],
        compiler_params=pltpu.CompilerParams(dimension_semantics=("arbitrary",), vmem_limit_bytes=V7X_VMEM_LIMIT),
        name="combine",
    )(_tile_major(pos, step), h1, gate_w, ys, ws_gate.astype(BF16), ws_up.astype(BF16), ws_down.astype(BF16),
      ln2_g.reshape(1, D_MODEL), ln2_b.reshape(1, D_MODEL))


def _time_major(x):
    b, l, d = x.shape
    return jnp.transpose(x, (1, 0, 2)).reshape(l * b, d)


def _batch_major(x, b):
    return jnp.transpose(x.reshape(x.shape[0] // b, b, x.shape[1]), (1, 0, 2))


def _block_diag(blocks):
    n, r, c = blocks.shape
    eye = jnp.eye(n, dtype=blocks.dtype)
    return (blocks[:, :, None, :] * eye[:, None, :, None]).reshape(n * r, n * c)


def kernel(x_prompt, x_sample, state_ssm_re, state_ssm_im, cache_conv, ln_in_g, ln_in_b, w_in, lam_re, lam_im, log_dt, ssm_b_re, ssm_b_im, ssm_c_re, ssm_c_im, ssm_d, w_glu, b_glu, conv_w, beta_ssm, beta_conv, w_out, ln1_g, ln1_b, w_router, router_bias, w_gate, w_up, w_down, ws_gate, ws_up, ws_down, ln2_g, ln2_b):
    bp, lp, _ = x_prompt.shape
    bs, ls, _ = x_sample.shape
    assert bp == BATCH and bs == BATCH and ls == CHUNK_T and lp % CHUNK_T == 0
    assert w_in.shape[0] == 1, "single-layer model"
    n_prompt = bp * lp
    row = lambda a: a.reshape(1, -1)

    a_re, a_im, bb_re, bb_im = _prep(lam_re[0], lam_im[0], log_dt[0], ssm_b_re[0], ssm_b_im[0])
    groups_in = 128 // SSM_GROUP
    groups_out = 256 // SSM_GROUP
    bbd = lambda bb: jnp.stack([_block_diag(bb[j * groups_in:(j + 1) * groups_in])
                                for j in range(N_GROUPS // groups_in)]).astype(BF16)
    ct = lambda cc: jnp.transpose(cc, (0, 2, 1))
    cbd = lambda cc: jnp.stack([_block_diag(ct(cc)[j * groups_out:(j + 1) * groups_out])
                                for j in range(N_GROUPS // groups_out)]).astype(BF16)
    mix_weights = (row(ln_in_g), row(ln_in_b), w_in[0].astype(BF16), row(a_re), row(a_im),
                   bbd(bb_re), bbd(bb_im), cbd(ssm_c_re[0]), cbd(-ssm_c_im[0]), row(ssm_d[0]),
                   w_glu[0].astype(BF16), row(b_glu[0]), conv_w[0], row(beta_ssm[0]), row(beta_conv[0]),
                   w_out[0].astype(BF16), row(ln1_g[0]), row(ln1_b[0]))

    h0r = state_ssm_re[0].reshape(BATCH, D_STATE)
    h0i = state_ssm_im[0].reshape(BATCH, D_STATE)
    cbuf = _time_major(cache_conv[0])
    h1, p_re, p_im, p_conv, s_re, s_im, s_conv = _mix(x_prompt, x_sample, h0r, h0i, cbuf, mix_weights)

    t_rows = h1.shape[0]
    assert t_rows % (COMBINE_TILE * COMBINE_TILES_PER_STEP) == 0 and n_prompt % COMBINE_TILE == 0
    n_rows = t_rows * TOP_K + N_EXPERTS * EXPERT_BLOCK
    n_blocks = n_rows // EXPERT_BLOCK
    n_blocks_pad = -(-n_blocks // 128) * 128
    e, gate_w, rank, cnt = _route(h1, w_router[0], router_bias[0])
    pos, block_e, n_used, starts = _pos(e, rank, cnt, n_blocks_pad)
    n_used = n_used[0, :1]
    xs = _scatter(h1, pos, starts[:, 0], cnt[:, 0].astype(I32), n_used, n_rows)
    ys = _experts(xs, block_e[0, :n_blocks], n_used, w_gate[0], w_up[0], w_down[0])
    y_prompt, y_sample = _combine(h1, pos, gate_w, ys, ws_gate[0], ws_up[0], ws_down[0], ln2_g[0], ln2_b[0],
                                  x_prompt.shape, x_sample.shape)

    st = lambda s: s.reshape(1, BATCH, N_GROUPS, SSM_STATE)
    cv = lambda t: _batch_major(t, BATCH)[None]
    return (y_prompt, y_sample, st(p_re), st(p_im), cv(p_conv), st(s_re), st(s_im), cv(s_conv))
```

```python
import functools
import math

import jax
import jax.numpy as jnp
from jax import lax
from jax.experimental import pallas as pl
from jax.experimental.pallas import tpu as pltpu

F32 = jnp.float32
BF16 = jnp.bfloat16
I32 = jnp.int32
U32 = jnp.uint32

D_MODEL = 1024
D_SSM = 512
D_CONV = 512
SSM_GROUP = 16
N_GROUPS = 32
SSM_STATE = 64
D_STATE = N_GROUPS * SSM_STATE
N_EXPERTS = 64
TOP_K = 6
N_EXPERT_GROUPS = 8
GROUP_SIZE = N_EXPERTS // N_EXPERT_GROUPS
TOPK_GROUPS = 4
D_EXPERT = 256
ROUTED_SCALE = 2.5
DEPTH = 1
DEEPNORM_ALPHA = (2.0 * DEPTH) ** 0.25
LN_EPS = 1e-5
RMS_EPS = 1e-6

BATCH = 8
CHUNK_T = 64
CHUNK_ROWS = CHUNK_T * BATCH
SCAN_COLS = 512
MOE_TILE = 256
RUN_PIECES = tuple(1 << b for b in range(MOE_TILE.bit_length() - 1, -1, -1))
EXPERT_BLOCK = 512
LANES = 128
TOKEN_ROWS = D_MODEL // (2 * LANES)
V7X_VMEM_LIMIT = 56 * 1024 * 1024
NEG_INF = float("-inf")


def _layer_norm(x, g, b):
    mu = jnp.mean(x, axis=-1, keepdims=True)
    xc = x - mu
    var = jnp.mean(xc * xc, axis=-1, keepdims=True)
    return xc * lax.rsqrt(var + LN_EPS) * g + b


def _rms_norm(x, g):
    return x * lax.rsqrt(jnp.mean(x * x, axis=-1, keepdims=True) + RMS_EPS) * g


def _dot(a, b):
    return jnp.dot(a, b, preferred_element_type=F32)


def _discretise(lr, li, log_dt):
    dt = jnp.exp(log_dt)
    mag = jnp.exp(lr * dt)
    ar = mag * jnp.cos(li * dt)
    ai = mag * jnp.sin(li * dt)
    den = lr * lr + li * li
    qr = ((ar - 1.0) * lr + ai * li) / den
    qi = (ai * lr - (ar - 1.0) * li) / den
    return ar, ai, qr, qi


def _prep_kernel(lr_ref, li_ref, ldt_ref, lrc_ref, lic_ref, ldtc_ref, br_ref, bi_ref,
                 ar_ref, ai_ref, bbr_ref, bbi_ref):
    ar, ai, _, _ = _discretise(lr_ref[...], li_ref[...], ldt_ref[...])
    ar_ref[...] = ar
    ai_ref[...] = ai
    _, _, qr, qi = _discretise(lrc_ref[...], lic_ref[...], ldtc_ref[...])
    br = br_ref[...]
    bi = bi_ref[...]
    bbr_ref[...] = qr * br - qi * bi
    bbi_ref[...] = qr * bi + qi * br


def _prep(lam_re, lam_im, log_dt, b_re, b_im):
    g, p = lam_re.shape
    per_channel = lambda a: jnp.repeat(a, SSM_GROUP, axis=0)
    rows = lambda b: jnp.transpose(b, (0, 2, 1)).reshape(g * SSM_GROUP, p)
    ldt = log_dt.reshape(g, 1)
    ar, ai, bbr, bbi = pl.pallas_call(
        _prep_kernel,
        out_shape=(jax.ShapeDtypeStruct((g, p), F32), jax.ShapeDtypeStruct((g, p), F32),
                   jax.ShapeDtypeStruct((g * SSM_GROUP, p), F32), jax.ShapeDtypeStruct((g * SSM_GROUP, p), F32)),
        name="prep",
    )(lam_re, lam_im, ldt, per_channel(lam_re), per_channel(lam_im), per_channel(ldt), rows(b_re), rows(b_im))
    return ar, ai, bbr.reshape(g, SSM_GROUP, p), bbi.reshape(g, SSM_GROUP, p)


def _chunk_copies(src_hbm, t0, xbuf, slot, sems):
    return [pltpu.make_async_copy(src_hbm.at[b, pl.ds(t0, CHUNK_T), :], xbuf.at[slot, :, b, :], sems.at[slot])
            for b in range(BATCH)]


def _mix_kernel(n_prompt_chunks,
                xp_hbm, xs_hbm, h0r_ref, h0i_ref, cbuf_ref, lng_ref, lnb_ref, win_ref, ar_ref, ai_ref,
                bbr_ref, bbi_ref, cbr_ref, cbi_ref, dsk_ref, wglu_ref, bglu_ref, cw_ref,
                bs_ref, bc_ref, wout_ref, g1_ref, b1_ref,
                h1_ref, pr_ref, pi_ref, pc_ref, sr_ref, si_ref, sc_ref,
                str_ref, sti_ref, hr_ref, hi_ref, cv_ref, xbuf, xsems):
    c = pl.program_id(0)
    n = CHUNK_ROWS
    carry_rows = 2 * BATCH
    slot = c % 2

    @pl.when(c == 0)
    def _():
        for cp in _chunk_copies(xp_hbm, 0, xbuf, 0, xsems):
            cp.start()

    @pl.when(c + 1 < n_prompt_chunks)
    def _():
        for cp in _chunk_copies(xp_hbm, pl.multiple_of((c + 1) * CHUNK_T, CHUNK_T), xbuf, 1 - slot, xsems):
            cp.start()

    @pl.when(c + 1 == n_prompt_chunks)
    def _():
        for cp in _chunk_copies(xs_hbm, 0, xbuf, 1 - slot, xsems):
            cp.start()

    @pl.when(c == 0)
    def _():
        hr_ref[...] = jnp.zeros_like(hr_ref)
        hi_ref[...] = jnp.zeros_like(hi_ref)
        cv_ref[0:carry_rows, :] = jnp.zeros((carry_rows, D_CONV), F32)

    @pl.when(c == n_prompt_chunks)
    def _():
        hr_ref[...] = h0r_ref[...]
        hi_ref[...] = h0i_ref[...]
        cv_ref[0:carry_rows, :] = cbuf_ref[...]

    for cp in _chunk_copies(xp_hbm, 0, xbuf, slot, xsems):
        cp.wait()
    h = _layer_norm(xbuf[slot].reshape(n, D_MODEL), lng_ref[...], lnb_ref[...])
    hb = h.astype(BF16)

    u = _dot(hb, win_ref[:, 0:D_SSM])
    ub = u.astype(BF16)
    u_tile = 128
    s_tile = u_tile // SSM_GROUP * SSM_STATE
    for j in range(D_SSM // u_tile):
        uj = ub[:, j * u_tile:(j + 1) * u_tile]
        str_ref[:, j * s_tile:(j + 1) * s_tile] = _dot(uj, bbr_ref[j])
        sti_ref[:, j * s_tile:(j + 1) * s_tile] = _dot(uj, bbi_ref[j])

    for cb in range(D_STATE // SCAN_COLS):
        cols = slice(cb * SCAN_COLS, (cb + 1) * SCAN_COLS)
        ar = jnp.broadcast_to(ar_ref[:, cols], (BATCH, SCAN_COLS))
        ai = jnp.broadcast_to(ai_ref[:, cols], (BATCH, SCAN_COLS))

        def step(t, carry, cols=cols, ar=ar, ai=ai):
            sr, si = carry
            r0 = pl.multiple_of(t * BATCH, BATCH)
            nr = ar * sr - ai * si + str_ref[pl.ds(r0, BATCH), cols]
            ni = ar * si + ai * sr + sti_ref[pl.ds(r0, BATCH), cols]
            str_ref[pl.ds(r0, BATCH), cols] = nr
            sti_ref[pl.ds(r0, BATCH), cols] = ni
            return nr, ni

        sr, si = lax.fori_loop(0, CHUNK_T, step, (hr_ref[:, cols], hi_ref[:, cols]), unroll=8)
        hr_ref[:, cols] = sr
        hi_ref[:, cols] = si

    k_tile = 1024
    ys = []
    for j in range(D_STATE // k_tile):
        sl = slice(j * k_tile, (j + 1) * k_tile)
        ys.append(_dot(str_ref[:, sl].astype(BF16), cbr_ref[j]) + _dot(sti_ref[:, sl].astype(BF16), cbi_ref[j]))
    y_ssm = jnp.concatenate(ys, axis=-1) + dsk_ref[...] * u
    g = jax.nn.gelu(y_ssm)
    y_ssm = g * jax.nn.sigmoid(_dot(g.astype(BF16), wglu_ref[...]) + bglu_ref[...])
    mix = _dot(_rms_norm(y_ssm, bs_ref[...]).astype(BF16), wout_ref[0:D_SSM, :])

    gate_b = _dot(hb, win_ref[:, D_SSM:D_SSM + D_CONV])
    gate_c = _dot(hb, win_ref[:, D_SSM + D_CONV:D_SSM + 2 * D_CONV])
    v = _dot(hb, win_ref[:, D_SSM + 2 * D_CONV:D_SSM + 3 * D_CONV])
    cv_ref[carry_rows:carry_rows + n, :] = gate_c * v
    y_conv = gate_b * (cv_ref[0:n, :] * cw_ref[0:1, :]
                       + cv_ref[BATCH:BATCH + n, :] * cw_ref[1:2, :]
                       + cv_ref[carry_rows:carry_rows + n, :] * cw_ref[2:3, :])
    tail = cv_ref[n:n + carry_rows, :]
    cv_ref[0:carry_rows, :] = tail
    mix = mix + _dot(_rms_norm(y_conv, bc_ref[...]).astype(BF16), wout_ref[D_SSM:D_SSM + D_CONV, :])

    h1_ref[...] = _layer_norm(DEEPNORM_ALPHA * h + mix, g1_ref[...], b1_ref[...])

    @pl.when(c == n_prompt_chunks - 1)
    def _():
        pr_ref[...] = hr_ref[...]
        pi_ref[...] = hi_ref[...]
        pc_ref[...] = tail

    @pl.when(c == n_prompt_chunks)
    def _():
        sr_ref[...] = hr_ref[...]
        si_ref[...] = hi_ref[...]
        sc_ref[...] = tail


def _mix(x_prompt, x_sample, h0r, h0i, cbuf, weights):
    n_prompt_chunks = x_prompt.shape[1] // CHUNK_T
    n_chunks = n_prompt_chunks + 1
    t_rows = n_chunks * CHUNK_ROWS
    any_spec = pl.BlockSpec(memory_space=pl.ANY)
    const2 = lambda c: (0, 0)
    const3 = lambda c: (0, 0, 0)
    w_specs = [pl.BlockSpec(w.shape, const3 if w.ndim == 3 else const2) for w in weights]
    state = jax.ShapeDtypeStruct((BATCH, D_STATE), F32)
    tail = jax.ShapeDtypeStruct((2 * BATCH, D_CONV), F32)
    state_spec = pl.BlockSpec((BATCH, D_STATE), const2)
    tail_spec = pl.BlockSpec((2 * BATCH, D_CONV), const2)
    return pl.pallas_call(
        functools.partial(_mix_kernel, n_prompt_chunks),
        grid=(n_chunks,),
        in_specs=[any_spec, any_spec, state_spec, state_spec, tail_spec] + w_specs,
        out_specs=(pl.BlockSpec((CHUNK_ROWS, D_MODEL), lambda c: (c, 0)),
                   state_spec, state_spec, tail_spec, state_spec, state_spec, tail_spec),
        out_shape=(jax.ShapeDtypeStruct((t_rows, D_MODEL), F32), state, state, tail, state, state, tail),
        scratch_shapes=[pltpu.VMEM((CHUNK_ROWS, D_STATE), F32), pltpu.VMEM((CHUNK_ROWS, D_STATE), F32),
                        pltpu.VMEM((BATCH, D_STATE), F32), pltpu.VMEM((BATCH, D_STATE), F32),
                        pltpu.VMEM((CHUNK_ROWS + 2 * BATCH, D_CONV), F32),
                        pltpu.VMEM((2, CHUNK_T, BATCH, D_MODEL), F32), pltpu.SemaphoreType.DMA((2,))],
        compiler_params=pltpu.CompilerParams(dimension_semantics=("arbitrary",), vmem_limit_bytes=V7X_VMEM_LIMIT),
        name="mix",
    )(x_prompt, x_sample, h0r, h0i, cbuf, *weights)


def _route_kernel(h_ref, wrt_ref, bias_ref, tri_ref, ltri_ref, w_ref, slot_ref, tcnt_ref, trun_ref, cnt_ref,
                  run_col, run_row):
    i = pl.program_id(0)
    tr = MOE_TILE

    @pl.when(i == 0)
    def _():
        run_col[...] = jnp.zeros_like(run_col)
        run_row[...] = jnp.zeros_like(run_row)

    def split(a):
        hi = a.astype(BF16)
        return hi, (a - hi.astype(F32)).astype(BF16)

    nt_dot = lambda a, b: lax.dot_general(a, b, (((1,), (1,)), ((), ())), preferred_element_type=F32)
    w_hi, w_lo = split(wrt_ref[...])
    h_hi, h_lo = split(h_ref[...])
    logits = nt_dot(w_hi, h_hi) + (nt_dot(w_hi, h_lo) + nt_dot(w_lo, h_hi))
    scores = jax.nn.sigmoid(logits)
    sel = scores + bias_ref[...]

    sub = lax.broadcasted_iota(I32, (GROUP_SIZE, tr), 0).astype(F32)
    blocks, gscore = [], []
    for g in range(N_EXPERT_GROUPS):
        blk = sel[g * GROUP_SIZE:(g + 1) * GROUP_SIZE, :]
        m1 = jnp.max(blk, axis=0, keepdims=True)
        first = jnp.min(jnp.where(blk == m1, sub, float(GROUP_SIZE)), axis=0, keepdims=True)
        m2 = jnp.max(jnp.where(sub == first, NEG_INF, blk), axis=0, keepdims=True)
        blocks.append(blk)
        gscore.append(m1 + m2)
    masked = []
    for g in range(N_EXPERT_GROUPS):
        beaten = jnp.zeros((1, tr), F32)
        for o in range(N_EXPERT_GROUPS):
            if o == g:
                continue
            wins = gscore[o] >= gscore[g] if o < g else gscore[o] > gscore[g]
            beaten = beaten + wins.astype(F32)
        masked.append(jnp.where(beaten < float(TOPK_GROUPS), blocks[g], NEG_INF))
    masked = jnp.concatenate(masked, axis=0)

    row = lax.broadcasted_iota(I32, (N_EXPERTS, tr), 0).astype(F32)
    picked = jnp.zeros((N_EXPERTS, tr), F32)
    hots, gates = [], []
    for _ in range(TOP_K):
        m = jnp.max(masked, axis=0, keepdims=True)
        idx = jnp.min(jnp.where(masked == m, row, float(N_EXPERTS)), axis=0, keepdims=True)
        hot = row == idx
        hots.append(hot)
        gates.append(jnp.sum(jnp.where(hot, scores, 0.0), axis=0, keepdims=True))
        masked = jnp.where(hot, NEG_INF, masked)
        picked = picked + hot.astype(F32)
    total = gates[0]
    for k in range(1, TOP_K):
        total = total + gates[k]

    pb = picked.astype(BF16)
    earlier = _dot(pb, tri_ref[...])
    cnt_col = jnp.sum(picked, axis=1, keepdims=True)
    lower = _dot(ltri_ref[...], jnp.broadcast_to(cnt_col, (N_EXPERTS, LANES)).astype(BF16))[:, 0:1]
    slot_of = earlier + lower
    for k in range(TOP_K):
        w_ref[k:k + 1, :] = gates[k] / total * ROUTED_SCALE
        slot_ref[k:k + 1, :] = jnp.sum(jnp.where(hots[k], slot_of, 0.0), axis=0, keepdims=True).astype(I32)
    w_ref[TOP_K:, :] = jnp.zeros((8 - TOP_K, tr), F32)
    slot_ref[TOP_K:, :] = jnp.full((8 - TOP_K, tr), -1, I32)

    cnt_row = nt_dot(jnp.ones((8, tr), BF16), pb)
    tcnt_ref[0] = cnt_row.astype(I32)
    trun_ref[0] = run_row[...].astype(I32)
    run_row[...] = run_row[...] + cnt_row
    run_col[...] = run_col[...] + cnt_col
    cnt_ref[...] = jnp.broadcast_to(run_col[...], cnt_ref.shape)


def _route(h1, w_router, router_bias):
    t_rows = h1.shape[0]
    tr = MOE_TILE
    n_tiles = t_rows // tr
    tri = jnp.triu(jnp.ones((tr, tr), BF16), k=1)
    ltri = jnp.tril(jnp.ones((N_EXPERTS, N_EXPERTS), BF16), k=-1)
    tok = lambda i: (0, i)
    const2 = lambda i: (0, 0)
    per_tile = lambda i: (i, 0, 0)
    return pl.pallas_call(
        _route_kernel,
        grid=(n_tiles,),
        in_specs=[pl.BlockSpec((tr, D_MODEL), lambda i: (i, 0)),
                  pl.BlockSpec((N_EXPERTS, D_MODEL), const2),
                  pl.BlockSpec((N_EXPERTS, 1), const2),
                  pl.BlockSpec((tr, tr), const2),
                  pl.BlockSpec((N_EXPERTS, N_EXPERTS), const2)],
        out_specs=(pl.BlockSpec((8, tr), tok), pl.BlockSpec((8, tr), tok),
                   pl.BlockSpec((1, 8, N_EXPERTS), per_tile), pl.BlockSpec((1, 8, N_EXPERTS), per_tile),
                   pl.BlockSpec((N_EXPERTS, LANES), const2)),
        out_shape=(jax.ShapeDtypeStruct((8, t_rows), F32), jax.ShapeDtypeStruct((8, t_rows), I32),
                   jax.ShapeDtypeStruct((n_tiles, 8, N_EXPERTS), I32), jax.ShapeDtypeStruct((n_tiles, 8, N_EXPERTS), I32),
                   jax.ShapeDtypeStruct((N_EXPERTS, LANES), F32)),
        scratch_shapes=[pltpu.VMEM((N_EXPERTS, 1), F32), pltpu.VMEM((8, N_EXPERTS), F32)],
        compiler_params=pltpu.CompilerParams(dimension_semantics=("arbitrary",)),
        name="route",
    )(h1, w_router.T, router_bias.reshape(N_EXPERTS, 1), tri, ltri)


def _pos_kernel(n_blocks_pad, ccol_ref, crow_ref, be_ref, nb_ref, start_ref):
    blk = float(EXPERT_BLOCK)
    pad = lambda cnt: jnp.floor((cnt + (blk - 1.0)) * (1.0 / blk)) * blk
    padded_row = pad(crow_ref[0:1, :])
    lane = lax.broadcasted_iota(I32, (N_EXPERTS, 128), 1)
    sub = lax.broadcasted_iota(I32, (N_EXPERTS, 128), 0)
    start = jnp.sum(jnp.where(lane < sub, padded_row, 0.0), axis=1, keepdims=True)
    end = start + pad(ccol_ref[:, 0:1])

    first_row = lax.broadcasted_iota(I32, (N_EXPERTS, n_blocks_pad), 1).astype(F32) * blk
    owner = jnp.sum((end <= first_row).astype(F32), axis=0, keepdims=True)
    be_ref[...] = jnp.broadcast_to(jnp.minimum(owner, float(N_EXPERTS - 1)).astype(I32), be_ref.shape)
    used = jnp.max(end, axis=0, keepdims=True) * (1.0 / blk)
    nb_ref[...] = jnp.broadcast_to(used.astype(I32), nb_ref.shape)
    start_ref[...] = jnp.broadcast_to(start.astype(I32), start_ref.shape)


def _pos(cnt, n_blocks_pad):
    counts = cnt[:, 0]
    ccol = jnp.broadcast_to(counts[:, None], (N_EXPERTS, 128))
    crow = jnp.broadcast_to(jnp.pad(counts, (0, 128 - N_EXPERTS))[None, :], (8, 128))
    return pl.pallas_call(
        functools.partial(_pos_kernel, n_blocks_pad),
        out_shape=(jax.ShapeDtypeStruct((8, n_blocks_pad), I32), jax.ShapeDtypeStruct((8, 128), I32),
                   jax.ShapeDtypeStruct((N_EXPERTS, 128), I32)),
        name="pos",
    )(ccol, crow)


def _token_copy(src_ref, src_tok, dst_ref, dst_tok, sem, n_tok=1):
    rows = n_tok * TOKEN_ROWS
    src = pl.ds(pl.multiple_of(src_tok * TOKEN_ROWS, TOKEN_ROWS), rows)
    dst = pl.ds(pl.multiple_of(dst_tok * TOKEN_ROWS, TOKEN_ROWS), rows)
    return pltpu.make_async_copy(src_ref.at[src, :], dst_ref.at[dst, :], sem)


def _for_each_run_piece(tile, start_ref, trun_ref, tcnt_ref, fn):
    def per_expert(e, local):
        cnt = tcnt_ref[tile, e]
        sorted_row = start_ref[e] + trun_ref[tile, e]
        for piece in RUN_PIECES:
            done = cnt & (-2 * piece)

            @pl.when((cnt & piece) != 0)
            def _():
                fn(local + done, sorted_row + done, piece)

        return local + cnt

    lax.fori_loop(0, N_EXPERTS, per_expert, 0)


def _store_token_tiles(dst_ref, x):
    n = x.shape[0]
    half = TOKEN_ROWS * LANES
    bits = lambda v: lax.bitcast_convert_type(v.astype(BF16).astype(F32), U32)
    for s in range(TOKEN_ROWS):
        hi = bits(x[:, s * LANES:(s + 1) * LANES])
        lo = bits(x[:, half + s * LANES:half + (s + 1) * LANES])
        dst_ref[pl.ds(s, n, stride=TOKEN_ROWS), :] = hi | (lo >> 16)


def _load_token_tiles(src_ref, n):
    his, los = [], []
    for s in range(TOKEN_ROWS):
        words = src_ref[pl.ds(s, n, stride=TOKEN_ROWS), :]
        his.append(lax.bitcast_convert_type(words & jnp.uint32(0xFFFF0000), F32))
        los.append(lax.bitcast_convert_type(words << 16, F32))
    return his, los


def _scatter_kernel(n_blocks, start_ref, count_ref, nb_ref, trun_ref, tcnt_ref, h_ref, slot_ref, xs_out,
                    tiles, zeros, sems, zsem):
    i = pl.program_id(0)
    ts = MOE_TILE
    blk_rows = EXPERT_BLOCK * TOKEN_ROWS

    @pl.when(i == 0)
    def _():
        zeros[...] = jnp.zeros_like(zeros)
        zero_block = lambda b: pltpu.make_async_copy(
            zeros, xs_out.at[pl.ds(pl.multiple_of(b * blk_rows, blk_rows), blk_rows), :], zsem)

        def pad_rows(e, n_pad):
            lo = start_ref[e] + count_ref[e]
            hi = start_ref[e] + ((count_ref[e] + (EXPERT_BLOCK - 1)) & (-EXPERT_BLOCK))

            def zero_row(r, _):
                _token_copy(zeros, 0, xs_out, r, zsem).start()
                return 0

            lax.fori_loop(lo, hi, zero_row, 0)
            return n_pad + (hi - lo)

        n_pad = lax.fori_loop(0, N_EXPERTS, pad_rows, 0)

        def start_block(b, _):
            zero_block(b).start()
            return 0

        lax.fori_loop(nb_ref[0], n_blocks, start_block, 0)

        def wait_row(r, _):
            _token_copy(zeros, 0, xs_out, 0, zsem).wait()
            return 0

        lax.fori_loop(0, n_pad, wait_row, 0)

        def wait_block(b, _):
            zero_block(0).wait()
            return 0

        lax.fori_loop(nb_ref[0], n_blocks, wait_block, 0)

    def drain(half):
        pltpu.make_async_copy(tiles.at[half], xs_out.at[pl.ds(0, TOP_K * ts * TOKEN_ROWS), :], sems.at[half]).wait()

    for half in range(2):
        @pl.when(i > 0)
        def _():
            drain(half)

        h = h_ref[half * ts:(half + 1) * ts, :].astype(BF16)
        slots = slot_ref[:, half * ts:(half + 1) * ts]
        for jb in range(TOP_K):
            j = lax.broadcasted_iota(I32, (ts, ts), 0) + jb * ts
            perm = jnp.zeros((ts, ts), F32)
            for k in range(TOP_K):
                perm = perm + jnp.where(j == slots[k:k + 1, :], 1.0, 0.0)
            _store_token_tiles(tiles.at[half, pl.ds(jb * ts * TOKEN_ROWS, ts * TOKEN_ROWS), :],
                               _dot(perm.astype(BF16), h))

        tile = 2 * i + half
        _for_each_run_piece(
            tile, start_ref, trun_ref, tcnt_ref,
            lambda local, sorted_row, n, half=half: _token_copy(tiles.at[half], local, xs_out, sorted_row,
                                                                sems.at[half], n).start())

    @pl.when(i == pl.num_programs(0) - 1)
    def _():
        drain(0)
        drain(1)


def _scatter(h1, slots, starts, counts, n_used, tile_run, tile_cnt, n_rows):
    t_rows = h1.shape[0]
    step = 2 * MOE_TILE
    n_blocks = n_rows // EXPERT_BLOCK
    return pl.pallas_call(
        functools.partial(_scatter_kernel, n_blocks),
        grid_spec=pltpu.PrefetchScalarGridSpec(
            num_scalar_prefetch=5,
            grid=(t_rows // step,),
            in_specs=[pl.BlockSpec((step, D_MODEL), lambda i, *_: (i, 0)),
                      pl.BlockSpec((8, step), lambda i, *_: (0, i))],
            out_specs=pl.BlockSpec(memory_space=pl.ANY),
            scratch_shapes=[pltpu.VMEM((2, TOP_K * MOE_TILE * TOKEN_ROWS, LANES), U32),
                            pltpu.VMEM((EXPERT_BLOCK * TOKEN_ROWS, LANES), U32),
                            pltpu.SemaphoreType.DMA((2,)), pltpu.SemaphoreType.DMA(())],
        ),
        out_shape=jax.ShapeDtypeStruct((n_rows * TOKEN_ROWS, LANES), U32),
        compiler_params=pltpu.CompilerParams(dimension_semantics=("arbitrary",), vmem_limit_bytes=V7X_VMEM_LIMIT),
        name="scatter",
    )(starts, counts, n_used, tile_run, tile_cnt, h1, slots)


def _experts_kernel(be_ref, nb_ref, x_ref, wg_ref, wu_ref, wd_ref, y_ref, wg_bf, wu_bf, wd_bf):
    b = pl.program_id(0)

    @pl.when((b == 0) | (be_ref[b] != be_ref[jnp.maximum(b - 1, 0)]))
    def _():
        wg_bf[...] = wg_ref[...].astype(BF16)
        wu_bf[...] = wu_ref[...].astype(BF16)
        wd_bf[...] = wd_ref[...].astype(BF16)

    @pl.when(b < nb_ref[0])
    def _():
        his, los = _load_token_tiles(x_ref, EXPERT_BLOCK)
        xb = jnp.concatenate(his + los, axis=-1).astype(BF16)
        gate = _dot(xb, wg_bf[...])
        up = _dot(xb, wu_bf[...])
        hid = (gate * jax.nn.sigmoid(gate) * up).astype(BF16)
        _store_token_tiles(y_ref, _dot(hid, wd_bf[...]))

    @pl.when(b >= nb_ref[0])
    def _():
        y_ref[...] = jnp.zeros_like(y_ref)


def _experts(xs, block_e, n_used, w_gate, w_up, w_down):
    blk_rows = EXPERT_BLOCK * TOKEN_ROWS
    return pl.pallas_call(
        _experts_kernel,
        grid_spec=pltpu.PrefetchScalarGridSpec(
            num_scalar_prefetch=2,
            grid=(xs.shape[0] // blk_rows,),
            in_specs=[pl.BlockSpec((blk_rows, LANES), lambda b, be, nb: (b, 0)),
                      pl.BlockSpec((None, D_MODEL, D_EXPERT), lambda b, be, nb: (be[b], 0, 0)),
                      pl.BlockSpec((None, D_MODEL, D_EXPERT), lambda b, be, nb: (be[b], 0, 0)),
                      pl.BlockSpec((None, D_EXPERT, D_MODEL), lambda b, be, nb: (be[b], 0, 0))],
            out_specs=pl.BlockSpec((blk_rows, LANES), lambda b, be, nb: (b, 0)),
            scratch_shapes=[pltpu.VMEM((D_MODEL, D_EXPERT), BF16), pltpu.VMEM((D_MODEL, D_EXPERT), BF16),
                            pltpu.VMEM((D_EXPERT, D_MODEL), BF16)],
        ),
        out_shape=jax.ShapeDtypeStruct(xs.shape, U32),
        compiler_params=pltpu.CompilerParams(dimension_semantics=("arbitrary",)),
        name="experts",
    )(block_e, n_used, xs, w_gate, w_up, w_down)


def _combine_kernel(n_prompt_tiles, start_ref, trun_ref, tcnt_ref, h_ref, w_ref, slot_ref, ys_hbm,
                    wsg_ref, wsu_ref, wsd_ref, g2_ref, b2_ref, yp_hbm, ysm_hbm, ybuf, obuf, gsems, osems):
    i = pl.program_id(0)
    n_steps = pl.num_programs(0)
    tc = MOE_TILE
    tile_t = tc // BATCH

    def gather(tile, slot):
        _for_each_run_piece(
            tile, start_ref, trun_ref, tcnt_ref,
            lambda local, sorted_row, n: _token_copy(ys_hbm, sorted_row, ybuf.at[slot], local,
                                                     gsems.at[slot], n).start())

    def gather_wait(slot):
        pltpu.make_async_copy(ys_hbm.at[pl.ds(0, TOP_K * tc * TOKEN_ROWS), :], ybuf.at[slot], gsems.at[slot]).wait()

    def out_copies(dst_hbm, t0, slot):
        return [pltpu.make_async_copy(obuf.at[slot, :, b, :], dst_hbm.at[b, pl.ds(t0, tile_t), :], osems.at[slot])
                for b in range(BATCH)]

    def out_wait(slot):
        for cp in out_copies(yp_hbm, 0, slot):
            cp.wait()

    def compute(half, slot):
        rows = slice(half * tc, (half + 1) * tc)
        h = h_ref[rows, :]
        hb = h.astype(BF16)
        gate = _dot(hb, wsg_ref[...])
        up = _dot(hb, wsu_ref[...])
        shared = _dot((gate * jax.nn.sigmoid(gate) * up).astype(BF16), wsd_ref[...])

        eye = (lax.broadcasted_iota(I32, (tc, tc), 0) == lax.broadcasted_iota(I32, (tc, tc), 1)).astype(F32)
        to_cols = lambda a: lax.dot_general(eye, a, (((1,), (1,)), ((), ())),
                                            precision=lax.Precision.HIGHEST, preferred_element_type=F32)
        w_cols = to_cols(w_ref[:, rows])
        s_cols = to_cols(slot_ref[:, rows].astype(F32)).astype(I32)

        j = lax.broadcasted_iota(I32, (tc, TOP_K * tc), 1)
        mix_w = jnp.zeros((tc, TOP_K * tc), F32)
        for k in range(TOP_K):
            mix_w = mix_w + jnp.where(j == s_cols[:, k:k + 1], w_cols[:, k:k + 1], 0.0)
        mix_hi = mix_w.astype(BF16)
        mix_lo = (mix_w - mix_hi.astype(F32)).astype(BF16)
        gather_wait(slot)
        his, los = _load_token_tiles(ybuf.at[slot], TOP_K * tc)
        y = jnp.concatenate(his + los, axis=-1).astype(BF16)
        routed = _dot(mix_hi, y) + _dot(mix_lo, y)
        out = _layer_norm(DEEPNORM_ALPHA * h + (routed + shared), g2_ref[...], b2_ref[...])
        obuf[slot] = out.reshape(tile_t, BATCH, D_MODEL)

        g = 2 * i + half

        @pl.when(g < n_prompt_tiles)
        def _():
            for cp in out_copies(yp_hbm, pl.multiple_of(g * tile_t, tile_t), slot):
                cp.start()

        @pl.when(g >= n_prompt_tiles)
        def _():
            for cp in out_copies(ysm_hbm, pl.multiple_of((g - n_prompt_tiles) * tile_t, tile_t), slot):
                cp.start()

    @pl.when(i == 0)
    def _():
        gather(0, 0)

    gather(2 * i + 1, 1)

    @pl.when(i > 0)
    def _():
        out_wait(0)

    compute(0, 0)

    @pl.when(i + 1 < n_steps)
    def _():
        gather(2 * i + 2, 0)

    @pl.when(i > 0)
    def _():
        out_wait(1)

    compute(1, 1)

    @pl.when(i + 1 == n_steps)
    def _():
        out_wait(0)
        out_wait(1)


def _combine(h1, gate_w, slots, ys, starts, tile_run, tile_cnt, ws_gate, ws_up, ws_down, ln2_g, ln2_b,
             prompt_shape, sample_shape):
    t_rows = h1.shape[0]
    tc = MOE_TILE
    step = 2 * tc
    n_prompt_tiles = prompt_shape[0] * prompt_shape[1] // tc
    const2 = lambda i, *_: (0, 0)
    any_spec = pl.BlockSpec(memory_space=pl.ANY)
    return pl.pallas_call(
        functools.partial(_combine_kernel, n_prompt_tiles),
        grid_spec=pltpu.PrefetchScalarGridSpec(
            num_scalar_prefetch=3,
            grid=(t_rows // step,),
            in_specs=[pl.BlockSpec((step, D_MODEL), lambda i, *_: (i, 0)),
                      pl.BlockSpec((8, step), lambda i, *_: (0, i)),
                      pl.BlockSpec((8, step), lambda i, *_: (0, i)),
                      any_spec,
                      pl.BlockSpec((D_MODEL, D_EXPERT), const2),
                      pl.BlockSpec((D_MODEL, D_EXPERT), const2),
                      pl.BlockSpec((D_EXPERT, D_MODEL), const2),
                      pl.BlockSpec((1, D_MODEL), const2),
                      pl.BlockSpec((1, D_MODEL), const2)],
            out_specs=(any_spec, any_spec),
            scratch_shapes=[pltpu.VMEM((2, TOP_K * tc * TOKEN_ROWS, LANES), U32),
                            pltpu.VMEM((2, tc // BATCH, BATCH, D_MODEL), F32),
                            pltpu.SemaphoreType.DMA((2,)), pltpu.SemaphoreType.DMA((2,))],
        ),
        out_shape=(jax.ShapeDtypeStruct(prompt_shape, F32), jax.ShapeDtypeStruct(sample_shape, F32)),
        compiler_params=pltpu.CompilerParams(dimension_semantics=("arbitrary",), vmem_limit_bytes=V7X_VMEM_LIMIT),
        name="combine",
    )(starts, tile_run, tile_cnt, h1, gate_w, slots, ys, ws_gate.astype(BF16), ws_up.astype(BF16),
      ws_down.astype(BF16), ln2_g.reshape(1, D_MODEL), ln2_b.reshape(1, D_MODEL))


def _time_major(x):
    b, l, d = x.shape
    return jnp.transpose(x, (1, 0, 2)).reshape(l * b, d)


def _batch_major(x, b):
    return jnp.transpose(x.reshape(x.shape[0] // b, b, x.shape[1]), (1, 0, 2))


def _block_diag(blocks):
    n, r, c = blocks.shape
    eye = jnp.eye(n, dtype=blocks.dtype)
    return (blocks[:, :, None, :] * eye[:, None, :, None]).reshape(n * r, n * c)


def kernel(x_prompt, x_sample, state_ssm_re, state_ssm_im, cache_conv, ln_in_g, ln_in_b, w_in, lam_re, lam_im, log_dt, ssm_b_re, ssm_b_im, ssm_c_re, ssm_c_im, ssm_d, w_glu, b_glu, conv_w, beta_ssm, beta_conv, w_out, ln1_g, ln1_b, w_router, router_bias, w_gate, w_up, w_down, ws_gate, ws_up, ws_down, ln2_g, ln2_b):
    bp, lp, _ = x_prompt.shape
    bs, ls, _ = x_sample.shape
    assert bp == BATCH and bs == BATCH and ls == CHUNK_T and lp % CHUNK_T == 0
    assert w_in.shape[0] == 1, "single-layer model"
    n_prompt = bp * lp
    row = lambda a: a.reshape(1, -1)

    a_re, a_im, bb_re, bb_im = _prep(lam_re[0], lam_im[0], log_dt[0], ssm_b_re[0], ssm_b_im[0])
    groups_in = 128 // SSM_GROUP
    groups_out = 256 // SSM_GROUP
    bbd = lambda bb: jnp.stack([_block_diag(bb[j * groups_in:(j + 1) * groups_in])
                                for j in range(N_GROUPS // groups_in)]).astype(BF16)
    ct = lambda cc: jnp.transpose(cc, (0, 2, 1))
    cbd = lambda cc: jnp.stack([_block_diag(ct(cc)[j * groups_out:(j + 1) * groups_out])
                                for j in range(N_GROUPS // groups_out)]).astype(BF16)
    mix_weights = (row(ln_in_g), row(ln_in_b), w_in[0].astype(BF16), row(a_re), row(a_im),
                   bbd(bb_re), bbd(bb_im), cbd(ssm_c_re[0]), cbd(-ssm_c_im[0]), row(ssm_d[0]),
                   w_glu[0].astype(BF16), row(b_glu[0]), conv_w[0], row(beta_ssm[0]), row(beta_conv[0]),
                   w_out[0].astype(BF16), row(ln1_g[0]), row(ln1_b[0]))

    h0r = state_ssm_re[0].reshape(BATCH, D_STATE)
    h0i = state_ssm_im[0].reshape(BATCH, D_STATE)
    cbuf = _time_major(cache_conv[0])
    h1, p_re, p_im, p_conv, s_re, s_im, s_conv = _mix(x_prompt, x_sample, h0r, h0i, cbuf, mix_weights)

    t_rows = h1.shape[0]
    assert t_rows % (2 * MOE_TILE) == 0 and n_prompt % MOE_TILE == 0
    n_rows = t_rows * TOP_K + N_EXPERTS * EXPERT_BLOCK
    n_blocks = n_rows // EXPERT_BLOCK
    n_blocks_pad = -(-n_blocks // 128) * 128
    gate_w, slots, tile_cnt, tile_run, cnt = _route(h1, w_router[0], router_bias[0])
    block_e, n_used, starts = _pos(cnt, n_blocks_pad)
    n_used, starts, tile_cnt, tile_run = n_used[0, :1], starts[:, 0], tile_cnt[:, 0], tile_run[:, 0]
    xs = _scatter(h1, slots, starts, cnt[:, 0].astype(I32), n_used, tile_run, tile_cnt, n_rows)
    ys = _experts(xs, block_e[0, :n_blocks], n_used, w_gate[0], w_up[0], w_down[0])
    y_prompt, y_sample = _combine(h1, gate_w, slots, ys, starts, tile_run, tile_cnt, ws_gate[0], ws_up[0], ws_down[0],
                                  ln2_g[0], ln2_b[0], x_prompt.shape, x_sample.shape)

    st = lambda s: s.reshape(1, BATCH, N_GROUPS, SSM_STATE)
    cv = lambda t: _batch_major(t, BATCH)[None]
    return (y_prompt, y_sample, st(p_re), st(p_im), cv(p_conv), st(s_re), st(s_im), cv(s_conv))
```

```python
import functools
import math

import jax
import jax.numpy as jnp
from jax import lax
from jax.experimental import pallas as pl
from jax.experimental.pallas import tpu as pltpu

F32 = jnp.float32
BF16 = jnp.bfloat16
I32 = jnp.int32
U32 = jnp.uint32

D_MODEL = 1024
D_SSM = 512
D_CONV = 512
SSM_GROUP = 16
N_GROUPS = 32
SSM_STATE = 64
D_STATE = N_GROUPS * SSM_STATE
N_EXPERTS = 64
TOP_K = 6
N_EXPERT_GROUPS = 8
GROUP_SIZE = N_EXPERTS // N_EXPERT_GROUPS
TOPK_GROUPS = 4
D_EXPERT = 256
ROUTED_SCALE = 2.5
DEPTH = 1
DEEPNORM_ALPHA = (2.0 * DEPTH) ** 0.25
LN_EPS = 1e-5
RMS_EPS = 1e-6

BATCH = 8
CHUNK_T = 64
CHUNK_ROWS = CHUNK_T * BATCH
SCAN_COLS = 512
MOE_TILE = 256
RUN_CHUNK = 8
LOCAL_SLOTS = TOP_K * MOE_TILE + N_EXPERTS * RUN_CHUNK
N_CHUNKS = LOCAL_SLOTS // RUN_CHUNK
EXPERT_BLOCK = 512
LANES = 128
TOKEN_ROWS = D_MODEL // (2 * LANES)
V7X_VMEM_LIMIT = 56 * 1024 * 1024
NEG_INF = float("-inf")


def _layer_norm(x, g, b):
    mu = jnp.mean(x, axis=-1, keepdims=True)
    xc = x - mu
    var = jnp.mean(xc * xc, axis=-1, keepdims=True)
    return xc * lax.rsqrt(var + LN_EPS) * g + b


def _rms_norm(x, g):
    return x * lax.rsqrt(jnp.mean(x * x, axis=-1, keepdims=True) + RMS_EPS) * g


def _dot(a, b):
    return jnp.dot(a, b, preferred_element_type=F32)


def _discretise(lr, li, log_dt):
    dt = jnp.exp(log_dt)
    mag = jnp.exp(lr * dt)
    ar = mag * jnp.cos(li * dt)
    ai = mag * jnp.sin(li * dt)
    den = lr * lr + li * li
    qr = ((ar - 1.0) * lr + ai * li) / den
    qi = (ai * lr - (ar - 1.0) * li) / den
    return ar, ai, qr, qi


def _prep_kernel(lr_ref, li_ref, ldt_ref, lrc_ref, lic_ref, ldtc_ref, br_ref, bi_ref,
                 ar_ref, ai_ref, bbr_ref, bbi_ref):
    ar, ai, _, _ = _discretise(lr_ref[...], li_ref[...], ldt_ref[...])
    ar_ref[...] = ar
    ai_ref[...] = ai
    _, _, qr, qi = _discretise(lrc_ref[...], lic_ref[...], ldtc_ref[...])
    br = br_ref[...]
    bi = bi_ref[...]
    bbr_ref[...] = qr * br - qi * bi
    bbi_ref[...] = qr * bi + qi * br


def _prep(lam_re, lam_im, log_dt, b_re, b_im):
    g, p = lam_re.shape
    per_channel = lambda a: jnp.repeat(a, SSM_GROUP, axis=0)
    rows = lambda b: jnp.transpose(b, (0, 2, 1)).reshape(g * SSM_GROUP, p)
    ldt = log_dt.reshape(g, 1)
    ar, ai, bbr, bbi = pl.pallas_call(
        _prep_kernel,
        out_shape=(jax.ShapeDtypeStruct((g, p), F32), jax.ShapeDtypeStruct((g, p), F32),
                   jax.ShapeDtypeStruct((g * SSM_GROUP, p), F32), jax.ShapeDtypeStruct((g * SSM_GROUP, p), F32)),
        name="prep",
    )(lam_re, lam_im, ldt, per_channel(lam_re), per_channel(lam_im), per_channel(ldt), rows(b_re), rows(b_im))
    return ar, ai, bbr.reshape(g, SSM_GROUP, p), bbi.reshape(g, SSM_GROUP, p)


def _chunk_copies(src_hbm, t0, xbuf, slot, sems):
    return [pltpu.make_async_copy(src_hbm.at[b, pl.ds(t0, CHUNK_T), :], xbuf.at[slot, :, b, :], sems.at[slot])
            for b in range(BATCH)]


def _mix_kernel(n_prompt_chunks,
                xp_hbm, xs_hbm, h0r_ref, h0i_ref, cbuf_ref, lng_ref, lnb_ref, win_ref, ar_ref, ai_ref,
                bbr_ref, bbi_ref, cbr_ref, cbi_ref, dsk_ref, wglu_ref, bglu_ref, cw_ref,
                bs_ref, bc_ref, wout_ref, g1_ref, b1_ref,
                h1_ref, pr_ref, pi_ref, pc_ref, sr_ref, si_ref, sc_ref,
                str_ref, sti_ref, hr_ref, hi_ref, cv_ref, xbuf, xsems):
    c = pl.program_id(0)
    n = CHUNK_ROWS
    carry_rows = 2 * BATCH
    slot = c % 2

    @pl.when(c == 0)
    def _():
        for cp in _chunk_copies(xp_hbm, 0, xbuf, 0, xsems):
            cp.start()

    @pl.when(c + 1 < n_prompt_chunks)
    def _():
        for cp in _chunk_copies(xp_hbm, pl.multiple_of((c + 1) * CHUNK_T, CHUNK_T), xbuf, 1 - slot, xsems):
            cp.start()

    @pl.when(c + 1 == n_prompt_chunks)
    def _():
        for cp in _chunk_copies(xs_hbm, 0, xbuf, 1 - slot, xsems):
            cp.start()

    @pl.when(c == 0)
    def _():
        hr_ref[...] = jnp.zeros_like(hr_ref)
        hi_ref[...] = jnp.zeros_like(hi_ref)
        cv_ref[0:carry_rows, :] = jnp.zeros((carry_rows, D_CONV), F32)

    @pl.when(c == n_prompt_chunks)
    def _():
        hr_ref[...] = h0r_ref[...]
        hi_ref[...] = h0i_ref[...]
        cv_ref[0:carry_rows, :] = cbuf_ref[...]

    for cp in _chunk_copies(xp_hbm, 0, xbuf, slot, xsems):
        cp.wait()
    h = _layer_norm(xbuf[slot].reshape(n, D_MODEL), lng_ref[...], lnb_ref[...])
    hb = h.astype(BF16)

    u = _dot(hb, win_ref[:, 0:D_SSM])
    ub = u.astype(BF16)
    u_tile = 128
    s_tile = u_tile // SSM_GROUP * SSM_STATE
    for j in range(D_SSM // u_tile):
        uj = ub[:, j * u_tile:(j + 1) * u_tile]
        str_ref[:, j * s_tile:(j + 1) * s_tile] = _dot(uj, bbr_ref[j])
        sti_ref[:, j * s_tile:(j + 1) * s_tile] = _dot(uj, bbi_ref[j])

    for cb in range(D_STATE // SCAN_COLS):
        cols = slice(cb * SCAN_COLS, (cb + 1) * SCAN_COLS)
        ar = jnp.broadcast_to(ar_ref[:, cols], (BATCH, SCAN_COLS))
        ai = jnp.broadcast_to(ai_ref[:, cols], (BATCH, SCAN_COLS))

        def step(t, carry, cols=cols, ar=ar, ai=ai):
            sr, si = carry
            r0 = pl.multiple_of(t * BATCH, BATCH)
            nr = ar * sr - ai * si + str_ref[pl.ds(r0, BATCH), cols]
            ni = ar * si + ai * sr + sti_ref[pl.ds(r0, BATCH), cols]
            str_ref[pl.ds(r0, BATCH), cols] = nr
            sti_ref[pl.ds(r0, BATCH), cols] = ni
            return nr, ni

        sr, si = lax.fori_loop(0, CHUNK_T, step, (hr_ref[:, cols], hi_ref[:, cols]), unroll=8)
        hr_ref[:, cols] = sr
        hi_ref[:, cols] = si

    k_tile = 1024
    ys = []
    for j in range(D_STATE // k_tile):
        sl = slice(j * k_tile, (j + 1) * k_tile)
        ys.append(_dot(str_ref[:, sl].astype(BF16), cbr_ref[j]) + _dot(sti_ref[:, sl].astype(BF16), cbi_ref[j]))
    y_ssm = jnp.concatenate(ys, axis=-1) + dsk_ref[...] * u
    g = jax.nn.gelu(y_ssm)
    y_ssm = g * jax.nn.sigmoid(_dot(g.astype(BF16), wglu_ref[...]) + bglu_ref[...])
    mix = _dot(_rms_norm(y_ssm, bs_ref[...]).astype(BF16), wout_ref[0:D_SSM, :])

    gate_b = _dot(hb, win_ref[:, D_SSM:D_SSM + D_CONV])
    gate_c = _dot(hb, win_ref[:, D_SSM + D_CONV:D_SSM + 2 * D_CONV])
    v = _dot(hb, win_ref[:, D_SSM + 2 * D_CONV:D_SSM + 3 * D_CONV])
    cv_ref[carry_rows:carry_rows + n, :] = gate_c * v
    y_conv = gate_b * (cv_ref[0:n, :] * cw_ref[0:1, :]
                       + cv_ref[BATCH:BATCH + n, :] * cw_ref[1:2, :]
                       + cv_ref[carry_rows:carry_rows + n, :] * cw_ref[2:3, :])
    tail = cv_ref[n:n + carry_rows, :]
    cv_ref[0:carry_rows, :] = tail
    mix = mix + _dot(_rms_norm(y_conv, bc_ref[...]).astype(BF16), wout_ref[D_SSM:D_SSM + D_CONV, :])

    h1_ref[...] = _layer_norm(DEEPNORM_ALPHA * h + mix, g1_ref[...], b1_ref[...])

    @pl.when(c == n_prompt_chunks - 1)
    def _():
        pr_ref[...] = hr_ref[...]
        pi_ref[...] = hi_ref[...]
        pc_ref[...] = tail

    @pl.when(c == n_prompt_chunks)
    def _():
        sr_ref[...] = hr_ref[...]
        si_ref[...] = hi_ref[...]
        sc_ref[...] = tail


def _mix(x_prompt, x_sample, h0r, h0i, cbuf, weights):
    n_prompt_chunks = x_prompt.shape[1] // CHUNK_T
    n_chunks = n_prompt_chunks + 1
    t_rows = n_chunks * CHUNK_ROWS
    any_spec = pl.BlockSpec(memory_space=pl.ANY)
    const2 = lambda c: (0, 0)
    const3 = lambda c: (0, 0, 0)
    w_specs = [pl.BlockSpec(w.shape, const3 if w.ndim == 3 else const2) for w in weights]
    state = jax.ShapeDtypeStruct((BATCH, D_STATE), F32)
    tail = jax.ShapeDtypeStruct((2 * BATCH, D_CONV), F32)
    state_spec = pl.BlockSpec((BATCH, D_STATE), const2)
    tail_spec = pl.BlockSpec((2 * BATCH, D_CONV), const2)
    return pl.pallas_call(
        functools.partial(_mix_kernel, n_prompt_chunks),
        grid=(n_chunks,),
        in_specs=[any_spec, any_spec, state_spec, state_spec, tail_spec] + w_specs,
        out_specs=(pl.BlockSpec((CHUNK_ROWS, D_MODEL), lambda c: (c, 0)),
                   state_spec, state_spec, tail_spec, state_spec, state_spec, tail_spec),
        out_shape=(jax.ShapeDtypeStruct((t_rows, D_MODEL), F32), state, state, tail, state, state, tail),
        scratch_shapes=[pltpu.VMEM((CHUNK_ROWS, D_STATE), F32), pltpu.VMEM((CHUNK_ROWS, D_STATE), F32),
                        pltpu.VMEM((BATCH, D_STATE), F32), pltpu.VMEM((BATCH, D_STATE), F32),
                        pltpu.VMEM((CHUNK_ROWS + 2 * BATCH, D_CONV), F32),
                        pltpu.VMEM((2, CHUNK_T, BATCH, D_MODEL), F32), pltpu.SemaphoreType.DMA((2,))],
        compiler_params=pltpu.CompilerParams(dimension_semantics=("arbitrary",), vmem_limit_bytes=V7X_VMEM_LIMIT),
        name="mix",
    )(x_prompt, x_sample, h0r, h0i, cbuf, *weights)


def _route_kernel(h_ref, wrt_ref, bias_ref, tri_ref, ltri_ref, w_ref, slot_ref, ce_ref, crel_ref, cnt_ref, run_col):
    i = pl.program_id(0)
    tr = MOE_TILE

    @pl.when(i == 0)
    def _():
        run_col[...] = jnp.zeros_like(run_col)

    def split(a):
        hi = a.astype(BF16)
        return hi, (a - hi.astype(F32)).astype(BF16)

    nt_dot = lambda a, b: lax.dot_general(a, b, (((1,), (1,)), ((), ())), preferred_element_type=F32)
    w_hi, w_lo = split(wrt_ref[...])
    h_hi, h_lo = split(h_ref[...])
    logits = nt_dot(w_hi, h_hi) + (nt_dot(w_hi, h_lo) + nt_dot(w_lo, h_hi))
    scores = jax.nn.sigmoid(logits)
    sel = scores + bias_ref[...]

    sub = lax.broadcasted_iota(I32, (GROUP_SIZE, tr), 0).astype(F32)
    blocks, gscore = [], []
    for g in range(N_EXPERT_GROUPS):
        blk = sel[g * GROUP_SIZE:(g + 1) * GROUP_SIZE, :]
        m1 = jnp.max(blk, axis=0, keepdims=True)
        first = jnp.min(jnp.where(blk == m1, sub, float(GROUP_SIZE)), axis=0, keepdims=True)
        m2 = jnp.max(jnp.where(sub == first, NEG_INF, blk), axis=0, keepdims=True)
        blocks.append(blk)
        gscore.append(m1 + m2)
    masked = []
    for g in range(N_EXPERT_GROUPS):
        beaten = jnp.zeros((1, tr), F32)
        for o in range(N_EXPERT_GROUPS):
            if o == g:
                continue
            wins = gscore[o] >= gscore[g] if o < g else gscore[o] > gscore[g]
            beaten = beaten + wins.astype(F32)
        masked.append(jnp.where(beaten < float(TOPK_GROUPS), blocks[g], NEG_INF))
    masked = jnp.concatenate(masked, axis=0)

    row = lax.broadcasted_iota(I32, (N_EXPERTS, tr), 0).astype(F32)
    picked = jnp.zeros((N_EXPERTS, tr), F32)
    hots, gates = [], []
    for _ in range(TOP_K):
        m = jnp.max(masked, axis=0, keepdims=True)
        idx = jnp.min(jnp.where(masked == m, row, float(N_EXPERTS)), axis=0, keepdims=True)
        hot = row == idx
        hots.append(hot)
        gates.append(jnp.sum(jnp.where(hot, scores, 0.0), axis=0, keepdims=True))
        masked = jnp.where(hot, NEG_INF, masked)
        picked = picked + hot.astype(F32)
    total = gates[0]
    for k in range(1, TOP_K):
        total = total + gates[k]

    chunk = float(RUN_CHUNK)
    pb = picked.astype(BF16)
    earlier = _dot(pb, tri_ref[...])
    cnt_col = jnp.sum(picked, axis=1, keepdims=True)
    run_len = jnp.floor((cnt_col + (chunk - 1.0)) * (1.0 / chunk)) * chunk
    lower = _dot(ltri_ref[...], jnp.broadcast_to(run_len, (N_EXPERTS, LANES)).astype(BF16))[:, 0:1]
    slot_of = earlier + lower
    for k in range(TOP_K):
        w_ref[k:k + 1, :] = gates[k] / total * ROUTED_SCALE
        slot_ref[k:k + 1, :] = jnp.sum(jnp.where(hots[k], slot_of, 0.0), axis=0, keepdims=True).astype(I32)
    w_ref[TOP_K:, :] = jnp.zeros((8 - TOP_K, tr), F32)
    slot_ref[TOP_K:, :] = jnp.full((8 - TOP_K, tr), -1, I32)

    c_first = lax.broadcasted_iota(I32, (N_EXPERTS, N_CHUNKS), 1).astype(F32) * chunk
    owner = jnp.sum((lower + run_len <= c_first).astype(F32), axis=0, keepdims=True)
    hot_e = lax.broadcasted_iota(I32, (N_EXPERTS, N_CHUNKS), 0).astype(F32) == owner
    rel = jnp.sum(jnp.where(hot_e, run_col[...] - lower + c_first, 0.0), axis=0, keepdims=True)
    ce_ref[0] = jnp.broadcast_to(owner.astype(I32), (8, N_CHUNKS))
    crel_ref[0] = jnp.broadcast_to(rel.astype(I32), (8, N_CHUNKS))
    run_col[...] = run_col[...] + run_len
    cnt_ref[...] = jnp.broadcast_to(run_col[...], cnt_ref.shape)


def _route(h1, w_router, router_bias):
    t_rows = h1.shape[0]
    tr = MOE_TILE
    n_tiles = t_rows // tr
    tri = jnp.triu(jnp.ones((tr, tr), BF16), k=1)
    ltri = jnp.tril(jnp.ones((N_EXPERTS, N_EXPERTS), BF16), k=-1)
    tok = lambda i: (0, i)
    const2 = lambda i: (0, 0)
    per_tile = lambda i: (i, 0, 0)
    return pl.pallas_call(
        _route_kernel,
        grid=(n_tiles,),
        in_specs=[pl.BlockSpec((tr, D_MODEL), lambda i: (i, 0)),
                  pl.BlockSpec((N_EXPERTS, D_MODEL), const2),
                  pl.BlockSpec((N_EXPERTS, 1), const2),
                  pl.BlockSpec((tr, tr), const2),
                  pl.BlockSpec((N_EXPERTS, N_EXPERTS), const2)],
        out_specs=(pl.BlockSpec((8, tr), tok), pl.BlockSpec((8, tr), tok),
                   pl.BlockSpec((1, 8, N_CHUNKS), per_tile), pl.BlockSpec((1, 8, N_CHUNKS), per_tile),
                   pl.BlockSpec((N_EXPERTS, LANES), const2)),
        out_shape=(jax.ShapeDtypeStruct((8, t_rows), F32), jax.ShapeDtypeStruct((8, t_rows), I32),
                   jax.ShapeDtypeStruct((n_tiles, 8, N_CHUNKS), I32), jax.ShapeDtypeStruct((n_tiles, 8, N_CHUNKS), I32),
                   jax.ShapeDtypeStruct((N_EXPERTS, LANES), F32)),
        scratch_shapes=[pltpu.VMEM((N_EXPERTS, 1), F32)],
        compiler_params=pltpu.CompilerParams(dimension_semantics=("arbitrary",)),
        name="route",
    )(h1, w_router.T, router_bias.reshape(N_EXPERTS, 1), tri, ltri)


def _pos_kernel(n_blocks_pad, ccol_ref, crow_ref, be_ref, nb_ref, start_ref):
    blk = float(EXPERT_BLOCK)
    pad = lambda cnt: jnp.floor((cnt + (blk - 1.0)) * (1.0 / blk)) * blk
    padded_row = pad(crow_ref[0:1, :])
    lane = lax.broadcasted_iota(I32, (N_EXPERTS, 128), 1)
    sub = lax.broadcasted_iota(I32, (N_EXPERTS, 128), 0)
    start = jnp.sum(jnp.where(lane < sub, padded_row, 0.0), axis=1, keepdims=True)
    end = start + pad(ccol_ref[:, 0:1])

    first_row = lax.broadcasted_iota(I32, (N_EXPERTS, n_blocks_pad), 1).astype(F32) * blk
    owner = jnp.sum((end <= first_row).astype(F32), axis=0, keepdims=True)
    be_ref[...] = jnp.broadcast_to(jnp.minimum(owner, float(N_EXPERTS - 1)).astype(I32), be_ref.shape)
    used = jnp.max(end, axis=0, keepdims=True) * (1.0 / blk)
    nb_ref[...] = jnp.broadcast_to(used.astype(I32), nb_ref.shape)
    start_ref[...] = jnp.broadcast_to(start.astype(I32), start_ref.shape)


def _pos(cnt, n_blocks_pad):
    counts = cnt[:, 0]
    ccol = jnp.broadcast_to(counts[:, None], (N_EXPERTS, 128))
    crow = jnp.broadcast_to(jnp.pad(counts, (0, 128 - N_EXPERTS))[None, :], (8, 128))
    return pl.pallas_call(
        functools.partial(_pos_kernel, n_blocks_pad),
        out_shape=(jax.ShapeDtypeStruct((8, n_blocks_pad), I32), jax.ShapeDtypeStruct((8, 128), I32),
                   jax.ShapeDtypeStruct((N_EXPERTS, 128), I32)),
        name="pos",
    )(ccol, crow)


def _token_copy(src_ref, src_tok, dst_ref, dst_tok, sem, n_tok=1):
    rows = n_tok * TOKEN_ROWS
    src = pl.ds(pl.multiple_of(src_tok * TOKEN_ROWS, TOKEN_ROWS), rows)
    dst = pl.ds(pl.multiple_of(dst_tok * TOKEN_ROWS, TOKEN_ROWS), rows)
    return pltpu.make_async_copy(src_ref.at[src, :], dst_ref.at[dst, :], sem)


def _for_each_chunk(tile, start_ref, ce_ref, crel_ref, fn):
    def per_chunk(c, _):
        fn(c * RUN_CHUNK, start_ref[ce_ref[tile, c]] + crel_ref[tile, c])
        return 0

    lax.fori_loop(0, N_CHUNKS, per_chunk, 0, unroll=8)


def _store_token_tiles(dst_ref, x):
    n = x.shape[0]
    half = TOKEN_ROWS * LANES
    bits = lambda v: lax.bitcast_convert_type(v.astype(BF16).astype(F32), U32)
    for s in range(TOKEN_ROWS):
        hi = bits(x[:, s * LANES:(s + 1) * LANES])
        lo = bits(x[:, half + s * LANES:half + (s + 1) * LANES])
        dst_ref[pl.ds(s, n, stride=TOKEN_ROWS), :] = hi | (lo >> 16)


def _load_token_tiles(src_ref, n):
    his, los = [], []
    for s in range(TOKEN_ROWS):
        words = src_ref[pl.ds(s, n, stride=TOKEN_ROWS), :]
        his.append(lax.bitcast_convert_type(words & jnp.uint32(0xFFFF0000), F32))
        los.append(lax.bitcast_convert_type(words << 16, F32))
    return his, los


def _scatter_kernel(n_blocks, start_ref, count_ref, nb_ref, ce_ref, crel_ref, h_ref, slot_ref, xs_out,
                    tiles, zeros, sems, zsem):
    i = pl.program_id(0)
    ts = MOE_TILE
    blk_rows = EXPERT_BLOCK * TOKEN_ROWS

    @pl.when(i == 0)
    def _():
        zeros[...] = jnp.zeros_like(zeros)
        zero_block = lambda b: pltpu.make_async_copy(
            zeros, xs_out.at[pl.ds(pl.multiple_of(b * blk_rows, blk_rows), blk_rows), :], zsem)

        def pad_rows(e, n_pad):
            lo = start_ref[e] + count_ref[e]
            hi = start_ref[e] + ((count_ref[e] + (EXPERT_BLOCK - 1)) & (-EXPERT_BLOCK))

            def zero_row(r, _):
                _token_copy(zeros, 0, xs_out, r, zsem).start()
                return 0

            lax.fori_loop(lo, hi, zero_row, 0)
            return n_pad + (hi - lo)

        n_pad = lax.fori_loop(0, N_EXPERTS, pad_rows, 0)

        def start_block(b, _):
            zero_block(b).start()
            return 0

        lax.fori_loop(nb_ref[0], n_blocks, start_block, 0)

        def wait_row(r, _):
            _token_copy(zeros, 0, xs_out, 0, zsem).wait()
            return 0

        lax.fori_loop(0, n_pad, wait_row, 0)

        def wait_block(b, _):
            zero_block(0).wait()
            return 0

        lax.fori_loop(nb_ref[0], n_blocks, wait_block, 0)

    def drain(half):
        pltpu.make_async_copy(tiles.at[half], xs_out.at[pl.ds(0, LOCAL_SLOTS * TOKEN_ROWS), :], sems.at[half]).wait()

    for half in range(2):
        @pl.when(i > 0)
        def _():
            drain(half)

        h = h_ref[half * ts:(half + 1) * ts, :].astype(BF16)
        slots = slot_ref[:, half * ts:(half + 1) * ts]
        for jb in range(LOCAL_SLOTS // ts):
            j = lax.broadcasted_iota(I32, (ts, ts), 0) + jb * ts
            perm = jnp.zeros((ts, ts), F32)
            for k in range(TOP_K):
                perm = perm + jnp.where(j == slots[k:k + 1, :], 1.0, 0.0)
            _store_token_tiles(tiles.at[half, pl.ds(jb * ts * TOKEN_ROWS, ts * TOKEN_ROWS), :],
                               _dot(perm.astype(BF16), h))

        _for_each_chunk(
            2 * i + half, start_ref, ce_ref, crel_ref,
            lambda local, sorted_row, half=half: _token_copy(tiles.at[half], local, xs_out, sorted_row,
                                                             sems.at[half], RUN_CHUNK).start())

    @pl.when(i == pl.num_programs(0) - 1)
    def _():
        drain(0)
        drain(1)


def _scatter(h1, slots, starts, counts, n_used, chunk_e, chunk_rel, n_rows):
    t_rows = h1.shape[0]
    step = 2 * MOE_TILE
    n_blocks = n_rows // EXPERT_BLOCK
    return pl.pallas_call(
        functools.partial(_scatter_kernel, n_blocks),
        grid_spec=pltpu.PrefetchScalarGridSpec(
            num_scalar_prefetch=5,
            grid=(t_rows // step,),
            in_specs=[pl.BlockSpec((step, D_MODEL), lambda i, *_: (i, 0)),
                      pl.BlockSpec((8, step), lambda i, *_: (0, i))],
            out_specs=pl.BlockSpec(memory_space=pl.ANY),
            scratch_shapes=[pltpu.VMEM((2, LOCAL_SLOTS * TOKEN_ROWS, LANES), U32),
                            pltpu.VMEM((EXPERT_BLOCK * TOKEN_ROWS, LANES), U32),
                            pltpu.SemaphoreType.DMA((2,)), pltpu.SemaphoreType.DMA(())],
        ),
        out_shape=jax.ShapeDtypeStruct((n_rows * TOKEN_ROWS, LANES), U32),
        compiler_params=pltpu.CompilerParams(dimension_semantics=("arbitrary",), vmem_limit_bytes=V7X_VMEM_LIMIT),
        name="scatter",
    )(starts, counts, n_used, chunk_e, chunk_rel, h1, slots)


def _experts_kernel(be_ref, nb_ref, x_ref, wg_ref, wu_ref, wd_ref, y_ref, wg_bf, wu_bf, wd_bf):
    b = pl.program_id(0)

    @pl.when((b == 0) | (be_ref[b] != be_ref[jnp.maximum(b - 1, 0)]))
    def _():
        wg_bf[...] = wg_ref[...].astype(BF16)
        wu_bf[...] = wu_ref[...].astype(BF16)
        wd_bf[...] = wd_ref[...].astype(BF16)

    @pl.when(b < nb_ref[0])
    def _():
        his, los = _load_token_tiles(x_ref, EXPERT_BLOCK)
        xb = jnp.concatenate(his + los, axis=-1).astype(BF16)
        gate = _dot(xb, wg_bf[...])
        up = _dot(xb, wu_bf[...])
        hid = (gate * jax.nn.sigmoid(gate) * up).astype(BF16)
        _store_token_tiles(y_ref, _dot(hid, wd_bf[...]))

    @pl.when(b >= nb_ref[0])
    def _():
        y_ref[...] = jnp.zeros_like(y_ref)


def _experts(xs, block_e, n_used, w_gate, w_up, w_down):
    blk_rows = EXPERT_BLOCK * TOKEN_ROWS
    return pl.pallas_call(
        _experts_kernel,
        grid_spec=pltpu.PrefetchScalarGridSpec(
            num_scalar_prefetch=2,
            grid=(xs.shape[0] // blk_rows,),
            in_specs=[pl.BlockSpec((blk_rows, LANES), lambda b, be, nb: (b, 0)),
                      pl.BlockSpec((None, D_MODEL, D_EXPERT), lambda b, be, nb: (be[b], 0, 0)),
                      pl.BlockSpec((None, D_MODEL, D_EXPERT), lambda b, be, nb: (be[b], 0, 0)),
                      pl.BlockSpec((None, D_EXPERT, D_MODEL), lambda b, be, nb: (be[b], 0, 0))],
            out_specs=pl.BlockSpec((blk_rows, LANES), lambda b, be, nb: (b, 0)),
            scratch_shapes=[pltpu.VMEM((D_MODEL, D_EXPERT), BF16), pltpu.VMEM((D_MODEL, D_EXPERT), BF16),
                            pltpu.VMEM((D_EXPERT, D_MODEL), BF16)],
        ),
        out_shape=jax.ShapeDtypeStruct(xs.shape, U32),
        compiler_params=pltpu.CompilerParams(dimension_semantics=("arbitrary",)),
        name="experts",
    )(block_e, n_used, xs, w_gate, w_up, w_down)


def _combine_kernel(n_prompt_tiles, start_ref, ce_ref, crel_ref, h_ref, w_ref, slot_ref, ys_hbm,
                    wsg_ref, wsu_ref, wsd_ref, g2_ref, b2_ref, yp_hbm, ysm_hbm, ybuf, obuf, gsems, osems):
    i = pl.program_id(0)
    n_steps = pl.num_programs(0)
    tc = MOE_TILE
    tile_t = tc // BATCH

    def gather(tile, slot):
        _for_each_chunk(
            tile, start_ref, ce_ref, crel_ref,
            lambda local, sorted_row: _token_copy(ys_hbm, sorted_row, ybuf.at[slot], local,
                                                  gsems.at[slot], RUN_CHUNK).start())

    def gather_wait(slot):
        pltpu.make_async_copy(ys_hbm.at[pl.ds(0, LOCAL_SLOTS * TOKEN_ROWS), :], ybuf.at[slot], gsems.at[slot]).wait()

    def out_copies(dst_hbm, t0, slot):
        return [pltpu.make_async_copy(obuf.at[slot, :, b, :], dst_hbm.at[b, pl.ds(t0, tile_t), :], osems.at[slot])
                for b in range(BATCH)]

    def out_wait(slot):
        for cp in out_copies(yp_hbm, 0, slot):
            cp.wait()

    def compute(half, slot):
        rows = slice(half * tc, (half + 1) * tc)
        h = h_ref[rows, :]
        hb = h.astype(BF16)
        gate = _dot(hb, wsg_ref[...])
        up = _dot(hb, wsu_ref[...])
        shared = _dot((gate * jax.nn.sigmoid(gate) * up).astype(BF16), wsd_ref[...])

        eye = (lax.broadcasted_iota(I32, (tc, tc), 0) == lax.broadcasted_iota(I32, (tc, tc), 1)).astype(F32)
        to_cols = lambda a: lax.dot_general(eye, a, (((1,), (1,)), ((), ())),
                                            precision=lax.Precision.HIGHEST, preferred_element_type=F32)
        w_cols = to_cols(w_ref[:, rows])
        s_cols = to_cols(slot_ref[:, rows].astype(F32)).astype(I32)

        j = lax.broadcasted_iota(I32, (tc, LOCAL_SLOTS), 1)
        mix_w = jnp.zeros((tc, LOCAL_SLOTS), F32)
        for k in range(TOP_K):
            mix_w = mix_w + jnp.where(j == s_cols[:, k:k + 1], w_cols[:, k:k + 1], 0.0)
        mix_w = mix_w.astype(BF16)
        gather_wait(slot)
        his, los = _load_token_tiles(ybuf.at[slot], LOCAL_SLOTS)
        pieces = his + los
        routed = jnp.concatenate(
            [_dot(mix_w, jnp.concatenate(pieces[p:p + 2], axis=-1).astype(BF16)) for p in range(0, len(pieces), 2)],
            axis=-1)
        out = _layer_norm(DEEPNORM_ALPHA * h + (routed + shared), g2_ref[...], b2_ref[...])
        obuf[slot] = out.reshape(tile_t, BATCH, D_MODEL)

        g = 2 * i + half

        @pl.when(g < n_prompt_tiles)
        def _():
            for cp in out_copies(yp_hbm, pl.multiple_of(g * tile_t, tile_t), slot):
                cp.start()

        @pl.when(g >= n_prompt_tiles)
        def _():
            for cp in out_copies(ysm_hbm, pl.multiple_of((g - n_prompt_tiles) * tile_t, tile_t), slot):
                cp.start()

    @pl.when(i == 0)
    def _():
        gather(0, 0)

    gather(2 * i + 1, 1)

    @pl.when(i > 0)
    def _():
        out_wait(0)

    compute(0, 0)

    @pl.when(i + 1 < n_steps)
    def _():
        gather(2 * i + 2, 0)

    @pl.when(i > 0)
    def _():
        out_wait(1)

    compute(1, 1)

    @pl.when(i + 1 == n_steps)
    def _():
        out_wait(0)
        out_wait(1)


def _combine(h1, gate_w, slots, ys, starts, chunk_e, chunk_rel, ws_gate, ws_up, ws_down, ln2_g, ln2_b,
             prompt_shape, sample_shape):
    t_rows = h1.shape[0]
    tc = MOE_TILE
    step = 2 * tc
    n_prompt_tiles = prompt_shape[0] * prompt_shape[1] // tc
    const2 = lambda i, *_: (0, 0)
    any_spec = pl.BlockSpec(memory_space=pl.ANY)
    return pl.pallas_call(
        functools.partial(_combine_kernel, n_prompt_tiles),
        grid_spec=pltpu.PrefetchScalarGridSpec(
            num_scalar_prefetch=3,
            grid=(t_rows // step,),
            in_specs=[pl.BlockSpec((step, D_MODEL), lambda i, *_: (i, 0)),
                      pl.BlockSpec((8, step), lambda i, *_: (0, i)),
                      pl.BlockSpec((8, step), lambda i, *_: (0, i)),
                      any_spec,
                      pl.BlockSpec((D_MODEL, D_EXPERT), const2),
                      pl.BlockSpec((D_MODEL, D_EXPERT), const2),
                      pl.BlockSpec((D_EXPERT, D_MODEL), const2),
                      pl.BlockSpec((1, D_MODEL), const2),
                      pl.BlockSpec((1, D_MODEL), const2)],
            out_specs=(any_spec, any_spec),
            scratch_shapes=[pltpu.VMEM((2, LOCAL_SLOTS * TOKEN_ROWS, LANES), U32),
                            pltpu.VMEM((2, tc // BATCH, BATCH, D_MODEL), F32),
                            pltpu.SemaphoreType.DMA((2,)), pltpu.SemaphoreType.DMA((2,))],
        ),
        out_shape=(jax.ShapeDtypeStruct(prompt_shape, F32), jax.ShapeDtypeStruct(sample_shape, F32)),
        compiler_params=pltpu.CompilerParams(dimension_semantics=("arbitrary",), vmem_limit_bytes=V7X_VMEM_LIMIT),
        name="combine",
    )(starts, chunk_e, chunk_rel, h1, gate_w, slots, ys, ws_gate.astype(BF16), ws_up.astype(BF16),
      ws_down.astype(BF16), ln2_g.reshape(1, D_MODEL), ln2_b.reshape(1, D_MODEL))


def _time_major(x):
    b, l, d = x.shape
    return jnp.transpose(x, (1, 0, 2)).reshape(l * b, d)


def _batch_major(x, b):
    return jnp.transpose(x.reshape(x.shape[0] // b, b, x.shape[1]), (1, 0, 2))


def _block_diag(blocks):
    n, r, c = blocks.shape
    eye = jnp.eye(n, dtype=blocks.dtype)
    return (blocks[:, :, None, :] * eye[:, None, :, None]).reshape(n * r, n * c)


def kernel(x_prompt, x_sample, state_ssm_re, state_ssm_im, cache_conv, ln_in_g, ln_in_b, w_in, lam_re, lam_im, log_dt, ssm_b_re, ssm_b_im, ssm_c_re, ssm_c_im, ssm_d, w_glu, b_glu, conv_w, beta_ssm, beta_conv, w_out, ln1_g, ln1_b, w_router, router_bias, w_gate, w_up, w_down, ws_gate, ws_up, ws_down, ln2_g, ln2_b):
    bp, lp, _ = x_prompt.shape
    bs, ls, _ = x_sample.shape
    assert bp == BATCH and bs == BATCH and ls == CHUNK_T and lp % CHUNK_T == 0
    assert w_in.shape[0] == 1, "single-layer model"
    n_prompt = bp * lp
    row = lambda a: a.reshape(1, -1)

    a_re, a_im, bb_re, bb_im = _prep(lam_re[0], lam_im[0], log_dt[0], ssm_b_re[0], ssm_b_im[0])
    groups_in = 128 // SSM_GROUP
    groups_out = 256 // SSM_GROUP
    bbd = lambda bb: jnp.stack([_block_diag(bb[j * groups_in:(j + 1) * groups_in])
                                for j in range(N_GROUPS // groups_in)]).astype(BF16)
    ct = lambda cc: jnp.transpose(cc, (0, 2, 1))
    cbd = lambda cc: jnp.stack([_block_diag(ct(cc)[j * groups_out:(j + 1) * groups_out])
                                for j in range(N_GROUPS // groups_out)]).astype(BF16)
    mix_weights = (row(ln_in_g), row(ln_in_b), w_in[0].astype(BF16), row(a_re), row(a_im),
                   bbd(bb_re), bbd(bb_im), cbd(ssm_c_re[0]), cbd(-ssm_c_im[0]), row(ssm_d[0]),
                   w_glu[0].astype(BF16), row(b_glu[0]), conv_w[0], row(beta_ssm[0]), row(beta_conv[0]),
                   w_out[0].astype(BF16), row(ln1_g[0]), row(ln1_b[0]))

    h0r = state_ssm_re[0].reshape(BATCH, D_STATE)
    h0i = state_ssm_im[0].reshape(BATCH, D_STATE)
    cbuf = _time_major(cache_conv[0])
    h1, p_re, p_im, p_conv, s_re, s_im, s_conv = _mix(x_prompt, x_sample, h0r, h0i, cbuf, mix_weights)

    t_rows = h1.shape[0]
    assert t_rows % (2 * MOE_TILE) == 0 and n_prompt % MOE_TILE == 0
    n_tiles = t_rows // MOE_TILE
    max_rows = t_rows * TOP_K + n_tiles * N_EXPERTS * (RUN_CHUNK - 1) + N_EXPERTS * EXPERT_BLOCK
    n_blocks = -(-max_rows // EXPERT_BLOCK) + 1
    n_blocks_pad = -(-n_blocks // 128) * 128
    gate_w, slots, chunk_e, chunk_rel, cnt = _route(h1, w_router[0], router_bias[0])
    block_e, n_used, starts = _pos(cnt, n_blocks_pad)
    n_used, chunk_e, chunk_rel = n_used[0, :1], chunk_e[:, 0], chunk_rel[:, 0]
    starts = jnp.concatenate([starts[:, 0], jnp.full((1,), (n_blocks - 1) * EXPERT_BLOCK, I32)])
    xs = _scatter(h1, slots, starts, cnt[:, 0].astype(I32), n_used, chunk_e, chunk_rel, n_blocks * EXPERT_BLOCK)
    ys = _experts(xs, block_e[0, :n_blocks], n_used, w_gate[0], w_up[0], w_down[0])
    y_prompt, y_sample = _combine(h1, gate_w, slots, ys, starts, chunk_e, chunk_rel, ws_gate[0], ws_up[0], ws_down[0],
                                  ln2_g[0], ln2_b[0], x_prompt.shape, x_sample.shape)

    st = lambda s: s.reshape(1, BATCH, N_GROUPS, SSM_STATE)
    cv = lambda t: _batch_major(t, BATCH)[None]
    return (y_prompt, y_sample, st(p_re), st(p_im), cv(p_conv), st(s_re), st(s_im), cv(s_conv))
```

```python
import functools
import math

import jax
import jax.numpy as jnp
from jax import lax
from jax.experimental import pallas as pl
from jax.experimental.pallas import tpu as pltpu

F32 = jnp.float32
BF16 = jnp.bfloat16
I32 = jnp.int32
U32 = jnp.uint32

D_MODEL = 1024
D_SSM = 512
D_CONV = 512
SSM_GROUP = 16
N_GROUPS = 32
SSM_STATE = 64
D_STATE = N_GROUPS * SSM_STATE
N_EXPERTS = 64
TOP_K = 6
N_EXPERT_GROUPS = 8
GROUP_SIZE = N_EXPERTS // N_EXPERT_GROUPS
TOPK_GROUPS = 4
D_EXPERT = 256
ROUTED_SCALE = 2.5
DEPTH = 1
DEEPNORM_ALPHA = (2.0 * DEPTH) ** 0.25
LN_EPS = 1e-5
RMS_EPS = 1e-6

BATCH = 8
CHUNK_T = 64
CHUNK_ROWS = CHUNK_T * BATCH
SCAN_COLS = 512
MOE_TILE = 256
TILE_SHIFT = MOE_TILE.bit_length() - 1
RUN_CHUNK = 8
LOCAL_SLOTS = TOP_K * MOE_TILE + N_EXPERTS * RUN_CHUNK
N_CHUNKS = LOCAL_SLOTS // RUN_CHUNK
EXPERT_BLOCK = 512
LANES = 128
TOKEN_ROWS = D_MODEL // (2 * LANES)
V7X_VMEM_LIMIT = 56 * 1024 * 1024
NEG_INF = float("-inf")


def _layer_norm(x, g, b):
    mu = jnp.mean(x, axis=-1, keepdims=True)
    xc = x - mu
    var = jnp.mean(xc * xc, axis=-1, keepdims=True)
    return xc * lax.rsqrt(var + LN_EPS) * g + b


def _rms_norm(x, g):
    return x * lax.rsqrt(jnp.mean(x * x, axis=-1, keepdims=True) + RMS_EPS) * g


def _dot(a, b):
    return jnp.dot(a, b, preferred_element_type=F32)


def _discretise(lr, li, log_dt):
    dt = jnp.exp(log_dt)
    mag = jnp.exp(lr * dt)
    ar = mag * jnp.cos(li * dt)
    ai = mag * jnp.sin(li * dt)
    den = lr * lr + li * li
    qr = ((ar - 1.0) * lr + ai * li) / den
    qi = (ai * lr - (ar - 1.0) * li) / den
    return ar, ai, qr, qi


def _prep_kernel(lr_ref, li_ref, ldt_ref, lrc_ref, lic_ref, ldtc_ref, br_ref, bi_ref,
                 ar_ref, ai_ref, bbr_ref, bbi_ref):
    ar, ai, _, _ = _discretise(lr_ref[...], li_ref[...], ldt_ref[...])
    ar_ref[...] = ar
    ai_ref[...] = ai
    _, _, qr, qi = _discretise(lrc_ref[...], lic_ref[...], ldtc_ref[...])
    br = br_ref[...]
    bi = bi_ref[...]
    bbr_ref[...] = qr * br - qi * bi
    bbi_ref[...] = qr * bi + qi * br


def _prep(lam_re, lam_im, log_dt, b_re, b_im):
    g, p = lam_re.shape
    per_channel = lambda a: jnp.repeat(a, SSM_GROUP, axis=0)
    rows = lambda b: jnp.transpose(b, (0, 2, 1)).reshape(g * SSM_GROUP, p)
    ldt = log_dt.reshape(g, 1)
    ar, ai, bbr, bbi = pl.pallas_call(
        _prep_kernel,
        out_shape=(jax.ShapeDtypeStruct((g, p), F32), jax.ShapeDtypeStruct((g, p), F32),
                   jax.ShapeDtypeStruct((g * SSM_GROUP, p), F32), jax.ShapeDtypeStruct((g * SSM_GROUP, p), F32)),
        name="prep",
    )(lam_re, lam_im, ldt, per_channel(lam_re), per_channel(lam_im), per_channel(ldt), rows(b_re), rows(b_im))
    return ar, ai, bbr.reshape(g, SSM_GROUP, p), bbi.reshape(g, SSM_GROUP, p)


def _chunk_copies(src_hbm, t0, xbuf, slot, sems):
    return [pltpu.make_async_copy(src_hbm.at[b, pl.ds(t0, CHUNK_T), :], xbuf.at[slot, :, b, :], sems.at[slot])
            for b in range(BATCH)]


def _mix_kernel(n_prompt_chunks,
                xp_hbm, xs_hbm, h0r_ref, h0i_ref, cbuf_ref, lng_ref, lnb_ref, win_ref, ar_ref, ai_ref,
                bbr_ref, bbi_ref, cbr_ref, cbi_ref, dsk_ref, wglu_ref, bglu_ref, cw_ref,
                bs_ref, bc_ref, wout_ref, g1_ref, b1_ref,
                h1_ref, pr_ref, pi_ref, pc_ref, sr_ref, si_ref, sc_ref,
                str_ref, sti_ref, hr_ref, hi_ref, cv_ref, xbuf, xsems):
    c = pl.program_id(0)
    n = CHUNK_ROWS
    carry_rows = 2 * BATCH
    slot = c % 2

    @pl.when(c == 0)
    def _():
        for cp in _chunk_copies(xp_hbm, 0, xbuf, 0, xsems):
            cp.start()

    @pl.when(c + 1 < n_prompt_chunks)
    def _():
        for cp in _chunk_copies(xp_hbm, pl.multiple_of((c + 1) * CHUNK_T, CHUNK_T), xbuf, 1 - slot, xsems):
            cp.start()

    @pl.when(c + 1 == n_prompt_chunks)
    def _():
        for cp in _chunk_copies(xs_hbm, 0, xbuf, 1 - slot, xsems):
            cp.start()

    @pl.when(c == 0)
    def _():
        hr_ref[...] = jnp.zeros_like(hr_ref)
        hi_ref[...] = jnp.zeros_like(hi_ref)
        cv_ref[0:carry_rows, :] = jnp.zeros((carry_rows, D_CONV), F32)

    @pl.when(c == n_prompt_chunks)
    def _():
        hr_ref[...] = h0r_ref[...]
        hi_ref[...] = h0i_ref[...]
        cv_ref[0:carry_rows, :] = cbuf_ref[...]

    for cp in _chunk_copies(xp_hbm, 0, xbuf, slot, xsems):
        cp.wait()
    h = _layer_norm(xbuf[slot].reshape(n, D_MODEL), lng_ref[...], lnb_ref[...])
    hb = h.astype(BF16)

    u = _dot(hb, win_ref[:, 0:D_SSM])
    ub = u.astype(BF16)
    u_tile = 128
    s_tile = u_tile // SSM_GROUP * SSM_STATE
    for j in range(D_SSM // u_tile):
        uj = ub[:, j * u_tile:(j + 1) * u_tile]
        str_ref[:, j * s_tile:(j + 1) * s_tile] = _dot(uj, bbr_ref[j])
        sti_ref[:, j * s_tile:(j + 1) * s_tile] = _dot(uj, bbi_ref[j])

    for cb in range(D_STATE // SCAN_COLS):
        cols = slice(cb * SCAN_COLS, (cb + 1) * SCAN_COLS)
        ar = jnp.broadcast_to(ar_ref[:, cols], (BATCH, SCAN_COLS))
        ai = jnp.broadcast_to(ai_ref[:, cols], (BATCH, SCAN_COLS))

        def step(t, carry, cols=cols, ar=ar, ai=ai):
            sr, si = carry
            r0 = pl.multiple_of(t * BATCH, BATCH)
            nr = ar * sr - ai * si + str_ref[pl.ds(r0, BATCH), cols]
            ni = ar * si + ai * sr + sti_ref[pl.ds(r0, BATCH), cols]
            str_ref[pl.ds(r0, BATCH), cols] = nr
            sti_ref[pl.ds(r0, BATCH), cols] = ni
            return nr, ni

        sr, si = lax.fori_loop(0, CHUNK_T, step, (hr_ref[:, cols], hi_ref[:, cols]), unroll=8)
        hr_ref[:, cols] = sr
        hi_ref[:, cols] = si

    k_tile = 1024
    ys = []
    for j in range(D_STATE // k_tile):
        sl = slice(j * k_tile, (j + 1) * k_tile)
        ys.append(_dot(str_ref[:, sl].astype(BF16), cbr_ref[j]) + _dot(sti_ref[:, sl].astype(BF16), cbi_ref[j]))
    y_ssm = jnp.concatenate(ys, axis=-1) + dsk_ref[...] * u
    g = jax.nn.gelu(y_ssm)
    y_ssm = g * jax.nn.sigmoid(_dot(g.astype(BF16), wglu_ref[...]) + bglu_ref[...])
    mix = _dot(_rms_norm(y_ssm, bs_ref[...]).astype(BF16), wout_ref[0:D_SSM, :])

    gate_b = _dot(hb, win_ref[:, D_SSM:D_SSM + D_CONV])
    gate_c = _dot(hb, win_ref[:, D_SSM + D_CONV:D_SSM + 2 * D_CONV])
    v = _dot(hb, win_ref[:, D_SSM + 2 * D_CONV:D_SSM + 3 * D_CONV])
    cv_ref[carry_rows:carry_rows + n, :] = gate_c * v
    y_conv = gate_b * (cv_ref[0:n, :] * cw_ref[0:1, :]
                       + cv_ref[BATCH:BATCH + n, :] * cw_ref[1:2, :]
                       + cv_ref[carry_rows:carry_rows + n, :] * cw_ref[2:3, :])
    tail = cv_ref[n:n + carry_rows, :]
    cv_ref[0:carry_rows, :] = tail
    mix = mix + _dot(_rms_norm(y_conv, bc_ref[...]).astype(BF16), wout_ref[D_SSM:D_SSM + D_CONV, :])

    h1_ref[...] = _layer_norm(DEEPNORM_ALPHA * h + mix, g1_ref[...], b1_ref[...])

    @pl.when(c == n_prompt_chunks - 1)
    def _():
        pr_ref[...] = hr_ref[...]
        pi_ref[...] = hi_ref[...]
        pc_ref[...] = tail

    @pl.when(c == n_prompt_chunks)
    def _():
        sr_ref[...] = hr_ref[...]
        si_ref[...] = hi_ref[...]
        sc_ref[...] = tail


def _mix(x_prompt, x_sample, h0r, h0i, cbuf, weights):
    n_prompt_chunks = x_prompt.shape[1] // CHUNK_T
    n_chunks = n_prompt_chunks + 1
    t_rows = n_chunks * CHUNK_ROWS
    any_spec = pl.BlockSpec(memory_space=pl.ANY)
    const2 = lambda c: (0, 0)
    const3 = lambda c: (0, 0, 0)
    w_specs = [pl.BlockSpec(w.shape, const3 if w.ndim == 3 else const2) for w in weights]
    state = jax.ShapeDtypeStruct((BATCH, D_STATE), F32)
    tail = jax.ShapeDtypeStruct((2 * BATCH, D_CONV), F32)
    state_spec = pl.BlockSpec((BATCH, D_STATE), const2)
    tail_spec = pl.BlockSpec((2 * BATCH, D_CONV), const2)
    return pl.pallas_call(
        functools.partial(_mix_kernel, n_prompt_chunks),
        grid=(n_chunks,),
        in_specs=[any_spec, any_spec, state_spec, state_spec, tail_spec] + w_specs,
        out_specs=(pl.BlockSpec((CHUNK_ROWS, D_MODEL), lambda c: (c, 0)),
                   state_spec, state_spec, tail_spec, state_spec, state_spec, tail_spec),
        out_shape=(jax.ShapeDtypeStruct((t_rows, D_MODEL), F32), state, state, tail, state, state, tail),
        scratch_shapes=[pltpu.VMEM((CHUNK_ROWS, D_STATE), F32), pltpu.VMEM((CHUNK_ROWS, D_STATE), F32),
                        pltpu.VMEM((BATCH, D_STATE), F32), pltpu.VMEM((BATCH, D_STATE), F32),
                        pltpu.VMEM((CHUNK_ROWS + 2 * BATCH, D_CONV), F32),
                        pltpu.VMEM((2, CHUNK_T, BATCH, D_MODEL), F32), pltpu.SemaphoreType.DMA((2,))],
        compiler_params=pltpu.CompilerParams(dimension_semantics=("arbitrary",), vmem_limit_bytes=V7X_VMEM_LIMIT),
        name="mix",
    )(x_prompt, x_sample, h0r, h0i, cbuf, *weights)


def _route_kernel(h_ref, wrt_ref, bias_ref, tri_ref, ltri_ref, w_ref, slot_ref, ce_ref, crel_ref, cnt_ref, run_col):
    i = pl.program_id(0)
    tr = MOE_TILE

    @pl.when(i == 0)
    def _():
        run_col[...] = jnp.zeros_like(run_col)

    def split(a):
        hi = a.astype(BF16)
        return hi, (a - hi.astype(F32)).astype(BF16)

    nt_dot = lambda a, b: lax.dot_general(a, b, (((1,), (1,)), ((), ())), preferred_element_type=F32)
    w_hi, w_lo = split(wrt_ref[...])
    h_hi, h_lo = split(h_ref[...])
    logits = nt_dot(w_hi, h_hi) + (nt_dot(w_hi, h_lo) + nt_dot(w_lo, h_hi))
    scores = jax.nn.sigmoid(logits)
    sel = scores + bias_ref[...]

    sub = lax.broadcasted_iota(I32, (GROUP_SIZE, tr), 0).astype(F32)
    blocks, gscore = [], []
    for g in range(N_EXPERT_GROUPS):
        blk = sel[g * GROUP_SIZE:(g + 1) * GROUP_SIZE, :]
        m1 = jnp.max(blk, axis=0, keepdims=True)
        first = jnp.min(jnp.where(blk == m1, sub, float(GROUP_SIZE)), axis=0, keepdims=True)
        m2 = jnp.max(jnp.where(sub == first, NEG_INF, blk), axis=0, keepdims=True)
        blocks.append(blk)
        gscore.append(m1 + m2)
    masked = []
    for g in range(N_EXPERT_GROUPS):
        beaten = jnp.zeros((1, tr), F32)
        for o in range(N_EXPERT_GROUPS):
            if o == g:
                continue
            wins = gscore[o] >= gscore[g] if o < g else gscore[o] > gscore[g]
            beaten = beaten + wins.astype(F32)
        masked.append(jnp.where(beaten < float(TOPK_GROUPS), blocks[g], NEG_INF))
    masked = jnp.concatenate(masked, axis=0)

    row = lax.broadcasted_iota(I32, (N_EXPERTS, tr), 0).astype(F32)
    picked = jnp.zeros((N_EXPERTS, tr), F32)
    hots, gates = [], []
    for _ in range(TOP_K):
        m = jnp.max(masked, axis=0, keepdims=True)
        idx = jnp.min(jnp.where(masked == m, row, float(N_EXPERTS)), axis=0, keepdims=True)
        hot = row == idx
        hots.append(hot)
        gates.append(jnp.sum(jnp.where(hot, scores, 0.0), axis=0, keepdims=True))
        masked = jnp.where(hot, NEG_INF, masked)
        picked = picked + hot.astype(F32)
    total = gates[0]
    for k in range(1, TOP_K):
        total = total + gates[k]

    chunk = float(RUN_CHUNK)
    pb = picked.astype(BF16)
    earlier = _dot(pb, tri_ref[...])
    cnt_col = jnp.sum(picked, axis=1, keepdims=True)
    run_len = jnp.floor((cnt_col + (chunk - 1.0)) * (1.0 / chunk)) * chunk
    lower = _dot(ltri_ref[...], jnp.broadcast_to(run_len, (N_EXPERTS, LANES)).astype(BF16))[:, 0:1]
    slot_of = earlier + lower
    for k in range(TOP_K):
        w_ref[k:k + 1, :] = gates[k] / total * ROUTED_SCALE
        slot_ref[k:k + 1, :] = jnp.sum(jnp.where(hots[k], slot_of, 0.0), axis=0, keepdims=True).astype(I32)
    w_ref[TOP_K:, :] = jnp.zeros((8 - TOP_K, tr), F32)
    slot_ref[TOP_K:, :] = jnp.full((8 - TOP_K, tr), -1, I32)

    c_first = lax.broadcasted_iota(I32, (N_EXPERTS, N_CHUNKS), 1).astype(F32) * chunk
    owner = jnp.sum((lower + run_len <= c_first).astype(F32), axis=0, keepdims=True)
    hot_e = lax.broadcasted_iota(I32, (N_EXPERTS, N_CHUNKS), 0).astype(F32) == owner
    rel = jnp.sum(jnp.where(hot_e, run_col[...] - lower + c_first, 0.0), axis=0, keepdims=True)
    ce_ref[0] = jnp.broadcast_to(owner.astype(I32), (8, N_CHUNKS))
    crel_ref[0] = jnp.broadcast_to(rel.astype(I32), (8, N_CHUNKS))
    run_col[...] = run_col[...] + run_len
    cnt_ref[...] = jnp.broadcast_to(run_col[...], cnt_ref.shape)


def _route(h1, w_router, router_bias):
    t_rows = h1.shape[0]
    tr = MOE_TILE
    n_tiles = t_rows // tr
    tri = jnp.triu(jnp.ones((tr, tr), BF16), k=1)
    ltri = jnp.tril(jnp.ones((N_EXPERTS, N_EXPERTS), BF16), k=-1)
    tok = lambda i: (0, i)
    const2 = lambda i: (0, 0)
    per_tile = lambda i: (i, 0, 0)
    return pl.pallas_call(
        _route_kernel,
        grid=(n_tiles,),
        in_specs=[pl.BlockSpec((tr, D_MODEL), lambda i: (i, 0)),
                  pl.BlockSpec((N_EXPERTS, D_MODEL), const2),
                  pl.BlockSpec((N_EXPERTS, 1), const2),
                  pl.BlockSpec((tr, tr), const2),
                  pl.BlockSpec((N_EXPERTS, N_EXPERTS), const2)],
        out_specs=(pl.BlockSpec((8, tr), tok), pl.BlockSpec((8, tr), tok),
                   pl.BlockSpec((1, 8, N_CHUNKS), per_tile), pl.BlockSpec((1, 8, N_CHUNKS), per_tile),
                   pl.BlockSpec((N_EXPERTS, LANES), const2)),
        out_shape=(jax.ShapeDtypeStruct((8, t_rows), F32), jax.ShapeDtypeStruct((8, t_rows), I32),
                   jax.ShapeDtypeStruct((n_tiles, 8, N_CHUNKS), I32), jax.ShapeDtypeStruct((n_tiles, 8, N_CHUNKS), I32),
                   jax.ShapeDtypeStruct((N_EXPERTS, LANES), F32)),
        scratch_shapes=[pltpu.VMEM((N_EXPERTS, 1), F32)],
        compiler_params=pltpu.CompilerParams(dimension_semantics=("arbitrary",)),
        name="route",
    )(h1, w_router.T, router_bias.reshape(N_EXPERTS, 1), tri, ltri)


def _pos_kernel(n_blocks_pad, ccol_ref, crow_ref, be_ref, nb_ref, start_ref):
    blk = float(EXPERT_BLOCK)
    pad = lambda cnt: jnp.floor((cnt + (blk - 1.0)) * (1.0 / blk)) * blk
    padded_row = pad(crow_ref[0:1, :])
    lane = lax.broadcasted_iota(I32, (N_EXPERTS, 128), 1)
    sub = lax.broadcasted_iota(I32, (N_EXPERTS, 128), 0)
    start = jnp.sum(jnp.where(lane < sub, padded_row, 0.0), axis=1, keepdims=True)
    end = start + pad(ccol_ref[:, 0:1])

    first_row = lax.broadcasted_iota(I32, (N_EXPERTS, n_blocks_pad), 1).astype(F32) * blk
    owner = jnp.sum((end <= first_row).astype(F32), axis=0, keepdims=True)
    be_ref[...] = jnp.broadcast_to(jnp.minimum(owner, float(N_EXPERTS - 1)).astype(I32), be_ref.shape)
    used = jnp.max(end, axis=0, keepdims=True) * (1.0 / blk)
    nb_ref[...] = jnp.broadcast_to(used.astype(I32), nb_ref.shape)
    start_ref[...] = jnp.broadcast_to(start.astype(I32), start_ref.shape)


def _pos(cnt, n_blocks_pad):
    counts = cnt[:, 0]
    ccol = jnp.broadcast_to(counts[:, None], (N_EXPERTS, 128))
    crow = jnp.broadcast_to(jnp.pad(counts, (0, 128 - N_EXPERTS))[None, :], (8, 128))
    return pl.pallas_call(
        functools.partial(_pos_kernel, n_blocks_pad),
        out_shape=(jax.ShapeDtypeStruct((8, n_blocks_pad), I32), jax.ShapeDtypeStruct((8, 128), I32),
                   jax.ShapeDtypeStruct((N_EXPERTS, 128), I32)),
        name="pos",
    )(ccol, crow)


def _token_copy(src_ref, src_tok, dst_ref, dst_tok, sem, n_tok=1):
    rows = n_tok * TOKEN_ROWS
    src = pl.ds(pl.multiple_of(src_tok * TOKEN_ROWS, TOKEN_ROWS), rows)
    dst = pl.ds(pl.multiple_of(dst_tok * TOKEN_ROWS, TOKEN_ROWS), rows)
    return pltpu.make_async_copy(src_ref.at[src, :], dst_ref.at[dst, :], sem)


def _for_each_chunk(tile, start_ref, ce_ref, crel_ref, fn):
    def per_chunk(c, _):
        fn(c * RUN_CHUNK, start_ref[ce_ref[tile, c]] + crel_ref[tile, c])
        return 0

    lax.fori_loop(0, N_CHUNKS, per_chunk, 0, unroll=8)


def _store_token_tiles(dst_ref, x):
    n = x.shape[0]
    half = TOKEN_ROWS * LANES
    bits = lambda v: lax.bitcast_convert_type(v.astype(BF16).astype(F32), U32)
    for s in range(TOKEN_ROWS):
        hi = bits(x[:, s * LANES:(s + 1) * LANES])
        lo = bits(x[:, half + s * LANES:half + (s + 1) * LANES])
        dst_ref[pl.ds(s, n, stride=TOKEN_ROWS), :] = hi | (lo >> 16)


def _load_token_tiles(src_ref, n):
    his, los = [], []
    for s in range(TOKEN_ROWS):
        words = src_ref[pl.ds(s, n, stride=TOKEN_ROWS), :]
        his.append(lax.bitcast_convert_type(words & jnp.uint32(0xFFFF0000), F32))
        los.append(lax.bitcast_convert_type(words << 16, F32))
    return his, los


def _scatter_kernel(n_blocks, start_ref, count_ref, nb_ref, ce_ref, crel_ref, h_ref, slot_ref, xs_out,
                    tiles, zeros, sems, zsem):
    i = pl.program_id(0)
    ts = MOE_TILE
    blk_rows = EXPERT_BLOCK * TOKEN_ROWS

    @pl.when(i == 0)
    def _():
        zeros[...] = jnp.zeros_like(zeros)
        zero_block = lambda b: pltpu.make_async_copy(
            zeros, xs_out.at[pl.ds(pl.multiple_of(b * blk_rows, blk_rows), blk_rows), :], zsem)

        def pad_rows(e, n_pad):
            lo = (start_ref[e] + count_ref[e]) // RUN_CHUNK
            hi = (start_ref[e] + ((count_ref[e] + (EXPERT_BLOCK - 1)) & (-EXPERT_BLOCK))) // RUN_CHUNK

            def zero_chunk(c, _):
                _token_copy(zeros, 0, xs_out, c * RUN_CHUNK, zsem, RUN_CHUNK).start()
                return 0

            lax.fori_loop(lo, hi, zero_chunk, 0)
            return n_pad + (hi - lo)

        n_pad = lax.fori_loop(0, N_EXPERTS, pad_rows, 0)

        def start_block(b, _):
            zero_block(b).start()
            return 0

        lax.fori_loop(nb_ref[0], n_blocks, start_block, 0)

        def wait_chunk(c, _):
            _token_copy(zeros, 0, xs_out, 0, zsem, RUN_CHUNK).wait()
            return 0

        lax.fori_loop(0, n_pad, wait_chunk, 0)

        def wait_block(b, _):
            zero_block(0).wait()
            return 0

        lax.fori_loop(nb_ref[0], n_blocks, wait_block, 0)

    def drain(half):
        pltpu.make_async_copy(tiles.at[half], xs_out.at[pl.ds(0, LOCAL_SLOTS * TOKEN_ROWS), :], sems.at[half]).wait()

    for half in range(2):
        @pl.when(i > 0)
        def _():
            drain(half)

        h = h_ref[half * ts:(half + 1) * ts, :].astype(BF16)
        slots = slot_ref[:, half * ts:(half + 1) * ts]
        sub = ts // 2
        for jb in range(LOCAL_SLOTS // ts):
            in_block = jnp.where(slots >> TILE_SHIFT == jb, slots & (ts - 1), -1).astype(F32).astype(BF16)
            parts = []
            for r in range(ts // sub):
                j = (lax.broadcasted_iota(I32, (sub, ts), 0) + r * sub).astype(F32).astype(BF16)
                perm = jnp.zeros((sub, ts), BF16)
                for k in range(TOP_K):
                    perm = perm + jnp.where(j == in_block[k:k + 1, :], jnp.ones((), BF16), jnp.zeros((), BF16))
                parts.append(perm)
            _store_token_tiles(tiles.at[half, pl.ds(jb * ts * TOKEN_ROWS, ts * TOKEN_ROWS), :],
                               _dot(jnp.concatenate(parts, axis=0), h))

        _for_each_chunk(
            2 * i + half, start_ref, ce_ref, crel_ref,
            lambda local, sorted_row, half=half: _token_copy(tiles.at[half], local, xs_out, sorted_row,
                                                             sems.at[half], RUN_CHUNK).start())

    @pl.when(i == pl.num_programs(0) - 1)
    def _():
        drain(0)
        drain(1)


def _scatter(h1, slots, starts, counts, n_used, chunk_e, chunk_rel, n_rows):
    t_rows = h1.shape[0]
    step = 2 * MOE_TILE
    n_blocks = n_rows // EXPERT_BLOCK
    return pl.pallas_call(
        functools.partial(_scatter_kernel, n_blocks),
        grid_spec=pltpu.PrefetchScalarGridSpec(
            num_scalar_prefetch=5,
            grid=(t_rows // step,),
            in_specs=[pl.BlockSpec((step, D_MODEL), lambda i, *_: (i, 0)),
                      pl.BlockSpec((8, step), lambda i, *_: (0, i))],
            out_specs=pl.BlockSpec(memory_space=pl.ANY),
            scratch_shapes=[pltpu.VMEM((2, LOCAL_SLOTS * TOKEN_ROWS, LANES), U32),
                            pltpu.VMEM((EXPERT_BLOCK * TOKEN_ROWS, LANES), U32),
                            pltpu.SemaphoreType.DMA((2,)), pltpu.SemaphoreType.DMA(())],
        ),
        out_shape=jax.ShapeDtypeStruct((n_rows * TOKEN_ROWS, LANES), U32),
        compiler_params=pltpu.CompilerParams(dimension_semantics=("arbitrary",), vmem_limit_bytes=V7X_VMEM_LIMIT),
        name="scatter",
    )(starts, counts, n_used, chunk_e, chunk_rel, h1, slots)


def _experts_kernel(be_ref, nb_ref, x_ref, wg_ref, wu_ref, wd_ref, y_ref, wg_bf, wu_bf, wd_bf):
    b = pl.program_id(0)

    @pl.when((b == 0) | (be_ref[b] != be_ref[jnp.maximum(b - 1, 0)]))
    def _():
        wg_bf[...] = wg_ref[...].astype(BF16)
        wu_bf[...] = wu_ref[...].astype(BF16)
        wd_bf[...] = wd_ref[...].astype(BF16)

    @pl.when(b < nb_ref[0])
    def _():
        his, los = _load_token_tiles(x_ref, EXPERT_BLOCK)
        xb = jnp.concatenate(his + los, axis=-1).astype(BF16)
        gate = _dot(xb, wg_bf[...])
        up = _dot(xb, wu_bf[...])
        hid = (gate * jax.nn.sigmoid(gate) * up).astype(BF16)
        _store_token_tiles(y_ref, _dot(hid, wd_bf[...]))

    @pl.when(b >= nb_ref[0])
    def _():
        y_ref[...] = jnp.zeros_like(y_ref)


def _experts(xs, block_e, n_used, w_gate, w_up, w_down):
    blk_rows = EXPERT_BLOCK * TOKEN_ROWS
    return pl.pallas_call(
        _experts_kernel,
        grid_spec=pltpu.PrefetchScalarGridSpec(
            num_scalar_prefetch=2,
            grid=(xs.shape[0] // blk_rows,),
            in_specs=[pl.BlockSpec((blk_rows, LANES), lambda b, be, nb: (jnp.minimum(b, jnp.maximum(nb[0], 1) - 1), 0)),
                      pl.BlockSpec((None, D_MODEL, D_EXPERT), lambda b, be, nb: (be[b], 0, 0)),
                      pl.BlockSpec((None, D_MODEL, D_EXPERT), lambda b, be, nb: (be[b], 0, 0)),
                      pl.BlockSpec((None, D_EXPERT, D_MODEL), lambda b, be, nb: (be[b], 0, 0))],
            out_specs=pl.BlockSpec((blk_rows, LANES), lambda b, be, nb: (b, 0)),
            scratch_shapes=[pltpu.VMEM((D_MODEL, D_EXPERT), BF16), pltpu.VMEM((D_MODEL, D_EXPERT), BF16),
                            pltpu.VMEM((D_EXPERT, D_MODEL), BF16)],
        ),
        out_shape=jax.ShapeDtypeStruct(xs.shape, U32),
        compiler_params=pltpu.CompilerParams(dimension_semantics=("arbitrary",)),
        name="experts",
    )(block_e, n_used, xs, w_gate, w_up, w_down)


def _combine_kernel(n_prompt_tiles, start_ref, ce_ref, crel_ref, h_ref, w_ref, slot_ref, ys_hbm,
                    wsg_ref, wsu_ref, wsd_ref, g2_ref, b2_ref, yp_hbm, ysm_hbm, ybuf, obuf, gsems, osems):
    i = pl.program_id(0)
    n_steps = pl.num_programs(0)
    tc = MOE_TILE
    tile_t = tc // BATCH

    def gather(tile, slot):
        _for_each_chunk(
            tile, start_ref, ce_ref, crel_ref,
            lambda local, sorted_row: _token_copy(ys_hbm, sorted_row, ybuf.at[slot], local,
                                                  gsems.at[slot], RUN_CHUNK).start())

    def gather_wait(slot):
        pltpu.make_async_copy(ys_hbm.at[pl.ds(0, LOCAL_SLOTS * TOKEN_ROWS), :], ybuf.at[slot], gsems.at[slot]).wait()

    def out_copies(dst_hbm, t0, slot):
        return [pltpu.make_async_copy(obuf.at[slot, :, b, :], dst_hbm.at[b, pl.ds(t0, tile_t), :], osems.at[slot])
                for b in range(BATCH)]

    def out_wait(slot):
        for cp in out_copies(yp_hbm, 0, slot):
            cp.wait()

    def compute(half, slot):
        rows = slice(half * tc, (half + 1) * tc)
        h = h_ref[rows, :]
        hb = h.astype(BF16)
        gate = _dot(hb, wsg_ref[...])
        up = _dot(hb, wsu_ref[...])
        shared = _dot((gate * jax.nn.sigmoid(gate) * up).astype(BF16), wsd_ref[...])

        eye = (lax.broadcasted_iota(I32, (tc, tc), 0) == lax.broadcasted_iota(I32, (tc, tc), 1)).astype(F32)
        to_cols = lambda a: lax.dot_general(eye, a, (((1,), (1,)), ((), ())),
                                            precision=lax.Precision.HIGHEST, preferred_element_type=F32)
        w_cols = to_cols(w_ref[:, rows])
        s_cols = to_cols(slot_ref[:, rows].astype(F32)).astype(I32)

        w_bf = w_cols.astype(BF16)
        j = lax.broadcasted_iota(I32, (tc, tc), 1).astype(F32).astype(BF16)
        blocks = []
        for lb in range(LOCAL_SLOTS // tc):
            in_block = jnp.where(s_cols >> TILE_SHIFT == lb, s_cols & (tc - 1), -1).astype(F32).astype(BF16)
            blk = jnp.zeros((tc, tc), BF16)
            for k in range(TOP_K):
                blk = blk + jnp.where(j == in_block[:, k:k + 1], w_bf[:, k:k + 1], jnp.zeros((), BF16))
            blocks.append(blk)
        mix_w = jnp.concatenate(blocks, axis=-1)
        gather_wait(slot)
        his, los = _load_token_tiles(ybuf.at[slot], LOCAL_SLOTS)
        pieces = his + los
        routed = jnp.concatenate(
            [_dot(mix_w, jnp.concatenate(pieces[p:p + 2], axis=-1).astype(BF16)) for p in range(0, len(pieces), 2)],
            axis=-1)
        out = _layer_norm(DEEPNORM_ALPHA * h + (routed + shared), g2_ref[...], b2_ref[...])
        obuf[slot] = out.reshape(tile_t, BATCH, D_MODEL)

        g = 2 * i + half

        @pl.when(g < n_prompt_tiles)
        def _():
            for cp in out_copies(yp_hbm, pl.multiple_of(g * tile_t, tile_t), slot):
                cp.start()

        @pl.when(g >= n_prompt_tiles)
        def _():
            for cp in out_copies(ysm_hbm, pl.multiple_of((g - n_prompt_tiles) * tile_t, tile_t), slot):
                cp.start()

    @pl.when(i == 0)
    def _():
        gather(0, 0)

    gather(2 * i + 1, 1)

    @pl.when(i > 0)
    def _():
        out_wait(0)

    compute(0, 0)

    @pl.when(i + 1 < n_steps)
    def _():
        gather(2 * i + 2, 0)

    @pl.when(i > 0)
    def _():
        out_wait(1)

    compute(1, 1)

    @pl.when(i + 1 == n_steps)
    def _():
        out_wait(0)
        out_wait(1)


def _combine(h1, gate_w, slots, ys, starts, chunk_e, chunk_rel, ws_gate, ws_up, ws_down, ln2_g, ln2_b,
             prompt_shape, sample_shape):
    t_rows = h1.shape[0]
    tc = MOE_TILE
    step = 2 * tc
    n_prompt_tiles = prompt_shape[0] * prompt_shape[1] // tc
    const2 = lambda i, *_: (0, 0)
    any_spec = pl.BlockSpec(memory_space=pl.ANY)
    return pl.pallas_call(
        functools.partial(_combine_kernel, n_prompt_tiles),
        grid_spec=pltpu.PrefetchScalarGridSpec(
            num_scalar_prefetch=3,
            grid=(t_rows // step,),
            in_specs=[pl.BlockSpec((step, D_MODEL), lambda i, *_: (i, 0)),
                      pl.BlockSpec((8, step), lambda i, *_: (0, i)),
                      pl.BlockSpec((8, step), lambda i, *_: (0, i)),
                      any_spec,
                      pl.BlockSpec((D_MODEL, D_EXPERT), const2),
                      pl.BlockSpec((D_MODEL, D_EXPERT), const2),
                      pl.BlockSpec((D_EXPERT, D_MODEL), const2),
                      pl.BlockSpec((1, D_MODEL), const2),
                      pl.BlockSpec((1, D_MODEL), const2)],
            out_specs=(any_spec, any_spec),
            scratch_shapes=[pltpu.VMEM((2, LOCAL_SLOTS * TOKEN_ROWS, LANES), U32),
                            pltpu.VMEM((2, tc // BATCH, BATCH, D_MODEL), F32),
                            pltpu.SemaphoreType.DMA((2,)), pltpu.SemaphoreType.DMA((2,))],
        ),
        out_shape=(jax.ShapeDtypeStruct(prompt_shape, F32), jax.ShapeDtypeStruct(sample_shape, F32)),
        compiler_params=pltpu.CompilerParams(dimension_semantics=("arbitrary",), vmem_limit_bytes=V7X_VMEM_LIMIT),
        name="combine",
    )(starts, chunk_e, chunk_rel, h1, gate_w, slots, ys, ws_gate.astype(BF16), ws_up.astype(BF16),
      ws_down.astype(BF16), ln2_g.reshape(1, D_MODEL), ln2_b.reshape(1, D_MODEL))


def _time_major(x):
    b, l, d = x.shape
    return jnp.transpose(x, (1, 0, 2)).reshape(l * b, d)


def _batch_major(x, b):
    return jnp.transpose(x.reshape(x.shape[0] // b, b, x.shape[1]), (1, 0, 2))


def _block_diag(blocks):
    n, r, c = blocks.shape
    eye = jnp.eye(n, dtype=blocks.dtype)
    return (blocks[:, :, None, :] * eye[:, None, :, None]).reshape(n * r, n * c)


def kernel(x_prompt, x_sample, state_ssm_re, state_ssm_im, cache_conv, ln_in_g, ln_in_b, w_in, lam_re, lam_im, log_dt, ssm_b_re, ssm_b_im, ssm_c_re, ssm_c_im, ssm_d, w_glu, b_glu, conv_w, beta_ssm, beta_conv, w_out, ln1_g, ln1_b, w_router, router_bias, w_gate, w_up, w_down, ws_gate, ws_up, ws_down, ln2_g, ln2_b):
    bp, lp, _ = x_prompt.shape
    bs, ls, _ = x_sample.shape
    assert bp == BATCH and bs == BATCH and ls == CHUNK_T and lp % CHUNK_T == 0
    assert w_in.shape[0] == 1, "single-layer model"
    n_prompt = bp * lp
    row = lambda a: a.reshape(1, -1)

    a_re, a_im, bb_re, bb_im = _prep(lam_re[0], lam_im[0], log_dt[0], ssm_b_re[0], ssm_b_im[0])
    groups_in = 128 // SSM_GROUP
    groups_out = 256 // SSM_GROUP
    bbd = lambda bb: jnp.stack([_block_diag(bb[j * groups_in:(j + 1) * groups_in])
                                for j in range(N_GROUPS // groups_in)]).astype(BF16)
    ct = lambda cc: jnp.transpose(cc, (0, 2, 1))
    cbd = lambda cc: jnp.stack([_block_diag(ct(cc)[j * groups_out:(j + 1) * groups_out])
                                for j in range(N_GROUPS // groups_out)]).astype(BF16)
    mix_weights = (row(ln_in_g), row(ln_in_b), w_in[0].astype(BF16), row(a_re), row(a_im),
                   bbd(bb_re), bbd(bb_im), cbd(ssm_c_re[0]), cbd(-ssm_c_im[0]), row(ssm_d[0]),
                   w_glu[0].astype(BF16), row(b_glu[0]), conv_w[0], row(beta_ssm[0]), row(beta_conv[0]),
                   w_out[0].astype(BF16), row(ln1_g[0]), row(ln1_b[0]))

    h0r = state_ssm_re[0].reshape(BATCH, D_STATE)
    h0i = state_ssm_im[0].reshape(BATCH, D_STATE)
    cbuf = _time_major(cache_conv[0])
    h1, p_re, p_im, p_conv, s_re, s_im, s_conv = _mix(x_prompt, x_sample, h0r, h0i, cbuf, mix_weights)

    t_rows = h1.shape[0]
    assert t_rows % (2 * MOE_TILE) == 0 and n_prompt % MOE_TILE == 0
    n_tiles = t_rows // MOE_TILE
    max_rows = t_rows * TOP_K + n_tiles * N_EXPERTS * (RUN_CHUNK - 1) + N_EXPERTS * EXPERT_BLOCK
    n_blocks = -(-max_rows // EXPERT_BLOCK) + 1
    n_blocks_pad = -(-n_blocks // 128) * 128
    gate_w, slots, chunk_e, chunk_rel, cnt = _route(h1, w_router[0], router_bias[0])
    block_e, n_used, starts = _pos(cnt, n_blocks_pad)
    n_used, chunk_e, chunk_rel = n_used[0, :1], chunk_e[:, 0], chunk_rel[:, 0]
    starts = jnp.concatenate([starts[:, 0], jnp.full((1,), (n_blocks - 1) * EXPERT_BLOCK, I32)])
    xs = _scatter(h1, slots, starts, cnt[:, 0].astype(I32), n_used, chunk_e, chunk_rel, n_blocks * EXPERT_BLOCK)
    ys = _experts(xs, block_e[0, :n_blocks], n_used, w_gate[0], w_up[0], w_down[0])
    y_prompt, y_sample = _combine(h1, gate_w, slots, ys, starts, chunk_e, chunk_rel, ws_gate[0], ws_up[0], ws_down[0],
                                  ln2_g[0], ln2_b[0], x_prompt.shape, x_sample.shape)

    st = lambda s: s.reshape(1, BATCH, N_GROUPS, SSM_STATE)
    cv = lambda t: _batch_major(t, BATCH)[None]
    return (y_prompt, y_sample, st(p_re), st(p_im), cv(p_conv), st(s_re), st(s_im), cv(s_conv))
```

```python
import functools
import math

import jax
import jax.numpy as jnp
from jax import lax
from jax.experimental import pallas as pl
from jax.experimental.pallas import tpu as pltpu

F32 = jnp.float32
BF16 = jnp.bfloat16
I32 = jnp.int32

D_MODEL = 1024
D_SSM = 512
D_CONV = 512
SSM_GROUP = 16
N_GROUPS = 32
SSM_STATE = 64
D_STATE = N_GROUPS * SSM_STATE
N_EXPERTS = 64
TOP_K = 6
N_EXPERT_GROUPS = 8
GROUP_SIZE = N_EXPERTS // N_EXPERT_GROUPS
TOPK_GROUPS = 4
D_EXPERT = 256
ROUTED_SCALE = 2.5
DEPTH = 1
DEEPNORM_ALPHA = (2.0 * DEPTH) ** 0.25
LN_EPS = 1e-5
RMS_EPS = 1e-6

BATCH = 8
CHUNK_T = 64
CHUNK_ROWS = CHUNK_T * BATCH
SCAN_COLS = 512
MOE_TILE = 256
TILE_SHIFT = MOE_TILE.bit_length() - 1
RUN_CHUNK = 16
LOCAL_SLOTS = TOP_K * MOE_TILE + N_EXPERTS * RUN_CHUNK
N_CHUNKS = LOCAL_SLOTS // RUN_CHUNK
EXPERT_BLOCK = 512
LANES = 128
V7X_VMEM_LIMIT = 56 * 1024 * 1024
NEG_INF = float("-inf")


def _layer_norm(x, g, b):
    mu = jnp.mean(x, axis=-1, keepdims=True)
    xc = x - mu
    var = jnp.mean(xc * xc, axis=-1, keepdims=True)
    return xc * lax.rsqrt(var + LN_EPS) * g + b


def _rms_norm(x, g):
    return x * lax.rsqrt(jnp.mean(x * x, axis=-1, keepdims=True) + RMS_EPS) * g


def _dot(a, b):
    return jnp.dot(a, b, preferred_element_type=F32)


def _discretise(lr, li, log_dt):
    dt = jnp.exp(log_dt)
    mag = jnp.exp(lr * dt)
    ar = mag * jnp.cos(li * dt)
    ai = mag * jnp.sin(li * dt)
    den = lr * lr + li * li
    qr = ((ar - 1.0) * lr + ai * li) / den
    qi = (ai * lr - (ar - 1.0) * li) / den
    return ar, ai, qr, qi


def _prep_kernel(lr_ref, li_ref, ldt_ref, lrc_ref, lic_ref, ldtc_ref, br_ref, bi_ref,
                 ar_ref, ai_ref, bbr_ref, bbi_ref):
    ar, ai, _, _ = _discretise(lr_ref[...], li_ref[...], ldt_ref[...])
    ar_ref[...] = ar
    ai_ref[...] = ai
    _, _, qr, qi = _discretise(lrc_ref[...], lic_ref[...], ldtc_ref[...])
    br = br_ref[...]
    bi = bi_ref[...]
    bbr_ref[...] = qr * br - qi * bi
    bbi_ref[...] = qr * bi + qi * br


def _prep(lam_re, lam_im, log_dt, b_re, b_im):
    g, p = lam_re.shape
    per_channel = lambda a: jnp.repeat(a, SSM_GROUP, axis=0)
    rows = lambda b: jnp.transpose(b, (0, 2, 1)).reshape(g * SSM_GROUP, p)
    ldt = log_dt.reshape(g, 1)
    ar, ai, bbr, bbi = pl.pallas_call(
        _prep_kernel,
        out_shape=(jax.ShapeDtypeStruct((g, p), F32), jax.ShapeDtypeStruct((g, p), F32),
                   jax.ShapeDtypeStruct((g * SSM_GROUP, p), F32), jax.ShapeDtypeStruct((g * SSM_GROUP, p), F32)),
        name="prep",
    )(lam_re, lam_im, ldt, per_channel(lam_re), per_channel(lam_im), per_channel(ldt), rows(b_re), rows(b_im))
    return ar, ai, bbr.reshape(g, SSM_GROUP, p), bbi.reshape(g, SSM_GROUP, p)


def _chunk_copies(src_hbm, t0, xbuf, slot, sems):
    return [pltpu.make_async_copy(src_hbm.at[b, pl.ds(t0, CHUNK_T), :], xbuf.at[slot, :, b, :], sems.at[slot])
            for b in range(BATCH)]


def _mix_kernel(n_prompt_chunks,
                xp_hbm, xs_hbm, h0r_ref, h0i_ref, cbuf_ref, lng_ref, lnb_ref, win_ref, ar_ref, ai_ref,
                bbr_ref, bbi_ref, cbr_ref, cbi_ref, dsk_ref, wglu_ref, bglu_ref, cw_ref,
                bs_ref, bc_ref, wout_ref, g1_ref, b1_ref,
                h1_ref, pr_ref, pi_ref, pc_ref, sr_ref, si_ref, sc_ref,
                str_ref, sti_ref, hr_ref, hi_ref, cv_ref, xbuf, xsems):
    c = pl.program_id(0)
    n = CHUNK_ROWS
    carry_rows = 2 * BATCH
    slot = c % 2

    @pl.when(c == 0)
    def _():
        for cp in _chunk_copies(xp_hbm, 0, xbuf, 0, xsems):
            cp.start()

    @pl.when(c + 1 < n_prompt_chunks)
    def _():
        for cp in _chunk_copies(xp_hbm, pl.multiple_of((c + 1) * CHUNK_T, CHUNK_T), xbuf, 1 - slot, xsems):
            cp.start()

    @pl.when(c + 1 == n_prompt_chunks)
    def _():
        for cp in _chunk_copies(xs_hbm, 0, xbuf, 1 - slot, xsems):
            cp.start()

    @pl.when(c == 0)
    def _():
        hr_ref[...] = jnp.zeros_like(hr_ref)
        hi_ref[...] = jnp.zeros_like(hi_ref)
        cv_ref[0:carry_rows, :] = jnp.zeros((carry_rows, D_CONV), F32)

    @pl.when(c == n_prompt_chunks)
    def _():
        hr_ref[...] = h0r_ref[...]
        hi_ref[...] = h0i_ref[...]
        cv_ref[0:carry_rows, :] = cbuf_ref[...]

    for cp in _chunk_copies(xp_hbm, 0, xbuf, slot, xsems):
        cp.wait()
    h = _layer_norm(xbuf[slot].reshape(n, D_MODEL), lng_ref[...], lnb_ref[...])
    hb = h.astype(BF16)

    u = _dot(hb, win_ref[:, 0:D_SSM])
    ub = u.astype(BF16)
    u_tile = 128
    s_tile = u_tile // SSM_GROUP * SSM_STATE
    for j in range(D_SSM // u_tile):
        uj = ub[:, j * u_tile:(j + 1) * u_tile]
        str_ref[:, j * s_tile:(j + 1) * s_tile] = _dot(uj, bbr_ref[j])
        sti_ref[:, j * s_tile:(j + 1) * s_tile] = _dot(uj, bbi_ref[j])

    for cb in range(D_STATE // SCAN_COLS):
        cols = slice(cb * SCAN_COLS, (cb + 1) * SCAN_COLS)
        ar = jnp.broadcast_to(ar_ref[:, cols], (BATCH, SCAN_COLS))
        ai = jnp.broadcast_to(ai_ref[:, cols], (BATCH, SCAN_COLS))

        def step(t, carry, cols=cols, ar=ar, ai=ai):
            sr, si = carry
            r0 = pl.multiple_of(t * BATCH, BATCH)
            nr = ar * sr - ai * si + str_ref[pl.ds(r0, BATCH), cols]
            ni = ar * si + ai * sr + sti_ref[pl.ds(r0, BATCH), cols]
            str_ref[pl.ds(r0, BATCH), cols] = nr
            sti_ref[pl.ds(r0, BATCH), cols] = ni
            return nr, ni

        sr, si = lax.fori_loop(0, CHUNK_T, step, (hr_ref[:, cols], hi_ref[:, cols]), unroll=8)
        hr_ref[:, cols] = sr
        hi_ref[:, cols] = si

    k_tile = 1024
    ys = []
    for j in range(D_STATE // k_tile):
        sl = slice(j * k_tile, (j + 1) * k_tile)
        ys.append(_dot(str_ref[:, sl].astype(BF16), cbr_ref[j]) + _dot(sti_ref[:, sl].astype(BF16), cbi_ref[j]))
    y_ssm = jnp.concatenate(ys, axis=-1) + dsk_ref[...] * u
    g = jax.nn.gelu(y_ssm)
    y_ssm = g * jax.nn.sigmoid(_dot(g.astype(BF16), wglu_ref[...]) + bglu_ref[...])
    mix = _dot(_rms_norm(y_ssm, bs_ref[...]).astype(BF16), wout_ref[0:D_SSM, :])

    gate_b = _dot(hb, win_ref[:, D_SSM:D_SSM + D_CONV])
    gate_c = _dot(hb, win_ref[:, D_SSM + D_CONV:D_SSM + 2 * D_CONV])
    v = _dot(hb, win_ref[:, D_SSM + 2 * D_CONV:D_SSM + 3 * D_CONV])
    cv_ref[carry_rows:carry_rows + n, :] = gate_c * v
    y_conv = gate_b * (cv_ref[0:n, :] * cw_ref[0:1, :]
                       + cv_ref[BATCH:BATCH + n, :] * cw_ref[1:2, :]
                       + cv_ref[carry_rows:carry_rows + n, :] * cw_ref[2:3, :])
    tail = cv_ref[n:n + carry_rows, :]
    cv_ref[0:carry_rows, :] = tail
    mix = mix + _dot(_rms_norm(y_conv, bc_ref[...]).astype(BF16), wout_ref[D_SSM:D_SSM + D_CONV, :])

    h1_ref[...] = _layer_norm(DEEPNORM_ALPHA * h + mix, g1_ref[...], b1_ref[...])

    @pl.when(c == n_prompt_chunks - 1)
    def _():
        pr_ref[...] = hr_ref[...]
        pi_ref[...] = hi_ref[...]
        pc_ref[...] = tail

    @pl.when(c == n_prompt_chunks)
    def _():
        sr_ref[...] = hr_ref[...]
        si_ref[...] = hi_ref[...]
        sc_ref[...] = tail


def _mix(x_prompt, x_sample, h0r, h0i, cbuf, weights):
    n_prompt_chunks = x_prompt.shape[1] // CHUNK_T
    n_chunks = n_prompt_chunks + 1
    t_rows = n_chunks * CHUNK_ROWS
    any_spec = pl.BlockSpec(memory_space=pl.ANY)
    const2 = lambda c: (0, 0)
    const3 = lambda c: (0, 0, 0)
    w_specs = [pl.BlockSpec(w.shape, const3 if w.ndim == 3 else const2) for w in weights]
    state = jax.ShapeDtypeStruct((BATCH, D_STATE), F32)
    tail = jax.ShapeDtypeStruct((2 * BATCH, D_CONV), F32)
    state_spec = pl.BlockSpec((BATCH, D_STATE), const2)
    tail_spec = pl.BlockSpec((2 * BATCH, D_CONV), const2)
    return pl.pallas_call(
        functools.partial(_mix_kernel, n_prompt_chunks),
        grid=(n_chunks,),
        in_specs=[any_spec, any_spec, state_spec, state_spec, tail_spec] + w_specs,
        out_specs=(pl.BlockSpec((CHUNK_ROWS, D_MODEL), lambda c: (c, 0)),
                   state_spec, state_spec, tail_spec, state_spec, state_spec, tail_spec),
        out_shape=(jax.ShapeDtypeStruct((t_rows, D_MODEL), F32), state, state, tail, state, state, tail),
        scratch_shapes=[pltpu.VMEM((CHUNK_ROWS, D_STATE), F32), pltpu.VMEM((CHUNK_ROWS, D_STATE), F32),
                        pltpu.VMEM((BATCH, D_STATE), F32), pltpu.VMEM((BATCH, D_STATE), F32),
                        pltpu.VMEM((CHUNK_ROWS + 2 * BATCH, D_CONV), F32),
                        pltpu.VMEM((2, CHUNK_T, BATCH, D_MODEL), F32), pltpu.SemaphoreType.DMA((2,))],
        compiler_params=pltpu.CompilerParams(dimension_semantics=("arbitrary",), vmem_limit_bytes=V7X_VMEM_LIMIT),
        name="mix",
    )(x_prompt, x_sample, h0r, h0i, cbuf, *weights)


def _route_kernel(h_ref, wrt_ref, bias_ref, tri_ref, ltri_ref, w_ref, slot_ref, ce_ref, crel_ref, cnt_ref, run_col):
    i = pl.program_id(0)
    tr = MOE_TILE

    @pl.when(i == 0)
    def _():
        run_col[...] = jnp.zeros_like(run_col)

    def split(a):
        hi = a.astype(BF16)
        return hi, (a - hi.astype(F32)).astype(BF16)

    nt_dot = lambda a, b: lax.dot_general(a, b, (((1,), (1,)), ((), ())), preferred_element_type=F32)
    w_hi, w_lo = split(wrt_ref[...])
    h_hi, h_lo = split(h_ref[...])
    logits = nt_dot(w_hi, h_hi) + (nt_dot(w_hi, h_lo) + nt_dot(w_lo, h_hi))
    scores = jax.nn.sigmoid(logits)
    sel = scores + bias_ref[...]

    sub = lax.broadcasted_iota(I32, (GROUP_SIZE, tr), 0).astype(F32)
    blocks, gscore = [], []
    for g in range(N_EXPERT_GROUPS):
        blk = sel[g * GROUP_SIZE:(g + 1) * GROUP_SIZE, :]
        m1 = jnp.max(blk, axis=0, keepdims=True)
        first = jnp.min(jnp.where(blk == m1, sub, float(GROUP_SIZE)), axis=0, keepdims=True)
        m2 = jnp.max(jnp.where(sub == first, NEG_INF, blk), axis=0, keepdims=True)
        blocks.append(blk)
        gscore.append(m1 + m2)
    masked = []
    for g in range(N_EXPERT_GROUPS):
        beaten = jnp.zeros((1, tr), F32)
        for o in range(N_EXPERT_GROUPS):
            if o == g:
                continue
            wins = gscore[o] >= gscore[g] if o < g else gscore[o] > gscore[g]
            beaten = beaten + wins.astype(F32)
        masked.append(jnp.where(beaten < float(TOPK_GROUPS), blocks[g], NEG_INF))
    masked = jnp.concatenate(masked, axis=0)

    row = lax.broadcasted_iota(I32, (N_EXPERTS, tr), 0).astype(F32)
    picked = jnp.zeros((N_EXPERTS, tr), F32)
    hots, gates = [], []
    for _ in range(TOP_K):
        m = jnp.max(masked, axis=0, keepdims=True)
        idx = jnp.min(jnp.where(masked == m, row, float(N_EXPERTS)), axis=0, keepdims=True)
        hot = row == idx
        hots.append(hot)
        gates.append(jnp.sum(jnp.where(hot, scores, 0.0), axis=0, keepdims=True))
        masked = jnp.where(hot, NEG_INF, masked)
        picked = picked + hot.astype(F32)
    total = gates[0]
    for k in range(1, TOP_K):
        total = total + gates[k]

    chunk = float(RUN_CHUNK)
    pb = picked.astype(BF16)
    earlier = _dot(pb, tri_ref[...])
    cnt_col = jnp.sum(picked, axis=1, keepdims=True)
    run_len = jnp.floor((cnt_col + (chunk - 1.0)) * (1.0 / chunk)) * chunk
    lower = _dot(ltri_ref[...], jnp.broadcast_to(run_len, (N_EXPERTS, LANES)).astype(BF16))[:, 0:1]
    slot_of = earlier + lower
    for k in range(TOP_K):
        w_ref[k:k + 1, :] = gates[k] / total * ROUTED_SCALE
        slot_ref[k:k + 1, :] = jnp.sum(jnp.where(hots[k], slot_of, 0.0), axis=0, keepdims=True).astype(I32)
    w_ref[TOP_K:, :] = jnp.zeros((8 - TOP_K, tr), F32)
    slot_ref[TOP_K:, :] = jnp.full((8 - TOP_K, tr), -1, I32)

    c_first = lax.broadcasted_iota(I32, (N_EXPERTS, N_CHUNKS), 1).astype(F32) * chunk
    owner = jnp.sum((lower + run_len <= c_first).astype(F32), axis=0, keepdims=True)
    hot_e = lax.broadcasted_iota(I32, (N_EXPERTS, N_CHUNKS), 0).astype(F32) == owner
    rel = jnp.sum(jnp.where(hot_e, run_col[...] - lower + c_first, 0.0), axis=0, keepdims=True)
    ce_ref[0] = jnp.broadcast_to(owner.astype(I32), (8, N_CHUNKS))
    crel_ref[0] = jnp.broadcast_to(rel.astype(I32), (8, N_CHUNKS))
    run_col[...] = run_col[...] + run_len
    cnt_ref[...] = jnp.broadcast_to(run_col[...], cnt_ref.shape)


def _route(h1, w_router, router_bias):
    t_rows = h1.shape[0]
    tr = MOE_TILE
    n_tiles = t_rows // tr
    tri = jnp.triu(jnp.ones((tr, tr), BF16), k=1)
    ltri = jnp.tril(jnp.ones((N_EXPERTS, N_EXPERTS), BF16), k=-1)
    tok = lambda i: (0, i)
    const2 = lambda i: (0, 0)
    per_tile = lambda i: (i, 0, 0)
    return pl.pallas_call(
        _route_kernel,
        grid=(n_tiles,),
        in_specs=[pl.BlockSpec((tr, D_MODEL), lambda i: (i, 0)),
                  pl.BlockSpec((N_EXPERTS, D_MODEL), const2),
                  pl.BlockSpec((N_EXPERTS, 1), const2),
                  pl.BlockSpec((tr, tr), const2),
                  pl.BlockSpec((N_EXPERTS, N_EXPERTS), const2)],
        out_specs=(pl.BlockSpec((8, tr), tok), pl.BlockSpec((8, tr), tok),
                   pl.BlockSpec((1, 8, N_CHUNKS), per_tile), pl.BlockSpec((1, 8, N_CHUNKS), per_tile),
                   pl.BlockSpec((N_EXPERTS, LANES), const2)),
        out_shape=(jax.ShapeDtypeStruct((8, t_rows), F32), jax.ShapeDtypeStruct((8, t_rows), I32),
                   jax.ShapeDtypeStruct((n_tiles, 8, N_CHUNKS), I32), jax.ShapeDtypeStruct((n_tiles, 8, N_CHUNKS), I32),
                   jax.ShapeDtypeStruct((N_EXPERTS, LANES), F32)),
        scratch_shapes=[pltpu.VMEM((N_EXPERTS, 1), F32)],
        compiler_params=pltpu.CompilerParams(dimension_semantics=("arbitrary",)),
        name="route",
    )(h1, w_router.T, router_bias.reshape(N_EXPERTS, 1), tri, ltri)


def _pos_kernel(n_blocks_pad, ccol_ref, crow_ref, be_ref, nb_ref, start_ref):
    blk = float(EXPERT_BLOCK)
    pad = lambda cnt: jnp.floor((cnt + (blk - 1.0)) * (1.0 / blk)) * blk
    padded_row = pad(crow_ref[0:1, :])
    lane = lax.broadcasted_iota(I32, (N_EXPERTS, 128), 1)
    sub = lax.broadcasted_iota(I32, (N_EXPERTS, 128), 0)
    start = jnp.sum(jnp.where(lane < sub, padded_row, 0.0), axis=1, keepdims=True)
    end = start + pad(ccol_ref[:, 0:1])

    first_row = lax.broadcasted_iota(I32, (N_EXPERTS, n_blocks_pad), 1).astype(F32) * blk
    owner = jnp.sum((end <= first_row).astype(F32), axis=0, keepdims=True)
    be_ref[...] = jnp.broadcast_to(jnp.minimum(owner, float(N_EXPERTS - 1)).astype(I32), be_ref.shape)
    used = jnp.max(end, axis=0, keepdims=True) * (1.0 / blk)
    nb_ref[...] = jnp.broadcast_to(used.astype(I32), nb_ref.shape)
    start_ref[...] = jnp.broadcast_to(start.astype(I32), start_ref.shape)


def _pos(cnt, n_blocks_pad):
    counts = cnt[:, 0]
    ccol = jnp.broadcast_to(counts[:, None], (N_EXPERTS, 128))
    crow = jnp.broadcast_to(jnp.pad(counts, (0, 128 - N_EXPERTS))[None, :], (8, 128))
    return pl.pallas_call(
        functools.partial(_pos_kernel, n_blocks_pad),
        out_shape=(jax.ShapeDtypeStruct((8, n_blocks_pad), I32), jax.ShapeDtypeStruct((8, 128), I32),
                   jax.ShapeDtypeStruct((N_EXPERTS, 128), I32)),
        name="pos",
    )(ccol, crow)


def _chunk_copy(src_ref, src_row, dst_ref, dst_row, sem):
    src = pl.ds(pl.multiple_of(src_row, RUN_CHUNK), RUN_CHUNK)
    dst = pl.ds(pl.multiple_of(dst_row, RUN_CHUNK), RUN_CHUNK)
    return pltpu.make_async_copy(src_ref.at[src, :], dst_ref.at[dst, :], sem)


def _for_each_chunk(tile, start_ref, ce_ref, crel_ref, fn):
    def per_chunk(c, _):
        fn(c * RUN_CHUNK, start_ref[ce_ref[tile, c]] + crel_ref[tile, c])
        return 0

    lax.fori_loop(0, N_CHUNKS, per_chunk, 0, unroll=8)


def _scatter_kernel(n_blocks, start_ref, count_ref, nb_ref, ce_ref, crel_ref, h_ref, slot_ref, xs_out,
                    tiles, zeros, sems, zsem):
    i = pl.program_id(0)
    ts = MOE_TILE

    @pl.when(i == 0)
    def _():
        zeros[...] = jnp.zeros_like(zeros)
        zero_block = lambda b: pltpu.make_async_copy(
            zeros, xs_out.at[pl.ds(pl.multiple_of(b * EXPERT_BLOCK, EXPERT_BLOCK), EXPERT_BLOCK), :], zsem)

        def pad_rows(e, n_pad):
            lo = (start_ref[e] + count_ref[e]) // RUN_CHUNK
            hi = (start_ref[e] + ((count_ref[e] + (EXPERT_BLOCK - 1)) & (-EXPERT_BLOCK))) // RUN_CHUNK

            def zero_chunk(c, _):
                _chunk_copy(zeros, 0, xs_out, c * RUN_CHUNK, zsem).start()
                return 0

            lax.fori_loop(lo, hi, zero_chunk, 0)
            return n_pad + (hi - lo)

        n_pad = lax.fori_loop(0, N_EXPERTS, pad_rows, 0)

        def start_block(b, _):
            zero_block(b).start()
            return 0

        lax.fori_loop(nb_ref[0], n_blocks, start_block, 0)

        def wait_chunk(c, _):
            _chunk_copy(zeros, 0, xs_out, 0, zsem).wait()
            return 0

        lax.fori_loop(0, n_pad, wait_chunk, 0)

        def wait_block(b, _):
            zero_block(0).wait()
            return 0

        lax.fori_loop(nb_ref[0], n_blocks, wait_block, 0)

    def drain(half):
        pltpu.make_async_copy(tiles.at[half], xs_out.at[pl.ds(0, LOCAL_SLOTS), :], sems.at[half]).wait()

    for half in range(2):
        @pl.when(i > 0)
        def _():
            drain(half)

        h = h_ref[half * ts:(half + 1) * ts, :].astype(BF16)
        slots = slot_ref[:, half * ts:(half + 1) * ts]
        sub = ts // 2
        for jb in range(LOCAL_SLOTS // ts):
            in_block = jnp.where(slots >> TILE_SHIFT == jb, slots & (ts - 1), -1).astype(F32).astype(BF16)
            parts = []
            for r in range(ts // sub):
                j = (lax.broadcasted_iota(I32, (sub, ts), 0) + r * sub).astype(F32).astype(BF16)
                perm = jnp.zeros((sub, ts), BF16)
                for k in range(TOP_K):
                    perm = perm + jnp.where(j == in_block[k:k + 1, :], jnp.ones((), BF16), jnp.zeros((), BF16))
                parts.append(perm)
            tiles[half, jb * ts:(jb + 1) * ts, :] = _dot(jnp.concatenate(parts, axis=0), h).astype(BF16)

        _for_each_chunk(
            2 * i + half, start_ref, ce_ref, crel_ref,
            lambda local, sorted_row, half=half: _chunk_copy(tiles.at[half], local, xs_out, sorted_row,
                                                             sems.at[half]).start())

    @pl.when(i == pl.num_programs(0) - 1)
    def _():
        drain(0)
        drain(1)


def _scatter(h1, slots, starts, counts, n_used, chunk_e, chunk_rel, n_rows):
    t_rows = h1.shape[0]
    step = 2 * MOE_TILE
    n_blocks = n_rows // EXPERT_BLOCK
    return pl.pallas_call(
        functools.partial(_scatter_kernel, n_blocks),
        grid_spec=pltpu.PrefetchScalarGridSpec(
            num_scalar_prefetch=5,
            grid=(t_rows // step,),
            in_specs=[pl.BlockSpec((step, D_MODEL), lambda i, *_: (i, 0)),
                      pl.BlockSpec((8, step), lambda i, *_: (0, i))],
            out_specs=pl.BlockSpec(memory_space=pl.ANY),
            scratch_shapes=[pltpu.VMEM((2, LOCAL_SLOTS, D_MODEL), BF16),
                            pltpu.VMEM((EXPERT_BLOCK, D_MODEL), BF16),
                            pltpu.SemaphoreType.DMA((2,)), pltpu.SemaphoreType.DMA(())],
        ),
        out_shape=jax.ShapeDtypeStruct((n_rows, D_MODEL), BF16),
        compiler_params=pltpu.CompilerParams(dimension_semantics=("arbitrary",), vmem_limit_bytes=V7X_VMEM_LIMIT),
        name="scatter",
    )(starts, counts, n_used, chunk_e, chunk_rel, h1, slots)


def _experts_kernel(be_ref, nb_ref, x_ref, wg_ref, wu_ref, wd_ref, y_ref, wg_bf, wu_bf, wd_bf):
    b = pl.program_id(0)

    @pl.when((b == 0) | (be_ref[b] != be_ref[jnp.maximum(b - 1, 0)]))
    def _():
        wg_bf[...] = wg_ref[...].astype(BF16)
        wu_bf[...] = wu_ref[...].astype(BF16)
        wd_bf[...] = wd_ref[...].astype(BF16)

    @pl.when(b < nb_ref[0])
    def _():
        half = EXPERT_BLOCK // 2
        for r in range(2):
            rows = slice(r * half, (r + 1) * half)
            xb = x_ref[rows, :]
            gate = _dot(xb, wg_bf[...])
            up = _dot(xb, wu_bf[...])
            hid = (gate * jax.nn.sigmoid(gate) * up).astype(BF16)
            y_ref[rows, :] = _dot(hid, wd_bf[...]).astype(BF16)

    @pl.when(b >= nb_ref[0])
    def _():
        y_ref[...] = jnp.zeros_like(y_ref)


def _experts(xs, block_e, n_used, w_gate, w_up, w_down):
    blk_rows = EXPERT_BLOCK
    return pl.pallas_call(
        _experts_kernel,
        grid_spec=pltpu.PrefetchScalarGridSpec(
            num_scalar_prefetch=2,
            grid=(xs.shape[0] // blk_rows,),
            in_specs=[pl.BlockSpec((blk_rows, D_MODEL), lambda b, be, nb: (jnp.minimum(b, jnp.maximum(nb[0], 1) - 1), 0)),
                      pl.BlockSpec((None, D_MODEL, D_EXPERT), lambda b, be, nb: (be[b], 0, 0)),
                      pl.BlockSpec((None, D_MODEL, D_EXPERT), lambda b, be, nb: (be[b], 0, 0)),
                      pl.BlockSpec((None, D_EXPERT, D_MODEL), lambda b, be, nb: (be[b], 0, 0))],
            out_specs=pl.BlockSpec((blk_rows, D_MODEL), lambda b, be, nb: (b, 0)),
            scratch_shapes=[pltpu.VMEM((D_MODEL, D_EXPERT), BF16), pltpu.VMEM((D_MODEL, D_EXPERT), BF16),
                            pltpu.VMEM((D_EXPERT, D_MODEL), BF16)],
        ),
        out_shape=jax.ShapeDtypeStruct(xs.shape, BF16),
        compiler_params=pltpu.CompilerParams(dimension_semantics=("arbitrary",)),
        name="experts",
    )(block_e, n_used, xs, w_gate, w_up, w_down)


def _combine_kernel(n_prompt_tiles, start_ref, ce_ref, crel_ref, h_ref, w_ref, slot_ref, ys_hbm,
                    wsg_ref, wsu_ref, wsd_ref, g2_ref, b2_ref, yp_hbm, ysm_hbm, ybuf, obuf, gsems, osems):
    i = pl.program_id(0)
    n_steps = pl.num_programs(0)
    tc = MOE_TILE
    tile_t = tc // BATCH

    def gather(tile, slot):
        _for_each_chunk(
            tile, start_ref, ce_ref, crel_ref,
            lambda local, sorted_row: _chunk_copy(ys_hbm, sorted_row, ybuf.at[slot], local, gsems.at[slot]).start())

    def gather_wait(slot):
        pltpu.make_async_copy(ys_hbm.at[pl.ds(0, LOCAL_SLOTS), :], ybuf.at[slot], gsems.at[slot]).wait()

    def out_copies(dst_hbm, t0, slot):
        return [pltpu.make_async_copy(obuf.at[slot, :, b, :], dst_hbm.at[b, pl.ds(t0, tile_t), :], osems.at[slot])
                for b in range(BATCH)]

    def out_wait(slot):
        for cp in out_copies(yp_hbm, 0, slot):
            cp.wait()

    def compute(half, slot):
        rows = slice(half * tc, (half + 1) * tc)
        h = h_ref[rows, :]
        hb = h.astype(BF16)
        gate = _dot(hb, wsg_ref[...])
        up = _dot(hb, wsu_ref[...])
        shared = _dot((gate * jax.nn.sigmoid(gate) * up).astype(BF16), wsd_ref[...])

        eye = (lax.broadcasted_iota(I32, (tc, tc), 0) == lax.broadcasted_iota(I32, (tc, tc), 1)).astype(F32)
        to_cols = lambda a: lax.dot_general(eye, a, (((1,), (1,)), ((), ())),
                                            precision=lax.Precision.HIGHEST, preferred_element_type=F32)
        w_cols = to_cols(w_ref[:, rows])
        s_cols = to_cols(slot_ref[:, rows].astype(F32)).astype(I32)

        w_bf = w_cols.astype(BF16)
        j = lax.broadcasted_iota(I32, (tc, tc), 1).astype(F32).astype(BF16)
        blocks = []
        for lb in range(LOCAL_SLOTS // tc):
            in_block = jnp.where(s_cols >> TILE_SHIFT == lb, s_cols & (tc - 1), -1).astype(F32).astype(BF16)
            blk = jnp.zeros((tc, tc), BF16)
            for k in range(TOP_K):
                blk = blk + jnp.where(j == in_block[:, k:k + 1], w_bf[:, k:k + 1], jnp.zeros((), BF16))
            blocks.append(blk)
        mix_w = jnp.concatenate(blocks, axis=-1)
        gather_wait(slot)
        routed = _dot(mix_w, ybuf[slot])
        out = _layer_norm(DEEPNORM_ALPHA * h + (routed + shared), g2_ref[...], b2_ref[...])
        obuf[slot] = out.reshape(tile_t, BATCH, D_MODEL)

        g = 2 * i + half

        @pl.when(g < n_prompt_tiles)
        def _():
            for cp in out_copies(yp_hbm, pl.multiple_of(g * tile_t, tile_t), slot):
                cp.start()

        @pl.when(g >= n_prompt_tiles)
        def _():
            for cp in out_copies(ysm_hbm, pl.multiple_of((g - n_prompt_tiles) * tile_t, tile_t), slot):
                cp.start()

    @pl.when(i == 0)
    def _():
        gather(0, 0)

    gather(2 * i + 1, 1)

    @pl.when(i > 0)
    def _():
        out_wait(0)

    compute(0, 0)

    @pl.when(i + 1 < n_steps)
    def _():
        gather(2 * i + 2, 0)

    @pl.when(i > 0)
    def _():
        out_wait(1)

    compute(1, 1)

    @pl.when(i + 1 == n_steps)
    def _():
        out_wait(0)
        out_wait(1)


def _combine(h1, gate_w, slots, ys, starts, chunk_e, chunk_rel, ws_gate, ws_up, ws_down, ln2_g, ln2_b,
             prompt_shape, sample_shape):
    t_rows = h1.shape[0]
    tc = MOE_TILE
    step = 2 * tc
    n_prompt_tiles = prompt_shape[0] * prompt_shape[1] // tc
    const2 = lambda i, *_: (0, 0)
    any_spec = pl.BlockSpec(memory_space=pl.ANY)
    return pl.pallas_call(
        functools.partial(_combine_kernel, n_prompt_tiles),
        grid_spec=pltpu.PrefetchScalarGridSpec(
            num_scalar_prefetch=3,
            grid=(t_rows // step,),
            in_specs=[pl.BlockSpec((step, D_MODEL), lambda i, *_: (i, 0)),
                      pl.BlockSpec((8, step), lambda i, *_: (0, i)),
                      pl.BlockSpec((8, step), lambda i, *_: (0, i)),
                      any_spec,
                      pl.BlockSpec((D_MODEL, D_EXPERT), const2),
                      pl.BlockSpec((D_MODEL, D_EXPERT), const2),
                      pl.BlockSpec((D_EXPERT, D_MODEL), const2),
                      pl.BlockSpec((1, D_MODEL), const2),
                      pl.BlockSpec((1, D_MODEL), const2)],
            out_specs=(any_spec, any_spec),
            scratch_shapes=[pltpu.VMEM((2, LOCAL_SLOTS, D_MODEL), BF16),
                            pltpu.VMEM((2, tc // BATCH, BATCH, D_MODEL), F32),
                            pltpu.SemaphoreType.DMA((2,)), pltpu.SemaphoreType.DMA((2,))],
        ),
        out_shape=(jax.ShapeDtypeStruct(prompt_shape, F32), jax.ShapeDtypeStruct(sample_shape, F32)),
        compiler_params=pltpu.CompilerParams(dimension_semantics=("arbitrary",), vmem_limit_bytes=V7X_VMEM_LIMIT),
        name="combine",
    )(starts, chunk_e, chunk_rel, h1, gate_w, slots, ys, ws_gate.astype(BF16), ws_up.astype(BF16),
      ws_down.astype(BF16), ln2_g.reshape(1, D_MODEL), ln2_b.reshape(1, D_MODEL))


def _time_major(x):
    b, l, d = x.shape
    return jnp.transpose(x, (1, 0, 2)).reshape(l * b, d)


def _batch_major(x, b):
    return jnp.transpose(x.reshape(x.shape[0] // b, b, x.shape[1]), (1, 0, 2))


def _block_diag(blocks):
    n, r, c = blocks.shape
    eye = jnp.eye(n, dtype=blocks.dtype)
    return (blocks[:, :, None, :] * eye[:, None, :, None]).reshape(n * r, n * c)


def kernel(x_prompt, x_sample, state_ssm_re, state_ssm_im, cache_conv, ln_in_g, ln_in_b, w_in, lam_re, lam_im, log_dt, ssm_b_re, ssm_b_im, ssm_c_re, ssm_c_im, ssm_d, w_glu, b_glu, conv_w, beta_ssm, beta_conv, w_out, ln1_g, ln1_b, w_router, router_bias, w_gate, w_up, w_down, ws_gate, ws_up, ws_down, ln2_g, ln2_b):
    bp, lp, _ = x_prompt.shape
    bs, ls, _ = x_sample.shape
    assert bp == BATCH and bs == BATCH and ls == CHUNK_T and lp % CHUNK_T == 0
    assert w_in.shape[0] == 1, "single-layer model"
    n_prompt = bp * lp
    row = lambda a: a.reshape(1, -1)

    a_re, a_im, bb_re, bb_im = _prep(lam_re[0], lam_im[0], log_dt[0], ssm_b_re[0], ssm_b_im[0])
    groups_in = 128 // SSM_GROUP
    groups_out = 256 // SSM_GROUP
    bbd = lambda bb: jnp.stack([_block_diag(bb[j * groups_in:(j + 1) * groups_in])
                                for j in range(N_GROUPS // groups_in)]).astype(BF16)
    ct = lambda cc: jnp.transpose(cc, (0, 2, 1))
    cbd = lambda cc: jnp.stack([_block_diag(ct(cc)[j * groups_out:(j + 1) * groups_out])
                                for j in range(N_GROUPS // groups_out)]).astype(BF16)
    mix_weights = (row(ln_in_g), row(ln_in_b), w_in[0].astype(BF16), row(a_re), row(a_im),
                   bbd(bb_re), bbd(bb_im), cbd(ssm_c_re[0]), cbd(-ssm_c_im[0]), row(ssm_d[0]),
                   w_glu[0].astype(BF16), row(b_glu[0]), conv_w[0], row(beta_ssm[0]), row(beta_conv[0]),
                   w_out[0].astype(BF16), row(ln1_g[0]), row(ln1_b[0]))

    h0r = state_ssm_re[0].reshape(BATCH, D_STATE)
    h0i = state_ssm_im[0].reshape(BATCH, D_STATE)
    cbuf = _time_major(cache_conv[0])
    h1, p_re, p_im, p_conv, s_re, s_im, s_conv = _mix(x_prompt, x_sample, h0r, h0i, cbuf, mix_weights)

    t_rows = h1.shape[0]
    assert t_rows % (2 * MOE_TILE) == 0 and n_prompt % MOE_TILE == 0
    n_tiles = t_rows // MOE_TILE
    max_rows = t_rows * TOP_K + n_tiles * N_EXPERTS * (RUN_CHUNK - 1) + N_EXPERTS * EXPERT_BLOCK
    n_blocks = -(-max_rows // EXPERT_BLOCK) + 1
    n_blocks_pad = -(-n_blocks // 128) * 128
    gate_w, slots, chunk_e, chunk_rel, cnt = _route(h1, w_router[0], router_bias[0])
    block_e, n_used, starts = _pos(cnt, n_blocks_pad)
    n_used, chunk_e, chunk_rel = n_used[0, :1], chunk_e[:, 0], chunk_rel[:, 0]
    starts = jnp.concatenate([starts[:, 0], jnp.full((1,), (n_blocks - 1) * EXPERT_BLOCK, I32)])
    xs = _scatter(h1, slots, starts, cnt[:, 0].astype(I32), n_used, chunk_e, chunk_rel, n_blocks * EXPERT_BLOCK)
    ys = _experts(xs, block_e[0, :n_blocks], n_used, w_gate[0], w_up[0], w_down[0])
    y_prompt, y_sample = _combine(h1, gate_w, slots, ys, starts, chunk_e, chunk_rel, ws_gate[0], ws_up[0], ws_down[0],
                                  ln2_g[0], ln2_b[0], x_prompt.shape, x_sample.shape)

    st = lambda s: s.reshape(1, BATCH, N_GROUPS, SSM_STATE)
    cv = lambda t: _batch_major(t, BATCH)[None]
    return (y_prompt, y_sample, st(p_re), st(p_im), cv(p_conv), st(s_re), st(s_im), cv(s_conv))
```

```python
import functools

import jax
import jax.numpy as jnp
from jax import lax
from jax.experimental import pallas as pl
from jax.experimental.pallas import tpu as pltpu

F32 = jnp.float32
BF16 = jnp.bfloat16
I32 = jnp.int32
U32 = jnp.uint32

D_MODEL = 1024
D_SSM = 512
D_CONV = 512
SSM_GROUP = 16
N_GROUPS = 32
SSM_STATE = 64
D_STATE = N_GROUPS * SSM_STATE
N_EXPERTS = 64
TOP_K = 6
N_EXPERT_GROUPS = 8
GROUP_SIZE = N_EXPERTS // N_EXPERT_GROUPS
TOPK_GROUPS = 4
D_EXPERT = 256
ROUTED_SCALE = 2.5
DEPTH = 1
DEEPNORM_ALPHA = (2.0 * DEPTH) ** 0.25
LN_EPS = 1e-5
RMS_EPS = 1e-6

BATCH = 8
CHUNK_T = 64
CHUNK_ROWS = CHUNK_T * BATCH
SCAN_COLS = 512
MOE_TILE = 256
TILE_SHIFT = MOE_TILE.bit_length() - 1
RUN_CHUNK = 8
LOCAL_SLOTS = TOP_K * MOE_TILE + N_EXPERTS * RUN_CHUNK
N_CHUNKS = LOCAL_SLOTS // RUN_CHUNK
EXPERT_BLOCK = 512
LANES = 128
TOKEN_ROWS = D_MODEL // (2 * LANES)
CHUNK_WROWS = RUN_CHUNK * TOKEN_ROWS
V7X_VMEM_LIMIT = 56 * 1024 * 1024
NEG_INF = float("-inf")


def _layer_norm(x, g, b):
    mu = jnp.mean(x, axis=-1, keepdims=True)
    xc = x - mu
    var = jnp.mean(xc * xc, axis=-1, keepdims=True)
    return xc * lax.rsqrt(var + LN_EPS) * g + b


def _rms_norm(x, g):
    return x * lax.rsqrt(jnp.mean(x * x, axis=-1, keepdims=True) + RMS_EPS) * g


def _dot(a, b):
    return jnp.dot(a, b, preferred_element_type=F32)


def _discretise(lr, li, log_dt):
    dt = jnp.exp(log_dt)
    mag = jnp.exp(lr * dt)
    ar = mag * jnp.cos(li * dt)
    ai = mag * jnp.sin(li * dt)
    den = lr * lr + li * li
    qr = ((ar - 1.0) * lr + ai * li) / den
    qi = (ai * lr - (ar - 1.0) * li) / den
    return ar, ai, qr, qi


def _prep_kernel(lr_ref, li_ref, ldt_ref, lrc_ref, lic_ref, ldtc_ref, br_ref, bi_ref,
                 ar_ref, ai_ref, bbr_ref, bbi_ref):
    ar, ai, _, _ = _discretise(lr_ref[...], li_ref[...], ldt_ref[...])
    ar_ref[...] = ar
    ai_ref[...] = ai
    _, _, qr, qi = _discretise(lrc_ref[...], lic_ref[...], ldtc_ref[...])
    br = br_ref[...]
    bi = bi_ref[...]
    bbr_ref[...] = qr * br - qi * bi
    bbi_ref[...] = qr * bi + qi * br


def _prep(lam_re, lam_im, log_dt, b_re, b_im):
    g, p = lam_re.shape
    per_channel = lambda a: jnp.repeat(a, SSM_GROUP, axis=0)
    rows = lambda b: jnp.transpose(b, (0, 2, 1)).reshape(g * SSM_GROUP, p)
    ldt = log_dt.reshape(g, 1)
    ar, ai, bbr, bbi = pl.pallas_call(
        _prep_kernel,
        out_shape=(jax.ShapeDtypeStruct((g, p), F32), jax.ShapeDtypeStruct((g, p), F32),
                   jax.ShapeDtypeStruct((g * SSM_GROUP, p), F32), jax.ShapeDtypeStruct((g * SSM_GROUP, p), F32)),
        name="prep",
    )(lam_re, lam_im, ldt, per_channel(lam_re), per_channel(lam_im), per_channel(ldt), rows(b_re), rows(b_im))
    return ar, ai, bbr.reshape(g, SSM_GROUP, p), bbi.reshape(g, SSM_GROUP, p)


def _chunk_copies(src_hbm, t0, xbuf, slot, sems):
    return [pltpu.make_async_copy(src_hbm.at[b, pl.ds(t0, CHUNK_T), :], xbuf.at[slot, :, b, :], sems.at[slot])
            for b in range(BATCH)]


def _mix_kernel(n_prompt_chunks,
                xp_hbm, xs_hbm, h0r_ref, h0i_ref, cbuf_ref, lng_ref, lnb_ref, win_ref, ar_ref, ai_ref,
                bbr_ref, bbi_ref, cbr_ref, cbi_ref, dsk_ref, wglu_ref, bglu_ref, cw_ref,
                bs_ref, bc_ref, wout_ref, g1_ref, b1_ref,
                h1_ref, pr_ref, pi_ref, pc_ref, sr_ref, si_ref, sc_ref,
                str_ref, sti_ref, hr_ref, hi_ref, cv_ref, xbuf, xsems):
    c = pl.program_id(0)
    n = CHUNK_ROWS
    carry_rows = 2 * BATCH
    slot = c % 2

    @pl.when(c == 0)
    def _():
        for cp in _chunk_copies(xp_hbm, 0, xbuf, 0, xsems):
            cp.start()

    @pl.when(c + 1 < n_prompt_chunks)
    def _():
        for cp in _chunk_copies(xp_hbm, pl.multiple_of((c + 1) * CHUNK_T, CHUNK_T), xbuf, 1 - slot, xsems):
            cp.start()

    @pl.when(c + 1 == n_prompt_chunks)
    def _():
        for cp in _chunk_copies(xs_hbm, 0, xbuf, 1 - slot, xsems):
            cp.start()

    @pl.when(c == 0)
    def _():
        hr_ref[...] = jnp.zeros_like(hr_ref)
        hi_ref[...] = jnp.zeros_like(hi_ref)
        cv_ref[0:carry_rows, :] = jnp.zeros((carry_rows, D_CONV), F32)

    @pl.when(c == n_prompt_chunks)
    def _():
        hr_ref[...] = h0r_ref[...]
        hi_ref[...] = h0i_ref[...]
        cv_ref[0:carry_rows, :] = cbuf_ref[...]

    for cp in _chunk_copies(xp_hbm, 0, xbuf, slot, xsems):
        cp.wait()
    h = _layer_norm(xbuf[slot].reshape(n, D_MODEL), lng_ref[...], lnb_ref[...])
    hb = h.astype(BF16)

    u = _dot(hb, win_ref[:, 0:D_SSM])
    ub = u.astype(BF16)
    u_tile = 128
    s_tile = u_tile // SSM_GROUP * SSM_STATE
    for j in range(D_SSM // u_tile):
        uj = ub[:, j * u_tile:(j + 1) * u_tile]
        str_ref[:, j * s_tile:(j + 1) * s_tile] = _dot(uj, bbr_ref[j])
        sti_ref[:, j * s_tile:(j + 1) * s_tile] = _dot(uj, bbi_ref[j])

    for cb in range(D_STATE // SCAN_COLS):
        cols = slice(cb * SCAN_COLS, (cb + 1) * SCAN_COLS)
        ar = jnp.broadcast_to(ar_ref[:, cols], (BATCH, SCAN_COLS))
        ai = jnp.broadcast_to(ai_ref[:, cols], (BATCH, SCAN_COLS))

        def step(t, carry, cols=cols, ar=ar, ai=ai):
            sr, si = carry
            r0 = pl.multiple_of(t * BATCH, BATCH)
            nr = ar * sr - ai * si + str_ref[pl.ds(r0, BATCH), cols]
            ni = ar * si + ai * sr + sti_ref[pl.ds(r0, BATCH), cols]
            str_ref[pl.ds(r0, BATCH), cols] = nr
            sti_ref[pl.ds(r0, BATCH), cols] = ni
            return nr, ni

        sr, si = lax.fori_loop(0, CHUNK_T, step, (hr_ref[:, cols], hi_ref[:, cols]), unroll=8)
        hr_ref[:, cols] = sr
        hi_ref[:, cols] = si

    k_tile = 1024
    ys = []
    for j in range(D_STATE // k_tile):
        sl = slice(j * k_tile, (j + 1) * k_tile)
        ys.append(_dot(str_ref[:, sl].astype(BF16), cbr_ref[j]) + _dot(sti_ref[:, sl].astype(BF16), cbi_ref[j]))
    y_ssm = jnp.concatenate(ys, axis=-1) + dsk_ref[...] * u
    g = jax.nn.gelu(y_ssm)
    y_ssm = g * jax.nn.sigmoid(_dot(g.astype(BF16), wglu_ref[...]) + bglu_ref[...])
    mix = _dot(_rms_norm(y_ssm, bs_ref[...]).astype(BF16), wout_ref[0:D_SSM, :])

    gate_b = _dot(hb, win_ref[:, D_SSM:D_SSM + D_CONV])
    gate_c = _dot(hb, win_ref[:, D_SSM + D_CONV:D_SSM + 2 * D_CONV])
    v = _dot(hb, win_ref[:, D_SSM + 2 * D_CONV:D_SSM + 3 * D_CONV])
    cv_ref[carry_rows:carry_rows + n, :] = gate_c * v
    y_conv = gate_b * (cv_ref[0:n, :] * cw_ref[0:1, :]
                       + cv_ref[BATCH:BATCH + n, :] * cw_ref[1:2, :]
                       + cv_ref[carry_rows:carry_rows + n, :] * cw_ref[2:3, :])
    tail = cv_ref[n:n + carry_rows, :]
    cv_ref[0:carry_rows, :] = tail
    mix = mix + _dot(_rms_norm(y_conv, bc_ref[...]).astype(BF16), wout_ref[D_SSM:D_SSM + D_CONV, :])

    h1_ref[...] = _layer_norm(DEEPNORM_ALPHA * h + mix, g1_ref[...], b1_ref[...])

    @pl.when(c == n_prompt_chunks - 1)
    def _():
        pr_ref[...] = hr_ref[...]
        pi_ref[...] = hi_ref[...]
        pc_ref[...] = tail

    @pl.when(c == n_prompt_chunks)
    def _():
        sr_ref[...] = hr_ref[...]
        si_ref[...] = hi_ref[...]
        sc_ref[...] = tail


def _mix(x_prompt, x_sample, h0r, h0i, cbuf, weights):
    n_prompt_chunks = x_prompt.shape[1] // CHUNK_T
    n_chunks = n_prompt_chunks + 1
    t_rows = n_chunks * CHUNK_ROWS
    any_spec = pl.BlockSpec(memory_space=pl.ANY)
    const2 = lambda c: (0, 0)
    const3 = lambda c: (0, 0, 0)
    w_specs = [pl.BlockSpec(w.shape, const3 if w.ndim == 3 else const2) for w in weights]
    state = jax.ShapeDtypeStruct((BATCH, D_STATE), F32)
    tail = jax.ShapeDtypeStruct((2 * BATCH, D_CONV), F32)
    state_spec = pl.BlockSpec((BATCH, D_STATE), const2)
    tail_spec = pl.BlockSpec((2 * BATCH, D_CONV), const2)
    return pl.pallas_call(
        functools.partial(_mix_kernel, n_prompt_chunks),
        grid=(n_chunks,),
        in_specs=[any_spec, any_spec, state_spec, state_spec, tail_spec] + w_specs,
        out_specs=(pl.BlockSpec((CHUNK_ROWS, D_MODEL), lambda c: (c, 0)),
                   state_spec, state_spec, tail_spec, state_spec, state_spec, tail_spec),
        out_shape=(jax.ShapeDtypeStruct((t_rows, D_MODEL), F32), state, state, tail, state, state, tail),
        scratch_shapes=[pltpu.VMEM((CHUNK_ROWS, D_STATE), F32), pltpu.VMEM((CHUNK_ROWS, D_STATE), F32),
                        pltpu.VMEM((BATCH, D_STATE), F32), pltpu.VMEM((BATCH, D_STATE), F32),
                        pltpu.VMEM((CHUNK_ROWS + 2 * BATCH, D_CONV), F32),
                        pltpu.VMEM((2, CHUNK_T, BATCH, D_MODEL), F32), pltpu.SemaphoreType.DMA((2,))],
        compiler_params=pltpu.CompilerParams(dimension_semantics=("arbitrary",), vmem_limit_bytes=V7X_VMEM_LIMIT),
        name="mix",
    )(x_prompt, x_sample, h0r, h0i, cbuf, *weights)


def _route_kernel(h_ref, wrt_ref, bias_ref, tri_ref, ltri_ref, w_ref, slot_ref, ce_ref, crel_ref, cnt_ref, run_col):
    i = pl.program_id(0)
    tr = MOE_TILE

    @pl.when(i == 0)
    def _():
        run_col[...] = jnp.zeros_like(run_col)

    def split(a):
        hi = a.astype(BF16)
        return hi, (a - hi.astype(F32)).astype(BF16)

    nt_dot = lambda a, b: lax.dot_general(a, b, (((1,), (1,)), ((), ())), preferred_element_type=F32)
    w_hi, w_lo = split(wrt_ref[...])
    h_hi, h_lo = split(h_ref[...])
    logits = nt_dot(w_hi, h_hi) + (nt_dot(w_hi, h_lo) + nt_dot(w_lo, h_hi))
    scores = jax.nn.sigmoid(logits)
    sel = scores + bias_ref[...]

    sub = lax.broadcasted_iota(I32, (GROUP_SIZE, tr), 0).astype(F32)
    blocks, gscore = [], []
    for g in range(N_EXPERT_GROUPS):
        blk = sel[g * GROUP_SIZE:(g + 1) * GROUP_SIZE, :]
        m1 = jnp.max(blk, axis=0, keepdims=True)
        first = jnp.min(jnp.where(blk == m1, sub, float(GROUP_SIZE)), axis=0, keepdims=True)
        m2 = jnp.max(jnp.where(sub == first, NEG_INF, blk), axis=0, keepdims=True)
        blocks.append(blk)
        gscore.append(m1 + m2)
    masked = []
    for g in range(N_EXPERT_GROUPS):
        beaten = jnp.zeros((1, tr), F32)
        for o in range(N_EXPERT_GROUPS):
            if o == g:
                continue
            wins = gscore[o] >= gscore[g] if o < g else gscore[o] > gscore[g]
            beaten = beaten + wins.astype(F32)
        masked.append(jnp.where(beaten < float(TOPK_GROUPS), blocks[g], NEG_INF))
    masked = jnp.concatenate(masked, axis=0)

    row = lax.broadcasted_iota(I32, (N_EXPERTS, tr), 0).astype(F32)
    picked = jnp.zeros((N_EXPERTS, tr), F32)
    hots, gates = [], []
    for _ in range(TOP_K):
        m = jnp.max(masked, axis=0, keepdims=True)
        idx = jnp.min(jnp.where(masked == m, row, float(N_EXPERTS)), axis=0, keepdims=True)
        hot = row == idx
        hots.append(hot)
        gates.append(jnp.sum(jnp.where(hot, scores, 0.0), axis=0, keepdims=True))
        masked = jnp.where(hot, NEG_INF, masked)
        picked = picked + hot.astype(F32)
    total = gates[0]
    for k in range(1, TOP_K):
        total = total + gates[k]

    chunk = float(RUN_CHUNK)
    pb = picked.astype(BF16)
    earlier = _dot(pb, tri_ref[...])
    cnt_col = jnp.sum(picked, axis=1, keepdims=True)
    run_len = jnp.floor((cnt_col + (chunk - 1.0)) * (1.0 / chunk)) * chunk
    lower = _dot(ltri_ref[...], jnp.broadcast_to(run_len, (N_EXPERTS, LANES)).astype(BF16))[:, 0:1]
    slot_of = earlier + lower
    for k in range(TOP_K):
        w_ref[k:k + 1, :] = gates[k] / total * ROUTED_SCALE
        slot_ref[k:k + 1, :] = jnp.sum(jnp.where(hots[k], slot_of, 0.0), axis=0, keepdims=True).astype(I32)
    w_ref[TOP_K:, :] = jnp.zeros((8 - TOP_K, tr), F32)
    slot_ref[TOP_K:, :] = jnp.full((8 - TOP_K, tr), -1, I32)

    c_first = lax.broadcasted_iota(I32, (N_EXPERTS, N_CHUNKS), 1).astype(F32) * chunk
    owner = jnp.sum((lower + run_len <= c_first).astype(F32), axis=0, keepdims=True)
    hot_e = lax.broadcasted_iota(I32, (N_EXPERTS, N_CHUNKS), 0).astype(F32) == owner
    rel = jnp.sum(jnp.where(hot_e, run_col[...] - lower + c_first, 0.0), axis=0, keepdims=True)
    ce_ref[0] = jnp.broadcast_to(owner.astype(I32), (8, N_CHUNKS))
    crel_ref[0] = jnp.broadcast_to(rel.astype(I32), (8, N_CHUNKS))
    run_col[...] = run_col[...] + run_len
    cnt_ref[...] = jnp.broadcast_to(run_col[...], cnt_ref.shape)


def _route(h1, w_router, router_bias):
    t_rows = h1.shape[0]
    tr = MOE_TILE
    n_tiles = t_rows // tr
    tri = jnp.triu(jnp.ones((tr, tr), BF16), k=1)
    ltri = jnp.tril(jnp.ones((N_EXPERTS, N_EXPERTS), BF16), k=-1)
    tok = lambda i: (0, i)
    const2 = lambda i: (0, 0)
    per_tile = lambda i: (i, 0, 0)
    return pl.pallas_call(
        _route_kernel,
        grid=(n_tiles,),
        in_specs=[pl.BlockSpec((tr, D_MODEL), lambda i: (i, 0)),
                  pl.BlockSpec((N_EXPERTS, D_MODEL), const2),
                  pl.BlockSpec((N_EXPERTS, 1), const2),
                  pl.BlockSpec((tr, tr), const2),
                  pl.BlockSpec((N_EXPERTS, N_EXPERTS), const2)],
        out_specs=(pl.BlockSpec((8, tr), tok), pl.BlockSpec((8, tr), tok),
                   pl.BlockSpec((1, 8, N_CHUNKS), per_tile), pl.BlockSpec((1, 8, N_CHUNKS), per_tile),
                   pl.BlockSpec((N_EXPERTS, LANES), const2)),
        out_shape=(jax.ShapeDtypeStruct((8, t_rows), F32), jax.ShapeDtypeStruct((8, t_rows), I32),
                   jax.ShapeDtypeStruct((n_tiles, 8, N_CHUNKS), I32), jax.ShapeDtypeStruct((n_tiles, 8, N_CHUNKS), I32),
                   jax.ShapeDtypeStruct((N_EXPERTS, LANES), F32)),
        scratch_shapes=[pltpu.VMEM((N_EXPERTS, 1), F32)],
        compiler_params=pltpu.CompilerParams(dimension_semantics=("arbitrary",)),
        name="route",
    )(h1, w_router.T, router_bias.reshape(N_EXPERTS, 1), tri, ltri)


def _pos_kernel(n_blocks_pad, dummy_row, ccol_ref, crow_ref, ce_ref, crel_ref, be_ref, nb_ref, start_ref, wrow_ref):
    blk = float(EXPERT_BLOCK)
    pad = lambda cnt: jnp.floor((cnt + (blk - 1.0)) * (1.0 / blk)) * blk
    padded_row = pad(crow_ref[0:1, :])
    lane = lax.broadcasted_iota(I32, (N_EXPERTS, 128), 1)
    sub = lax.broadcasted_iota(I32, (N_EXPERTS, 128), 0)
    start = jnp.sum(jnp.where(lane < sub, padded_row, 0.0), axis=1, keepdims=True)
    end = start + pad(ccol_ref[:, 0:1])

    first_row = lax.broadcasted_iota(I32, (N_EXPERTS, n_blocks_pad), 1).astype(F32) * blk
    owner = jnp.sum((end <= first_row).astype(F32), axis=0, keepdims=True)
    be_ref[...] = jnp.broadcast_to(jnp.minimum(owner, float(N_EXPERTS - 1)).astype(I32), be_ref.shape)
    used = jnp.max(end, axis=0, keepdims=True) * (1.0 / blk)
    nb_ref[...] = jnp.broadcast_to(used.astype(I32), nb_ref.shape)
    start_ref[...] = jnp.broadcast_to(start.astype(I32), start_ref.shape)

    ce = ce_ref[...]
    base = jnp.full(ce.shape, float(dummy_row), F32)
    for e in range(N_EXPERTS):
        base = jnp.where(ce == e, start[e:e + 1, 0:1], base)
    wrow_ref[...] = (base.astype(I32) + crel_ref[...]) * TOKEN_ROWS


def _pos(cnt, chunk_e, chunk_rel, n_blocks_pad, dummy_row):
    counts = cnt[:, 0]
    ccol = jnp.broadcast_to(counts[:, None], (N_EXPERTS, 128))
    crow = jnp.broadcast_to(jnp.pad(counts, (0, 128 - N_EXPERTS))[None, :], (8, 128))
    return pl.pallas_call(
        functools.partial(_pos_kernel, n_blocks_pad, dummy_row),
        out_shape=(jax.ShapeDtypeStruct((8, n_blocks_pad), I32), jax.ShapeDtypeStruct((8, 128), I32),
                   jax.ShapeDtypeStruct((N_EXPERTS, 128), I32), jax.ShapeDtypeStruct(chunk_e.shape, I32)),
        name="pos",
    )(ccol, crow, chunk_e, chunk_rel)


def _chunk_copy(src_ref, src_wrow, dst_ref, dst_wrow, sem):
    src = pl.ds(pl.multiple_of(src_wrow, CHUNK_WROWS), CHUNK_WROWS)
    dst = pl.ds(pl.multiple_of(dst_wrow, CHUNK_WROWS), CHUNK_WROWS)
    return pltpu.make_async_copy(src_ref.at[src, :], dst_ref.at[dst, :], sem)


def _for_each_chunk(tile, wrow_ref, fn):
    def per_chunk(c, _):
        fn(c * CHUNK_WROWS, wrow_ref[tile * N_CHUNKS + c])
        return 0

    lax.fori_loop(0, N_CHUNKS, per_chunk, 0, unroll=8)


def _store_token_tiles(dst_ref, x):
    n = x.shape[0]
    half = TOKEN_ROWS * LANES
    bits = lambda v: lax.bitcast_convert_type(v.astype(BF16).astype(F32), U32)
    for s in range(TOKEN_ROWS):
        hi = bits(x[:, s * LANES:(s + 1) * LANES])
        lo = bits(x[:, half + s * LANES:half + (s + 1) * LANES])
        dst_ref[pl.ds(s, n, stride=TOKEN_ROWS), :] = hi | (lo >> 16)


def _load_token_tiles(src_ref, n):
    his, los = [], []
    for s in range(TOKEN_ROWS):
        words = src_ref[pl.ds(s, n, stride=TOKEN_ROWS), :]
        his.append(lax.bitcast_convert_type(words & jnp.uint32(0xFFFF0000), F32))
        los.append(lax.bitcast_convert_type(words << 16, F32))
    return his, los


def _scatter_kernel(n_blocks, start_ref, count_ref, nb_ref, wrow_ref, h_ref, slot_ref, xs_out,
                    tiles, zeros, sems, zsem):
    i = pl.program_id(0)
    ts = MOE_TILE
    blk_rows = EXPERT_BLOCK * TOKEN_ROWS

    @pl.when(i == 0)
    def _():
        zeros[...] = jnp.zeros_like(zeros)
        zero_block = lambda b: pltpu.make_async_copy(
            zeros, xs_out.at[pl.ds(pl.multiple_of(b * blk_rows, blk_rows), blk_rows), :], zsem)

        def pad_rows(e, n_pad):
            lo = (start_ref[e] + count_ref[e]) // RUN_CHUNK
            hi = (start_ref[e] + ((count_ref[e] + (EXPERT_BLOCK - 1)) & (-EXPERT_BLOCK))) // RUN_CHUNK

            def zero_chunk(c, _):
                _chunk_copy(zeros, 0, xs_out, c * CHUNK_WROWS, zsem).start()
                return 0

            lax.fori_loop(lo, hi, zero_chunk, 0)
            return n_pad + (hi - lo)

        n_pad = lax.fori_loop(0, N_EXPERTS, pad_rows, 0)

        def start_block(b, _):
            zero_block(b).start()
            return 0

        lax.fori_loop(nb_ref[0], n_blocks, start_block, 0)

        def wait_chunk(c, _):
            _chunk_copy(zeros, 0, xs_out, 0, zsem).wait()
            return 0

        lax.fori_loop(0, n_pad, wait_chunk, 0)

        def wait_block(b, _):
            zero_block(0).wait()
            return 0

        lax.fori_loop(nb_ref[0], n_blocks, wait_block, 0)

    def drain(half):
        pltpu.make_async_copy(tiles.at[half], xs_out.at[pl.ds(0, LOCAL_SLOTS * TOKEN_ROWS), :], sems.at[half]).wait()

    for half in range(2):
        @pl.when(i > 0)
        def _():
            drain(half)

        h = h_ref[half * ts:(half + 1) * ts, :].astype(BF16)
        slots = slot_ref[:, half * ts:(half + 1) * ts]
        sub = ts // 2
        for jb in range(LOCAL_SLOTS // ts):
            in_block = jnp.where(slots >> TILE_SHIFT == jb, slots & (ts - 1), -1).astype(F32).astype(BF16)
            parts = []
            for r in range(ts // sub):
                j = (lax.broadcasted_iota(I32, (sub, ts), 0) + r * sub).astype(F32).astype(BF16)
                perm = jnp.zeros((sub, ts), BF16)
                for k in range(TOP_K):
                    perm = perm + jnp.where(j == in_block[k:k + 1, :], jnp.ones((), BF16), jnp.zeros((), BF16))
                parts.append(perm)
            _store_token_tiles(tiles.at[half, pl.ds(jb * ts * TOKEN_ROWS, ts * TOKEN_ROWS), :],
                               _dot(jnp.concatenate(parts, axis=0), h))

        _for_each_chunk(
            2 * i + half, wrow_ref,
            lambda local, sorted_wrow, half=half: _chunk_copy(tiles.at[half], local, xs_out, sorted_wrow,
                                                              sems.at[half]).start())

    @pl.when(i == pl.num_programs(0) - 1)
    def _():
        drain(0)
        drain(1)


def _scatter(h1, slots, starts, counts, n_used, chunk_wrow, n_rows):
    t_rows = h1.shape[0]
    step = 2 * MOE_TILE
    n_blocks = n_rows // EXPERT_BLOCK
    return pl.pallas_call(
        functools.partial(_scatter_kernel, n_blocks),
        grid_spec=pltpu.PrefetchScalarGridSpec(
            num_scalar_prefetch=4,
            grid=(t_rows // step,),
            in_specs=[pl.BlockSpec((step, D_MODEL), lambda i, *_: (i, 0)),
                      pl.BlockSpec((8, step), lambda i, *_: (0, i))],
            out_specs=pl.BlockSpec(memory_space=pl.ANY),
            scratch_shapes=[pltpu.VMEM((2, LOCAL_SLOTS * TOKEN_ROWS, LANES), U32),
                            pltpu.VMEM((EXPERT_BLOCK * TOKEN_ROWS, LANES), U32),
                            pltpu.SemaphoreType.DMA((2,)), pltpu.SemaphoreType.DMA(())],
        ),
        out_shape=jax.ShapeDtypeStruct((n_rows * TOKEN_ROWS, LANES), U32),
        compiler_params=pltpu.CompilerParams(dimension_semantics=("arbitrary",), vmem_limit_bytes=V7X_VMEM_LIMIT),
        name="scatter",
    )(starts, counts, n_used, chunk_wrow, h1, slots)


def _experts_kernel(be_ref, nb_ref, x_ref, wg_ref, wu_ref, wd_ref, y_ref, wg_bf, wu_bf, wd_bf):
    b = pl.program_id(0)

    @pl.when((b == 0) | (be_ref[b] != be_ref[jnp.maximum(b - 1, 0)]))
    def _():
        wg_bf[...] = wg_ref[...].astype(BF16)
        wu_bf[...] = wu_ref[...].astype(BF16)
        wd_bf[...] = wd_ref[...].astype(BF16)

    @pl.when(b < nb_ref[0])
    def _():
        his, los = _load_token_tiles(x_ref, EXPERT_BLOCK)
        xb = jnp.concatenate(his + los, axis=-1).astype(BF16)
        gate = _dot(xb, wg_bf[...])
        up = _dot(xb, wu_bf[...])
        hid = (gate * jax.nn.sigmoid(gate) * up).astype(BF16)
        _store_token_tiles(y_ref, _dot(hid, wd_bf[...]))

    @pl.when(b >= nb_ref[0])
    def _():
        y_ref[...] = jnp.zeros_like(y_ref)


def _experts(xs, block_e, n_used, w_gate, w_up, w_down):
    blk_rows = EXPERT_BLOCK * TOKEN_ROWS
    return pl.pallas_call(
        _experts_kernel,
        grid_spec=pltpu.PrefetchScalarGridSpec(
            num_scalar_prefetch=2,
            grid=(xs.shape[0] // blk_rows,),
            in_specs=[pl.BlockSpec((blk_rows, LANES), lambda b, be, nb: (jnp.minimum(b, jnp.maximum(nb[0], 1) - 1), 0)),
                      pl.BlockSpec((None, D_MODEL, D_EXPERT), lambda b, be, nb: (be[b], 0, 0)),
                      pl.BlockSpec((None, D_MODEL, D_EXPERT), lambda b, be, nb: (be[b], 0, 0)),
                      pl.BlockSpec((None, D_EXPERT, D_MODEL), lambda b, be, nb: (be[b], 0, 0))],
            out_specs=pl.BlockSpec((blk_rows, LANES), lambda b, be, nb: (b, 0)),
            scratch_shapes=[pltpu.VMEM((D_MODEL, D_EXPERT), BF16), pltpu.VMEM((D_MODEL, D_EXPERT), BF16),
                            pltpu.VMEM((D_EXPERT, D_MODEL), BF16)],
        ),
        out_shape=jax.ShapeDtypeStruct(xs.shape, U32),
        compiler_params=pltpu.CompilerParams(dimension_semantics=("arbitrary",)),
        name="experts",
    )(block_e, n_used, xs, w_gate, w_up, w_down)


def _combine_kernel(n_prompt_tiles, wrow_ref, h_ref, w_ref, slot_ref, ys_hbm,
                    wsg_ref, wsu_ref, wsd_ref, g2_ref, b2_ref, yp_hbm, ysm_hbm, ybuf, obuf, gsems, osems):
    i = pl.program_id(0)
    n_steps = pl.num_programs(0)
    tc = MOE_TILE
    tile_t = tc // BATCH

    def gather(tile, slot):
        _for_each_chunk(
            tile, wrow_ref,
            lambda local, sorted_wrow: _chunk_copy(ys_hbm, sorted_wrow, ybuf.at[slot], local, gsems.at[slot]).start())

    def gather_wait(slot):
        pltpu.make_async_copy(ys_hbm.at[pl.ds(0, LOCAL_SLOTS * TOKEN_ROWS), :], ybuf.at[slot], gsems.at[slot]).wait()

    def out_copies(dst_hbm, t0, slot):
        return [pltpu.make_async_copy(obuf.at[slot, :, b, :], dst_hbm.at[b, pl.ds(t0, tile_t), :], osems.at[slot])
                for b in range(BATCH)]

    def out_wait(slot):
        for cp in out_copies(yp_hbm, 0, slot):
            cp.wait()

    def compute(half, slot):
        rows = slice(half * tc, (half + 1) * tc)
        h = h_ref[rows, :]
        hb = h.astype(BF16)
        gate = _dot(hb, wsg_ref[...])
        up = _dot(hb, wsu_ref[...])
        shared = _dot((gate * jax.nn.sigmoid(gate) * up).astype(BF16), wsd_ref[...])

        eye = (lax.broadcasted_iota(I32, (tc, tc), 0) == lax.broadcasted_iota(I32, (tc, tc), 1)).astype(BF16)
        to_cols = lambda a: lax.dot_general(eye, a.astype(F32).astype(BF16), (((1,), (1,)), ((), ())),
                                            preferred_element_type=F32)
        slots = slot_ref[:, rows]
        w_bf = to_cols(w_ref[:, rows]).astype(BF16)
        s_block = to_cols(slots >> TILE_SHIFT)
        s_offset = to_cols(slots & (tc - 1))

        j = lax.broadcasted_iota(I32, (tc, tc), 1).astype(F32).astype(BF16)
        blocks = []
        for lb in range(LOCAL_SLOTS // tc):
            in_block = jnp.where(s_block == float(lb), s_offset, -1.0).astype(BF16)
            blk = jnp.zeros((tc, tc), BF16)
            for k in range(TOP_K):
                blk = blk + jnp.where(j == in_block[:, k:k + 1], w_bf[:, k:k + 1], jnp.zeros((), BF16))
            blocks.append(blk)
        mix_w = jnp.concatenate(blocks, axis=-1)
        gather_wait(slot)
        his, los = _load_token_tiles(ybuf.at[slot], LOCAL_SLOTS)
        pieces = his + los
        routed = jnp.concatenate(
            [_dot(mix_w, jnp.concatenate(pieces[p:p + 2], axis=-1).astype(BF16)) for p in range(0, len(pieces), 2)],
            axis=-1)
        out = _layer_norm(DEEPNORM_ALPHA * h + (routed + shared), g2_ref[...], b2_ref[...])
        obuf[slot] = out.reshape(tile_t, BATCH, D_MODEL)

        g = 2 * i + half

        @pl.when(g < n_prompt_tiles)
        def _():
            for cp in out_copies(yp_hbm, pl.multiple_of(g * tile_t, tile_t), slot):
                cp.start()

        @pl.when(g >= n_prompt_tiles)
        def _():
            for cp in out_copies(ysm_hbm, pl.multiple_of((g - n_prompt_tiles) * tile_t, tile_t), slot):
                cp.start()

    @pl.when(i == 0)
    def _():
        gather(0, 0)

    gather(2 * i + 1, 1)

    @pl.when(i > 0)
    def _():
        out_wait(0)

    compute(0, 0)

    @pl.when(i + 1 < n_steps)
    def _():
        gather(2 * i + 2, 0)

    @pl.when(i > 0)
    def _():
        out_wait(1)

    compute(1, 1)

    @pl.when(i + 1 == n_steps)
    def _():
        out_wait(0)
        out_wait(1)


def _combine(h1, gate_w, slots, ys, chunk_wrow, ws_gate, ws_up, ws_down, ln2_g, ln2_b, prompt_shape, sample_shape):
    t_rows = h1.shape[0]
    tc = MOE_TILE
    step = 2 * tc
    n_prompt_tiles = prompt_shape[0] * prompt_shape[1] // tc
    const2 = lambda i, *_: (0, 0)
    any_spec = pl.BlockSpec(memory_space=pl.ANY)
    return pl.pallas_call(
        functools.partial(_combine_kernel, n_prompt_tiles),
        grid_spec=pltpu.PrefetchScalarGridSpec(
            num_scalar_prefetch=1,
            grid=(t_rows // step,),
            in_specs=[pl.BlockSpec((step, D_MODEL), lambda i, *_: (i, 0)),
                      pl.BlockSpec((8, step), lambda i, *_: (0, i)),
                      pl.BlockSpec((8, step), lambda i, *_: (0, i)),
                      any_spec,
                      pl.BlockSpec((D_MODEL, D_EXPERT), const2),
                      pl.BlockSpec((D_MODEL, D_EXPERT), const2),
                      pl.BlockSpec((D_EXPERT, D_MODEL), const2),
                      pl.BlockSpec((1, D_MODEL), const2),
                      pl.BlockSpec((1, D_MODEL), const2)],
            out_specs=(any_spec, any_spec),
            scratch_shapes=[pltpu.VMEM((2, LOCAL_SLOTS * TOKEN_ROWS, LANES), U32),
                            pltpu.VMEM((2, tc // BATCH, BATCH, D_MODEL), F32),
                            pltpu.SemaphoreType.DMA((2,)), pltpu.SemaphoreType.DMA((2,))],
        ),
        out_shape=(jax.ShapeDtypeStruct(prompt_shape, F32), jax.ShapeDtypeStruct(sample_shape, F32)),
        compiler_params=pltpu.CompilerParams(dimension_semantics=("arbitrary",), vmem_limit_bytes=V7X_VMEM_LIMIT),
        name="combine",
    )(chunk_wrow, h1, gate_w, slots, ys, ws_gate.astype(BF16), ws_up.astype(BF16),
      ws_down.astype(BF16), ln2_g.reshape(1, D_MODEL), ln2_b.reshape(1, D_MODEL))


def _time_major(x):
    b, l, d = x.shape
    return jnp.transpose(x, (1, 0, 2)).reshape(l * b, d)


def _batch_major(x, b):
    return jnp.transpose(x.reshape(x.shape[0] // b, b, x.shape[1]), (1, 0, 2))


def _block_diag(blocks):
    n, r, c = blocks.shape
    eye = jnp.eye(n, dtype=blocks.dtype)
    return (blocks[:, :, None, :] * eye[:, None, :, None]).reshape(n * r, n * c)


def kernel(x_prompt, x_sample, state_ssm_re, state_ssm_im, cache_conv, ln_in_g, ln_in_b, w_in, lam_re, lam_im, log_dt, ssm_b_re, ssm_b_im, ssm_c_re, ssm_c_im, ssm_d, w_glu, b_glu, conv_w, beta_ssm, beta_conv, w_out, ln1_g, ln1_b, w_router, router_bias, w_gate, w_up, w_down, ws_gate, ws_up, ws_down, ln2_g, ln2_b):
    bp, lp, _ = x_prompt.shape
    bs, ls, _ = x_sample.shape
    assert bp == BATCH and bs == BATCH and ls == CHUNK_T and lp % CHUNK_T == 0
    assert w_in.shape[0] == 1, "single-layer model"
    n_prompt = bp * lp
    row = lambda a: a.reshape(1, -1)

    a_re, a_im, bb_re, bb_im = _prep(lam_re[0], lam_im[0], log_dt[0], ssm_b_re[0], ssm_b_im[0])
    groups_in = 128 // SSM_GROUP
    groups_out = 256 // SSM_GROUP
    bbd = lambda bb: jnp.stack([_block_diag(bb[j * groups_in:(j + 1) * groups_in])
                                for j in range(N_GROUPS // groups_in)]).astype(BF16)
    ct = lambda cc: jnp.transpose(cc, (0, 2, 1))
    cbd = lambda cc: jnp.stack([_block_diag(ct(cc)[j * groups_out:(j + 1) * groups_out])
                                for j in range(N_GROUPS // groups_out)]).astype(BF16)
    mix_weights = (row(ln_in_g), row(ln_in_b), w_in[0].astype(BF16), row(a_re), row(a_im),
                   bbd(bb_re), bbd(bb_im), cbd(ssm_c_re[0]), cbd(-ssm_c_im[0]), row(ssm_d[0]),
                   w_glu[0].astype(BF16), row(b_glu[0]), conv_w[0], row(beta_ssm[0]), row(beta_conv[0]),
                   w_out[0].astype(BF16), row(ln1_g[0]), row(ln1_b[0]))

    h0r = state_ssm_re[0].reshape(BATCH, D_STATE)
    h0i = state_ssm_im[0].reshape(BATCH, D_STATE)
    cbuf = _time_major(cache_conv[0])
    h1, p_re, p_im, p_conv, s_re, s_im, s_conv = _mix(x_prompt, x_sample, h0r, h0i, cbuf, mix_weights)

    t_rows = h1.shape[0]
    assert t_rows % (2 * MOE_TILE) == 0 and n_prompt % MOE_TILE == 0
    n_tiles = t_rows // MOE_TILE
    max_rows = t_rows * TOP_K + n_tiles * N_EXPERTS * (RUN_CHUNK - 1) + N_EXPERTS * EXPERT_BLOCK
    n_blocks = -(-max_rows // EXPERT_BLOCK) + 1
    n_blocks_pad = -(-n_blocks // 128) * 128
    gate_w, slots, chunk_e, chunk_rel, cnt = _route(h1, w_router[0], router_bias[0])
    block_e, n_used, starts, chunk_wrow = _pos(cnt, chunk_e[:, 0], chunk_rel[:, 0], n_blocks_pad,
                                               (n_blocks - 1) * EXPERT_BLOCK)
    n_used, chunk_wrow = n_used[0, :1], chunk_wrow.reshape(-1)
    xs = _scatter(h1, slots, starts[:, 0], cnt[:, 0].astype(I32), n_used, chunk_wrow, n_blocks * EXPERT_BLOCK)
    ys = _experts(xs, block_e[0, :n_blocks], n_used, w_gate[0], w_up[0], w_down[0])
    y_prompt, y_sample = _combine(h1, gate_w, slots, ys, chunk_wrow, ws_gate[0], ws_up[0], ws_down[0],
                                  ln2_g[0], ln2_b[0], x_prompt.shape, x_sample.shape)

    st = lambda s: s.reshape(1, BATCH, N_GROUPS, SSM_STATE)
    cv = lambda t: _batch_major(t, BATCH)[None]
    return (y_prompt, y_sample, st(p_re), st(p_im), cv(p_conv), st(s_re), st(s_im), cv(s_conv))
```

```python
import functools

import jax
import jax.numpy as jnp
from jax import lax
from jax.experimental import pallas as pl
from jax.experimental.pallas import tpu as pltpu

F32 = jnp.float32
BF16 = jnp.bfloat16
I32 = jnp.int32
U32 = jnp.uint32

D_MODEL = 1024
D_SSM = 512
D_CONV = 512
SSM_GROUP = 16
N_GROUPS = 32
SSM_STATE = 64
D_STATE = N_GROUPS * SSM_STATE
N_EXPERTS = 64
TOP_K = 6
N_EXPERT_GROUPS = 8
GROUP_SIZE = N_EXPERTS // N_EXPERT_GROUPS
TOPK_GROUPS = 4
D_EXPERT = 256
ROUTED_SCALE = 2.5
DEPTH = 1
DEEPNORM_ALPHA = (2.0 * DEPTH) ** 0.25
LN_EPS = 1e-5
RMS_EPS = 1e-6

BATCH = 8
CHUNK_T = 64
CHUNK_ROWS = CHUNK_T * BATCH
SCAN_COLS = 512
MOE_TILE = 256
TILE_SHIFT = MOE_TILE.bit_length() - 1
RUN_CHUNK = 8
LOCAL_SLOTS = TOP_K * MOE_TILE + N_EXPERTS * RUN_CHUNK
N_CHUNKS = LOCAL_SLOTS // RUN_CHUNK
EXPERT_BLOCK = 1024
LANES = 128
TOKEN_ROWS = D_MODEL // (2 * LANES)
CHUNK_WROWS = RUN_CHUNK * TOKEN_ROWS
V7X_VMEM_LIMIT = 56 * 1024 * 1024
NEG_INF = float("-inf")


def _layer_norm(x, g, b):
    mu = jnp.mean(x, axis=-1, keepdims=True)
    xc = x - mu
    var = jnp.mean(xc * xc, axis=-1, keepdims=True)
    return xc * lax.rsqrt(var + LN_EPS) * g + b


def _rms_norm(x, g):
    return x * lax.rsqrt(jnp.mean(x * x, axis=-1, keepdims=True) + RMS_EPS) * g


def _dot(a, b):
    return jnp.dot(a, b, preferred_element_type=F32)


def _discretise(lr, li, log_dt):
    dt = jnp.exp(log_dt)
    mag = jnp.exp(lr * dt)
    ar = mag * jnp.cos(li * dt)
    ai = mag * jnp.sin(li * dt)
    den = lr * lr + li * li
    qr = ((ar - 1.0) * lr + ai * li) / den
    qi = (ai * lr - (ar - 1.0) * li) / den
    return ar, ai, qr, qi


def _prep_kernel(lr_ref, li_ref, ldt_ref, lrc_ref, lic_ref, ldtc_ref, br_ref, bi_ref,
                 ar_ref, ai_ref, bbr_ref, bbi_ref):
    ar, ai, _, _ = _discretise(lr_ref[...], li_ref[...], ldt_ref[...])
    ar_ref[...] = ar
    ai_ref[...] = ai
    _, _, qr, qi = _discretise(lrc_ref[...], lic_ref[...], ldtc_ref[...])
    br = br_ref[...]
    bi = bi_ref[...]
    bbr_ref[...] = qr * br - qi * bi
    bbi_ref[...] = qr * bi + qi * br


def _prep(lam_re, lam_im, log_dt, b_re, b_im):
    g, p = lam_re.shape
    per_channel = lambda a: jnp.repeat(a, SSM_GROUP, axis=0)
    rows = lambda b: jnp.transpose(b, (0, 2, 1)).reshape(g * SSM_GROUP, p)
    ldt = log_dt.reshape(g, 1)
    ar, ai, bbr, bbi = pl.pallas_call(
        _prep_kernel,
        out_shape=(jax.ShapeDtypeStruct((g, p), F32), jax.ShapeDtypeStruct((g, p), F32),
                   jax.ShapeDtypeStruct((g * SSM_GROUP, p), F32), jax.ShapeDtypeStruct((g * SSM_GROUP, p), F32)),
        name="prep",
    )(lam_re, lam_im, ldt, per_channel(lam_re), per_channel(lam_im), per_channel(ldt), rows(b_re), rows(b_im))
    return ar, ai, bbr.reshape(g, SSM_GROUP, p), bbi.reshape(g, SSM_GROUP, p)


def _chunk_copies(src_hbm, t0, xbuf, slot, sems):
    return [pltpu.make_async_copy(src_hbm.at[b, pl.ds(t0, CHUNK_T), :], xbuf.at[slot, :, b, :], sems.at[slot])
            for b in range(BATCH)]


def _mix_kernel(n_prompt_chunks,
                xp_hbm, xs_hbm, h0r_ref, h0i_ref, cbuf_ref, lng_ref, lnb_ref, win_ref, ar_ref, ai_ref,
                bbr_ref, bbi_ref, cbr_ref, cbi_ref, dsk_ref, wglu_ref, bglu_ref, cw_ref,
                bs_ref, bc_ref, wout_ref, g1_ref, b1_ref,
                h1_ref, pr_ref, pi_ref, pc_ref, sr_ref, si_ref, sc_ref,
                str_ref, sti_ref, hr_ref, hi_ref, cv_ref, xbuf, xsems):
    c = pl.program_id(0)
    n = CHUNK_ROWS
    carry_rows = 2 * BATCH
    slot = c % 2

    @pl.when(c == 0)
    def _():
        for cp in _chunk_copies(xp_hbm, 0, xbuf, 0, xsems):
            cp.start()

    @pl.when(c + 1 < n_prompt_chunks)
    def _():
        for cp in _chunk_copies(xp_hbm, pl.multiple_of((c + 1) * CHUNK_T, CHUNK_T), xbuf, 1 - slot, xsems):
            cp.start()

    @pl.when(c + 1 == n_prompt_chunks)
    def _():
        for cp in _chunk_copies(xs_hbm, 0, xbuf, 1 - slot, xsems):
            cp.start()

    @pl.when(c == 0)
    def _():
        hr_ref[...] = jnp.zeros_like(hr_ref)
        hi_ref[...] = jnp.zeros_like(hi_ref)
        cv_ref[0:carry_rows, :] = jnp.zeros((carry_rows, D_CONV), F32)

    @pl.when(c == n_prompt_chunks)
    def _():
        hr_ref[...] = h0r_ref[...]
        hi_ref[...] = h0i_ref[...]
        cv_ref[0:carry_rows, :] = cbuf_ref[...]

    for cp in _chunk_copies(xp_hbm, 0, xbuf, slot, xsems):
        cp.wait()
    h = _layer_norm(xbuf[slot].reshape(n, D_MODEL), lng_ref[...], lnb_ref[...])
    hb = h.astype(BF16)

    u = _dot(hb, win_ref[:, 0:D_SSM])
    ub = u.astype(BF16)
    u_tile = 128
    s_tile = u_tile // SSM_GROUP * SSM_STATE
    for j in range(D_SSM // u_tile):
        uj = ub[:, j * u_tile:(j + 1) * u_tile]
        str_ref[:, j * s_tile:(j + 1) * s_tile] = _dot(uj, bbr_ref[j])
        sti_ref[:, j * s_tile:(j + 1) * s_tile] = _dot(uj, bbi_ref[j])

    for cb in range(D_STATE // SCAN_COLS):
        cols = slice(cb * SCAN_COLS, (cb + 1) * SCAN_COLS)
        ar = jnp.broadcast_to(ar_ref[:, cols], (BATCH, SCAN_COLS))
        ai = jnp.broadcast_to(ai_ref[:, cols], (BATCH, SCAN_COLS))

        def step(t, carry, cols=cols, ar=ar, ai=ai):
            sr, si = carry
            r0 = pl.multiple_of(t * BATCH, BATCH)
            nr = ar * sr - ai * si + str_ref[pl.ds(r0, BATCH), cols]
            ni = ar * si + ai * sr + sti_ref[pl.ds(r0, BATCH), cols]
            str_ref[pl.ds(r0, BATCH), cols] = nr
            sti_ref[pl.ds(r0, BATCH), cols] = ni
            return nr, ni

        sr, si = lax.fori_loop(0, CHUNK_T, step, (hr_ref[:, cols], hi_ref[:, cols]), unroll=8)
        hr_ref[:, cols] = sr
        hi_ref[:, cols] = si

    k_tile = 1024
    ys = []
    for j in range(D_STATE // k_tile):
        sl = slice(j * k_tile, (j + 1) * k_tile)
        ys.append(_dot(str_ref[:, sl].astype(BF16), cbr_ref[j]) + _dot(sti_ref[:, sl].astype(BF16), cbi_ref[j]))
    y_ssm = jnp.concatenate(ys, axis=-1) + dsk_ref[...] * u
    g = jax.nn.gelu(y_ssm)
    y_ssm = g * jax.nn.sigmoid(_dot(g.astype(BF16), wglu_ref[...]) + bglu_ref[...])
    mix = _dot(_rms_norm(y_ssm, bs_ref[...]).astype(BF16), wout_ref[0:D_SSM, :])

    gate_b = _dot(hb, win_ref[:, D_SSM:D_SSM + D_CONV])
    gate_c = _dot(hb, win_ref[:, D_SSM + D_CONV:D_SSM + 2 * D_CONV])
    v = _dot(hb, win_ref[:, D_SSM + 2 * D_CONV:D_SSM + 3 * D_CONV])
    cv_ref[carry_rows:carry_rows + n, :] = gate_c * v
    y_conv = gate_b * (cv_ref[0:n, :] * cw_ref[0:1, :]
                       + cv_ref[BATCH:BATCH + n, :] * cw_ref[1:2, :]
                       + cv_ref[carry_rows:carry_rows + n, :] * cw_ref[2:3, :])
    tail = cv_ref[n:n + carry_rows, :]
    cv_ref[0:carry_rows, :] = tail
    mix = mix + _dot(_rms_norm(y_conv, bc_ref[...]).astype(BF16), wout_ref[D_SSM:D_SSM + D_CONV, :])

    h1_ref[...] = _layer_norm(DEEPNORM_ALPHA * h + mix, g1_ref[...], b1_ref[...])

    @pl.when(c == n_prompt_chunks - 1)
    def _():
        pr_ref[...] = hr_ref[...]
        pi_ref[...] = hi_ref[...]
        pc_ref[...] = tail

    @pl.when(c == n_prompt_chunks)
    def _():
        sr_ref[...] = hr_ref[...]
        si_ref[...] = hi_ref[...]
        sc_ref[...] = tail


def _mix(x_prompt, x_sample, h0r, h0i, cbuf, weights):
    n_prompt_chunks = x_prompt.shape[1] // CHUNK_T
    n_chunks = n_prompt_chunks + 1
    t_rows = n_chunks * CHUNK_ROWS
    any_spec = pl.BlockSpec(memory_space=pl.ANY)
    const2 = lambda c: (0, 0)
    const3 = lambda c: (0, 0, 0)
    w_specs = [pl.BlockSpec(w.shape, const3 if w.ndim == 3 else const2) for w in weights]
    state = jax.ShapeDtypeStruct((BATCH, D_STATE), F32)
    tail = jax.ShapeDtypeStruct((2 * BATCH, D_CONV), F32)
    state_spec = pl.BlockSpec((BATCH, D_STATE), const2)
    tail_spec = pl.BlockSpec((2 * BATCH, D_CONV), const2)
    return pl.pallas_call(
        functools.partial(_mix_kernel, n_prompt_chunks),
        grid=(n_chunks,),
        in_specs=[any_spec, any_spec, state_spec, state_spec, tail_spec] + w_specs,
        out_specs=(pl.BlockSpec((CHUNK_ROWS, D_MODEL), lambda c: (c, 0)),
                   state_spec, state_spec, tail_spec, state_spec, state_spec, tail_spec),
        out_shape=(jax.ShapeDtypeStruct((t_rows, D_MODEL), F32), state, state, tail, state, state, tail),
        scratch_shapes=[pltpu.VMEM((CHUNK_ROWS, D_STATE), F32), pltpu.VMEM((CHUNK_ROWS, D_STATE), F32),
                        pltpu.VMEM((BATCH, D_STATE), F32), pltpu.VMEM((BATCH, D_STATE), F32),
                        pltpu.VMEM((CHUNK_ROWS + 2 * BATCH, D_CONV), F32),
                        pltpu.VMEM((2, CHUNK_T, BATCH, D_MODEL), F32), pltpu.SemaphoreType.DMA((2,))],
        compiler_params=pltpu.CompilerParams(dimension_semantics=("arbitrary",), vmem_limit_bytes=V7X_VMEM_LIMIT),
        name="mix",
    )(x_prompt, x_sample, h0r, h0i, cbuf, *weights)


def _route_kernel(h_ref, wrt_ref, bias_ref, tri_ref, ltri_ref, w_ref, slot_ref, ce_ref, crel_ref, cnt_ref, run_col):
    i = pl.program_id(0)
    tr = MOE_TILE

    @pl.when(i == 0)
    def _():
        run_col[...] = jnp.zeros_like(run_col)

    def split(a):
        hi = a.astype(BF16)
        return hi, (a - hi.astype(F32)).astype(BF16)

    nt_dot = lambda a, b: lax.dot_general(a, b, (((1,), (1,)), ((), ())), preferred_element_type=F32)
    w_hi, w_lo = split(wrt_ref[...])
    h_hi, h_lo = split(h_ref[...])
    logits = nt_dot(w_hi, h_hi) + (nt_dot(w_hi, h_lo) + nt_dot(w_lo, h_hi))
    scores = jax.nn.sigmoid(logits)
    sel = scores + bias_ref[...]

    sub = lax.broadcasted_iota(I32, (GROUP_SIZE, tr), 0).astype(F32)
    blocks, gscore = [], []
    for g in range(N_EXPERT_GROUPS):
        blk = sel[g * GROUP_SIZE:(g + 1) * GROUP_SIZE, :]
        m1 = jnp.max(blk, axis=0, keepdims=True)
        first = jnp.min(jnp.where(blk == m1, sub, float(GROUP_SIZE)), axis=0, keepdims=True)
        m2 = jnp.max(jnp.where(sub == first, NEG_INF, blk), axis=0, keepdims=True)
        blocks.append(blk)
        gscore.append(m1 + m2)
    masked = []
    for g in range(N_EXPERT_GROUPS):
        beaten = jnp.zeros((1, tr), F32)
        for o in range(N_EXPERT_GROUPS):
            if o == g:
                continue
            wins = gscore[o] >= gscore[g] if o < g else gscore[o] > gscore[g]
            beaten = beaten + wins.astype(F32)
        masked.append(jnp.where(beaten < float(TOPK_GROUPS), blocks[g], NEG_INF))
    masked = jnp.concatenate(masked, axis=0)

    row = lax.broadcasted_iota(I32, (N_EXPERTS, tr), 0).astype(F32)
    picked = jnp.zeros((N_EXPERTS, tr), F32)
    hots, gates = [], []
    for _ in range(TOP_K):
        m = jnp.max(masked, axis=0, keepdims=True)
        idx = jnp.min(jnp.where(masked == m, row, float(N_EXPERTS)), axis=0, keepdims=True)
        hot = row == idx
        hots.append(hot)
        gates.append(jnp.sum(jnp.where(hot, scores, 0.0), axis=0, keepdims=True))
        masked = jnp.where(hot, NEG_INF, masked)
        picked = picked + hot.astype(F32)
    total = gates[0]
    for k in range(1, TOP_K):
        total = total + gates[k]

    chunk = float(RUN_CHUNK)
    pb = picked.astype(BF16)
    earlier = _dot(pb, tri_ref[...])
    cnt_col = jnp.sum(picked, axis=1, keepdims=True)
    run_len = jnp.floor((cnt_col + (chunk - 1.0)) * (1.0 / chunk)) * chunk
    lower = _dot(ltri_ref[...], jnp.broadcast_to(run_len, (N_EXPERTS, LANES)).astype(BF16))[:, 0:1]
    slot_of = earlier + lower
    for k in range(TOP_K):
        w_ref[k:k + 1, :] = gates[k] / total * ROUTED_SCALE
        slot_ref[k:k + 1, :] = jnp.sum(jnp.where(hots[k], slot_of, 0.0), axis=0, keepdims=True).astype(I32)
    w_ref[TOP_K:, :] = jnp.zeros((8 - TOP_K, tr), F32)
    slot_ref[TOP_K:, :] = jnp.full((8 - TOP_K, tr), -1, I32)

    c_first = lax.broadcasted_iota(I32, (N_EXPERTS, N_CHUNKS), 1).astype(F32) * chunk
    owner = jnp.sum((lower + run_len <= c_first).astype(F32), axis=0, keepdims=True)
    hot_e = lax.broadcasted_iota(I32, (N_EXPERTS, N_CHUNKS), 0).astype(F32) == owner
    rel = jnp.sum(jnp.where(hot_e, run_col[...] - lower + c_first, 0.0), axis=0, keepdims=True)
    ce_ref[0] = jnp.broadcast_to(owner.astype(I32), (8, N_CHUNKS))
    crel_ref[0] = jnp.broadcast_to(rel.astype(I32), (8, N_CHUNKS))
    run_col[...] = run_col[...] + run_len
    cnt_ref[...] = jnp.broadcast_to(run_col[...], cnt_ref.shape)


def _route(h1, w_router, router_bias):
    t_rows = h1.shape[0]
    tr = MOE_TILE
    n_tiles = t_rows // tr
    tri = jnp.triu(jnp.ones((tr, tr), BF16), k=1)
    ltri = jnp.tril(jnp.ones((N_EXPERTS, N_EXPERTS), BF16), k=-1)
    tok = lambda i: (0, i)
    const2 = lambda i: (0, 0)
    per_tile = lambda i: (i, 0, 0)
    return pl.pallas_call(
        _route_kernel,
        grid=(n_tiles,),
        in_specs=[pl.BlockSpec((tr, D_MODEL), lambda i: (i, 0)),
                  pl.BlockSpec((N_EXPERTS, D_MODEL), const2),
                  pl.BlockSpec((N_EXPERTS, 1), const2),
                  pl.BlockSpec((tr, tr), const2),
                  pl.BlockSpec((N_EXPERTS, N_EXPERTS), const2)],
        out_specs=(pl.BlockSpec((8, tr), tok), pl.BlockSpec((8, tr), tok),
                   pl.BlockSpec((1, 8, N_CHUNKS), per_tile), pl.BlockSpec((1, 8, N_CHUNKS), per_tile),
                   pl.BlockSpec((N_EXPERTS, LANES), const2)),
        out_shape=(jax.ShapeDtypeStruct((8, t_rows), F32), jax.ShapeDtypeStruct((8, t_rows), I32),
                   jax.ShapeDtypeStruct((n_tiles, 8, N_CHUNKS), I32), jax.ShapeDtypeStruct((n_tiles, 8, N_CHUNKS), I32),
                   jax.ShapeDtypeStruct((N_EXPERTS, LANES), F32)),
        scratch_shapes=[pltpu.VMEM((N_EXPERTS, 1), F32)],
        compiler_params=pltpu.CompilerParams(dimension_semantics=("arbitrary",)),
        name="route",
    )(h1, w_router.T, router_bias.reshape(N_EXPERTS, 1), tri, ltri)


def _pos_kernel(n_blocks_pad, dummy_row, ccol_ref, crow_ref, ce_ref, crel_ref, be_ref, nb_ref, start_ref, wrow_ref):
    blk = float(EXPERT_BLOCK)
    pad = lambda cnt: jnp.floor((cnt + (blk - 1.0)) * (1.0 / blk)) * blk
    padded_row = pad(crow_ref[0:1, :])
    lane = lax.broadcasted_iota(I32, (N_EXPERTS, 128), 1)
    sub = lax.broadcasted_iota(I32, (N_EXPERTS, 128), 0)
    start = jnp.sum(jnp.where(lane < sub, padded_row, 0.0), axis=1, keepdims=True)
    end = start + pad(ccol_ref[:, 0:1])

    first_row = lax.broadcasted_iota(I32, (N_EXPERTS, n_blocks_pad), 1).astype(F32) * blk
    owner = jnp.sum((end <= first_row).astype(F32), axis=0, keepdims=True)
    be_ref[...] = jnp.broadcast_to(jnp.minimum(owner, float(N_EXPERTS - 1)).astype(I32), be_ref.shape)
    used = jnp.max(end, axis=0, keepdims=True) * (1.0 / blk)
    nb_ref[...] = jnp.broadcast_to(used.astype(I32), nb_ref.shape)
    start_ref[...] = jnp.broadcast_to(start.astype(I32), start_ref.shape)

    ce = ce_ref[...]
    base = jnp.full(ce.shape, float(dummy_row), F32)
    for e in range(N_EXPERTS):
        base = jnp.where(ce == e, start[e:e + 1, 0:1], base)
    wrow_ref[...] = (base.astype(I32) + crel_ref[...]) * TOKEN_ROWS


def _pos(cnt, chunk_e, chunk_rel, n_blocks_pad, dummy_row):
    counts = cnt[:, 0]
    ccol = jnp.broadcast_to(counts[:, None], (N_EXPERTS, 128))
    crow = jnp.broadcast_to(jnp.pad(counts, (0, 128 - N_EXPERTS))[None, :], (8, 128))
    return pl.pallas_call(
        functools.partial(_pos_kernel, n_blocks_pad, dummy_row),
        out_shape=(jax.ShapeDtypeStruct((8, n_blocks_pad), I32), jax.ShapeDtypeStruct((8, 128), I32),
                   jax.ShapeDtypeStruct((N_EXPERTS, 128), I32), jax.ShapeDtypeStruct(chunk_e.shape, I32)),
        name="pos",
    )(ccol, crow, chunk_e, chunk_rel)


def _chunk_copy(src_ref, src_wrow, dst_ref, dst_wrow, sem):
    src = pl.ds(pl.multiple_of(src_wrow, CHUNK_WROWS), CHUNK_WROWS)
    dst = pl.ds(pl.multiple_of(dst_wrow, CHUNK_WROWS), CHUNK_WROWS)
    return pltpu.make_async_copy(src_ref.at[src, :], dst_ref.at[dst, :], sem)


def _for_each_chunk(tile, wrow_ref, fn):
    def per_chunk(c, _):
        fn(c * CHUNK_WROWS, wrow_ref[tile * N_CHUNKS + c])
        return 0

    lax.fori_loop(0, N_CHUNKS, per_chunk, 0, unroll=8)


def _store_token_tiles(dst_ref, x):
    n = x.shape[0]
    half = TOKEN_ROWS * LANES
    bits = lambda v: lax.bitcast_convert_type(v.astype(BF16).astype(F32), U32)
    for s in range(TOKEN_ROWS):
        hi = bits(x[:, s * LANES:(s + 1) * LANES])
        lo = bits(x[:, half + s * LANES:half + (s + 1) * LANES])
        dst_ref[pl.ds(s, n, stride=TOKEN_ROWS), :] = hi | (lo >> 16)


def _load_token_tiles(src_ref, n):
    his, los = [], []
    for s in range(TOKEN_ROWS):
        words = src_ref[pl.ds(s, n, stride=TOKEN_ROWS), :]
        his.append(lax.bitcast_convert_type(words & jnp.uint32(0xFFFF0000), F32))
        los.append(lax.bitcast_convert_type(words << 16, F32))
    return his, los


def _scatter_kernel(n_blocks, start_ref, count_ref, nb_ref, wrow_ref, h_ref, slot_ref, xs_out,
                    tiles, zeros, sems, zsem):
    i = pl.program_id(0)
    ts = MOE_TILE
    blk_rows = EXPERT_BLOCK * TOKEN_ROWS

    @pl.when(i == 0)
    def _():
        zeros[...] = jnp.zeros_like(zeros)
        zero_block = lambda b: pltpu.make_async_copy(
            zeros, xs_out.at[pl.ds(pl.multiple_of(b * blk_rows, blk_rows), blk_rows), :], zsem)

        def pad_rows(e, n_pad):
            lo = (start_ref[e] + count_ref[e]) // RUN_CHUNK
            hi = (start_ref[e] + ((count_ref[e] + (EXPERT_BLOCK - 1)) & (-EXPERT_BLOCK))) // RUN_CHUNK

            def zero_chunk(c, _):
                _chunk_copy(zeros, 0, xs_out, c * CHUNK_WROWS, zsem).start()
                return 0

            lax.fori_loop(lo, hi, zero_chunk, 0)
            return n_pad + (hi - lo)

        n_pad = lax.fori_loop(0, N_EXPERTS, pad_rows, 0)

        def start_block(b, _):
            zero_block(b).start()
            return 0

        lax.fori_loop(nb_ref[0], n_blocks, start_block, 0)

        def wait_chunk(c, _):
            _chunk_copy(zeros, 0, xs_out, 0, zsem).wait()
            return 0

        lax.fori_loop(0, n_pad, wait_chunk, 0)

        def wait_block(b, _):
            zero_block(0).wait()
            return 0

        lax.fori_loop(nb_ref[0], n_blocks, wait_block, 0)

    def drain(half):
        pltpu.make_async_copy(tiles.at[half], xs_out.at[pl.ds(0, LOCAL_SLOTS * TOKEN_ROWS), :], sems.at[half]).wait()

    for half in range(2):
        @pl.when(i > 0)
        def _():
            drain(half)

        h = h_ref[half * ts:(half + 1) * ts, :].astype(BF16)
        slots = slot_ref[:, half * ts:(half + 1) * ts]
        sub = ts // 2
        for jb in range(LOCAL_SLOTS // ts):
            in_block = jnp.where(slots >> TILE_SHIFT == jb, slots & (ts - 1), -1).astype(F32).astype(BF16)
            parts = []
            for r in range(ts // sub):
                j = (lax.broadcasted_iota(I32, (sub, ts), 0) + r * sub).astype(F32).astype(BF16)
                perm = jnp.zeros((sub, ts), BF16)
                for k in range(TOP_K):
                    perm = perm + jnp.where(j == in_block[k:k + 1, :], jnp.ones((), BF16), jnp.zeros((), BF16))
                parts.append(perm)
            _store_token_tiles(tiles.at[half, pl.ds(jb * ts * TOKEN_ROWS, ts * TOKEN_ROWS), :],
                               _dot(jnp.concatenate(parts, axis=0), h))

        _for_each_chunk(
            2 * i + half, wrow_ref,
            lambda local, sorted_wrow, half=half: _chunk_copy(tiles.at[half], local, xs_out, sorted_wrow,
                                                              sems.at[half]).start())

    @pl.when(i == pl.num_programs(0) - 1)
    def _():
        drain(0)
        drain(1)


def _scatter(h1, slots, starts, counts, n_used, chunk_wrow, n_rows):
    t_rows = h1.shape[0]
    step = 2 * MOE_TILE
    n_blocks = n_rows // EXPERT_BLOCK
    return pl.pallas_call(
        functools.partial(_scatter_kernel, n_blocks),
        grid_spec=pltpu.PrefetchScalarGridSpec(
            num_scalar_prefetch=4,
            grid=(t_rows // step,),
            in_specs=[pl.BlockSpec((step, D_MODEL), lambda i, *_: (i, 0)),
                      pl.BlockSpec((8, step), lambda i, *_: (0, i))],
            out_specs=pl.BlockSpec(memory_space=pl.ANY),
            scratch_shapes=[pltpu.VMEM((2, LOCAL_SLOTS * TOKEN_ROWS, LANES), U32),
                            pltpu.VMEM((EXPERT_BLOCK * TOKEN_ROWS, LANES), U32),
                            pltpu.SemaphoreType.DMA((2,)), pltpu.SemaphoreType.DMA(())],
        ),
        out_shape=jax.ShapeDtypeStruct((n_rows * TOKEN_ROWS, LANES), U32),
        compiler_params=pltpu.CompilerParams(dimension_semantics=("arbitrary",), vmem_limit_bytes=V7X_VMEM_LIMIT),
        name="scatter",
    )(starts, counts, n_used, chunk_wrow, h1, slots)


def _experts_kernel(be_ref, nb_ref, x_ref, wg_ref, wu_ref, wd_ref, y_ref, wg_bf, wu_bf, wd_bf):
    b = pl.program_id(0)

    @pl.when((b == 0) | (be_ref[b] != be_ref[jnp.maximum(b - 1, 0)]))
    def _():
        wg_bf[...] = wg_ref[...].astype(BF16)
        wu_bf[...] = wu_ref[...].astype(BF16)
        wd_bf[...] = wd_ref[...].astype(BF16)

    @pl.when(b < nb_ref[0])
    def _():
        his, los = _load_token_tiles(x_ref, EXPERT_BLOCK)
        xb = jnp.concatenate(his + los, axis=-1).astype(BF16)
        gate = _dot(xb, wg_bf[...])
        up = _dot(xb, wu_bf[...])
        hid = (gate * jax.nn.sigmoid(gate) * up).astype(BF16)
        _store_token_tiles(y_ref, _dot(hid, wd_bf[...]))

    @pl.when(b >= nb_ref[0])
    def _():
        y_ref[...] = jnp.zeros_like(y_ref)


def _experts(xs, block_e, n_used, w_gate, w_up, w_down):
    blk_rows = EXPERT_BLOCK * TOKEN_ROWS
    return pl.pallas_call(
        _experts_kernel,
        grid_spec=pltpu.PrefetchScalarGridSpec(
            num_scalar_prefetch=2,
            grid=(xs.shape[0] // blk_rows,),
            in_specs=[pl.BlockSpec((blk_rows, LANES), lambda b, be, nb: (jnp.minimum(b, jnp.maximum(nb[0], 1) - 1), 0)),
                      pl.BlockSpec((None, D_MODEL, D_EXPERT), lambda b, be, nb: (be[b], 0, 0)),
                      pl.BlockSpec((None, D_MODEL, D_EXPERT), lambda b, be, nb: (be[b], 0, 0)),
                      pl.BlockSpec((None, D_EXPERT, D_MODEL), lambda b, be, nb: (be[b], 0, 0))],
            out_specs=pl.BlockSpec((blk_rows, LANES), lambda b, be, nb: (b, 0)),
            scratch_shapes=[pltpu.VMEM((D_MODEL, D_EXPERT), BF16), pltpu.VMEM((D_MODEL, D_EXPERT), BF16),
                            pltpu.VMEM((D_EXPERT, D_MODEL), BF16)],
        ),
        out_shape=jax.ShapeDtypeStruct(xs.shape, U32),
        compiler_params=pltpu.CompilerParams(dimension_semantics=("arbitrary",)),
        name="experts",
    )(block_e, n_used, xs, w_gate, w_up, w_down)


def _combine_kernel(n_prompt_tiles, wrow_ref, h_ref, w_ref, slot_ref, ys_hbm,
                    wsg_ref, wsu_ref, wsd_ref, g2_ref, b2_ref, yp_hbm, ysm_hbm, ybuf, obuf, gsems, osems):
    i = pl.program_id(0)
    n_steps = pl.num_programs(0)
    tc = MOE_TILE
    tile_t = tc // BATCH

    def gather(tile, slot):
        _for_each_chunk(
            tile, wrow_ref,
            lambda local, sorted_wrow: _chunk_copy(ys_hbm, sorted_wrow, ybuf.at[slot], local, gsems.at[slot]).start())

    def gather_wait(slot):
        pltpu.make_async_copy(ys_hbm.at[pl.ds(0, LOCAL_SLOTS * TOKEN_ROWS), :], ybuf.at[slot], gsems.at[slot]).wait()

    def out_copies(dst_hbm, t0, slot):
        return [pltpu.make_async_copy(obuf.at[slot, :, b, :], dst_hbm.at[b, pl.ds(t0, tile_t), :], osems.at[slot])
                for b in range(BATCH)]

    def out_wait(slot):
        for cp in out_copies(yp_hbm, 0, slot):
            cp.wait()

    def compute(half, slot):
        rows = slice(half * tc, (half + 1) * tc)
        h = h_ref[rows, :]
        hb = h.astype(BF16)
        gate = _dot(hb, wsg_ref[...])
        up = _dot(hb, wsu_ref[...])
        shared = _dot((gate * jax.nn.sigmoid(gate) * up).astype(BF16), wsd_ref[...])

        eye = (lax.broadcasted_iota(I32, (tc, tc), 0) == lax.broadcasted_iota(I32, (tc, tc), 1)).astype(BF16)
        to_cols = lambda a: lax.dot_general(eye, a.astype(F32).astype(BF16), (((1,), (1,)), ((), ())),
                                            preferred_element_type=F32)
        slots = slot_ref[:, rows]
        w_bf = to_cols(w_ref[:, rows]).astype(BF16)
        s_block = to_cols(slots >> TILE_SHIFT)
        s_offset = to_cols(slots & (tc - 1))

        j = lax.broadcasted_iota(I32, (tc, tc), 1).astype(F32).astype(BF16)
        blocks = []
        for lb in range(LOCAL_SLOTS // tc):
            in_block = jnp.where(s_block == float(lb), s_offset, -1.0).astype(BF16)
            blk = jnp.zeros((tc, tc), BF16)
            for k in range(TOP_K):
                blk = blk + jnp.where(j == in_block[:, k:k + 1], w_bf[:, k:k + 1], jnp.zeros((), BF16))
            blocks.append(blk)
        mix_w = jnp.concatenate(blocks, axis=-1)
        gather_wait(slot)
        his, los = _load_token_tiles(ybuf.at[slot], LOCAL_SLOTS)
        pieces = his + los
        routed = jnp.concatenate(
            [_dot(mix_w, jnp.concatenate(pieces[p:p + 2], axis=-1).astype(BF16)) for p in range(0, len(pieces), 2)],
            axis=-1)
        out = _layer_norm(DEEPNORM_ALPHA * h + (routed + shared), g2_ref[...], b2_ref[...])
        obuf[slot] = out.reshape(tile_t, BATCH, D_MODEL)

        g = 2 * i + half

        @pl.when(g < n_prompt_tiles)
        def _():
            for cp in out_copies(yp_hbm, pl.multiple_of(g * tile_t, tile_t), slot):
                cp.start()

        @pl.when(g >= n_prompt_tiles)
        def _():
            for cp in out_copies(ysm_hbm, pl.multiple_of((g - n_prompt_tiles) * tile_t, tile_t), slot):
                cp.start()

    @pl.when(i == 0)
    def _():
        gather(0, 0)

    gather(2 * i + 1, 1)

    @pl.when(i > 0)
    def _():
        out_wait(0)

    compute(0, 0)

    @pl.when(i + 1 < n_steps)
    def _():
        gather(2 * i + 2, 0)

    @pl.when(i > 0)
    def _():
        out_wait(1)

    compute(1, 1)

    @pl.when(i + 1 == n_steps)
    def _():
        out_wait(0)
        out_wait(1)


def _combine(h1, gate_w, slots, ys, chunk_wrow, ws_gate, ws_up, ws_down, ln2_g, ln2_b, prompt_shape, sample_shape):
    t_rows = h1.shape[0]
    tc = MOE_TILE
    step = 2 * tc
    n_prompt_tiles = prompt_shape[0] * prompt_shape[1] // tc
    const2 = lambda i, *_: (0, 0)
    any_spec = pl.BlockSpec(memory_space=pl.ANY)
    return pl.pallas_call(
        functools.partial(_combine_kernel, n_prompt_tiles),
        grid_spec=pltpu.PrefetchScalarGridSpec(
            num_scalar_prefetch=1,
            grid=(t_rows // step,),
            in_specs=[pl.BlockSpec((step, D_MODEL), lambda i, *_: (i, 0)),
                      pl.BlockSpec((8, step), lambda i, *_: (0, i)),
                      pl.BlockSpec((8, step), lambda i, *_: (0, i)),
                      any_spec,
                      pl.BlockSpec((D_MODEL, D_EXPERT), const2),
                      pl.BlockSpec((D_MODEL, D_EXPERT), const2),
                      pl.BlockSpec((D_EXPERT, D_MODEL), const2),
                      pl.BlockSpec((1, D_MODEL), const2),
                      pl.BlockSpec((1, D_MODEL), const2)],
            out_specs=(any_spec, any_spec),
            scratch_shapes=[pltpu.VMEM((2, LOCAL_SLOTS * TOKEN_ROWS, LANES), U32),
                            pltpu.VMEM((2, tc // BATCH, BATCH, D_MODEL), F32),
                            pltpu.SemaphoreType.DMA((2,)), pltpu.SemaphoreType.DMA((2,))],
        ),
        out_shape=(jax.ShapeDtypeStruct(prompt_shape, F32), jax.ShapeDtypeStruct(sample_shape, F32)),
        compiler_params=pltpu.CompilerParams(dimension_semantics=("arbitrary",), vmem_limit_bytes=V7X_VMEM_LIMIT),
        name="combine",
    )(chunk_wrow, h1, gate_w, slots, ys, ws_gate.astype(BF16), ws_up.astype(BF16),
      ws_down.astype(BF16), ln2_g.reshape(1, D_MODEL), ln2_b.reshape(1, D_MODEL))


def _time_major(x):
    b, l, d = x.shape
    return jnp.transpose(x, (1, 0, 2)).reshape(l * b, d)


def _batch_major(x, b):
    return jnp.transpose(x.reshape(x.shape[0] // b, b, x.shape[1]), (1, 0, 2))


def _block_diag(blocks):
    n, r, c = blocks.shape
    eye = jnp.eye(n, dtype=blocks.dtype)
    return (blocks[:, :, None, :] * eye[:, None, :, None]).reshape(n * r, n * c)


def kernel(x_prompt, x_sample, state_ssm_re, state_ssm_im, cache_conv, ln_in_g, ln_in_b, w_in, lam_re, lam_im, log_dt, ssm_b_re, ssm_b_im, ssm_c_re, ssm_c_im, ssm_d, w_glu, b_glu, conv_w, beta_ssm, beta_conv, w_out, ln1_g, ln1_b, w_router, router_bias, w_gate, w_up, w_down, ws_gate, ws_up, ws_down, ln2_g, ln2_b):
    bp, lp, _ = x_prompt.shape
    bs, ls, _ = x_sample.shape
    assert bp == BATCH and bs == BATCH and ls == CHUNK_T and lp % CHUNK_T == 0
    assert w_in.shape[0] == 1, "single-layer model"
    n_prompt = bp * lp
    row = lambda a: a.reshape(1, -1)

    a_re, a_im, bb_re, bb_im = _prep(lam_re[0], lam_im[0], log_dt[0], ssm_b_re[0], ssm_b_im[0])
    groups_in = 128 // SSM_GROUP
    groups_out = 256 // SSM_GROUP
    bbd = lambda bb: jnp.stack([_block_diag(bb[j * groups_in:(j + 1) * groups_in])
                                for j in range(N_GROUPS // groups_in)]).astype(BF16)
    ct = lambda cc: jnp.transpose(cc, (0, 2, 1))
    cbd = lambda cc: jnp.stack([_block_diag(ct(cc)[j * groups_out:(j + 1) * groups_out])
                                for j in range(N_GROUPS // groups_out)]).astype(BF16)
    mix_weights = (row(ln_in_g), row(ln_in_b), w_in[0].astype(BF16), row(a_re), row(a_im),
                   bbd(bb_re), bbd(bb_im), cbd(ssm_c_re[0]), cbd(-ssm_c_im[0]), row(ssm_d[0]),
                   w_glu[0].astype(BF16), row(b_glu[0]), conv_w[0], row(beta_ssm[0]), row(beta_conv[0]),
                   w_out[0].astype(BF16), row(ln1_g[0]), row(ln1_b[0]))

    h0r = state_ssm_re[0].reshape(BATCH, D_STATE)
    h0i = state_ssm_im[0].reshape(BATCH, D_STATE)
    cbuf = _time_major(cache_conv[0])
    h1, p_re, p_im, p_conv, s_re, s_im, s_conv = _mix(x_prompt, x_sample, h0r, h0i, cbuf, mix_weights)

    t_rows = h1.shape[0]
    assert t_rows % (2 * MOE_TILE) == 0 and n_prompt % MOE_TILE == 0
    n_tiles = t_rows // MOE_TILE
    max_rows = t_rows * TOP_K + n_tiles * N_EXPERTS * (RUN_CHUNK - 1) + N_EXPERTS * EXPERT_BLOCK
    n_blocks = -(-max_rows // EXPERT_BLOCK) + 1
    n_blocks_pad = -(-n_blocks // 128) * 128
    gate_w, slots, chunk_e, chunk_rel, cnt = _route(h1, w_router[0], router_bias[0])
    block_e, n_used, starts, chunk_wrow = _pos(cnt, chunk_e[:, 0], chunk_rel[:, 0], n_blocks_pad,
                                               (n_blocks - 1) * EXPERT_BLOCK)
    n_used, chunk_wrow = n_used[0, :1], chunk_wrow.reshape(-1)
    xs = _scatter(h1, slots, starts[:, 0], cnt[:, 0].astype(I32), n_used, chunk_wrow, n_blocks * EXPERT_BLOCK)
    ys = _experts(xs, block_e[0, :n_blocks], n_used, w_gate[0], w_up[0], w_down[0])
    y_prompt, y_sample = _combine(h1, gate_w, slots, ys, chunk_wrow, ws_gate[0], ws_up[0], ws_down[0],
                                  ln2_g[0], ln2_b[0], x_prompt.shape, x_sample.shape)

    st = lambda s: s.reshape(1, BATCH, N_GROUPS, SSM_STATE)
    cv = lambda t: _batch_major(t, BATCH)[None]
    return (y_prompt, y_sample, st(p_re), st(p_im), cv(p_conv), st(s_re), st(s_im), cv(s_conv))
```

```python
import functools

import jax
import jax.numpy as jnp
from jax import lax
from jax.experimental import pallas as pl
from jax.experimental.pallas import tpu as pltpu

F32 = jnp.float32
BF16 = jnp.bfloat16
I32 = jnp.int32
U32 = jnp.uint32

D_MODEL = 1024
D_SSM = 512
D_CONV = 512
SSM_GROUP = 16
N_GROUPS = 32
SSM_STATE = 64
D_STATE = N_GROUPS * SSM_STATE
N_EXPERTS = 64
TOP_K = 6
N_EXPERT_GROUPS = 8
GROUP_SIZE = N_EXPERTS // N_EXPERT_GROUPS
TOPK_GROUPS = 4
D_EXPERT = 256
ROUTED_SCALE = 2.5
DEPTH = 1
DEEPNORM_ALPHA = (2.0 * DEPTH) ** 0.25
LN_EPS = 1e-5
RMS_EPS = 1e-6

BATCH = 8
CHUNK_T = 64
CHUNK_ROWS = CHUNK_T * BATCH
SCAN_COLS = 512
MOE_TILE = 256
TILE_SHIFT = MOE_TILE.bit_length() - 1
ROUTE_TILES_PER_STEP = 1
RUN_CHUNK = 8
LOCAL_SLOTS = TOP_K * MOE_TILE + N_EXPERTS * RUN_CHUNK
N_CHUNKS = LOCAL_SLOTS // RUN_CHUNK
EXPERT_BLOCK = 1024
PAD_PIECES = tuple(1 << b for b in range(EXPERT_BLOCK.bit_length() - 2, RUN_CHUNK.bit_length() - 2, -1))
LANES = 128
TOKEN_ROWS = D_MODEL // (2 * LANES)
CHUNK_WROWS = RUN_CHUNK * TOKEN_ROWS
V7X_VMEM_LIMIT = 56 * 1024 * 1024
NEG_INF = float("-inf")


def _layer_norm(x, g, b):
    mu = jnp.mean(x, axis=-1, keepdims=True)
    xc = x - mu
    var = jnp.mean(xc * xc, axis=-1, keepdims=True)
    return xc * lax.rsqrt(var + LN_EPS) * g + b


def _rms_norm(x, g):
    return x * lax.rsqrt(jnp.mean(x * x, axis=-1, keepdims=True) + RMS_EPS) * g


def _dot(a, b):
    return jnp.dot(a, b, preferred_element_type=F32)


def _discretise(lr, li, log_dt):
    dt = jnp.exp(log_dt)
    mag = jnp.exp(lr * dt)
    ar = mag * jnp.cos(li * dt)
    ai = mag * jnp.sin(li * dt)
    den = lr * lr + li * li
    qr = ((ar - 1.0) * lr + ai * li) / den
    qi = (ai * lr - (ar - 1.0) * li) / den
    return ar, ai, qr, qi


def _prep_kernel(lr_ref, li_ref, ldt_ref, lrc_ref, lic_ref, ldtc_ref, br_ref, bi_ref,
                 ar_ref, ai_ref, bbr_ref, bbi_ref):
    ar, ai, _, _ = _discretise(lr_ref[...], li_ref[...], ldt_ref[...])
    ar_ref[...] = ar
    ai_ref[...] = ai
    _, _, qr, qi = _discretise(lrc_ref[...], lic_ref[...], ldtc_ref[...])
    br = br_ref[...]
    bi = bi_ref[...]
    bbr_ref[...] = qr * br - qi * bi
    bbi_ref[...] = qr * bi + qi * br


def _prep(lam_re, lam_im, log_dt, b_re, b_im):
    g, p = lam_re.shape
    per_channel = lambda a: jnp.repeat(a, SSM_GROUP, axis=0)
    rows = lambda b: jnp.transpose(b, (0, 2, 1)).reshape(g * SSM_GROUP, p)
    ldt = log_dt.reshape(g, 1)
    ar, ai, bbr, bbi = pl.pallas_call(
        _prep_kernel,
        out_shape=(jax.ShapeDtypeStruct((g, p), F32), jax.ShapeDtypeStruct((g, p), F32),
                   jax.ShapeDtypeStruct((g * SSM_GROUP, p), F32), jax.ShapeDtypeStruct((g * SSM_GROUP, p), F32)),
        name="prep",
    )(lam_re, lam_im, ldt, per_channel(lam_re), per_channel(lam_im), per_channel(ldt), rows(b_re), rows(b_im))
    return ar, ai, bbr.reshape(g, SSM_GROUP, p), bbi.reshape(g, SSM_GROUP, p)


def _chunk_copies(src_hbm, t0, xbuf, slot, sems):
    return [pltpu.make_async_copy(src_hbm.at[b, pl.ds(t0, CHUNK_T), :], xbuf.at[slot, :, b, :], sems.at[slot])
            for b in range(BATCH)]


def _mix_kernel(n_prompt_chunks,
                xp_hbm, xs_hbm, h0r_ref, h0i_ref, cbuf_ref, lng_ref, lnb_ref, win_ref, ar_ref, ai_ref,
                bbr_ref, bbi_ref, cbr_ref, cbi_ref, dsk_ref, wglu_ref, bglu_ref, cw_ref,
                bs_ref, bc_ref, wout_ref, g1_ref, b1_ref,
                h1_ref, pr_ref, pi_ref, pc_ref, sr_ref, si_ref, sc_ref,
                str_ref, sti_ref, hr_ref, hi_ref, cv_ref, xbuf, xsems):
    c = pl.program_id(0)
    n = CHUNK_ROWS
    carry_rows = 2 * BATCH
    slot = c % 2

    @pl.when(c == 0)
    def _():
        for cp in _chunk_copies(xp_hbm, 0, xbuf, 0, xsems):
            cp.start()

    @pl.when(c + 1 < n_prompt_chunks)
    def _():
        for cp in _chunk_copies(xp_hbm, pl.multiple_of((c + 1) * CHUNK_T, CHUNK_T), xbuf, 1 - slot, xsems):
            cp.start()

    @pl.when(c + 1 == n_prompt_chunks)
    def _():
        for cp in _chunk_copies(xs_hbm, 0, xbuf, 1 - slot, xsems):
            cp.start()

    @pl.when(c == 0)
    def _():
        hr_ref[...] = jnp.zeros_like(hr_ref)
        hi_ref[...] = jnp.zeros_like(hi_ref)
        cv_ref[0:carry_rows, :] = jnp.zeros((carry_rows, D_CONV), F32)

    @pl.when(c == n_prompt_chunks)
    def _():
        hr_ref[...] = h0r_ref[...]
        hi_ref[...] = h0i_ref[...]
        cv_ref[0:carry_rows, :] = cbuf_ref[...]

    for cp in _chunk_copies(xp_hbm, 0, xbuf, slot, xsems):
        cp.wait()
    h = _layer_norm(xbuf[slot].reshape(n, D_MODEL), lng_ref[...], lnb_ref[...])
    hb = h.astype(BF16)

    u = _dot(hb, win_ref[:, 0:D_SSM])
    ub = u.astype(BF16)
    u_tile = 128
    s_tile = u_tile // SSM_GROUP * SSM_STATE
    for j in range(D_SSM // u_tile):
        uj = ub[:, j * u_tile:(j + 1) * u_tile]
        str_ref[:, j * s_tile:(j + 1) * s_tile] = _dot(uj, bbr_ref[j])
        sti_ref[:, j * s_tile:(j + 1) * s_tile] = _dot(uj, bbi_ref[j])

    gate_b = _dot(hb, win_ref[:, D_SSM:D_SSM + D_CONV])
    gate_c = _dot(hb, win_ref[:, D_SSM + D_CONV:D_SSM + 2 * D_CONV])
    v = _dot(hb, win_ref[:, D_SSM + 2 * D_CONV:D_SSM + 3 * D_CONV])
    cv_ref[carry_rows:carry_rows + n, :] = gate_c * v

    for cb in range(D_STATE // SCAN_COLS):
        cols = slice(cb * SCAN_COLS, (cb + 1) * SCAN_COLS)
        ar = jnp.broadcast_to(ar_ref[:, cols], (BATCH, SCAN_COLS))
        ai = jnp.broadcast_to(ai_ref[:, cols], (BATCH, SCAN_COLS))

        def step(t, carry, cols=cols, ar=ar, ai=ai):
            sr, si = carry
            r0 = pl.multiple_of(t * BATCH, BATCH)
            nr = ar * sr - ai * si + str_ref[pl.ds(r0, BATCH), cols]
            ni = ar * si + ai * sr + sti_ref[pl.ds(r0, BATCH), cols]
            str_ref[pl.ds(r0, BATCH), cols] = nr
            sti_ref[pl.ds(r0, BATCH), cols] = ni
            return nr, ni

        sr, si = lax.fori_loop(0, CHUNK_T, step, (hr_ref[:, cols], hi_ref[:, cols]), unroll=True)
        hr_ref[:, cols] = sr
        hi_ref[:, cols] = si

    k_tile = 1024
    ys = []
    for j in range(D_STATE // k_tile):
        sl = slice(j * k_tile, (j + 1) * k_tile)
        ys.append(_dot(str_ref[:, sl].astype(BF16), cbr_ref[j]) + _dot(sti_ref[:, sl].astype(BF16), cbi_ref[j]))
    y_ssm = jnp.concatenate(ys, axis=-1) + dsk_ref[...] * u
    g = jax.nn.gelu(y_ssm)
    y_ssm = g * jax.nn.sigmoid(_dot(g.astype(BF16), wglu_ref[...]) + bglu_ref[...])
    mix = _dot(_rms_norm(y_ssm, bs_ref[...]).astype(BF16), wout_ref[0:D_SSM, :])

    y_conv = gate_b * (cv_ref[0:n, :] * cw_ref[0:1, :]
                       + cv_ref[BATCH:BATCH + n, :] * cw_ref[1:2, :]
                       + cv_ref[carry_rows:carry_rows + n, :] * cw_ref[2:3, :])
    tail = cv_ref[n:n + carry_rows, :]
    cv_ref[0:carry_rows, :] = tail
    mix = mix + _dot(_rms_norm(y_conv, bc_ref[...]).astype(BF16), wout_ref[D_SSM:D_SSM + D_CONV, :])

    h1_ref[...] = _layer_norm(DEEPNORM_ALPHA * h + mix, g1_ref[...], b1_ref[...])

    @pl.when(c == n_prompt_chunks - 1)
    def _():
        pr_ref[...] = hr_ref[...]
        pi_ref[...] = hi_ref[...]
        pc_ref[...] = tail

    @pl.when(c == n_prompt_chunks)
    def _():
        sr_ref[...] = hr_ref[...]
        si_ref[...] = hi_ref[...]
        sc_ref[...] = tail


def _mix(x_prompt, x_sample, h0r, h0i, cbuf, weights):
    n_prompt_chunks = x_prompt.shape[1] // CHUNK_T
    n_chunks = n_prompt_chunks + 1
    t_rows = n_chunks * CHUNK_ROWS
    any_spec = pl.BlockSpec(memory_space=pl.ANY)
    const2 = lambda c: (0, 0)
    const3 = lambda c: (0, 0, 0)
    w_specs = [pl.BlockSpec(w.shape, const3 if w.ndim == 3 else const2) for w in weights]
    state = jax.ShapeDtypeStruct((BATCH, D_STATE), F32)
    tail = jax.ShapeDtypeStruct((2 * BATCH, D_CONV), F32)
    state_spec = pl.BlockSpec((BATCH, D_STATE), const2)
    tail_spec = pl.BlockSpec((2 * BATCH, D_CONV), const2)
    return pl.pallas_call(
        functools.partial(_mix_kernel, n_prompt_chunks),
        grid=(n_chunks,),
        in_specs=[any_spec, any_spec, state_spec, state_spec, tail_spec] + w_specs,
        out_specs=(pl.BlockSpec((CHUNK_ROWS, D_MODEL), lambda c: (c, 0)),
                   state_spec, state_spec, tail_spec, state_spec, state_spec, tail_spec),
        out_shape=(jax.ShapeDtypeStruct((t_rows, D_MODEL), F32), state, state, tail, state, state, tail),
        scratch_shapes=[pltpu.VMEM((CHUNK_ROWS, D_STATE), F32), pltpu.VMEM((CHUNK_ROWS, D_STATE), F32),
                        pltpu.VMEM((BATCH, D_STATE), F32), pltpu.VMEM((BATCH, D_STATE), F32),
                        pltpu.VMEM((CHUNK_ROWS + 2 * BATCH, D_CONV), F32),
                        pltpu.VMEM((2, CHUNK_T, BATCH, D_MODEL), F32), pltpu.SemaphoreType.DMA((2,))],
        compiler_params=pltpu.CompilerParams(dimension_semantics=("arbitrary",), vmem_limit_bytes=V7X_VMEM_LIMIT),
        name="mix",
    )(x_prompt, x_sample, h0r, h0i, cbuf, *weights)


def _route_tile(h, w_hi, w_lo, bias, tri, ltri, run_before):
    tr = MOE_TILE

    nt_dot = lambda a, b: lax.dot_general(a, b, (((1,), (1,)), ((), ())), preferred_element_type=F32)
    h_hi = h.astype(BF16)
    h_lo = (h - h_hi.astype(F32)).astype(BF16)
    logits = nt_dot(w_hi, h_hi) + (nt_dot(w_hi, h_lo) + nt_dot(w_lo, h_hi))
    scores = jax.nn.sigmoid(logits)
    sel = scores + bias

    sub = lax.broadcasted_iota(I32, (GROUP_SIZE, tr), 0).astype(F32)
    blocks, gscore = [], []
    for g in range(N_EXPERT_GROUPS):
        blk = sel[g * GROUP_SIZE:(g + 1) * GROUP_SIZE, :]
        m1 = jnp.max(blk, axis=0, keepdims=True)
        first = jnp.min(jnp.where(blk == m1, sub, float(GROUP_SIZE)), axis=0, keepdims=True)
        m2 = jnp.max(jnp.where(sub == first, NEG_INF, blk), axis=0, keepdims=True)
        blocks.append(blk)
        gscore.append(m1 + m2)
    masked = []
    for g in range(N_EXPERT_GROUPS):
        beaten = jnp.zeros((1, tr), F32)
        for o in range(N_EXPERT_GROUPS):
            if o == g:
                continue
            wins = gscore[o] >= gscore[g] if o < g else gscore[o] > gscore[g]
            beaten = beaten + wins.astype(F32)
        masked.append(jnp.where(beaten < float(TOPK_GROUPS), blocks[g], NEG_INF))
    masked = jnp.concatenate(masked, axis=0)

    row = lax.broadcasted_iota(I32, (N_EXPERTS, tr), 0).astype(F32)
    picked = jnp.zeros((N_EXPERTS, tr), F32)
    hots, gates = [], []
    for _ in range(TOP_K):
        m = jnp.max(masked, axis=0, keepdims=True)
        idx = jnp.min(jnp.where(masked == m, row, float(N_EXPERTS)), axis=0, keepdims=True)
        hot = row == idx
        hots.append(hot)
        gates.append(jnp.sum(jnp.where(hot, scores, 0.0), axis=0, keepdims=True))
        masked = jnp.where(hot, NEG_INF, masked)
        picked = picked + hot.astype(F32)
    total = gates[0]
    for k in range(1, TOP_K):
        total = total + gates[k]

    chunk = float(RUN_CHUNK)
    pb = picked.astype(BF16)
    earlier = _dot(pb, tri)
    cnt_col = jnp.sum(picked, axis=1, keepdims=True)
    run_len = jnp.floor((cnt_col + (chunk - 1.0)) * (1.0 / chunk)) * chunk
    lower = _dot(ltri, jnp.broadcast_to(run_len, (N_EXPERTS, LANES)).astype(BF16))[:, 0:1]
    slot_of = earlier + lower
    w_rows = [gates[k] / total * ROUTED_SCALE for k in range(TOP_K)]
    slot_rows = [jnp.sum(jnp.where(hots[k], slot_of, 0.0), axis=0, keepdims=True).astype(I32) for k in range(TOP_K)]

    c_first = lax.broadcasted_iota(I32, (N_EXPERTS, N_CHUNKS), 1).astype(F32) * chunk
    owner = jnp.sum((lower + run_len <= c_first).astype(F32), axis=0, keepdims=True)
    hot_e = lax.broadcasted_iota(I32, (N_EXPERTS, N_CHUNKS), 0).astype(F32) == owner
    rel = jnp.sum(jnp.where(hot_e, run_before - lower + c_first, 0.0), axis=0, keepdims=True)
    return w_rows, slot_rows, owner.astype(I32), rel.astype(I32), run_len


def _route_kernel(h_ref, wrt_ref, bias_ref, tri_ref, ltri_ref, w_ref, slot_ref, ce_ref, crel_ref, cnt_ref, run_col):
    i = pl.program_id(0)
    tr = MOE_TILE

    @pl.when(i == 0)
    def _():
        run_col[...] = jnp.zeros_like(run_col)

    wrt = wrt_ref[...]
    w_hi = wrt.astype(BF16)
    w_lo = (wrt - w_hi.astype(F32)).astype(BF16)
    run = run_col[...]
    for half in range(ROUTE_TILES_PER_STEP):
        cols = slice(half * tr, (half + 1) * tr)
        w_rows, slot_rows, owner, rel, run_len = _route_tile(
            h_ref[cols, :], w_hi, w_lo, bias_ref[...], tri_ref[...], ltri_ref[...], run)
        for k in range(TOP_K):
            w_ref[k:k + 1, cols] = w_rows[k]
            slot_ref[k:k + 1, cols] = slot_rows[k]
        w_ref[TOP_K:, cols] = jnp.zeros((8 - TOP_K, tr), F32)
        slot_ref[TOP_K:, cols] = jnp.full((8 - TOP_K, tr), -1, I32)
        ce_ref[half] = jnp.broadcast_to(owner, (8, N_CHUNKS))
        crel_ref[half] = jnp.broadcast_to(rel, (8, N_CHUNKS))
        run = run + run_len
    run_col[...] = run
    cnt_ref[...] = jnp.broadcast_to(run, cnt_ref.shape)


def _route(h1, w_router, router_bias):
    t_rows = h1.shape[0]
    tr = MOE_TILE
    n_tiles = t_rows // tr
    step = ROUTE_TILES_PER_STEP * tr
    tri = jnp.triu(jnp.ones((tr, tr), BF16), k=1)
    ltri = jnp.tril(jnp.ones((N_EXPERTS, N_EXPERTS), BF16), k=-1)
    tok = lambda i: (0, i)
    const2 = lambda i: (0, 0)
    per_tile = lambda i: (i, 0, 0)
    return pl.pallas_call(
        _route_kernel,
        grid=(t_rows // step,),
        in_specs=[pl.BlockSpec((step, D_MODEL), lambda i: (i, 0)),
                  pl.BlockSpec((N_EXPERTS, D_MODEL), const2),
                  pl.BlockSpec((N_EXPERTS, 1), const2),
                  pl.BlockSpec((tr, tr), const2),
                  pl.BlockSpec((N_EXPERTS, N_EXPERTS), const2)],
        out_specs=(pl.BlockSpec((8, step), tok), pl.BlockSpec((8, step), tok),
                   pl.BlockSpec((ROUTE_TILES_PER_STEP, 8, N_CHUNKS), per_tile),
                   pl.BlockSpec((ROUTE_TILES_PER_STEP, 8, N_CHUNKS), per_tile),
                   pl.BlockSpec((N_EXPERTS, LANES), const2)),
        out_shape=(jax.ShapeDtypeStruct((8, t_rows), F32), jax.ShapeDtypeStruct((8, t_rows), I32),
                   jax.ShapeDtypeStruct((n_tiles, 8, N_CHUNKS), I32), jax.ShapeDtypeStruct((n_tiles, 8, N_CHUNKS), I32),
                   jax.ShapeDtypeStruct((N_EXPERTS, LANES), F32)),
        scratch_shapes=[pltpu.VMEM((N_EXPERTS, 1), F32)],
        compiler_params=pltpu.CompilerParams(dimension_semantics=("arbitrary",)),
        name="route",
    )(h1, w_router.T, router_bias.reshape(N_EXPERTS, 1), tri, ltri)


def _pos_kernel(n_blocks_pad, dummy_row, ccol_ref, crow_ref, ce_ref, crel_ref, be_ref, nb_ref, start_ref, wrow_ref):
    blk = float(EXPERT_BLOCK)
    pad = lambda cnt: jnp.floor((cnt + (blk - 1.0)) * (1.0 / blk)) * blk
    padded_row = pad(crow_ref[0:1, :])
    lane = lax.broadcasted_iota(I32, (N_EXPERTS, 128), 1)
    sub = lax.broadcasted_iota(I32, (N_EXPERTS, 128), 0)
    start = jnp.sum(jnp.where(lane < sub, padded_row, 0.0), axis=1, keepdims=True)
    end = start + pad(ccol_ref[:, 0:1])

    first_row = lax.broadcasted_iota(I32, (N_EXPERTS, n_blocks_pad), 1).astype(F32) * blk
    owner = jnp.sum((end <= first_row).astype(F32), axis=0, keepdims=True)
    be_ref[...] = jnp.broadcast_to(jnp.minimum(owner, float(N_EXPERTS - 1)).astype(I32), be_ref.shape)
    used = jnp.max(end, axis=0, keepdims=True) * (1.0 / blk)
    nb_ref[...] = jnp.broadcast_to(used.astype(I32), nb_ref.shape)
    start_ref[...] = jnp.broadcast_to(start.astype(I32), start_ref.shape)

    ce = ce_ref[...]
    base = jnp.full(ce.shape, float(dummy_row), F32)
    for e in range(N_EXPERTS):
        base = jnp.where(ce == e, start[e:e + 1, 0:1], base)
    wrow_ref[...] = (base.astype(I32) + crel_ref[...]) * TOKEN_ROWS


def _pos(cnt, chunk_e, chunk_rel, n_blocks_pad, dummy_row):
    counts = cnt[:, 0]
    ccol = jnp.broadcast_to(counts[:, None], (N_EXPERTS, 128))
    crow = jnp.broadcast_to(jnp.pad(counts, (0, 128 - N_EXPERTS))[None, :], (8, 128))
    return pl.pallas_call(
        functools.partial(_pos_kernel, n_blocks_pad, dummy_row),
        out_shape=(jax.ShapeDtypeStruct((8, n_blocks_pad), I32), jax.ShapeDtypeStruct((8, 128), I32),
                   jax.ShapeDtypeStruct((N_EXPERTS, 128), I32), jax.ShapeDtypeStruct(chunk_e.shape, I32)),
        name="pos",
    )(ccol, crow, chunk_e, chunk_rel)


def _chunk_copy(src_ref, src_wrow, dst_ref, dst_wrow, sem):
    src = pl.ds(pl.multiple_of(src_wrow, CHUNK_WROWS), CHUNK_WROWS)
    dst = pl.ds(pl.multiple_of(dst_wrow, CHUNK_WROWS), CHUNK_WROWS)
    return pltpu.make_async_copy(src_ref.at[src, :], dst_ref.at[dst, :], sem)


def _for_each_chunk(tile, wrow_ref, fn):
    def per_chunk(c, _):
        fn(c * CHUNK_WROWS, wrow_ref[tile * N_CHUNKS + c])
        return 0

    lax.fori_loop(0, N_CHUNKS, per_chunk, 0, unroll=8)


def _store_token_tiles(dst_ref, x):
    n = x.shape[0]
    half = TOKEN_ROWS * LANES
    bits = lambda v: lax.bitcast_convert_type(v.astype(BF16).astype(F32), U32)
    for s in range(TOKEN_ROWS):
        hi = bits(x[:, s * LANES:(s + 1) * LANES])
        lo = bits(x[:, half + s * LANES:half + (s + 1) * LANES])
        dst_ref[pl.ds(s, n, stride=TOKEN_ROWS), :] = hi | (lo >> 16)


def _load_token_tiles(src_ref, n):
    his, los = [], []
    for s in range(TOKEN_ROWS):
        words = src_ref[pl.ds(s, n, stride=TOKEN_ROWS), :]
        his.append(lax.bitcast_convert_type(words & jnp.uint32(0xFFFF0000), F32))
        los.append(lax.bitcast_convert_type(words << 16, F32))
    return his, los


def _scatter_kernel(n_blocks, start_ref, count_ref, nb_ref, wrow_ref, h_ref, slot_ref, xs_out,
                    tiles, zeros, sems, zsem):
    i = pl.program_id(0)
    ts = MOE_TILE
    blk_rows = EXPERT_BLOCK * TOKEN_ROWS

    @pl.when(i == 0)
    def _():
        zeros[...] = jnp.zeros_like(zeros)
        zero_block = lambda b: pltpu.make_async_copy(
            zeros, xs_out.at[pl.ds(pl.multiple_of(b * blk_rows, blk_rows), blk_rows), :], zsem)

        def for_each_pad_piece(fn):
            def per_expert(e, _):
                first = start_ref[e] + count_ref[e]
                n = ((count_ref[e] + (EXPERT_BLOCK - 1)) & (-EXPERT_BLOCK)) - count_ref[e]
                for piece in PAD_PIECES:
                    done = n & (-2 * piece)

                    @pl.when((n & piece) != 0)
                    def _():
                        dst = pl.ds(pl.multiple_of((first + done) * TOKEN_ROWS, CHUNK_WROWS), piece * TOKEN_ROWS)
                        fn(pltpu.make_async_copy(zeros.at[pl.ds(0, piece * TOKEN_ROWS), :], xs_out.at[dst, :], zsem))

                return 0

            lax.fori_loop(0, N_EXPERTS, per_expert, 0)

        for_each_pad_piece(lambda cp: cp.start())

        def start_block(b, _):
            zero_block(b).start()
            return 0

        lax.fori_loop(nb_ref[0], n_blocks, start_block, 0)
        for_each_pad_piece(lambda cp: cp.wait())

        def wait_block(b, _):
            zero_block(0).wait()
            return 0

        lax.fori_loop(nb_ref[0], n_blocks, wait_block, 0)

    def drain(half):
        pltpu.make_async_copy(tiles.at[half], xs_out.at[pl.ds(0, LOCAL_SLOTS * TOKEN_ROWS), :], sems.at[half]).wait()

    for half in range(2):
        @pl.when(i > 0)
        def _():
            drain(half)

        h = h_ref[half * ts:(half + 1) * ts, :].astype(BF16)
        slots = slot_ref[:, half * ts:(half + 1) * ts]
        sub = ts // 2
        for jb in range(LOCAL_SLOTS // ts):
            in_block = jnp.where(slots >> TILE_SHIFT == jb, slots & (ts - 1), -1).astype(F32).astype(BF16)
            parts = []
            for r in range(ts // sub):
                j = (lax.broadcasted_iota(I32, (sub, ts), 0) + r * sub).astype(F32).astype(BF16)
                perm = jnp.zeros((sub, ts), BF16)
                for k in range(TOP_K):
                    perm = perm + jnp.where(j == in_block[k:k + 1, :], jnp.ones((), BF16), jnp.zeros((), BF16))
                parts.append(perm)
            _store_token_tiles(tiles.at[half, pl.ds(jb * ts * TOKEN_ROWS, ts * TOKEN_ROWS), :],
                               _dot(jnp.concatenate(parts, axis=0), h))

        _for_each_chunk(
            2 * i + half, wrow_ref,
            lambda local, sorted_wrow, half=half: _chunk_copy(tiles.at[half], local, xs_out, sorted_wrow,
                                                              sems.at[half]).start())

    @pl.when(i == pl.num_programs(0) - 1)
    def _():
        drain(0)
        drain(1)


def _scatter(h1, slots, starts, counts, n_used, chunk_wrow, n_rows):
    t_rows = h1.shape[0]
    step = 2 * MOE_TILE
    n_blocks = n_rows // EXPERT_BLOCK
    return pl.pallas_call(
        functools.partial(_scatter_kernel, n_blocks),
        grid_spec=pltpu.PrefetchScalarGridSpec(
            num_scalar_prefetch=4,
            grid=(t_rows // step,),
            in_specs=[pl.BlockSpec((step, D_MODEL), lambda i, *_: (i, 0)),
                      pl.BlockSpec((8, step), lambda i, *_: (0, i))],
            out_specs=pl.BlockSpec(memory_space=pl.ANY),
            scratch_shapes=[pltpu.VMEM((2, LOCAL_SLOTS * TOKEN_ROWS, LANES), U32),
                            pltpu.VMEM((EXPERT_BLOCK * TOKEN_ROWS, LANES), U32),
                            pltpu.SemaphoreType.DMA((2,)), pltpu.SemaphoreType.DMA(())],
        ),
        out_shape=jax.ShapeDtypeStruct((n_rows * TOKEN_ROWS, LANES), U32),
        compiler_params=pltpu.CompilerParams(dimension_semantics=("arbitrary",), vmem_limit_bytes=V7X_VMEM_LIMIT),
        name="scatter",
    )(starts, counts, n_used, chunk_wrow, h1, slots)


def _experts_kernel(be_ref, nb_ref, x_ref, wg_ref, wu_ref, wd_ref, y_ref, wg_bf, wu_bf, wd_bf):
    b = pl.program_id(0)

    @pl.when((b == 0) | (be_ref[b] != be_ref[jnp.maximum(b - 1, 0)]))
    def _():
        wg_bf[...] = wg_ref[...].astype(BF16)
        wu_bf[...] = wu_ref[...].astype(BF16)
        wd_bf[...] = wd_ref[...].astype(BF16)

    @pl.when(b < nb_ref[0])
    def _():
        his, los = _load_token_tiles(x_ref, EXPERT_BLOCK)
        xb = jnp.concatenate(his + los, axis=-1).astype(BF16)
        gate = _dot(xb, wg_bf[...])
        up = _dot(xb, wu_bf[...])
        hid = (gate * jax.nn.sigmoid(gate) * up).astype(BF16)
        _store_token_tiles(y_ref, _dot(hid, wd_bf[...]))

    @pl.when(b >= nb_ref[0])
    def _():
        y_ref[...] = jnp.zeros_like(y_ref)


def _experts(xs, block_e, n_used, w_gate, w_up, w_down):
    blk_rows = EXPERT_BLOCK * TOKEN_ROWS
    return pl.pallas_call(
        _experts_kernel,
        grid_spec=pltpu.PrefetchScalarGridSpec(
            num_scalar_prefetch=2,
            grid=(xs.shape[0] // blk_rows,),
            in_specs=[pl.BlockSpec((blk_rows, LANES), lambda b, be, nb: (jnp.minimum(b, jnp.maximum(nb[0], 1) - 1), 0)),
                      pl.BlockSpec((None, D_MODEL, D_EXPERT), lambda b, be, nb: (be[b], 0, 0)),
                      pl.BlockSpec((None, D_MODEL, D_EXPERT), lambda b, be, nb: (be[b], 0, 0)),
                      pl.BlockSpec((None, D_EXPERT, D_MODEL), lambda b, be, nb: (be[b], 0, 0))],
            out_specs=pl.BlockSpec((blk_rows, LANES), lambda b, be, nb: (b, 0)),
            scratch_shapes=[pltpu.VMEM((D_MODEL, D_EXPERT), BF16), pltpu.VMEM((D_MODEL, D_EXPERT), BF16),
                            pltpu.VMEM((D_EXPERT, D_MODEL), BF16)],
        ),
        out_shape=jax.ShapeDtypeStruct(xs.shape, U32),
        compiler_params=pltpu.CompilerParams(dimension_semantics=("arbitrary",)),
        name="experts",
    )(block_e, n_used, xs, w_gate, w_up, w_down)


def _combine_kernel(n_prompt_tiles, wrow_ref, h_ref, w_ref, slot_ref, ys_hbm,
                    wsg_ref, wsu_ref, wsd_ref, g2_ref, b2_ref, yp_hbm, ysm_hbm, ybuf, obuf, gsems, osems):
    i = pl.program_id(0)
    n_steps = pl.num_programs(0)
    tc = MOE_TILE
    tile_t = tc // BATCH

    def gather(tile, slot):
        _for_each_chunk(
            tile, wrow_ref,
            lambda local, sorted_wrow: _chunk_copy(ys_hbm, sorted_wrow, ybuf.at[slot], local, gsems.at[slot]).start())

    def gather_wait(slot):
        pltpu.make_async_copy(ys_hbm.at[pl.ds(0, LOCAL_SLOTS * TOKEN_ROWS), :], ybuf.at[slot], gsems.at[slot]).wait()

    def out_copies(dst_hbm, t0, slot):
        return [pltpu.make_async_copy(obuf.at[slot, :, b, :], dst_hbm.at[b, pl.ds(t0, tile_t), :], osems.at[slot])
                for b in range(BATCH)]

    def out_wait(slot):
        for cp in out_copies(yp_hbm, 0, slot):
            cp.wait()

    def compute(half, slot):
        rows = slice(half * tc, (half + 1) * tc)
        h = h_ref[rows, :]
        hb = h.astype(BF16)
        gate = _dot(hb, wsg_ref[...])
        up = _dot(hb, wsu_ref[...])
        shared = _dot((gate * jax.nn.sigmoid(gate) * up).astype(BF16), wsd_ref[...])

        eye = (lax.broadcasted_iota(I32, (tc, tc), 0) == lax.broadcasted_iota(I32, (tc, tc), 1)).astype(BF16)
        to_cols = lambda a: lax.dot_general(eye, a.astype(F32).astype(BF16), (((1,), (1,)), ((), ())),
                                            preferred_element_type=F32)
        slots = slot_ref[:, rows]
        w_bf = to_cols(w_ref[:, rows]).astype(BF16)
        s_block = to_cols(slots >> TILE_SHIFT)
        s_offset = to_cols(slots & (tc - 1))

        j = lax.broadcasted_iota(I32, (tc, tc), 1).astype(F32).astype(BF16)
        blocks = []
        for lb in range(LOCAL_SLOTS // tc):
            in_block = jnp.where(s_block == float(lb), s_offset, -1.0).astype(BF16)
            blk = jnp.zeros((tc, tc), BF16)
            for k in range(TOP_K):
                blk = blk + jnp.where(j == in_block[:, k:k + 1], w_bf[:, k:k + 1], jnp.zeros((), BF16))
            blocks.append(blk)
        mix_w = jnp.concatenate(blocks, axis=-1)
        gather_wait(slot)
        his, los = _load_token_tiles(ybuf.at[slot], LOCAL_SLOTS)
        pieces = his + los
        routed = jnp.concatenate(
            [_dot(mix_w, jnp.concatenate(pieces[p:p + 2], axis=-1).astype(BF16)) for p in range(0, len(pieces), 2)],
            axis=-1)
        out = _layer_norm(DEEPNORM_ALPHA * h + (routed + shared), g2_ref[...], b2_ref[...])
        obuf[slot] = out.reshape(tile_t, BATCH, D_MODEL)

        g = 2 * i + half

        @pl.when(g < n_prompt_tiles)
        def _():
            for cp in out_copies(yp_hbm, pl.multiple_of(g * tile_t, tile_t), slot):
                cp.start()

        @pl.when(g >= n_prompt_tiles)
        def _():
            for cp in out_copies(ysm_hbm, pl.multiple_of((g - n_prompt_tiles) * tile_t, tile_t), slot):
                cp.start()

    @pl.when(i == 0)
    def _():
        gather(0, 0)

    gather(2 * i + 1, 1)

    @pl.when(i > 0)
    def _():
        out_wait(0)

    compute(0, 0)

    @pl.when(i + 1 < n_steps)
    def _():
        gather(2 * i + 2, 0)

    @pl.when(i > 0)
    def _():
        out_wait(1)

    compute(1, 1)

    @pl.when(i + 1 == n_steps)
    def _():
        out_wait(0)
        out_wait(1)


def _combine(h1, gate_w, slots, ys, chunk_wrow, ws_gate, ws_up, ws_down, ln2_g, ln2_b, prompt_shape, sample_shape):
    t_rows = h1.shape[0]
    tc = MOE_TILE
    step = 2 * tc
    n_prompt_tiles = prompt_shape[0] * prompt_shape[1] // tc
    const2 = lambda i, *_: (0, 0)
    any_spec = pl.BlockSpec(memory_space=pl.ANY)
    return pl.pallas_call(
        functools.partial(_combine_kernel, n_prompt_tiles),
        grid_spec=pltpu.PrefetchScalarGridSpec(
            num_scalar_prefetch=1,
            grid=(t_rows // step,),
            in_specs=[pl.BlockSpec((step, D_MODEL), lambda i, *_: (i, 0)),
                      pl.BlockSpec((8, step), lambda i, *_: (0, i)),
                      pl.BlockSpec((8, step), lambda i, *_: (0, i)),
                      any_spec,
                      pl.BlockSpec((D_MODEL, D_EXPERT), const2),
                      pl.BlockSpec((D_MODEL, D_EXPERT), const2),
                      pl.BlockSpec((D_EXPERT, D_MODEL), const2),
                      pl.BlockSpec((1, D_MODEL), const2),
                      pl.BlockSpec((1, D_MODEL), const2)],
            out_specs=(any_spec, any_spec),
            scratch_shapes=[pltpu.VMEM((2, LOCAL_SLOTS * TOKEN_ROWS, LANES), U32),
                            pltpu.VMEM((2, tc // BATCH, BATCH, D_MODEL), F32),
                            pltpu.SemaphoreType.DMA((2,)), pltpu.SemaphoreType.DMA((2,))],
        ),
        out_shape=(jax.ShapeDtypeStruct(prompt_shape, F32), jax.ShapeDtypeStruct(sample_shape, F32)),
        compiler_params=pltpu.CompilerParams(dimension_semantics=("arbitrary",), vmem_limit_bytes=V7X_VMEM_LIMIT),
        name="combine",
    )(chunk_wrow, h1, gate_w, slots, ys, ws_gate.astype(BF16), ws_up.astype(BF16),
      ws_down.astype(BF16), ln2_g.reshape(1, D_MODEL), ln2_b.reshape(1, D_MODEL))


def _time_major(x):
    b, l, d = x.shape
    return jnp.transpose(x, (1, 0, 2)).reshape(l * b, d)


def _batch_major(x, b):
    return jnp.transpose(x.reshape(x.shape[0] // b, b, x.shape[1]), (1, 0, 2))


def _block_diag(blocks):
    n, r, c = blocks.shape
    eye = jnp.eye(n, dtype=blocks.dtype)
    return (blocks[:, :, None, :] * eye[:, None, :, None]).reshape(n * r, n * c)


def kernel(x_prompt, x_sample, state_ssm_re, state_ssm_im, cache_conv, ln_in_g, ln_in_b, w_in, lam_re, lam_im, log_dt, ssm_b_re, ssm_b_im, ssm_c_re, ssm_c_im, ssm_d, w_glu, b_glu, conv_w, beta_ssm, beta_conv, w_out, ln1_g, ln1_b, w_router, router_bias, w_gate, w_up, w_down, ws_gate, ws_up, ws_down, ln2_g, ln2_b):
    bp, lp, _ = x_prompt.shape
    bs, ls, _ = x_sample.shape
    assert bp == BATCH and bs == BATCH and ls == CHUNK_T and lp % CHUNK_T == 0
    assert w_in.shape[0] == 1, "single-layer model"
    n_prompt = bp * lp
    row = lambda a: a.reshape(1, -1)

    a_re, a_im, bb_re, bb_im = _prep(lam_re[0], lam_im[0], log_dt[0], ssm_b_re[0], ssm_b_im[0])
    groups_in = 128 // SSM_GROUP
    groups_out = 256 // SSM_GROUP
    bbd = lambda bb: jnp.stack([_block_diag(bb[j * groups_in:(j + 1) * groups_in])
                                for j in range(N_GROUPS // groups_in)]).astype(BF16)
    ct = lambda cc: jnp.transpose(cc, (0, 2, 1))
    cbd = lambda cc: jnp.stack([_block_diag(ct(cc)[j * groups_out:(j + 1) * groups_out])
                                for j in range(N_GROUPS // groups_out)]).astype(BF16)
    mix_weights = (row(ln_in_g), row(ln_in_b), w_in[0].astype(BF16), row(a_re), row(a_im),
                   bbd(bb_re), bbd(bb_im), cbd(ssm_c_re[0]), cbd(-ssm_c_im[0]), row(ssm_d[0]),
                   w_glu[0].astype(BF16), row(b_glu[0]), conv_w[0], row(beta_ssm[0]), row(beta_conv[0]),
                   w_out[0].astype(BF16), row(ln1_g[0]), row(ln1_b[0]))

    h0r = state_ssm_re[0].reshape(BATCH, D_STATE)
    h0i = state_ssm_im[0].reshape(BATCH, D_STATE)
    cbuf = _time_major(cache_conv[0])
    h1, p_re, p_im, p_conv, s_re, s_im, s_conv = _mix(x_prompt, x_sample, h0r, h0i, cbuf, mix_weights)

    t_rows = h1.shape[0]
    assert t_rows % (2 * MOE_TILE) == 0 and n_prompt % MOE_TILE == 0
    n_tiles = t_rows // MOE_TILE
    max_rows = t_rows * TOP_K + n_tiles * N_EXPERTS * (RUN_CHUNK - 1) + N_EXPERTS * EXPERT_BLOCK
    n_blocks = -(-max_rows // EXPERT_BLOCK) + 1
    n_blocks_pad = -(-n_blocks // 128) * 128
    gate_w, slots, chunk_e, chunk_rel, cnt = _route(h1, w_router[0], router_bias[0])
    block_e, n_used, starts, chunk_wrow = _pos(cnt, chunk_e[:, 0], chunk_rel[:, 0], n_blocks_pad,
                                               (n_blocks - 1) * EXPERT_BLOCK)
    n_used, chunk_wrow = n_used[0, :1], chunk_wrow.reshape(-1)
    xs = _scatter(h1, slots, starts[:, 0], cnt[:, 0].astype(I32), n_used, chunk_wrow, n_blocks * EXPERT_BLOCK)
    ys = _experts(xs, block_e[0, :n_blocks], n_used, w_gate[0], w_up[0], w_down[0])
    y_prompt, y_sample = _combine(h1, gate_w, slots, ys, chunk_wrow, ws_gate[0], ws_up[0], ws_down[0],
                                  ln2_g[0], ln2_b[0], x_prompt.shape, x_sample.shape)

    st = lambda s: s.reshape(1, BATCH, N_GROUPS, SSM_STATE)
    cv = lambda t: _batch_major(t, BATCH)[None]
    return (y_prompt, y_sample, st(p_re), st(p_im), cv(p_conv), st(s_re), st(s_im), cv(s_conv))
```

```python
import functools

import jax
import jax.numpy as jnp
from jax import lax
from jax.experimental import pallas as pl
from jax.experimental.pallas import tpu as pltpu

F32 = jnp.float32
BF16 = jnp.bfloat16
I32 = jnp.int32
U32 = jnp.uint32

D_MODEL = 1024
D_SSM = 512
D_CONV = 512
SSM_GROUP = 16
N_GROUPS = 32
SSM_STATE = 64
D_STATE = N_GROUPS * SSM_STATE
N_EXPERTS = 64
TOP_K = 6
N_EXPERT_GROUPS = 8
GROUP_SIZE = N_EXPERTS // N_EXPERT_GROUPS
TOPK_GROUPS = 4
D_EXPERT = 256
ROUTED_SCALE = 2.5
DEPTH = 1
DEEPNORM_ALPHA = (2.0 * DEPTH) ** 0.25
LN_EPS = 1e-5
RMS_EPS = 1e-6

BATCH = 8
CHUNK_T = 64
CHUNK_ROWS = CHUNK_T * BATCH
SCAN_COLS = 512
MOE_TILE = 256
TILE_SHIFT = MOE_TILE.bit_length() - 1
ROUTE_TILES_PER_STEP = 1
RUN_CHUNK = 8
LOCAL_SLOTS = TOP_K * MOE_TILE + N_EXPERTS * RUN_CHUNK
N_CHUNKS = LOCAL_SLOTS // RUN_CHUNK
EXPERT_BLOCK = 1024
PAD_PIECES = tuple(1 << b for b in range(EXPERT_BLOCK.bit_length() - 2, RUN_CHUNK.bit_length() - 2, -1))
LANES = 128
TOKEN_ROWS = D_MODEL // (2 * LANES)
CHUNK_WROWS = RUN_CHUNK * TOKEN_ROWS
V7X_VMEM_LIMIT = 56 * 1024 * 1024
NEG_INF = float("-inf")


def _layer_norm(x, g, b):
    mu = jnp.mean(x, axis=-1, keepdims=True)
    xc = x - mu
    var = jnp.mean(xc * xc, axis=-1, keepdims=True)
    return xc * lax.rsqrt(var + LN_EPS) * g + b


def _rms_norm(x, g):
    return x * lax.rsqrt(jnp.mean(x * x, axis=-1, keepdims=True) + RMS_EPS) * g


def _dot(a, b):
    return jnp.dot(a, b, preferred_element_type=F32)


def _discretise(lr, li, log_dt):
    dt = jnp.exp(log_dt)
    mag = jnp.exp(lr * dt)
    ar = mag * jnp.cos(li * dt)
    ai = mag * jnp.sin(li * dt)
    den = lr * lr + li * li
    qr = ((ar - 1.0) * lr + ai * li) / den
    qi = (ai * lr - (ar - 1.0) * li) / den
    return ar, ai, qr, qi


def _prep_kernel(lr_ref, li_ref, ldt_ref, lrc_ref, lic_ref, ldtc_ref, br_ref, bi_ref,
                 ar_ref, ai_ref, bbr_ref, bbi_ref):
    ar, ai, _, _ = _discretise(lr_ref[...], li_ref[...], ldt_ref[...])
    ar_ref[...] = ar
    ai_ref[...] = ai
    _, _, qr, qi = _discretise(lrc_ref[...], lic_ref[...], ldtc_ref[...])
    br = br_ref[...]
    bi = bi_ref[...]
    bbr_ref[...] = qr * br - qi * bi
    bbi_ref[...] = qr * bi + qi * br


def _prep(lam_re, lam_im, log_dt, b_re, b_im):
    g, p = lam_re.shape
    per_channel = lambda a: jnp.repeat(a, SSM_GROUP, axis=0)
    rows = lambda b: jnp.transpose(b, (0, 2, 1)).reshape(g * SSM_GROUP, p)
    ldt = log_dt.reshape(g, 1)
    ar, ai, bbr, bbi = pl.pallas_call(
        _prep_kernel,
        out_shape=(jax.ShapeDtypeStruct((g, p), F32), jax.ShapeDtypeStruct((g, p), F32),
                   jax.ShapeDtypeStruct((g * SSM_GROUP, p), F32), jax.ShapeDtypeStruct((g * SSM_GROUP, p), F32)),
        name="prep",
    )(lam_re, lam_im, ldt, per_channel(lam_re), per_channel(lam_im), per_channel(ldt), rows(b_re), rows(b_im))
    return ar, ai, bbr.reshape(g, SSM_GROUP, p), bbi.reshape(g, SSM_GROUP, p)


def _chunk_copies(src_hbm, t0, xbuf, slot, sems):
    return [pltpu.make_async_copy(src_hbm.at[b, pl.ds(t0, CHUNK_T), :], xbuf.at[slot, :, b, :], sems.at[slot])
            for b in range(BATCH)]


def _mix_kernel(n_prompt_chunks,
                xp_hbm, xs_hbm, h0r_ref, h0i_ref, cbuf_ref, lng_ref, lnb_ref, win_ref, ar_ref, ai_ref,
                bbr_ref, bbi_ref, cbr_ref, cbi_ref, dsk_ref, wglu_ref, bglu_ref, cw_ref,
                bs_ref, bc_ref, wout_ref, g1_ref, b1_ref,
                h1_ref, pr_ref, pi_ref, pc_ref, sr_ref, si_ref, sc_ref,
                str_ref, sti_ref, hr_ref, hi_ref, cv_ref, xbuf, xsems):
    c = pl.program_id(0)
    n = CHUNK_ROWS
    carry_rows = 2 * BATCH
    slot = c % 2

    @pl.when(c == 0)
    def _():
        for cp in _chunk_copies(xp_hbm, 0, xbuf, 0, xsems):
            cp.start()

    @pl.when(c + 1 < n_prompt_chunks)
    def _():
        for cp in _chunk_copies(xp_hbm, pl.multiple_of((c + 1) * CHUNK_T, CHUNK_T), xbuf, 1 - slot, xsems):
            cp.start()

    @pl.when(c + 1 == n_prompt_chunks)
    def _():
        for cp in _chunk_copies(xs_hbm, 0, xbuf, 1 - slot, xsems):
            cp.start()

    @pl.when(c == 0)
    def _():
        hr_ref[...] = jnp.zeros_like(hr_ref)
        hi_ref[...] = jnp.zeros_like(hi_ref)
        cv_ref[0:carry_rows, :] = jnp.zeros((carry_rows, D_CONV), F32)

    @pl.when(c == n_prompt_chunks)
    def _():
        hr_ref[...] = h0r_ref[...]
        hi_ref[...] = h0i_ref[...]
        cv_ref[0:carry_rows, :] = cbuf_ref[...]

    for cp in _chunk_copies(xp_hbm, 0, xbuf, slot, xsems):
        cp.wait()
    h = _layer_norm(xbuf[slot].reshape(n, D_MODEL), lng_ref[...], lnb_ref[...])
    hb = h.astype(BF16)

    u = _dot(hb, win_ref[:, 0:D_SSM])
    ub = u.astype(BF16)
    u_tile = 128
    s_tile = u_tile // SSM_GROUP * SSM_STATE
    for j in range(D_SSM // u_tile):
        uj = ub[:, j * u_tile:(j + 1) * u_tile]
        str_ref[:, j * s_tile:(j + 1) * s_tile] = _dot(uj, bbr_ref[j])
        sti_ref[:, j * s_tile:(j + 1) * s_tile] = _dot(uj, bbi_ref[j])

    gate_b = _dot(hb, win_ref[:, D_SSM:D_SSM + D_CONV])
    gate_c = _dot(hb, win_ref[:, D_SSM + D_CONV:D_SSM + 2 * D_CONV])
    v = _dot(hb, win_ref[:, D_SSM + 2 * D_CONV:D_SSM + 3 * D_CONV])
    cv_ref[carry_rows:carry_rows + n, :] = gate_c * v

    for cb in range(D_STATE // SCAN_COLS):
        cols = slice(cb * SCAN_COLS, (cb + 1) * SCAN_COLS)
        ar = jnp.broadcast_to(ar_ref[:, cols], (BATCH, SCAN_COLS))
        ai = jnp.broadcast_to(ai_ref[:, cols], (BATCH, SCAN_COLS))

        def step(t, carry, cols=cols, ar=ar, ai=ai):
            sr, si = carry
            r0 = pl.multiple_of(t * BATCH, BATCH)
            nr = ar * sr - ai * si + str_ref[pl.ds(r0, BATCH), cols]
            ni = ar * si + ai * sr + sti_ref[pl.ds(r0, BATCH), cols]
            str_ref[pl.ds(r0, BATCH), cols] = nr
            sti_ref[pl.ds(r0, BATCH), cols] = ni
            return nr, ni

        sr, si = lax.fori_loop(0, CHUNK_T, step, (hr_ref[:, cols], hi_ref[:, cols]), unroll=True)
        hr_ref[:, cols] = sr
        hi_ref[:, cols] = si

    k_tile = 1024
    ys = []
    for j in range(D_STATE // k_tile):
        sl = slice(j * k_tile, (j + 1) * k_tile)
        ys.append(_dot(str_ref[:, sl].astype(BF16), cbr_ref[j]) + _dot(sti_ref[:, sl].astype(BF16), cbi_ref[j]))
    y_ssm = jnp.concatenate(ys, axis=-1) + dsk_ref[...] * u
    g = jax.nn.gelu(y_ssm)
    y_ssm = g * jax.nn.sigmoid(_dot(g.astype(BF16), wglu_ref[...]) + bglu_ref[...])
    mix = _dot(_rms_norm(y_ssm, bs_ref[...]).astype(BF16), wout_ref[0:D_SSM, :])

    y_conv = gate_b * (cv_ref[0:n, :] * cw_ref[0:1, :]
                       + cv_ref[BATCH:BATCH + n, :] * cw_ref[1:2, :]
                       + cv_ref[carry_rows:carry_rows + n, :] * cw_ref[2:3, :])
    tail = cv_ref[n:n + carry_rows, :]
    cv_ref[0:carry_rows, :] = tail
    mix = mix + _dot(_rms_norm(y_conv, bc_ref[...]).astype(BF16), wout_ref[D_SSM:D_SSM + D_CONV, :])

    h1_ref[...] = _layer_norm(DEEPNORM_ALPHA * h + mix, g1_ref[...], b1_ref[...])

    @pl.when(c == n_prompt_chunks - 1)
    def _():
        pr_ref[...] = hr_ref[...]
        pi_ref[...] = hi_ref[...]
        pc_ref[...] = tail

    @pl.when(c == n_prompt_chunks)
    def _():
        sr_ref[...] = hr_ref[...]
        si_ref[...] = hi_ref[...]
        sc_ref[...] = tail


def _mix(x_prompt, x_sample, h0r, h0i, cbuf, weights):
    n_prompt_chunks = x_prompt.shape[1] // CHUNK_T
    n_chunks = n_prompt_chunks + 1
    t_rows = n_chunks * CHUNK_ROWS
    any_spec = pl.BlockSpec(memory_space=pl.ANY)
    const2 = lambda c: (0, 0)
    const3 = lambda c: (0, 0, 0)
    w_specs = [pl.BlockSpec(w.shape, const3 if w.ndim == 3 else const2) for w in weights]
    state = jax.ShapeDtypeStruct((BATCH, D_STATE), F32)
    tail = jax.ShapeDtypeStruct((2 * BATCH, D_CONV), F32)
    state_spec = pl.BlockSpec((BATCH, D_STATE), const2)
    tail_spec = pl.BlockSpec((2 * BATCH, D_CONV), const2)
    return pl.pallas_call(
        functools.partial(_mix_kernel, n_prompt_chunks),
        grid=(n_chunks,),
        in_specs=[any_spec, any_spec, state_spec, state_spec, tail_spec] + w_specs,
        out_specs=(pl.BlockSpec((CHUNK_ROWS, D_MODEL), lambda c: (c, 0)),
                   state_spec, state_spec, tail_spec, state_spec, state_spec, tail_spec),
        out_shape=(jax.ShapeDtypeStruct((t_rows, D_MODEL), F32), state, state, tail, state, state, tail),
        scratch_shapes=[pltpu.VMEM((CHUNK_ROWS, D_STATE), F32), pltpu.VMEM((CHUNK_ROWS, D_STATE), F32),
                        pltpu.VMEM((BATCH, D_STATE), F32), pltpu.VMEM((BATCH, D_STATE), F32),
                        pltpu.VMEM((CHUNK_ROWS + 2 * BATCH, D_CONV), F32),
                        pltpu.VMEM((2, CHUNK_T, BATCH, D_MODEL), F32), pltpu.SemaphoreType.DMA((2,))],
        compiler_params=pltpu.CompilerParams(dimension_semantics=("arbitrary",), vmem_limit_bytes=V7X_VMEM_LIMIT),
        name="mix",
    )(x_prompt, x_sample, h0r, h0i, cbuf, *weights)


def _route_tile(h, w_hi, w_lo, bias, tri, ltri, run_before):
    tr = MOE_TILE

    nt_dot = lambda a, b: lax.dot_general(a, b, (((1,), (1,)), ((), ())), preferred_element_type=F32)
    h_hi = h.astype(BF16)
    h_lo = (h - h_hi.astype(F32)).astype(BF16)
    logits = nt_dot(w_hi, h_hi) + (nt_dot(w_hi, h_lo) + nt_dot(w_lo, h_hi))
    scores = jax.nn.sigmoid(logits)
    sel = scores + bias

    sub = lax.broadcasted_iota(I32, (GROUP_SIZE, tr), 0).astype(F32)
    blocks, gscore = [], []
    for g in range(N_EXPERT_GROUPS):
        blk = sel[g * GROUP_SIZE:(g + 1) * GROUP_SIZE, :]
        m1 = jnp.max(blk, axis=0, keepdims=True)
        first = jnp.min(jnp.where(blk == m1, sub, float(GROUP_SIZE)), axis=0, keepdims=True)
        m2 = jnp.max(jnp.where(sub == first, NEG_INF, blk), axis=0, keepdims=True)
        blocks.append(blk)
        gscore.append(m1 + m2)
    masked = []
    for g in range(N_EXPERT_GROUPS):
        beaten = jnp.zeros((1, tr), F32)
        for o in range(N_EXPERT_GROUPS):
            if o == g:
                continue
            wins = gscore[o] >= gscore[g] if o < g else gscore[o] > gscore[g]
            beaten = beaten + wins.astype(F32)
        masked.append(jnp.where(beaten < float(TOPK_GROUPS), blocks[g], NEG_INF))
    masked = jnp.concatenate(masked, axis=0)

    row = lax.broadcasted_iota(I32, (N_EXPERTS, tr), 0).astype(F32)
    picked = jnp.zeros((N_EXPERTS, tr), F32)
    hots, gates = [], []
    for _ in range(TOP_K):
        m = jnp.max(masked, axis=0, keepdims=True)
        idx = jnp.min(jnp.where(masked == m, row, float(N_EXPERTS)), axis=0, keepdims=True)
        hot = row == idx
        hots.append(hot)
        gates.append(jnp.sum(jnp.where(hot, scores, 0.0), axis=0, keepdims=True))
        masked = jnp.where(hot, NEG_INF, masked)
        picked = picked + hot.astype(F32)
    total = gates[0]
    for k in range(1, TOP_K):
        total = total + gates[k]

    chunk = float(RUN_CHUNK)
    pb = picked.astype(BF16)
    earlier = _dot(pb, tri)
    cnt_col = jnp.sum(picked, axis=1, keepdims=True)
    run_len = jnp.floor((cnt_col + (chunk - 1.0)) * (1.0 / chunk)) * chunk
    lower = _dot(ltri, jnp.broadcast_to(run_len, (N_EXPERTS, LANES)).astype(BF16))[:, 0:1]
    slot_of = earlier + lower
    w_rows = [gates[k] / total * ROUTED_SCALE for k in range(TOP_K)]
    slot_rows = [jnp.sum(jnp.where(hots[k], slot_of, 0.0), axis=0, keepdims=True).astype(I32) for k in range(TOP_K)]

    c_first = lax.broadcasted_iota(I32, (N_EXPERTS, N_CHUNKS), 1).astype(F32) * chunk
    owner = jnp.sum((lower + run_len <= c_first).astype(F32), axis=0, keepdims=True)
    hot_e = lax.broadcasted_iota(I32, (N_EXPERTS, N_CHUNKS), 0).astype(F32) == owner
    rel = jnp.sum(jnp.where(hot_e, run_before - lower + c_first, 0.0), axis=0, keepdims=True)
    return w_rows, slot_rows, owner.astype(I32), rel.astype(I32), run_len


def _route_kernel(h_ref, wrt_ref, bias_ref, tri_ref, ltri_ref, w_ref, slot_ref, ce_ref, crel_ref, cnt_ref, run_col):
    i = pl.program_id(0)
    tr = MOE_TILE

    @pl.when(i == 0)
    def _():
        run_col[...] = jnp.zeros_like(run_col)

    wrt = wrt_ref[...]
    w_hi = wrt.astype(BF16)
    w_lo = (wrt - w_hi.astype(F32)).astype(BF16)
    run = run_col[...]
    for half in range(ROUTE_TILES_PER_STEP):
        cols = slice(half * tr, (half + 1) * tr)
        w_rows, slot_rows, owner, rel, run_len = _route_tile(
            h_ref[cols, :], w_hi, w_lo, bias_ref[...], tri_ref[...], ltri_ref[...], run)
        for k in range(TOP_K):
            w_ref[k:k + 1, cols] = w_rows[k]
            slot_ref[k:k + 1, cols] = slot_rows[k]
        w_ref[TOP_K:, cols] = jnp.zeros((8 - TOP_K, tr), F32)
        slot_ref[TOP_K:, cols] = jnp.full((8 - TOP_K, tr), -1, I32)
        ce_ref[half] = jnp.broadcast_to(owner, (8, N_CHUNKS))
        crel_ref[half] = jnp.broadcast_to(rel, (8, N_CHUNKS))
        run = run + run_len
    run_col[...] = run
    cnt_ref[...] = jnp.broadcast_to(run, cnt_ref.shape)


def _route(h1, w_router, router_bias):
    t_rows = h1.shape[0]
    tr = MOE_TILE
    n_tiles = t_rows // tr
    step = ROUTE_TILES_PER_STEP * tr
    tri = jnp.triu(jnp.ones((tr, tr), BF16), k=1)
    ltri = jnp.tril(jnp.ones((N_EXPERTS, N_EXPERTS), BF16), k=-1)
    tok = lambda i: (0, i)
    const2 = lambda i: (0, 0)
    per_tile = lambda i: (i, 0, 0)
    return pl.pallas_call(
        _route_kernel,
        grid=(t_rows // step,),
        in_specs=[pl.BlockSpec((step, D_MODEL), lambda i: (i, 0)),
                  pl.BlockSpec((N_EXPERTS, D_MODEL), const2),
                  pl.BlockSpec((N_EXPERTS, 1), const2),
                  pl.BlockSpec((tr, tr), const2),
                  pl.BlockSpec((N_EXPERTS, N_EXPERTS), const2)],
        out_specs=(pl.BlockSpec((8, step), tok), pl.BlockSpec((8, step), tok),
                   pl.BlockSpec((ROUTE_TILES_PER_STEP, 8, N_CHUNKS), per_tile),
                   pl.BlockSpec((ROUTE_TILES_PER_STEP, 8, N_CHUNKS), per_tile),
                   pl.BlockSpec((N_EXPERTS, LANES), const2)),
        out_shape=(jax.ShapeDtypeStruct((8, t_rows), F32), jax.ShapeDtypeStruct((8, t_rows), I32),
                   jax.ShapeDtypeStruct((n_tiles, 8, N_CHUNKS), I32), jax.ShapeDtypeStruct((n_tiles, 8, N_CHUNKS), I32),
                   jax.ShapeDtypeStruct((N_EXPERTS, LANES), F32)),
        scratch_shapes=[pltpu.VMEM((N_EXPERTS, 1), F32)],
        compiler_params=pltpu.CompilerParams(dimension_semantics=("arbitrary",)),
        name="route",
    )(h1, w_router.T, router_bias.reshape(N_EXPERTS, 1), tri, ltri)


def _pos_kernel(n_blocks_pad, dummy_row, ccol_ref, crow_ref, ce_ref, crel_ref, be_ref, nb_ref, start_ref, wrow_ref):
    blk = float(EXPERT_BLOCK)
    pad = lambda cnt: jnp.floor((cnt + (blk - 1.0)) * (1.0 / blk)) * blk
    padded_row = pad(crow_ref[0:1, :])
    lane = lax.broadcasted_iota(I32, (N_EXPERTS, 128), 1)
    sub = lax.broadcasted_iota(I32, (N_EXPERTS, 128), 0)
    start = jnp.sum(jnp.where(lane < sub, padded_row, 0.0), axis=1, keepdims=True)
    end = start + pad(ccol_ref[:, 0:1])

    first_row = lax.broadcasted_iota(I32, (N_EXPERTS, n_blocks_pad), 1).astype(F32) * blk
    owner = jnp.sum((end <= first_row).astype(F32), axis=0, keepdims=True)
    be_ref[...] = jnp.broadcast_to(jnp.minimum(owner, float(N_EXPERTS - 1)).astype(I32), be_ref.shape)
    used = jnp.max(end, axis=0, keepdims=True) * (1.0 / blk)
    nb_ref[...] = jnp.broadcast_to(used.astype(I32), nb_ref.shape)
    start_ref[...] = jnp.broadcast_to(start.astype(I32), start_ref.shape)

    ce = ce_ref[...]
    base = jnp.full(ce.shape, float(dummy_row), F32)
    for e in range(N_EXPERTS):
        base = jnp.where(ce == e, start[e:e + 1, 0:1], base)
    wrow_ref[...] = (base.astype(I32) + crel_ref[...]) * TOKEN_ROWS


def _pos(cnt, chunk_e, chunk_rel, n_blocks_pad, dummy_row):
    counts = cnt[:, 0]
    ccol = jnp.broadcast_to(counts[:, None], (N_EXPERTS, 128))
    crow = jnp.broadcast_to(jnp.pad(counts, (0, 128 - N_EXPERTS))[None, :], (8, 128))
    return pl.pallas_call(
        functools.partial(_pos_kernel, n_blocks_pad, dummy_row),
        out_shape=(jax.ShapeDtypeStruct((8, n_blocks_pad), I32), jax.ShapeDtypeStruct((8, 128), I32),
                   jax.ShapeDtypeStruct((N_EXPERTS, 128), I32), jax.ShapeDtypeStruct(chunk_e.shape, I32)),
        name="pos",
    )(ccol, crow, chunk_e, chunk_rel)


def _chunk_copy(src_ref, src_wrow, dst_ref, dst_wrow, sem):
    src = pl.ds(pl.multiple_of(src_wrow, CHUNK_WROWS), CHUNK_WROWS)
    dst = pl.ds(pl.multiple_of(dst_wrow, CHUNK_WROWS), CHUNK_WROWS)
    return pltpu.make_async_copy(src_ref.at[src, :], dst_ref.at[dst, :], sem)


def _for_each_chunk(tile, wrow_ref, fn):
    def per_chunk(c, _):
        fn(c * CHUNK_WROWS, wrow_ref[tile * N_CHUNKS + c])
        return 0

    lax.fori_loop(0, N_CHUNKS, per_chunk, 0, unroll=8)


def _store_token_tiles(dst_ref, x, already_bf16=False):
    n = x.shape[0]
    half = TOKEN_ROWS * LANES
    if already_bf16:
        bits = lambda v: lax.bitcast_convert_type(v, U32) & jnp.uint32(0xFFFF0000)
    else:
        bits = lambda v: lax.bitcast_convert_type(v.astype(BF16).astype(F32), U32)
    for s in range(TOKEN_ROWS):
        hi = bits(x[:, s * LANES:(s + 1) * LANES])
        lo = bits(x[:, half + s * LANES:half + (s + 1) * LANES])
        dst_ref[pl.ds(s, n, stride=TOKEN_ROWS), :] = hi | (lo >> 16)


def _load_token_tiles(src_ref, n):
    his, los = [], []
    for s in range(TOKEN_ROWS):
        words = src_ref[pl.ds(s, n, stride=TOKEN_ROWS), :]
        his.append(lax.bitcast_convert_type(words & jnp.uint32(0xFFFF0000), F32))
        los.append(lax.bitcast_convert_type(words << 16, F32))
    return his, los


def _scatter_kernel(n_blocks, start_ref, count_ref, nb_ref, wrow_ref, h_ref, slot_ref, xs_out,
                    tiles, zeros, sems, zsem):
    i = pl.program_id(0)
    ts = MOE_TILE
    blk_rows = EXPERT_BLOCK * TOKEN_ROWS

    @pl.when(i == 0)
    def _():
        zeros[...] = jnp.zeros_like(zeros)
        zero_block = lambda b: pltpu.make_async_copy(
            zeros, xs_out.at[pl.ds(pl.multiple_of(b * blk_rows, blk_rows), blk_rows), :], zsem)

        def for_each_pad_piece(fn):
            def per_expert(e, _):
                first = start_ref[e] + count_ref[e]
                n = ((count_ref[e] + (EXPERT_BLOCK - 1)) & (-EXPERT_BLOCK)) - count_ref[e]
                for piece in PAD_PIECES:
                    done = n & (-2 * piece)

                    @pl.when((n & piece) != 0)
                    def _():
                        dst = pl.ds(pl.multiple_of((first + done) * TOKEN_ROWS, CHUNK_WROWS), piece * TOKEN_ROWS)
                        fn(pltpu.make_async_copy(zeros.at[pl.ds(0, piece * TOKEN_ROWS), :], xs_out.at[dst, :], zsem))

                return 0

            lax.fori_loop(0, N_EXPERTS, per_expert, 0)

        for_each_pad_piece(lambda cp: cp.start())

        def start_block(b, _):
            zero_block(b).start()
            return 0

        lax.fori_loop(nb_ref[0], n_blocks, start_block, 0)
        for_each_pad_piece(lambda cp: cp.wait())

        def wait_block(b, _):
            zero_block(0).wait()
            return 0

        lax.fori_loop(nb_ref[0], n_blocks, wait_block, 0)

    def drain(half):
        pltpu.make_async_copy(tiles.at[half], xs_out.at[pl.ds(0, LOCAL_SLOTS * TOKEN_ROWS), :], sems.at[half]).wait()

    for half in range(2):
        @pl.when(i > 0)
        def _():
            drain(half)

        h = h_ref[half * ts:(half + 1) * ts, :].astype(BF16)
        slots = slot_ref[:, half * ts:(half + 1) * ts]
        sub = ts // 2
        parts = []
        for jb in range(LOCAL_SLOTS // ts):
            in_block = jnp.where(slots >> TILE_SHIFT == jb, slots & (ts - 1), -1).astype(F32).astype(BF16)
            for r in range(ts // sub):
                j = (lax.broadcasted_iota(I32, (sub, ts), 0) + r * sub).astype(F32).astype(BF16)
                perm = jnp.zeros((sub, ts), BF16)
                for k in range(TOP_K):
                    perm = jnp.where(j == in_block[k:k + 1, :], jnp.ones((), BF16), perm)
                parts.append(perm)
        _store_token_tiles(tiles.at[half], _dot(jnp.concatenate(parts, axis=0), h), already_bf16=True)

        _for_each_chunk(
            2 * i + half, wrow_ref,
            lambda local, sorted_wrow, half=half: _chunk_copy(tiles.at[half], local, xs_out, sorted_wrow,
                                                              sems.at[half]).start())

    @pl.when(i == pl.num_programs(0) - 1)
    def _():
        drain(0)
        drain(1)


def _scatter(h1, slots, starts, counts, n_used, chunk_wrow, n_rows):
    t_rows = h1.shape[0]
    step = 2 * MOE_TILE
    n_blocks = n_rows // EXPERT_BLOCK
    return pl.pallas_call(
        functools.partial(_scatter_kernel, n_blocks),
        grid_spec=pltpu.PrefetchScalarGridSpec(
            num_scalar_prefetch=4,
            grid=(t_rows // step,),
            in_specs=[pl.BlockSpec((step, D_MODEL), lambda i, *_: (i, 0)),
                      pl.BlockSpec((8, step), lambda i, *_: (0, i))],
            out_specs=pl.BlockSpec(memory_space=pl.ANY),
            scratch_shapes=[pltpu.VMEM((2, LOCAL_SLOTS * TOKEN_ROWS, LANES), U32),
                            pltpu.VMEM((EXPERT_BLOCK * TOKEN_ROWS, LANES), U32),
                            pltpu.SemaphoreType.DMA((2,)), pltpu.SemaphoreType.DMA(())],
        ),
        out_shape=jax.ShapeDtypeStruct((n_rows * TOKEN_ROWS, LANES), U32),
        compiler_params=pltpu.CompilerParams(dimension_semantics=("arbitrary",), vmem_limit_bytes=V7X_VMEM_LIMIT),
        name="scatter",
    )(starts, counts, n_used, chunk_wrow, h1, slots)


def _experts_kernel(be_ref, nb_ref, x_ref, wg_ref, wu_ref, wd_ref, y_ref, wg_bf, wu_bf, wd_bf):
    b = pl.program_id(0)

    @pl.when((b == 0) | (be_ref[b] != be_ref[jnp.maximum(b - 1, 0)]))
    def _():
        wg_bf[...] = wg_ref[...].astype(BF16)
        wu_bf[...] = wu_ref[...].astype(BF16)
        wd_bf[...] = wd_ref[...].astype(BF16)

    @pl.when(b < nb_ref[0])
    def _():
        his, los = _load_token_tiles(x_ref, EXPERT_BLOCK)
        xb = jnp.concatenate(his + los, axis=-1).astype(BF16)
        gate = _dot(xb, wg_bf[...])
        up = _dot(xb, wu_bf[...])
        hid = (gate * jax.nn.sigmoid(gate) * up).astype(BF16)
        _store_token_tiles(y_ref, _dot(hid, wd_bf[...]))

    @pl.when(b >= nb_ref[0])
    def _():
        y_ref[...] = jnp.zeros_like(y_ref)


def _experts(xs, block_e, n_used, w_gate, w_up, w_down):
    blk_rows = EXPERT_BLOCK * TOKEN_ROWS
    return pl.pallas_call(
        _experts_kernel,
        grid_spec=pltpu.PrefetchScalarGridSpec(
            num_scalar_prefetch=2,
            grid=(xs.shape[0] // blk_rows,),
            in_specs=[pl.BlockSpec((blk_rows, LANES), lambda b, be, nb: (jnp.minimum(b, jnp.maximum(nb[0], 1) - 1), 0)),
                      pl.BlockSpec((None, D_MODEL, D_EXPERT), lambda b, be, nb: (be[b], 0, 0)),
                      pl.BlockSpec((None, D_MODEL, D_EXPERT), lambda b, be, nb: (be[b], 0, 0)),
                      pl.BlockSpec((None, D_EXPERT, D_MODEL), lambda b, be, nb: (be[b], 0, 0))],
            out_specs=pl.BlockSpec((blk_rows, LANES), lambda b, be, nb: (b, 0)),
            scratch_shapes=[pltpu.VMEM((D_MODEL, D_EXPERT), BF16), pltpu.VMEM((D_MODEL, D_EXPERT), BF16),
                            pltpu.VMEM((D_EXPERT, D_MODEL), BF16)],
        ),
        out_shape=jax.ShapeDtypeStruct(xs.shape, U32),
        compiler_params=pltpu.CompilerParams(dimension_semantics=("arbitrary",)),
        name="experts",
    )(block_e, n_used, xs, w_gate, w_up, w_down)


def _combine_kernel(n_prompt_tiles, wrow_ref, h_ref, w_ref, slot_ref, ys_hbm,
                    wsg_ref, wsu_ref, wsd_ref, g2_ref, b2_ref, yp_hbm, ysm_hbm, ybuf, obuf, gsems, osems):
    i = pl.program_id(0)
    n_steps = pl.num_programs(0)
    tc = MOE_TILE
    tile_t = tc // BATCH

    def gather(tile, slot):
        _for_each_chunk(
            tile, wrow_ref,
            lambda local, sorted_wrow: _chunk_copy(ys_hbm, sorted_wrow, ybuf.at[slot], local, gsems.at[slot]).start())

    def gather_wait(slot):
        pltpu.make_async_copy(ys_hbm.at[pl.ds(0, LOCAL_SLOTS * TOKEN_ROWS), :], ybuf.at[slot], gsems.at[slot]).wait()

    def out_copies(dst_hbm, t0, slot):
        return [pltpu.make_async_copy(obuf.at[slot, :, b, :], dst_hbm.at[b, pl.ds(t0, tile_t), :], osems.at[slot])
                for b in range(BATCH)]

    def out_wait(slot):
        for cp in out_copies(yp_hbm, 0, slot):
            cp.wait()

    def prepare(half):
        rows = slice(half * tc, (half + 1) * tc)
        h = h_ref[rows, :]
        hb = h.astype(BF16)
        gate = _dot(hb, wsg_ref[...])
        up = _dot(hb, wsu_ref[...])
        shared = _dot((gate * jax.nn.sigmoid(gate) * up).astype(BF16), wsd_ref[...])

        eye = (lax.broadcasted_iota(I32, (tc, tc), 0) == lax.broadcasted_iota(I32, (tc, tc), 1)).astype(BF16)
        to_cols = lambda a: lax.dot_general(eye, a.astype(F32).astype(BF16), (((1,), (1,)), ((), ())),
                                            preferred_element_type=F32)
        slots = slot_ref[:, rows]
        w_bf = to_cols(w_ref[:, rows]).astype(BF16)
        s_block = to_cols(slots >> TILE_SHIFT)
        s_offset = to_cols(slots & (tc - 1))

        j = lax.broadcasted_iota(I32, (tc, tc), 1).astype(F32).astype(BF16)
        blocks = []
        for lb in range(LOCAL_SLOTS // tc):
            in_block = jnp.where(s_block == float(lb), s_offset, -1.0).astype(BF16)
            blk = jnp.zeros((tc, tc), BF16)
            for k in range(TOP_K):
                blk = jnp.where(j == in_block[:, k:k + 1], w_bf[:, k:k + 1], blk)
            blocks.append(blk)
        return h, shared, jnp.concatenate(blocks, axis=-1)

    def finish(half, buf, h, shared, mix_w):
        gather_wait(buf)
        his, los = _load_token_tiles(ybuf.at[buf], LOCAL_SLOTS)
        pieces = his + los
        routed = jnp.concatenate(
            [_dot(mix_w, jnp.concatenate(pieces[p:p + 2], axis=-1).astype(BF16)) for p in range(0, len(pieces), 2)],
            axis=-1)
        out = _layer_norm(DEEPNORM_ALPHA * h + (routed + shared), g2_ref[...], b2_ref[...])
        obuf[half] = out.reshape(tile_t, BATCH, D_MODEL)

        g = 2 * i + half

        @pl.when(g < n_prompt_tiles)
        def _():
            for cp in out_copies(yp_hbm, pl.multiple_of(g * tile_t, tile_t), half):
                cp.start()

        @pl.when(g >= n_prompt_tiles)
        def _():
            for cp in out_copies(ysm_hbm, pl.multiple_of((g - n_prompt_tiles) * tile_t, tile_t), half):
                cp.start()

    cur = 2 * (i % 2)
    nxt = 2 - cur

    @pl.when(i == 0)
    def _():
        gather(0, 0)
        gather(1, 1)

    @pl.when(i + 1 < n_steps)
    def _():
        gather(2 * i + 2, nxt)
        gather(2 * i + 3, nxt + 1)

    @pl.when(i > 0)
    def _():
        out_wait(0)
        out_wait(1)

    prepared = [prepare(half) for half in range(2)]
    for half in range(2):
        finish(half, cur + half, *prepared[half])

    @pl.when(i + 1 == n_steps)
    def _():
        out_wait(0)
        out_wait(1)


def _combine(h1, gate_w, slots, ys, chunk_wrow, ws_gate, ws_up, ws_down, ln2_g, ln2_b, prompt_shape, sample_shape):
    t_rows = h1.shape[0]
    tc = MOE_TILE
    step = 2 * tc
    n_prompt_tiles = prompt_shape[0] * prompt_shape[1] // tc
    const2 = lambda i, *_: (0, 0)
    any_spec = pl.BlockSpec(memory_space=pl.ANY)
    return pl.pallas_call(
        functools.partial(_combine_kernel, n_prompt_tiles),
        grid_spec=pltpu.PrefetchScalarGridSpec(
            num_scalar_prefetch=1,
            grid=(t_rows // step,),
            in_specs=[pl.BlockSpec((step, D_MODEL), lambda i, *_: (i, 0)),
                      pl.BlockSpec((8, step), lambda i, *_: (0, i)),
                      pl.BlockSpec((8, step), lambda i, *_: (0, i)),
                      any_spec,
                      pl.BlockSpec((D_MODEL, D_EXPERT), const2),
                      pl.BlockSpec((D_MODEL, D_EXPERT), const2),
                      pl.BlockSpec((D_EXPERT, D_MODEL), const2),
                      pl.BlockSpec((1, D_MODEL), const2),
                      pl.BlockSpec((1, D_MODEL), const2)],
            out_specs=(any_spec, any_spec),
            scratch_shapes=[pltpu.VMEM((4, LOCAL_SLOTS * TOKEN_ROWS, LANES), U32),
                            pltpu.VMEM((2, tc // BATCH, BATCH, D_MODEL), F32),
                            pltpu.SemaphoreType.DMA((4,)), pltpu.SemaphoreType.DMA((2,))],
        ),
        out_shape=(jax.ShapeDtypeStruct(prompt_shape, F32), jax.ShapeDtypeStruct(sample_shape, F32)),
        compiler_params=pltpu.CompilerParams(dimension_semantics=("arbitrary",), vmem_limit_bytes=V7X_VMEM_LIMIT),
        name="combine",
    )(chunk_wrow, h1, gate_w, slots, ys, ws_gate.astype(BF16), ws_up.astype(BF16),
      ws_down.astype(BF16), ln2_g.reshape(1, D_MODEL), ln2_b.reshape(1, D_MODEL))


def _time_major(x):
    b, l, d = x.shape
    return jnp.transpose(x, (1, 0, 2)).reshape(l * b, d)


def _batch_major(x, b):
    return jnp.transpose(x.reshape(x.shape[0] // b, b, x.shape[1]), (1, 0, 2))


def _block_diag(blocks):
    n, r, c = blocks.shape
    eye = jnp.eye(n, dtype=blocks.dtype)
    return (blocks[:, :, None, :] * eye[:, None, :, None]).reshape(n * r, n * c)


def kernel(x_prompt, x_sample, state_ssm_re, state_ssm_im, cache_conv, ln_in_g, ln_in_b, w_in, lam_re, lam_im, log_dt, ssm_b_re, ssm_b_im, ssm_c_re, ssm_c_im, ssm_d, w_glu, b_glu, conv_w, beta_ssm, beta_conv, w_out, ln1_g, ln1_b, w_router, router_bias, w_gate, w_up, w_down, ws_gate, ws_up, ws_down, ln2_g, ln2_b):
    bp, lp, _ = x_prompt.shape
    bs, ls, _ = x_sample.shape
    assert bp == BATCH and bs == BATCH and ls == CHUNK_T and lp % CHUNK_T == 0
    assert w_in.shape[0] == 1, "single-layer model"
    n_prompt = bp * lp
    row = lambda a: a.reshape(1, -1)

    a_re, a_im, bb_re, bb_im = _prep(lam_re[0], lam_im[0], log_dt[0], ssm_b_re[0], ssm_b_im[0])
    groups_in = 128 // SSM_GROUP
    groups_out = 256 // SSM_GROUP
    bbd = lambda bb: jnp.stack([_block_diag(bb[j * groups_in:(j + 1) * groups_in])
                                for j in range(N_GROUPS // groups_in)]).astype(BF16)
    ct = lambda cc: jnp.transpose(cc, (0, 2, 1))
    cbd = lambda cc: jnp.stack([_block_diag(ct(cc)[j * groups_out:(j + 1) * groups_out])
                                for j in range(N_GROUPS // groups_out)]).astype(BF16)
    mix_weights = (row(ln_in_g), row(ln_in_b), w_in[0].astype(BF16), row(a_re), row(a_im),
                   bbd(bb_re), bbd(bb_im), cbd(ssm_c_re[0]), cbd(-ssm_c_im[0]), row(ssm_d[0]),
                   w_glu[0].astype(BF16), row(b_glu[0]), conv_w[0], row(beta_ssm[0]), row(beta_conv[0]),
                   w_out[0].astype(BF16), row(ln1_g[0]), row(ln1_b[0]))

    h0r = state_ssm_re[0].reshape(BATCH, D_STATE)
    h0i = state_ssm_im[0].reshape(BATCH, D_STATE)
    cbuf = _time_major(cache_conv[0])
    h1, p_re, p_im, p_conv, s_re, s_im, s_conv = _mix(x_prompt, x_sample, h0r, h0i, cbuf, mix_weights)

    t_rows = h1.shape[0]
    assert t_rows % (2 * MOE_TILE) == 0 and n_prompt % MOE_TILE == 0
    n_tiles = t_rows // MOE_TILE
    max_rows = t_rows * TOP_K + n_tiles * N_EXPERTS * (RUN_CHUNK - 1) + N_EXPERTS * EXPERT_BLOCK
    n_blocks = -(-max_rows // EXPERT_BLOCK) + 1
    n_blocks_pad = -(-n_blocks // 128) * 128
    gate_w, slots, chunk_e, chunk_rel, cnt = _route(h1, w_router[0], router_bias[0])
    block_e, n_used, starts, chunk_wrow = _pos(cnt, chunk_e[:, 0], chunk_rel[:, 0], n_blocks_pad,
                                               (n_blocks - 1) * EXPERT_BLOCK)
    n_used, chunk_wrow = n_used[0, :1], chunk_wrow.reshape(-1)
    xs = _scatter(h1, slots, starts[:, 0], cnt[:, 0].astype(I32), n_used, chunk_wrow, n_blocks * EXPERT_BLOCK)
    ys = _experts(xs, block_e[0, :n_blocks], n_used, w_gate[0], w_up[0], w_down[0])
    y_prompt, y_sample = _combine(h1, gate_w, slots, ys, chunk_wrow, ws_gate[0], ws_up[0], ws_down[0],
                                  ln2_g[0], ln2_b[0], x_prompt.shape, x_sample.shape)

    st = lambda s: s.reshape(1, BATCH, N_GROUPS, SSM_STATE)
    cv = lambda t: _batch_major(t, BATCH)[None]
    return (y_prompt, y_sample, st(p_re), st(p_im), cv(p_conv), st(s_re), st(s_im), cv(s_conv))
```

```python
import functools

import jax
import jax.numpy as jnp
from jax import lax
from jax.experimental import pallas as pl
from jax.experimental.pallas import tpu as pltpu

F32 = jnp.float32
BF16 = jnp.bfloat16
I32 = jnp.int32
U32 = jnp.uint32

D_MODEL = 1024
D_SSM = 512
D_CONV = 512
SSM_GROUP = 16
N_GROUPS = 32
SSM_STATE = 64
D_STATE = N_GROUPS * SSM_STATE
N_EXPERTS = 64
TOP_K = 6
N_EXPERT_GROUPS = 8
GROUP_SIZE = N_EXPERTS // N_EXPERT_GROUPS
TOPK_GROUPS = 4
D_EXPERT = 256
ROUTED_SCALE = 2.5
DEPTH = 1
DEEPNORM_ALPHA = (2.0 * DEPTH) ** 0.25
LN_EPS = 1e-5
RMS_EPS = 1e-6

BATCH = 8
CHUNK_T = 64
CHUNK_ROWS = CHUNK_T * BATCH
SCAN_COLS = 512
MOE_TILE = 256
TILE_SHIFT = MOE_TILE.bit_length() - 1
ROUTE_TILES_PER_STEP = 1
RUN_CHUNK = 8
LOCAL_SLOTS = TOP_K * MOE_TILE + N_EXPERTS * RUN_CHUNK
N_CHUNKS = LOCAL_SLOTS // RUN_CHUNK
CHUNK_COUNT_PIECES = tuple(1 << b for b in range(N_CHUNKS.bit_length() - 1, -1, -1))
EXPERT_BLOCK = 1024
PAD_PIECES = tuple(1 << b for b in range(EXPERT_BLOCK.bit_length() - 2, RUN_CHUNK.bit_length() - 2, -1))
LANES = 128
TOKEN_ROWS = D_MODEL // (2 * LANES)
CHUNK_WROWS = RUN_CHUNK * TOKEN_ROWS
V7X_VMEM_LIMIT = 56 * 1024 * 1024
NEG_INF = float("-inf")


def _layer_norm(x, g, b):
    mu = jnp.mean(x, axis=-1, keepdims=True)
    xc = x - mu
    var = jnp.mean(xc * xc, axis=-1, keepdims=True)
    return xc * lax.rsqrt(var + LN_EPS) * g + b


def _rms_norm(x, g):
    return x * lax.rsqrt(jnp.mean(x * x, axis=-1, keepdims=True) + RMS_EPS) * g


def _dot(a, b):
    return jnp.dot(a, b, preferred_element_type=F32)


def _discretise(lr, li, log_dt):
    dt = jnp.exp(log_dt)
    mag = jnp.exp(lr * dt)
    ar = mag * jnp.cos(li * dt)
    ai = mag * jnp.sin(li * dt)
    den = lr * lr + li * li
    qr = ((ar - 1.0) * lr + ai * li) / den
    qi = (ai * lr - (ar - 1.0) * li) / den
    return ar, ai, qr, qi


def _prep_kernel(lr_ref, li_ref, ldt_ref, lrc_ref, lic_ref, ldtc_ref, br_ref, bi_ref,
                 ar_ref, ai_ref, bbr_ref, bbi_ref):
    ar, ai, _, _ = _discretise(lr_ref[...], li_ref[...], ldt_ref[...])
    ar_ref[...] = ar
    ai_ref[...] = ai
    _, _, qr, qi = _discretise(lrc_ref[...], lic_ref[...], ldtc_ref[...])
    br = br_ref[...]
    bi = bi_ref[...]
    bbr_ref[...] = qr * br - qi * bi
    bbi_ref[...] = qr * bi + qi * br


def _prep(lam_re, lam_im, log_dt, b_re, b_im):
    g, p = lam_re.shape
    per_channel = lambda a: jnp.repeat(a, SSM_GROUP, axis=0)
    rows = lambda b: jnp.transpose(b, (0, 2, 1)).reshape(g * SSM_GROUP, p)
    ldt = log_dt.reshape(g, 1)
    ar, ai, bbr, bbi = pl.pallas_call(
        _prep_kernel,
        out_shape=(jax.ShapeDtypeStruct((g, p), F32), jax.ShapeDtypeStruct((g, p), F32),
                   jax.ShapeDtypeStruct((g * SSM_GROUP, p), F32), jax.ShapeDtypeStruct((g * SSM_GROUP, p), F32)),
        name="prep",
    )(lam_re, lam_im, ldt, per_channel(lam_re), per_channel(lam_im), per_channel(ldt), rows(b_re), rows(b_im))
    return ar, ai, bbr.reshape(g, SSM_GROUP, p), bbi.reshape(g, SSM_GROUP, p)


def _chunk_copies(src_hbm, t0, xbuf, slot, sems):
    return [pltpu.make_async_copy(src_hbm.at[b, pl.ds(t0, CHUNK_T), :], xbuf.at[slot, :, b, :], sems.at[slot])
            for b in range(BATCH)]


def _mix_kernel(n_prompt_chunks,
                xp_hbm, xs_hbm, h0r_ref, h0i_ref, cbuf_ref, lng_ref, lnb_ref, win_ref, ar_ref, ai_ref,
                bbr_ref, bbi_ref, cbr_ref, cbi_ref, dsk_ref, wglu_ref, bglu_ref, cw_ref,
                bs_ref, bc_ref, wout_ref, g1_ref, b1_ref,
                h1_ref, pr_ref, pi_ref, pc_ref, sr_ref, si_ref, sc_ref,
                str_ref, sti_ref, hr_ref, hi_ref, cv_ref, xbuf, xsems):
    c = pl.program_id(0)
    n = CHUNK_ROWS
    carry_rows = 2 * BATCH
    slot = c % 2

    @pl.when(c == 0)
    def _():
        for cp in _chunk_copies(xp_hbm, 0, xbuf, 0, xsems):
            cp.start()

    @pl.when(c + 1 < n_prompt_chunks)
    def _():
        for cp in _chunk_copies(xp_hbm, pl.multiple_of((c + 1) * CHUNK_T, CHUNK_T), xbuf, 1 - slot, xsems):
            cp.start()

    @pl.when(c + 1 == n_prompt_chunks)
    def _():
        for cp in _chunk_copies(xs_hbm, 0, xbuf, 1 - slot, xsems):
            cp.start()

    @pl.when(c == 0)
    def _():
        hr_ref[...] = jnp.zeros_like(hr_ref)
        hi_ref[...] = jnp.zeros_like(hi_ref)
        cv_ref[0:carry_rows, :] = jnp.zeros((carry_rows, D_CONV), F32)

    @pl.when(c == n_prompt_chunks)
    def _():
        hr_ref[...] = h0r_ref[...]
        hi_ref[...] = h0i_ref[...]
        cv_ref[0:carry_rows, :] = cbuf_ref[...]

    for cp in _chunk_copies(xp_hbm, 0, xbuf, slot, xsems):
        cp.wait()
    h = _layer_norm(xbuf[slot].reshape(n, D_MODEL), lng_ref[...], lnb_ref[...])
    hb = h.astype(BF16)

    u = _dot(hb, win_ref[:, 0:D_SSM])
    ub = u.astype(BF16)
    u_tile = 128
    s_tile = u_tile // SSM_GROUP * SSM_STATE
    for j in range(D_SSM // u_tile):
        uj = ub[:, j * u_tile:(j + 1) * u_tile]
        str_ref[:, j * s_tile:(j + 1) * s_tile] = _dot(uj, bbr_ref[j])
        sti_ref[:, j * s_tile:(j + 1) * s_tile] = _dot(uj, bbi_ref[j])

    gate_b = _dot(hb, win_ref[:, D_SSM:D_SSM + D_CONV])
    gate_c = _dot(hb, win_ref[:, D_SSM + D_CONV:D_SSM + 2 * D_CONV])
    v = _dot(hb, win_ref[:, D_SSM + 2 * D_CONV:D_SSM + 3 * D_CONV])
    cv_ref[carry_rows:carry_rows + n, :] = gate_c * v

    for cb in range(D_STATE // SCAN_COLS):
        cols = slice(cb * SCAN_COLS, (cb + 1) * SCAN_COLS)
        ar = jnp.broadcast_to(ar_ref[:, cols], (BATCH, SCAN_COLS))
        ai = jnp.broadcast_to(ai_ref[:, cols], (BATCH, SCAN_COLS))

        def step(t, carry, cols=cols, ar=ar, ai=ai):
            sr, si = carry
            r0 = pl.multiple_of(t * BATCH, BATCH)
            nr = ar * sr - ai * si + str_ref[pl.ds(r0, BATCH), cols]
            ni = ar * si + ai * sr + sti_ref[pl.ds(r0, BATCH), cols]
            str_ref[pl.ds(r0, BATCH), cols] = nr
            sti_ref[pl.ds(r0, BATCH), cols] = ni
            return nr, ni

        sr, si = lax.fori_loop(0, CHUNK_T, step, (hr_ref[:, cols], hi_ref[:, cols]), unroll=True)
        hr_ref[:, cols] = sr
        hi_ref[:, cols] = si

    k_tile = 1024
    ys = []
    for j in range(D_STATE // k_tile):
        sl = slice(j * k_tile, (j + 1) * k_tile)
        ys.append(_dot(str_ref[:, sl].astype(BF16), cbr_ref[j]) + _dot(sti_ref[:, sl].astype(BF16), cbi_ref[j]))
    y_ssm = jnp.concatenate(ys, axis=-1) + dsk_ref[...] * u
    g = jax.nn.gelu(y_ssm)
    y_ssm = g * jax.nn.sigmoid(_dot(g.astype(BF16), wglu_ref[...]) + bglu_ref[...])
    mix = _dot(_rms_norm(y_ssm, bs_ref[...]).astype(BF16), wout_ref[0:D_SSM, :])

    y_conv = gate_b * (cv_ref[0:n, :] * cw_ref[0:1, :]
                       + cv_ref[BATCH:BATCH + n, :] * cw_ref[1:2, :]
                       + cv_ref[carry_rows:carry_rows + n, :] * cw_ref[2:3, :])
    tail = cv_ref[n:n + carry_rows, :]
    cv_ref[0:carry_rows, :] = tail
    mix = mix + _dot(_rms_norm(y_conv, bc_ref[...]).astype(BF16), wout_ref[D_SSM:D_SSM + D_CONV, :])

    h1_ref[...] = _layer_norm(DEEPNORM_ALPHA * h + mix, g1_ref[...], b1_ref[...])

    @pl.when(c == n_prompt_chunks - 1)
    def _():
        pr_ref[...] = hr_ref[...]
        pi_ref[...] = hi_ref[...]
        pc_ref[...] = tail

    @pl.when(c == n_prompt_chunks)
    def _():
        sr_ref[...] = hr_ref[...]
        si_ref[...] = hi_ref[...]
        sc_ref[...] = tail


def _mix(x_prompt, x_sample, h0r, h0i, cbuf, weights):
    n_prompt_chunks = x_prompt.shape[1] // CHUNK_T
    n_chunks = n_prompt_chunks + 1
    t_rows = n_chunks * CHUNK_ROWS
    any_spec = pl.BlockSpec(memory_space=pl.ANY)
    const2 = lambda c: (0, 0)
    const3 = lambda c: (0, 0, 0)
    w_specs = [pl.BlockSpec(w.shape, const3 if w.ndim == 3 else const2) for w in weights]
    state = jax.ShapeDtypeStruct((BATCH, D_STATE), F32)
    tail = jax.ShapeDtypeStruct((2 * BATCH, D_CONV), F32)
    state_spec = pl.BlockSpec((BATCH, D_STATE), const2)
    tail_spec = pl.BlockSpec((2 * BATCH, D_CONV), const2)
    return pl.pallas_call(
        functools.partial(_mix_kernel, n_prompt_chunks),
        grid=(n_chunks,),
        in_specs=[any_spec, any_spec, state_spec, state_spec, tail_spec] + w_specs,
        out_specs=(pl.BlockSpec((CHUNK_ROWS, D_MODEL), lambda c: (c, 0)),
                   state_spec, state_spec, tail_spec, state_spec, state_spec, tail_spec),
        out_shape=(jax.ShapeDtypeStruct((t_rows, D_MODEL), F32), state, state, tail, state, state, tail),
        scratch_shapes=[pltpu.VMEM((CHUNK_ROWS, D_STATE), F32), pltpu.VMEM((CHUNK_ROWS, D_STATE), F32),
                        pltpu.VMEM((BATCH, D_STATE), F32), pltpu.VMEM((BATCH, D_STATE), F32),
                        pltpu.VMEM((CHUNK_ROWS + 2 * BATCH, D_CONV), F32),
                        pltpu.VMEM((2, CHUNK_T, BATCH, D_MODEL), F32), pltpu.SemaphoreType.DMA((2,))],
        compiler_params=pltpu.CompilerParams(dimension_semantics=("arbitrary",), vmem_limit_bytes=V7X_VMEM_LIMIT),
        name="mix",
    )(x_prompt, x_sample, h0r, h0i, cbuf, *weights)


def _route_tile(h, w_hi, w_lo, bias, tri, ltri, run_before):
    tr = MOE_TILE

    nt_dot = lambda a, b: lax.dot_general(a, b, (((1,), (1,)), ((), ())), preferred_element_type=F32)
    h_hi = h.astype(BF16)
    h_lo = (h - h_hi.astype(F32)).astype(BF16)
    logits = nt_dot(w_hi, h_hi) + (nt_dot(w_hi, h_lo) + nt_dot(w_lo, h_hi))
    scores = jax.nn.sigmoid(logits)
    sel = scores + bias

    sub = lax.broadcasted_iota(I32, (GROUP_SIZE, tr), 0).astype(F32)
    blocks, gscore = [], []
    for g in range(N_EXPERT_GROUPS):
        blk = sel[g * GROUP_SIZE:(g + 1) * GROUP_SIZE, :]
        m1 = jnp.max(blk, axis=0, keepdims=True)
        first = jnp.min(jnp.where(blk == m1, sub, float(GROUP_SIZE)), axis=0, keepdims=True)
        m2 = jnp.max(jnp.where(sub == first, NEG_INF, blk), axis=0, keepdims=True)
        blocks.append(blk)
        gscore.append(m1 + m2)
    masked = []
    for g in range(N_EXPERT_GROUPS):
        beaten = jnp.zeros((1, tr), F32)
        for o in range(N_EXPERT_GROUPS):
            if o == g:
                continue
            wins = gscore[o] >= gscore[g] if o < g else gscore[o] > gscore[g]
            beaten = beaten + wins.astype(F32)
        masked.append(jnp.where(beaten < float(TOPK_GROUPS), blocks[g], NEG_INF))
    masked = jnp.concatenate(masked, axis=0)

    row = lax.broadcasted_iota(I32, (N_EXPERTS, tr), 0).astype(F32)
    picked = jnp.zeros((N_EXPERTS, tr), F32)
    hots, gates = [], []
    for _ in range(TOP_K):
        m = jnp.max(masked, axis=0, keepdims=True)
        idx = jnp.min(jnp.where(masked == m, row, float(N_EXPERTS)), axis=0, keepdims=True)
        hot = row == idx
        hots.append(hot)
        gates.append(jnp.sum(jnp.where(hot, scores, 0.0), axis=0, keepdims=True))
        masked = jnp.where(hot, NEG_INF, masked)
        picked = picked + hot.astype(F32)
    total = gates[0]
    for k in range(1, TOP_K):
        total = total + gates[k]

    chunk = float(RUN_CHUNK)
    pb = picked.astype(BF16)
    earlier = _dot(pb, tri)
    cnt_col = jnp.sum(picked, axis=1, keepdims=True)
    run_len = jnp.floor((cnt_col + (chunk - 1.0)) * (1.0 / chunk)) * chunk
    lower = _dot(ltri, jnp.broadcast_to(run_len, (N_EXPERTS, LANES)).astype(BF16))[:, 0:1]
    slot_of = earlier + lower
    w_rows = [gates[k] / total * ROUTED_SCALE for k in range(TOP_K)]
    slot_rows = [jnp.sum(jnp.where(hots[k], slot_of, 0.0), axis=0, keepdims=True).astype(I32) for k in range(TOP_K)]

    c_first = lax.broadcasted_iota(I32, (N_EXPERTS, N_CHUNKS), 1).astype(F32) * chunk
    owner = jnp.sum((lower + run_len <= c_first).astype(F32), axis=0, keepdims=True)
    hot_e = lax.broadcasted_iota(I32, (N_EXPERTS, N_CHUNKS), 0).astype(F32) == owner
    rel = jnp.sum(jnp.where(hot_e, run_before - lower + c_first, 0.0), axis=0, keepdims=True)
    return w_rows, slot_rows, owner.astype(I32), rel.astype(I32), run_len


def _route_kernel(h_ref, wrt_ref, bias_ref, tri_ref, ltri_ref, w_ref, slot_ref, ce_ref, crel_ref, cnt_ref, run_col):
    i = pl.program_id(0)
    tr = MOE_TILE

    @pl.when(i == 0)
    def _():
        run_col[...] = jnp.zeros_like(run_col)

    wrt = wrt_ref[...]
    w_hi = wrt.astype(BF16)
    w_lo = (wrt - w_hi.astype(F32)).astype(BF16)
    run = run_col[...]
    for half in range(ROUTE_TILES_PER_STEP):
        cols = slice(half * tr, (half + 1) * tr)
        w_rows, slot_rows, owner, rel, run_len = _route_tile(
            h_ref[cols, :], w_hi, w_lo, bias_ref[...], tri_ref[...], ltri_ref[...], run)
        for k in range(TOP_K):
            w_ref[k:k + 1, cols] = w_rows[k]
            slot_ref[k:k + 1, cols] = slot_rows[k]
        w_ref[TOP_K:, cols] = jnp.zeros((8 - TOP_K, tr), F32)
        slot_ref[TOP_K:, cols] = jnp.full((8 - TOP_K, tr), -1, I32)
        ce_ref[half] = jnp.broadcast_to(owner, (8, N_CHUNKS))
        crel_ref[half] = jnp.broadcast_to(rel, (8, N_CHUNKS))
        run = run + run_len
    run_col[...] = run
    cnt_ref[...] = jnp.broadcast_to(run, cnt_ref.shape)


def _route(h1, w_router, router_bias):
    t_rows = h1.shape[0]
    tr = MOE_TILE
    n_tiles = t_rows // tr
    step = ROUTE_TILES_PER_STEP * tr
    tri = jnp.triu(jnp.ones((tr, tr), BF16), k=1)
    ltri = jnp.tril(jnp.ones((N_EXPERTS, N_EXPERTS), BF16), k=-1)
    tok = lambda i: (0, i)
    const2 = lambda i: (0, 0)
    per_tile = lambda i: (i, 0, 0)
    return pl.pallas_call(
        _route_kernel,
        grid=(t_rows // step,),
        in_specs=[pl.BlockSpec((step, D_MODEL), lambda i: (i, 0)),
                  pl.BlockSpec((N_EXPERTS, D_MODEL), const2),
                  pl.BlockSpec((N_EXPERTS, 1), const2),
                  pl.BlockSpec((tr, tr), const2),
                  pl.BlockSpec((N_EXPERTS, N_EXPERTS), const2)],
        out_specs=(pl.BlockSpec((8, step), tok), pl.BlockSpec((8, step), tok),
                   pl.BlockSpec((ROUTE_TILES_PER_STEP, 8, N_CHUNKS), per_tile),
                   pl.BlockSpec((ROUTE_TILES_PER_STEP, 8, N_CHUNKS), per_tile),
                   pl.BlockSpec((N_EXPERTS, LANES), const2)),
        out_shape=(jax.ShapeDtypeStruct((8, t_rows), F32), jax.ShapeDtypeStruct((8, t_rows), I32),
                   jax.ShapeDtypeStruct((n_tiles, 8, N_CHUNKS), I32), jax.ShapeDtypeStruct((n_tiles, 8, N_CHUNKS), I32),
                   jax.ShapeDtypeStruct((N_EXPERTS, LANES), F32)),
        scratch_shapes=[pltpu.VMEM((N_EXPERTS, 1), F32)],
        compiler_params=pltpu.CompilerParams(dimension_semantics=("arbitrary",)),
        name="route",
    )(h1, w_router.T, router_bias.reshape(N_EXPERTS, 1), tri, ltri)


def _pos_kernel(n_blocks_pad, dummy_row, ccol_ref, crow_ref, ce_ref, crel_ref, be_ref, nb_ref, start_ref, wrow_ref):
    blk = float(EXPERT_BLOCK)
    pad = lambda cnt: jnp.floor((cnt + (blk - 1.0)) * (1.0 / blk)) * blk
    padded_row = pad(crow_ref[0:1, :])
    lane = lax.broadcasted_iota(I32, (N_EXPERTS, 128), 1)
    sub = lax.broadcasted_iota(I32, (N_EXPERTS, 128), 0)
    start = jnp.sum(jnp.where(lane < sub, padded_row, 0.0), axis=1, keepdims=True)
    end = start + pad(ccol_ref[:, 0:1])

    first_row = lax.broadcasted_iota(I32, (N_EXPERTS, n_blocks_pad), 1).astype(F32) * blk
    owner = jnp.sum((end <= first_row).astype(F32), axis=0, keepdims=True)
    be_ref[...] = jnp.broadcast_to(jnp.minimum(owner, float(N_EXPERTS - 1)).astype(I32), be_ref.shape)
    used = jnp.max(end, axis=0, keepdims=True) * (1.0 / blk)
    nb_ref[...] = jnp.broadcast_to(used.astype(I32), nb_ref.shape)
    start_ref[...] = jnp.broadcast_to(start.astype(I32), start_ref.shape)

    ce = ce_ref[...]
    base = jnp.full(ce.shape, float(dummy_row), F32)
    for e in range(N_EXPERTS):
        base = jnp.where(ce == e, start[e:e + 1, 0:1], base)
    wrow_ref[...] = (base.astype(I32) + crel_ref[...]) * TOKEN_ROWS


def _pos(cnt, chunk_e, chunk_rel, n_blocks_pad, dummy_row):
    counts = cnt[:, 0]
    ccol = jnp.broadcast_to(counts[:, None], (N_EXPERTS, 128))
    crow = jnp.broadcast_to(jnp.pad(counts, (0, 128 - N_EXPERTS))[None, :], (8, 128))
    return pl.pallas_call(
        functools.partial(_pos_kernel, n_blocks_pad, dummy_row),
        out_shape=(jax.ShapeDtypeStruct((8, n_blocks_pad), I32), jax.ShapeDtypeStruct((8, 128), I32),
                   jax.ShapeDtypeStruct((N_EXPERTS, 128), I32), jax.ShapeDtypeStruct(chunk_e.shape, I32)),
        name="pos",
    )(ccol, crow, chunk_e, chunk_rel)


def _chunk_copy(src_ref, src_wrow, dst_ref, dst_wrow, sem):
    src = pl.ds(pl.multiple_of(src_wrow, CHUNK_WROWS), CHUNK_WROWS)
    dst = pl.ds(pl.multiple_of(dst_wrow, CHUNK_WROWS), CHUNK_WROWS)
    return pltpu.make_async_copy(src_ref.at[src, :], dst_ref.at[dst, :], sem)


def _for_each_chunk(tile, wrow_ref, fn, skip_wrow=None):
    def per_chunk(c, n_done):
        wrow = wrow_ref[tile * N_CHUNKS + c]
        if skip_wrow is None:
            fn(c * CHUNK_WROWS, wrow)
            return n_done + 1

        @pl.when(wrow != skip_wrow)
        def _():
            fn(c * CHUNK_WROWS, wrow)

        return n_done + (wrow != skip_wrow).astype(I32)

    return lax.fori_loop(0, N_CHUNKS, per_chunk, 0, unroll=8)


def _store_token_tiles(dst_ref, x):
    n = x.shape[0]
    half = TOKEN_ROWS * LANES
    bits = lambda v: lax.bitcast_convert_type(v.astype(BF16).astype(F32), U32)
    for s in range(TOKEN_ROWS):
        hi = bits(x[:, s * LANES:(s + 1) * LANES])
        lo = bits(x[:, half + s * LANES:half + (s + 1) * LANES])
        dst_ref[pl.ds(s, n, stride=TOKEN_ROWS), :] = hi | (lo >> 16)


def _load_token_tiles(src_ref, n):
    his, los = [], []
    for s in range(TOKEN_ROWS):
        words = src_ref[pl.ds(s, n, stride=TOKEN_ROWS), :]
        his.append(lax.bitcast_convert_type(words & jnp.uint32(0xFFFF0000), F32))
        los.append(lax.bitcast_convert_type(words << 16, F32))
    return his, los


def _scatter_kernel(n_blocks, start_ref, count_ref, nb_ref, wrow_ref, h_ref, slot_ref, xs_out,
                    tiles, zeros, issued, sems, zsem):
    i = pl.program_id(0)
    ts = MOE_TILE
    blk_rows = EXPERT_BLOCK * TOKEN_ROWS
    dummy_wrow = (n_blocks - 1) * blk_rows

    @pl.when(i == 0)
    def _():
        zeros[...] = jnp.zeros_like(zeros)
        zero_block = lambda b: pltpu.make_async_copy(
            zeros, xs_out.at[pl.ds(pl.multiple_of(b * blk_rows, blk_rows), blk_rows), :], zsem)

        def for_each_pad_piece(fn):
            def per_expert(e, _):
                first = start_ref[e] + count_ref[e]
                n = ((count_ref[e] + (EXPERT_BLOCK - 1)) & (-EXPERT_BLOCK)) - count_ref[e]
                for piece in PAD_PIECES:
                    done = n & (-2 * piece)

                    @pl.when((n & piece) != 0)
                    def _():
                        dst = pl.ds(pl.multiple_of((first + done) * TOKEN_ROWS, CHUNK_WROWS), piece * TOKEN_ROWS)
                        fn(pltpu.make_async_copy(zeros.at[pl.ds(0, piece * TOKEN_ROWS), :], xs_out.at[dst, :], zsem))

                return 0

            lax.fori_loop(0, N_EXPERTS, per_expert, 0)

        for_each_pad_piece(lambda cp: cp.start())

        def start_block(b, _):
            zero_block(b).start()
            return 0

        lax.fori_loop(nb_ref[0], n_blocks, start_block, 0)
        for_each_pad_piece(lambda cp: cp.wait())

        def wait_block(b, _):
            zero_block(0).wait()
            return 0

        lax.fori_loop(nb_ref[0], n_blocks, wait_block, 0)

    def drain(half):
        n = issued[half]
        for piece in CHUNK_COUNT_PIECES:
            @pl.when((n & piece) != 0)
            def _():
                rows = piece * CHUNK_WROWS
                pltpu.make_async_copy(tiles.at[half, pl.ds(0, rows), :], xs_out.at[pl.ds(0, rows), :],
                                      sems.at[half]).wait()

    for half in range(2):
        @pl.when(i > 0)
        def _():
            drain(half)

        h = h_ref[half * ts:(half + 1) * ts, :].astype(BF16)
        slots = slot_ref[:, half * ts:(half + 1) * ts]
        sub = ts // 2
        for jb in range(LOCAL_SLOTS // ts):
            in_block = jnp.where(slots >> TILE_SHIFT == jb, slots & (ts - 1), -1).astype(F32).astype(BF16)
            parts = []
            for r in range(ts // sub):
                j = (lax.broadcasted_iota(I32, (sub, ts), 0) + r * sub).astype(F32).astype(BF16)
                perm = jnp.zeros((sub, ts), BF16)
                for k in range(TOP_K):
                    perm = perm + jnp.where(j == in_block[k:k + 1, :], jnp.ones((), BF16), jnp.zeros((), BF16))
                parts.append(perm)
            _store_token_tiles(tiles.at[half, pl.ds(jb * ts * TOKEN_ROWS, ts * TOKEN_ROWS), :],
                               _dot(jnp.concatenate(parts, axis=0), h))

        issued[half] = _for_each_chunk(
            2 * i + half, wrow_ref,
            lambda local, sorted_wrow, half=half: _chunk_copy(tiles.at[half], local, xs_out, sorted_wrow,
                                                              sems.at[half]).start(),
            skip_wrow=dummy_wrow)

    @pl.when(i == pl.num_programs(0) - 1)
    def _():
        drain(0)
        drain(1)


def _scatter(h1, slots, starts, counts, n_used, chunk_wrow, n_rows):
    t_rows = h1.shape[0]
    step = 2 * MOE_TILE
    n_blocks = n_rows // EXPERT_BLOCK
    return pl.pallas_call(
        functools.partial(_scatter_kernel, n_blocks),
        grid_spec=pltpu.PrefetchScalarGridSpec(
            num_scalar_prefetch=4,
            grid=(t_rows // step,),
            in_specs=[pl.BlockSpec((step, D_MODEL), lambda i, *_: (i, 0)),
                      pl.BlockSpec((8, step), lambda i, *_: (0, i))],
            out_specs=pl.BlockSpec(memory_space=pl.ANY),
            scratch_shapes=[pltpu.VMEM((2, LOCAL_SLOTS * TOKEN_ROWS, LANES), U32),
                            pltpu.VMEM((EXPERT_BLOCK * TOKEN_ROWS, LANES), U32),
                            pltpu.SMEM((2,), I32),
                            pltpu.SemaphoreType.DMA((2,)), pltpu.SemaphoreType.DMA(())],
        ),
        out_shape=jax.ShapeDtypeStruct((n_rows * TOKEN_ROWS, LANES), U32),
        compiler_params=pltpu.CompilerParams(dimension_semantics=("arbitrary",), vmem_limit_bytes=V7X_VMEM_LIMIT),
        name="scatter",
    )(starts, counts, n_used, chunk_wrow, h1, slots)


def _experts_kernel(be_ref, nb_ref, x_ref, wg_ref, wu_ref, wd_ref, y_ref, wg_bf, wu_bf, wd_bf):
    b = pl.program_id(0)

    @pl.when((b == 0) | (be_ref[b] != be_ref[jnp.maximum(b - 1, 0)]))
    def _():
        wg_bf[...] = wg_ref[...].astype(BF16)
        wu_bf[...] = wu_ref[...].astype(BF16)
        wd_bf[...] = wd_ref[...].astype(BF16)

    @pl.when(b < nb_ref[0])
    def _():
        his, los = _load_token_tiles(x_ref, EXPERT_BLOCK)
        xb = jnp.concatenate(his + los, axis=-1).astype(BF16)
        gate = _dot(xb, wg_bf[...])
        up = _dot(xb, wu_bf[...])
        hid = (gate * jax.nn.sigmoid(gate) * up).astype(BF16)
        _store_token_tiles(y_ref, _dot(hid, wd_bf[...]))

    @pl.when(b >= nb_ref[0])
    def _():
        y_ref[...] = jnp.zeros_like(y_ref)


def _experts(xs, block_e, n_used, w_gate, w_up, w_down):
    blk_rows = EXPERT_BLOCK * TOKEN_ROWS
    return pl.pallas_call(
        _experts_kernel,
        grid_spec=pltpu.PrefetchScalarGridSpec(
            num_scalar_prefetch=2,
            grid=(xs.shape[0] // blk_rows,),
            in_specs=[pl.BlockSpec((blk_rows, LANES), lambda b, be, nb: (jnp.minimum(b, jnp.maximum(nb[0], 1) - 1), 0)),
                      pl.BlockSpec((None, D_MODEL, D_EXPERT), lambda b, be, nb: (be[b], 0, 0)),
                      pl.BlockSpec((None, D_MODEL, D_EXPERT), lambda b, be, nb: (be[b], 0, 0)),
                      pl.BlockSpec((None, D_EXPERT, D_MODEL), lambda b, be, nb: (be[b], 0, 0))],
            out_specs=pl.BlockSpec((blk_rows, LANES), lambda b, be, nb: (b, 0)),
            scratch_shapes=[pltpu.VMEM((D_MODEL, D_EXPERT), BF16), pltpu.VMEM((D_MODEL, D_EXPERT), BF16),
                            pltpu.VMEM((D_EXPERT, D_MODEL), BF16)],
        ),
        out_shape=jax.ShapeDtypeStruct(xs.shape, U32),
        compiler_params=pltpu.CompilerParams(dimension_semantics=("arbitrary",)),
        name="experts",
    )(block_e, n_used, xs, w_gate, w_up, w_down)


def _combine_kernel(n_prompt_tiles, wrow_ref, h_ref, w_ref, slot_ref, ys_hbm,
                    wsg_ref, wsu_ref, wsd_ref, g2_ref, b2_ref, yp_hbm, ysm_hbm, ybuf, obuf, gsems, osems):
    i = pl.program_id(0)
    n_steps = pl.num_programs(0)
    tc = MOE_TILE
    tile_t = tc // BATCH

    def gather(tile, slot):
        _for_each_chunk(
            tile, wrow_ref,
            lambda local, sorted_wrow: _chunk_copy(ys_hbm, sorted_wrow, ybuf.at[slot], local, gsems.at[slot]).start())

    def gather_wait(slot):
        pltpu.make_async_copy(ys_hbm.at[pl.ds(0, LOCAL_SLOTS * TOKEN_ROWS), :], ybuf.at[slot], gsems.at[slot]).wait()

    def out_copies(dst_hbm, t0, slot):
        return [pltpu.make_async_copy(obuf.at[slot, :, b, :], dst_hbm.at[b, pl.ds(t0, tile_t), :], osems.at[slot])
                for b in range(BATCH)]

    def out_wait(slot):
        for cp in out_copies(yp_hbm, 0, slot):
            cp.wait()

    def prepare(half):
        rows = slice(half * tc, (half + 1) * tc)
        h = h_ref[rows, :]
        hb = h.astype(BF16)
        gate = _dot(hb, wsg_ref[...])
        up = _dot(hb, wsu_ref[...])
        shared = _dot((gate * jax.nn.sigmoid(gate) * up).astype(BF16), wsd_ref[...])

        eye = (lax.broadcasted_iota(I32, (tc, tc), 0) == lax.broadcasted_iota(I32, (tc, tc), 1)).astype(BF16)
        to_cols = lambda a: lax.dot_general(eye, a.astype(F32).astype(BF16), (((1,), (1,)), ((), ())),
                                            preferred_element_type=F32)
        slots = slot_ref[:, rows]
        w_bf = to_cols(w_ref[:, rows]).astype(BF16)
        s_block = to_cols(slots >> TILE_SHIFT)
        s_offset = to_cols(slots & (tc - 1))

        j = lax.broadcasted_iota(I32, (tc, tc), 1).astype(F32).astype(BF16)
        blocks = []
        for lb in range(LOCAL_SLOTS // tc):
            in_block = jnp.where(s_block == float(lb), s_offset, -1.0).astype(BF16)
            blk = jnp.zeros((tc, tc), BF16)
            for k in range(TOP_K):
                blk = blk + jnp.where(j == in_block[:, k:k + 1], w_bf[:, k:k + 1], jnp.zeros((), BF16))
            blocks.append(blk)
        return h, shared, jnp.concatenate(blocks, axis=-1)

    def finish(half, buf, h, shared, mix_w):
        gather_wait(buf)
        his, los = _load_token_tiles(ybuf.at[buf], LOCAL_SLOTS)
        pieces = his + los
        routed = jnp.concatenate(
            [_dot(mix_w, jnp.concatenate(pieces[p:p + 2], axis=-1).astype(BF16)) for p in range(0, len(pieces), 2)],
            axis=-1)
        out = _layer_norm(DEEPNORM_ALPHA * h + (routed + shared), g2_ref[...], b2_ref[...])
        obuf[half] = out.reshape(tile_t, BATCH, D_MODEL)

        g = 2 * i + half

        @pl.when(g < n_prompt_tiles)
        def _():
            for cp in out_copies(yp_hbm, pl.multiple_of(g * tile_t, tile_t), half):
                cp.start()

        @pl.when(g >= n_prompt_tiles)
        def _():
            for cp in out_copies(ysm_hbm, pl.multiple_of((g - n_prompt_tiles) * tile_t, tile_t), half):
                cp.start()

    cur = 2 * (i % 2)
    nxt = 2 - cur

    @pl.when(i == 0)
    def _():
        gather(0, 0)
        gather(1, 1)

    @pl.when(i + 1 < n_steps)
    def _():
        gather(2 * i + 2, nxt)
        gather(2 * i + 3, nxt + 1)

    @pl.when(i > 0)
    def _():
        out_wait(0)
        out_wait(1)

    prepared = [prepare(half) for half in range(2)]
    for half in range(2):
        finish(half, cur + half, *prepared[half])

    @pl.when(i + 1 == n_steps)
    def _():
        out_wait(0)
        out_wait(1)


def _combine(h1, gate_w, slots, ys, chunk_wrow, ws_gate, ws_up, ws_down, ln2_g, ln2_b, prompt_shape, sample_shape):
    t_rows = h1.shape[0]
    tc = MOE_TILE
    step = 2 * tc
    n_prompt_tiles = prompt_shape[0] * prompt_shape[1] // tc
    const2 = lambda i, *_: (0, 0)
    any_spec = pl.BlockSpec(memory_space=pl.ANY)
    return pl.pallas_call(
        functools.partial(_combine_kernel, n_prompt_tiles),
        grid_spec=pltpu.PrefetchScalarGridSpec(
            num_scalar_prefetch=1,
            grid=(t_rows // step,),
            in_specs=[pl.BlockSpec((step, D_MODEL), lambda i, *_: (i, 0)),
                      pl.BlockSpec((8, step), lambda i, *_: (0, i)),
                      pl.BlockSpec((8, step), lambda i, *_: (0, i)),
                      any_spec,
                      pl.BlockSpec((D_MODEL, D_EXPERT), const2),
                      pl.BlockSpec((D_MODEL, D_EXPERT), const2),
                      pl.BlockSpec((D_EXPERT, D_MODEL), const2),
                      pl.BlockSpec((1, D_MODEL), const2),
                      pl.BlockSpec((1, D_MODEL), const2)],
            out_specs=(any_spec, any_spec),
            scratch_shapes=[pltpu.VMEM((4, LOCAL_SLOTS * TOKEN_ROWS, LANES), U32),
                            pltpu.VMEM((2, tc // BATCH, BATCH, D_MODEL), F32),
                            pltpu.SemaphoreType.DMA((4,)), pltpu.SemaphoreType.DMA((2,))],
        ),
        out_shape=(jax.ShapeDtypeStruct(prompt_shape, F32), jax.ShapeDtypeStruct(sample_shape, F32)),
        compiler_params=pltpu.CompilerParams(dimension_semantics=("arbitrary",), vmem_limit_bytes=V7X_VMEM_LIMIT),
        name="combine",
    )(chunk_wrow, h1, gate_w, slots, ys, ws_gate.astype(BF16), ws_up.astype(BF16),
      ws_down.astype(BF16), ln2_g.reshape(1, D_MODEL), ln2_b.reshape(1, D_MODEL))


def _time_major(x):
    b, l, d = x.shape
    return jnp.transpose(x, (1, 0, 2)).reshape(l * b, d)


def _batch_major(x, b):
    return jnp.transpose(x.reshape(x.shape[0] // b, b, x.shape[1]), (1, 0, 2))


def _block_diag(blocks):
    n, r, c = blocks.shape
    eye = jnp.eye(n, dtype=blocks.dtype)
    return (blocks[:, :, None, :] * eye[:, None, :, None]).reshape(n * r, n * c)


def kernel(x_prompt, x_sample, state_ssm_re, state_ssm_im, cache_conv, ln_in_g, ln_in_b, w_in, lam_re, lam_im, log_dt, ssm_b_re, ssm_b_im, ssm_c_re, ssm_c_im, ssm_d, w_glu, b_glu, conv_w, beta_ssm, beta_conv, w_out, ln1_g, ln1_b, w_router, router_bias, w_gate, w_up, w_down, ws_gate, ws_up, ws_down, ln2_g, ln2_b):
    bp, lp, _ = x_prompt.shape
    bs, ls, _ = x_sample.shape
    assert bp == BATCH and bs == BATCH and ls == CHUNK_T and lp % CHUNK_T == 0
    assert w_in.shape[0] == 1, "single-layer model"
    n_prompt = bp * lp
    row = lambda a: a.reshape(1, -1)

    a_re, a_im, bb_re, bb_im = _prep(lam_re[0], lam_im[0], log_dt[0], ssm_b_re[0], ssm_b_im[0])
    groups_in = 128 // SSM_GROUP
    groups_out = 256 // SSM_GROUP
    bbd = lambda bb: jnp.stack([_block_diag(bb[j * groups_in:(j + 1) * groups_in])
                                for j in range(N_GROUPS // groups_in)]).astype(BF16)
    ct = lambda cc: jnp.transpose(cc, (0, 2, 1))
    cbd = lambda cc: jnp.stack([_block_diag(ct(cc)[j * groups_out:(j + 1) * groups_out])
                                for j in range(N_GROUPS // groups_out)]).astype(BF16)
    mix_weights = (row(ln_in_g), row(ln_in_b), w_in[0].astype(BF16), row(a_re), row(a_im),
                   bbd(bb_re), bbd(bb_im), cbd(ssm_c_re[0]), cbd(-ssm_c_im[0]), row(ssm_d[0]),
                   w_glu[0].astype(BF16), row(b_glu[0]), conv_w[0], row(beta_ssm[0]), row(beta_conv[0]),
                   w_out[0].astype(BF16), row(ln1_g[0]), row(ln1_b[0]))

    h0r = state_ssm_re[0].reshape(BATCH, D_STATE)
    h0i = state_ssm_im[0].reshape(BATCH, D_STATE)
    cbuf = _time_major(cache_conv[0])
    h1, p_re, p_im, p_conv, s_re, s_im, s_conv = _mix(x_prompt, x_sample, h0r, h0i, cbuf, mix_weights)

    t_rows = h1.shape[0]
    assert t_rows % (2 * MOE_TILE) == 0 and n_prompt % MOE_TILE == 0
    n_tiles = t_rows // MOE_TILE
    max_rows = t_rows * TOP_K + n_tiles * N_EXPERTS * (RUN_CHUNK - 1) + N_EXPERTS * EXPERT_BLOCK
    n_blocks = -(-max_rows // EXPERT_BLOCK) + 1
    n_blocks_pad = -(-n_blocks // 128) * 128
    gate_w, slots, chunk_e, chunk_rel, cnt = _route(h1, w_router[0], router_bias[0])
    block_e, n_used, starts, chunk_wrow = _pos(cnt, chunk_e[:, 0], chunk_rel[:, 0], n_blocks_pad,
                                               (n_blocks - 1) * EXPERT_BLOCK)
    n_used, chunk_wrow = n_used[0, :1], chunk_wrow.reshape(-1)
    xs = _scatter(h1, slots, starts[:, 0], cnt[:, 0].astype(I32), n_used, chunk_wrow, n_blocks * EXPERT_BLOCK)
    ys = _experts(xs, block_e[0, :n_blocks], n_used, w_gate[0], w_up[0], w_down[0])
    y_prompt, y_sample = _combine(h1, gate_w, slots, ys, chunk_wrow, ws_gate[0], ws_up[0], ws_down[0],
                                  ln2_g[0], ln2_b[0], x_prompt.shape, x_sample.shape)

    st = lambda s: s.reshape(1, BATCH, N_GROUPS, SSM_STATE)
    cv = lambda t: _batch_major(t, BATCH)[None]
    return (y_prompt, y_sample, st(p_re), st(p_im), cv(p_conv), st(s_re), st(s_im), cv(s_conv))
```

```python
import functools

import jax
import jax.numpy as jnp
from jax import lax
from jax.experimental import pallas as pl
from jax.experimental.pallas import tpu as pltpu

F32 = jnp.float32
BF16 = jnp.bfloat16
I32 = jnp.int32

D_MODEL = 1024
D_SSM = 512
D_CONV = 512
SSM_GROUP = 16
N_GROUPS = 32
SSM_STATE = 64
D_STATE = N_GROUPS * SSM_STATE
N_EXPERTS = 64
TOP_K = 6
N_EXPERT_GROUPS = 8
GROUP_SIZE = N_EXPERTS // N_EXPERT_GROUPS
TOPK_GROUPS = 4
D_EXPERT = 256
ROUTED_SCALE = 2.5
DEPTH = 1
DEEPNORM_ALPHA = (2.0 * DEPTH) ** 0.25
LN_EPS = 1e-5
RMS_EPS = 1e-6

BATCH = 8
CHUNK_T = 64
CHUNK_ROWS = CHUNK_T * BATCH
SCAN_COLS = 512
MOE_TILE = 256
TILE_SHIFT = MOE_TILE.bit_length() - 1
ROUTE_TILES_PER_STEP = 1
RUN_CHUNK = 16
LOCAL_SLOTS = TOP_K * MOE_TILE + N_EXPERTS * RUN_CHUNK
N_CHUNKS = LOCAL_SLOTS // RUN_CHUNK
EXPERT_BLOCK = 1024
PAD_PIECES = tuple(1 << b for b in range(EXPERT_BLOCK.bit_length() - 2, RUN_CHUNK.bit_length() - 2, -1))
LANES = 128
V7X_VMEM_LIMIT = 56 * 1024 * 1024
NEG_INF = float("-inf")


def _layer_norm(x, g, b):
    mu = jnp.mean(x, axis=-1, keepdims=True)
    xc = x - mu
    var = jnp.mean(xc * xc, axis=-1, keepdims=True)
    return xc * lax.rsqrt(var + LN_EPS) * g + b


def _rms_norm(x, g):
    return x * lax.rsqrt(jnp.mean(x * x, axis=-1, keepdims=True) + RMS_EPS) * g


def _dot(a, b):
    return jnp.dot(a, b, preferred_element_type=F32)


def _discretise(lr, li, log_dt):
    dt = jnp.exp(log_dt)
    mag = jnp.exp(lr * dt)
    ar = mag * jnp.cos(li * dt)
    ai = mag * jnp.sin(li * dt)
    den = lr * lr + li * li
    qr = ((ar - 1.0) * lr + ai * li) / den
    qi = (ai * lr - (ar - 1.0) * li) / den
    return ar, ai, qr, qi


def _prep_kernel(lr_ref, li_ref, ldt_ref, lrc_ref, lic_ref, ldtc_ref, br_ref, bi_ref,
                 ar_ref, ai_ref, bbr_ref, bbi_ref):
    ar, ai, _, _ = _discretise(lr_ref[...], li_ref[...], ldt_ref[...])
    ar_ref[...] = ar
    ai_ref[...] = ai
    _, _, qr, qi = _discretise(lrc_ref[...], lic_ref[...], ldtc_ref[...])
    br = br_ref[...]
    bi = bi_ref[...]
    bbr_ref[...] = qr * br - qi * bi
    bbi_ref[...] = qr * bi + qi * br


def _prep(lam_re, lam_im, log_dt, b_re, b_im):
    g, p = lam_re.shape
    per_channel = lambda a: jnp.repeat(a, SSM_GROUP, axis=0)
    rows = lambda b: jnp.transpose(b, (0, 2, 1)).reshape(g * SSM_GROUP, p)
    ldt = log_dt.reshape(g, 1)
    ar, ai, bbr, bbi = pl.pallas_call(
        _prep_kernel,
        out_shape=(jax.ShapeDtypeStruct((g, p), F32), jax.ShapeDtypeStruct((g, p), F32),
                   jax.ShapeDtypeStruct((g * SSM_GROUP, p), F32), jax.ShapeDtypeStruct((g * SSM_GROUP, p), F32)),
        name="prep",
    )(lam_re, lam_im, ldt, per_channel(lam_re), per_channel(lam_im), per_channel(ldt), rows(b_re), rows(b_im))
    return ar, ai, bbr.reshape(g, SSM_GROUP, p), bbi.reshape(g, SSM_GROUP, p)


def _chunk_copies(src_hbm, t0, xbuf, slot, sems):
    return [pltpu.make_async_copy(src_hbm.at[b, pl.ds(t0, CHUNK_T), :], xbuf.at[slot, :, b, :], sems.at[slot])
            for b in range(BATCH)]


def _mix_kernel(n_prompt_chunks,
                xp_hbm, xs_hbm, h0r_ref, h0i_ref, cbuf_ref, lng_ref, lnb_ref, win_ref, ar_ref, ai_ref,
                bbr_ref, bbi_ref, cbr_ref, cbi_ref, dsk_ref, wglu_ref, bglu_ref, cw_ref,
                bs_ref, bc_ref, wout_ref, g1_ref, b1_ref,
                h1_ref, pr_ref, pi_ref, pc_ref, sr_ref, si_ref, sc_ref,
                str_ref, sti_ref, hr_ref, hi_ref, cv_ref, xbuf, xsems):
    c = pl.program_id(0)
    n = CHUNK_ROWS
    carry_rows = 2 * BATCH
    slot = c % 2

    @pl.when(c == 0)
    def _():
        for cp in _chunk_copies(xp_hbm, 0, xbuf, 0, xsems):
            cp.start()

    @pl.when(c + 1 < n_prompt_chunks)
    def _():
        for cp in _chunk_copies(xp_hbm, pl.multiple_of((c + 1) * CHUNK_T, CHUNK_T), xbuf, 1 - slot, xsems):
            cp.start()

    @pl.when(c + 1 == n_prompt_chunks)
    def _():
        for cp in _chunk_copies(xs_hbm, 0, xbuf, 1 - slot, xsems):
            cp.start()

    @pl.when(c == 0)
    def _():
        hr_ref[...] = jnp.zeros_like(hr_ref)
        hi_ref[...] = jnp.zeros_like(hi_ref)
        cv_ref[0:carry_rows, :] = jnp.zeros((carry_rows, D_CONV), F32)

    @pl.when(c == n_prompt_chunks)
    def _():
        hr_ref[...] = h0r_ref[...]
        hi_ref[...] = h0i_ref[...]
        cv_ref[0:carry_rows, :] = cbuf_ref[...]

    for cp in _chunk_copies(xp_hbm, 0, xbuf, slot, xsems):
        cp.wait()
    h = _layer_norm(xbuf[slot].reshape(n, D_MODEL), lng_ref[...], lnb_ref[...])
    hb = h.astype(BF16)

    u = _dot(hb, win_ref[:, 0:D_SSM])
    ub = u.astype(BF16)
    u_tile = 128
    s_tile = u_tile // SSM_GROUP * SSM_STATE
    for j in range(D_SSM // u_tile):
        uj = ub[:, j * u_tile:(j + 1) * u_tile]
        str_ref[:, j * s_tile:(j + 1) * s_tile] = _dot(uj, bbr_ref[j])
        sti_ref[:, j * s_tile:(j + 1) * s_tile] = _dot(uj, bbi_ref[j])

    gate_b = _dot(hb, win_ref[:, D_SSM:D_SSM + D_CONV])
    gate_c = _dot(hb, win_ref[:, D_SSM + D_CONV:D_SSM + 2 * D_CONV])
    v = _dot(hb, win_ref[:, D_SSM + 2 * D_CONV:D_SSM + 3 * D_CONV])
    cv_ref[carry_rows:carry_rows + n, :] = gate_c * v

    for cb in range(D_STATE // SCAN_COLS):
        cols = slice(cb * SCAN_COLS, (cb + 1) * SCAN_COLS)
        ar = jnp.broadcast_to(ar_ref[:, cols], (BATCH, SCAN_COLS))
        ai = jnp.broadcast_to(ai_ref[:, cols], (BATCH, SCAN_COLS))

        def step(t, carry, cols=cols, ar=ar, ai=ai):
            sr, si = carry
            r0 = pl.multiple_of(t * BATCH, BATCH)
            nr = ar * sr - ai * si + str_ref[pl.ds(r0, BATCH), cols]
            ni = ar * si + ai * sr + sti_ref[pl.ds(r0, BATCH), cols]
            str_ref[pl.ds(r0, BATCH), cols] = nr
            sti_ref[pl.ds(r0, BATCH), cols] = ni
            return nr, ni

        sr, si = lax.fori_loop(0, CHUNK_T, step, (hr_ref[:, cols], hi_ref[:, cols]), unroll=True)
        hr_ref[:, cols] = sr
        hi_ref[:, cols] = si

    k_tile = 1024
    ys = []
    for j in range(D_STATE // k_tile):
        sl = slice(j * k_tile, (j + 1) * k_tile)
        ys.append(_dot(str_ref[:, sl].astype(BF16), cbr_ref[j]) + _dot(sti_ref[:, sl].astype(BF16), cbi_ref[j]))
    y_ssm = jnp.concatenate(ys, axis=-1) + dsk_ref[...] * u
    g = jax.nn.gelu(y_ssm)
    y_ssm = g * jax.nn.sigmoid(_dot(g.astype(BF16), wglu_ref[...]) + bglu_ref[...])
    mix = _dot(_rms_norm(y_ssm, bs_ref[...]).astype(BF16), wout_ref[0:D_SSM, :])

    y_conv = gate_b * (cv_ref[0:n, :] * cw_ref[0:1, :]
                       + cv_ref[BATCH:BATCH + n, :] * cw_ref[1:2, :]
                       + cv_ref[carry_rows:carry_rows + n, :] * cw_ref[2:3, :])
    tail = cv_ref[n:n + carry_rows, :]
    cv_ref[0:carry_rows, :] = tail
    mix = mix + _dot(_rms_norm(y_conv, bc_ref[...]).astype(BF16), wout_ref[D_SSM:D_SSM + D_CONV, :])

    h1_ref[...] = _layer_norm(DEEPNORM_ALPHA * h + mix, g1_ref[...], b1_ref[...])

    @pl.when(c == n_prompt_chunks - 1)
    def _():
        pr_ref[...] = hr_ref[...]
        pi_ref[...] = hi_ref[...]
        pc_ref[...] = tail

    @pl.when(c == n_prompt_chunks)
    def _():
        sr_ref[...] = hr_ref[...]
        si_ref[...] = hi_ref[...]
        sc_ref[...] = tail


def _mix(x_prompt, x_sample, h0r, h0i, cbuf, weights):
    n_prompt_chunks = x_prompt.shape[1] // CHUNK_T
    n_chunks = n_prompt_chunks + 1
    t_rows = n_chunks * CHUNK_ROWS
    any_spec = pl.BlockSpec(memory_space=pl.ANY)
    const2 = lambda c: (0, 0)
    const3 = lambda c: (0, 0, 0)
    w_specs = [pl.BlockSpec(w.shape, const3 if w.ndim == 3 else const2) for w in weights]
    state = jax.ShapeDtypeStruct((BATCH, D_STATE), F32)
    tail = jax.ShapeDtypeStruct((2 * BATCH, D_CONV), F32)
    state_spec = pl.BlockSpec((BATCH, D_STATE), const2)
    tail_spec = pl.BlockSpec((2 * BATCH, D_CONV), const2)
    return pl.pallas_call(
        functools.partial(_mix_kernel, n_prompt_chunks),
        grid=(n_chunks,),
        in_specs=[any_spec, any_spec, state_spec, state_spec, tail_spec] + w_specs,
        out_specs=(pl.BlockSpec((CHUNK_ROWS, D_MODEL), lambda c: (c, 0)),
                   state_spec, state_spec, tail_spec, state_spec, state_spec, tail_spec),
        out_shape=(jax.ShapeDtypeStruct((t_rows, D_MODEL), F32), state, state, tail, state, state, tail),
        scratch_shapes=[pltpu.VMEM((CHUNK_ROWS, D_STATE), F32), pltpu.VMEM((CHUNK_ROWS, D_STATE), F32),
                        pltpu.VMEM((BATCH, D_STATE), F32), pltpu.VMEM((BATCH, D_STATE), F32),
                        pltpu.VMEM((CHUNK_ROWS + 2 * BATCH, D_CONV), F32),
                        pltpu.VMEM((2, CHUNK_T, BATCH, D_MODEL), F32), pltpu.SemaphoreType.DMA((2,))],
        compiler_params=pltpu.CompilerParams(dimension_semantics=("arbitrary",), vmem_limit_bytes=V7X_VMEM_LIMIT),
        name="mix",
    )(x_prompt, x_sample, h0r, h0i, cbuf, *weights)


def _route_tile(h, w_hi, w_lo, bias, tri, ltri, run_before):
    tr = MOE_TILE

    nt_dot = lambda a, b: lax.dot_general(a, b, (((1,), (1,)), ((), ())), preferred_element_type=F32)
    h_hi = h.astype(BF16)
    h_lo = (h - h_hi.astype(F32)).astype(BF16)
    logits = nt_dot(w_hi, h_hi) + (nt_dot(w_hi, h_lo) + nt_dot(w_lo, h_hi))
    scores = jax.nn.sigmoid(logits)
    sel = scores + bias

    sub = lax.broadcasted_iota(I32, (GROUP_SIZE, tr), 0).astype(F32)
    blocks, gscore = [], []
    for g in range(N_EXPERT_GROUPS):
        blk = sel[g * GROUP_SIZE:(g + 1) * GROUP_SIZE, :]
        m1 = jnp.max(blk, axis=0, keepdims=True)
        first = jnp.min(jnp.where(blk == m1, sub, float(GROUP_SIZE)), axis=0, keepdims=True)
        m2 = jnp.max(jnp.where(sub == first, NEG_INF, blk), axis=0, keepdims=True)
        blocks.append(blk)
        gscore.append(m1 + m2)
    masked = []
    for g in range(N_EXPERT_GROUPS):
        beaten = jnp.zeros((1, tr), F32)
        for o in range(N_EXPERT_GROUPS):
            if o == g:
                continue
            wins = gscore[o] >= gscore[g] if o < g else gscore[o] > gscore[g]
            beaten = beaten + wins.astype(F32)
        masked.append(jnp.where(beaten < float(TOPK_GROUPS), blocks[g], NEG_INF))
    masked = jnp.concatenate(masked, axis=0)

    row = lax.broadcasted_iota(I32, (N_EXPERTS, tr), 0).astype(F32)
    picked = jnp.zeros((N_EXPERTS, tr), F32)
    hots, gates = [], []
    for _ in range(TOP_K):
        m = jnp.max(masked, axis=0, keepdims=True)
        idx = jnp.min(jnp.where(masked == m, row, float(N_EXPERTS)), axis=0, keepdims=True)
        hot = row == idx
        hots.append(hot)
        gates.append(jnp.sum(jnp.where(hot, scores, 0.0), axis=0, keepdims=True))
        masked = jnp.where(hot, NEG_INF, masked)
        picked = picked + hot.astype(F32)
    total = gates[0]
    for k in range(1, TOP_K):
        total = total + gates[k]

    chunk = float(RUN_CHUNK)
    pb = picked.astype(BF16)
    earlier = _dot(pb, tri)
    cnt_col = jnp.sum(picked, axis=1, keepdims=True)
    run_len = jnp.floor((cnt_col + (chunk - 1.0)) * (1.0 / chunk)) * chunk
    lower = _dot(ltri, jnp.broadcast_to(run_len, (N_EXPERTS, LANES)).astype(BF16))[:, 0:1]
    slot_of = earlier + lower
    w_rows = [gates[k] / total * ROUTED_SCALE for k in range(TOP_K)]
    slot_rows = [jnp.sum(jnp.where(hots[k], slot_of, 0.0), axis=0, keepdims=True).astype(I32) for k in range(TOP_K)]

    c_first = lax.broadcasted_iota(I32, (N_EXPERTS, N_CHUNKS), 1).astype(F32) * chunk
    owner = jnp.sum((lower + run_len <= c_first).astype(F32), axis=0, keepdims=True)
    hot_e = lax.broadcasted_iota(I32, (N_EXPERTS, N_CHUNKS), 0).astype(F32) == owner
    rel = jnp.sum(jnp.where(hot_e, run_before - lower + c_first, 0.0), axis=0, keepdims=True)
    return w_rows, slot_rows, owner.astype(I32), rel.astype(I32), run_len


def _route_kernel(h_ref, wrt_ref, bias_ref, tri_ref, ltri_ref, w_ref, slot_ref, ce_ref, crel_ref, cnt_ref, run_col):
    i = pl.program_id(0)
    tr = MOE_TILE

    @pl.when(i == 0)
    def _():
        run_col[...] = jnp.zeros_like(run_col)

    wrt = wrt_ref[...]
    w_hi = wrt.astype(BF16)
    w_lo = (wrt - w_hi.astype(F32)).astype(BF16)
    run = run_col[...]
    for half in range(ROUTE_TILES_PER_STEP):
        cols = slice(half * tr, (half + 1) * tr)
        w_rows, slot_rows, owner, rel, run_len = _route_tile(
            h_ref[cols, :], w_hi, w_lo, bias_ref[...], tri_ref[...], ltri_ref[...], run)
        for k in range(TOP_K):
            w_ref[k:k + 1, cols] = w_rows[k]
            slot_ref[k:k + 1, cols] = slot_rows[k]
        w_ref[TOP_K:, cols] = jnp.zeros((8 - TOP_K, tr), F32)
        slot_ref[TOP_K:, cols] = jnp.full((8 - TOP_K, tr), -1, I32)
        ce_ref[half] = jnp.broadcast_to(owner, (8, N_CHUNKS))
        crel_ref[half] = jnp.broadcast_to(rel, (8, N_CHUNKS))
        run = run + run_len
    run_col[...] = run
    cnt_ref[...] = jnp.broadcast_to(run, cnt_ref.shape)


def _route(h1, w_router, router_bias):
    t_rows = h1.shape[0]
    tr = MOE_TILE
    n_tiles = t_rows // tr
    step = ROUTE_TILES_PER_STEP * tr
    tri = jnp.triu(jnp.ones((tr, tr), BF16), k=1)
    ltri = jnp.tril(jnp.ones((N_EXPERTS, N_EXPERTS), BF16), k=-1)
    tok = lambda i: (0, i)
    const2 = lambda i: (0, 0)
    per_tile = lambda i: (i, 0, 0)
    return pl.pallas_call(
        _route_kernel,
        grid=(t_rows // step,),
        in_specs=[pl.BlockSpec((step, D_MODEL), lambda i: (i, 0)),
                  pl.BlockSpec((N_EXPERTS, D_MODEL), const2),
                  pl.BlockSpec((N_EXPERTS, 1), const2),
                  pl.BlockSpec((tr, tr), const2),
                  pl.BlockSpec((N_EXPERTS, N_EXPERTS), const2)],
        out_specs=(pl.BlockSpec((8, step), tok), pl.BlockSpec((8, step), tok),
                   pl.BlockSpec((ROUTE_TILES_PER_STEP, 8, N_CHUNKS), per_tile),
                   pl.BlockSpec((ROUTE_TILES_PER_STEP, 8, N_CHUNKS), per_tile),
                   pl.BlockSpec((N_EXPERTS, LANES), const2)),
        out_shape=(jax.ShapeDtypeStruct((8, t_rows), F32), jax.ShapeDtypeStruct((8, t_rows), I32),
                   jax.ShapeDtypeStruct((n_tiles, 8, N_CHUNKS), I32), jax.ShapeDtypeStruct((n_tiles, 8, N_CHUNKS), I32),
                   jax.ShapeDtypeStruct((N_EXPERTS, LANES), F32)),
        scratch_shapes=[pltpu.VMEM((N_EXPERTS, 1), F32)],
        compiler_params=pltpu.CompilerParams(dimension_semantics=("arbitrary",)),
        name="route",
    )(h1, w_router.T, router_bias.reshape(N_EXPERTS, 1), tri, ltri)


def _pos_kernel(n_blocks_pad, dummy_row, ccol_ref, crow_ref, ce_ref, crel_ref, be_ref, nb_ref, start_ref, row_ref):
    blk = float(EXPERT_BLOCK)
    pad = lambda cnt: jnp.floor((cnt + (blk - 1.0)) * (1.0 / blk)) * blk
    padded_row = pad(crow_ref[0:1, :])
    lane = lax.broadcasted_iota(I32, (N_EXPERTS, 128), 1)
    sub = lax.broadcasted_iota(I32, (N_EXPERTS, 128), 0)
    start = jnp.sum(jnp.where(lane < sub, padded_row, 0.0), axis=1, keepdims=True)
    end = start + pad(ccol_ref[:, 0:1])

    first_row = lax.broadcasted_iota(I32, (N_EXPERTS, n_blocks_pad), 1).astype(F32) * blk
    owner = jnp.sum((end <= first_row).astype(F32), axis=0, keepdims=True)
    be_ref[...] = jnp.broadcast_to(jnp.minimum(owner, float(N_EXPERTS - 1)).astype(I32), be_ref.shape)
    used = jnp.max(end, axis=0, keepdims=True) * (1.0 / blk)
    nb_ref[...] = jnp.broadcast_to(used.astype(I32), nb_ref.shape)
    start_ref[...] = jnp.broadcast_to(start.astype(I32), start_ref.shape)

    ce = ce_ref[...]
    tile = lax.broadcasted_iota(I32, ce.shape, 0)
    c = lax.broadcasted_iota(I32, ce.shape, 1)
    base = (dummy_row + ((tile & 1) * N_CHUNKS + c) * RUN_CHUNK).astype(F32)
    for e in range(N_EXPERTS):
        base = jnp.where(ce == e, start[e:e + 1, 0:1], base)
    row_ref[...] = base.astype(I32) + crel_ref[...]


def _pos(cnt, chunk_e, chunk_rel, n_blocks_pad, dummy_row):
    counts = cnt[:, 0]
    ccol = jnp.broadcast_to(counts[:, None], (N_EXPERTS, 128))
    crow = jnp.broadcast_to(jnp.pad(counts, (0, 128 - N_EXPERTS))[None, :], (8, 128))
    return pl.pallas_call(
        functools.partial(_pos_kernel, n_blocks_pad, dummy_row),
        out_shape=(jax.ShapeDtypeStruct((8, n_blocks_pad), I32), jax.ShapeDtypeStruct((8, 128), I32),
                   jax.ShapeDtypeStruct((N_EXPERTS, 128), I32), jax.ShapeDtypeStruct(chunk_e.shape, I32)),
        name="pos",
    )(ccol, crow, chunk_e, chunk_rel)


def _chunk_copy(src_ref, src_row, dst_ref, dst_row, sem):
    src = pl.ds(pl.multiple_of(src_row, RUN_CHUNK), RUN_CHUNK)
    dst = pl.ds(pl.multiple_of(dst_row, RUN_CHUNK), RUN_CHUNK)
    return pltpu.make_async_copy(src_ref.at[src, :], dst_ref.at[dst, :], sem)


def _for_each_chunk(tile, row_ref, fn):
    def per_chunk(c, _):
        fn(c * RUN_CHUNK, row_ref[tile * N_CHUNKS + c])
        return 0

    lax.fori_loop(0, N_CHUNKS, per_chunk, 0, unroll=8)


def _scatter_kernel(n_blocks, start_ref, count_ref, nb_ref, row_ref, h_ref, slot_ref, xs_out,
                    tiles, zeros, sems, zsem):
    i = pl.program_id(0)
    ts = MOE_TILE
    blk_rows = EXPERT_BLOCK

    @pl.when(i == 0)
    def _():
        zeros[...] = jnp.zeros_like(zeros)
        zero_block = lambda b: pltpu.make_async_copy(
            zeros, xs_out.at[pl.ds(pl.multiple_of(b * blk_rows, blk_rows), blk_rows), :], zsem)

        def for_each_pad_piece(fn):
            def per_expert(e, _):
                first = start_ref[e] + count_ref[e]
                n = ((count_ref[e] + (EXPERT_BLOCK - 1)) & (-EXPERT_BLOCK)) - count_ref[e]
                for piece in PAD_PIECES:
                    done = n & (-2 * piece)

                    @pl.when((n & piece) != 0)
                    def _():
                        dst = pl.ds(pl.multiple_of(first + done, RUN_CHUNK), piece)
                        fn(pltpu.make_async_copy(zeros.at[pl.ds(0, piece), :], xs_out.at[dst, :], zsem))

                return 0

            lax.fori_loop(0, N_EXPERTS, per_expert, 0)

        for_each_pad_piece(lambda cp: cp.start())

        def start_block(b, _):
            zero_block(b).start()
            return 0

        lax.fori_loop(nb_ref[0], n_blocks, start_block, 0)
        for_each_pad_piece(lambda cp: cp.wait())

        def wait_block(b, _):
            zero_block(0).wait()
            return 0

        lax.fori_loop(nb_ref[0], n_blocks, wait_block, 0)

    def drain(half):
        pltpu.make_async_copy(tiles.at[half], xs_out.at[pl.ds(0, LOCAL_SLOTS), :], sems.at[half]).wait()

    for half in range(2):
        @pl.when(i > 0)
        def _():
            drain(half)

        h = h_ref[half * ts:(half + 1) * ts, :].astype(BF16)
        slots = slot_ref[:, half * ts:(half + 1) * ts]
        sub = ts // 2
        for jb in range(LOCAL_SLOTS // ts):
            in_block = jnp.where(slots >> TILE_SHIFT == jb, slots & (ts - 1), -1).astype(F32).astype(BF16)
            parts = []
            for r in range(ts // sub):
                j = (lax.broadcasted_iota(I32, (sub, ts), 0) + r * sub).astype(F32).astype(BF16)
                perm = jnp.zeros((sub, ts), BF16)
                for k in range(TOP_K):
                    perm = perm + jnp.where(j == in_block[k:k + 1, :], jnp.ones((), BF16), jnp.zeros((), BF16))
                parts.append(perm)
            tiles[half, jb * ts:(jb + 1) * ts, :] = _dot(jnp.concatenate(parts, axis=0), h).astype(BF16)

        _for_each_chunk(
            2 * i + half, row_ref,
            lambda local, sorted_row, half=half: _chunk_copy(tiles.at[half], local, xs_out, sorted_row,
                                                             sems.at[half]).start())

    @pl.when(i == pl.num_programs(0) - 1)
    def _():
        drain(0)
        drain(1)


def _scatter(h1, slots, starts, counts, n_used, chunk_row, n_rows):
    t_rows = h1.shape[0]
    step = 2 * MOE_TILE
    n_blocks = n_rows // EXPERT_BLOCK
    return pl.pallas_call(
        functools.partial(_scatter_kernel, n_blocks),
        grid_spec=pltpu.PrefetchScalarGridSpec(
            num_scalar_prefetch=4,
            grid=(t_rows // step,),
            in_specs=[pl.BlockSpec((step, D_MODEL), lambda i, *_: (i, 0)),
                      pl.BlockSpec((8, step), lambda i, *_: (0, i))],
            out_specs=pl.BlockSpec(memory_space=pl.ANY),
            scratch_shapes=[pltpu.VMEM((2, LOCAL_SLOTS, D_MODEL), BF16),
                            pltpu.VMEM((EXPERT_BLOCK, D_MODEL), BF16),
                            pltpu.SemaphoreType.DMA((2,)), pltpu.SemaphoreType.DMA(())],
        ),
        out_shape=jax.ShapeDtypeStruct((n_rows, D_MODEL), BF16),
        compiler_params=pltpu.CompilerParams(dimension_semantics=("arbitrary",), vmem_limit_bytes=V7X_VMEM_LIMIT),
        name="scatter",
    )(starts, counts, n_used, chunk_row, h1, slots)


def _experts_kernel(be_ref, nb_ref, x_ref, wg_ref, wu_ref, wd_ref, y_ref, wg_bf, wu_bf, wd_bf):
    b = pl.program_id(0)

    @pl.when((b == 0) | (be_ref[b] != be_ref[jnp.maximum(b - 1, 0)]))
    def _():
        wg_bf[...] = wg_ref[...].astype(BF16)
        wu_bf[...] = wu_ref[...].astype(BF16)
        wd_bf[...] = wd_ref[...].astype(BF16)

    @pl.when(b < nb_ref[0])
    def _():
        xb = x_ref[...]
        gate = _dot(xb, wg_bf[...])
        up = _dot(xb, wu_bf[...])
        hid = (gate * jax.nn.sigmoid(gate) * up).astype(BF16)
        y_ref[...] = _dot(hid, wd_bf[...]).astype(BF16)

    @pl.when(b >= nb_ref[0])
    def _():
        y_ref[...] = jnp.zeros_like(y_ref)


def _experts(xs, block_e, n_used, w_gate, w_up, w_down):
    blk_rows = EXPERT_BLOCK
    return pl.pallas_call(
        _experts_kernel,
        grid_spec=pltpu.PrefetchScalarGridSpec(
            num_scalar_prefetch=2,
            grid=(xs.shape[0] // blk_rows,),
            in_specs=[pl.BlockSpec((blk_rows, D_MODEL), lambda b, be, nb: (jnp.minimum(b, jnp.maximum(nb[0], 1) - 1), 0)),
                      pl.BlockSpec((None, D_MODEL, D_EXPERT), lambda b, be, nb: (be[b], 0, 0)),
                      pl.BlockSpec((None, D_MODEL, D_EXPERT), lambda b, be, nb: (be[b], 0, 0)),
                      pl.BlockSpec((None, D_EXPERT, D_MODEL), lambda b, be, nb: (be[b], 0, 0))],
            out_specs=pl.BlockSpec((blk_rows, D_MODEL), lambda b, be, nb: (b, 0)),
            scratch_shapes=[pltpu.VMEM((D_MODEL, D_EXPERT), BF16), pltpu.VMEM((D_MODEL, D_EXPERT), BF16),
                            pltpu.VMEM((D_EXPERT, D_MODEL), BF16)],
        ),
        out_shape=jax.ShapeDtypeStruct(xs.shape, BF16),
        compiler_params=pltpu.CompilerParams(dimension_semantics=("arbitrary",)),
        name="experts",
    )(block_e, n_used, xs, w_gate, w_up, w_down)


def _combine_kernel(n_prompt_tiles, row_ref, h_ref, w_ref, slot_ref, ys_hbm,
                    wsg_ref, wsu_ref, wsd_ref, g2_ref, b2_ref, yp_hbm, ysm_hbm, ybuf, obuf, gsems, osems):
    i = pl.program_id(0)
    n_steps = pl.num_programs(0)
    tc = MOE_TILE
    tile_t = tc // BATCH

    def gather(tile, slot):
        _for_each_chunk(
            tile, row_ref,
            lambda local, sorted_row: _chunk_copy(ys_hbm, sorted_row, ybuf.at[slot], local, gsems.at[slot]).start())

    def gather_wait(slot):
        pltpu.make_async_copy(ys_hbm.at[pl.ds(0, LOCAL_SLOTS), :], ybuf.at[slot], gsems.at[slot]).wait()

    def out_copies(dst_hbm, t0, slot):
        return [pltpu.make_async_copy(obuf.at[slot, :, b, :], dst_hbm.at[b, pl.ds(t0, tile_t), :], osems.at[slot])
                for b in range(BATCH)]

    def out_wait(slot):
        for cp in out_copies(yp_hbm, 0, slot):
            cp.wait()

    def prepare(half):
        rows = slice(half * tc, (half + 1) * tc)
        h = h_ref[rows, :]
        hb = h.astype(BF16)
        gate = _dot(hb, wsg_ref[...])
        up = _dot(hb, wsu_ref[...])
        shared = _dot((gate * jax.nn.sigmoid(gate) * up).astype(BF16), wsd_ref[...])

        eye = (lax.broadcasted_iota(I32, (tc, tc), 0) == lax.broadcasted_iota(I32, (tc, tc), 1)).astype(BF16)
        to_cols = lambda a: lax.dot_general(eye, a.astype(F32).astype(BF16), (((1,), (1,)), ((), ())),
                                            preferred_element_type=F32)
        slots = slot_ref[:, rows]
        w_bf = to_cols(w_ref[:, rows]).astype(BF16)
        s_block = to_cols(slots >> TILE_SHIFT)
        s_offset = to_cols(slots & (tc - 1))

        j = lax.broadcasted_iota(I32, (tc, tc), 1).astype(F32).astype(BF16)
        blocks = []
        for lb in range(LOCAL_SLOTS // tc):
            in_block = jnp.where(s_block == float(lb), s_offset, -1.0).astype(BF16)
            blk = jnp.zeros((tc, tc), BF16)
            for k in range(TOP_K):
                blk = blk + jnp.where(j == in_block[:, k:k + 1], w_bf[:, k:k + 1], jnp.zeros((), BF16))
            blocks.append(blk)
        return h, shared, jnp.concatenate(blocks, axis=-1)

    def finish(half, buf, h, shared, mix_w):
        gather_wait(buf)
        routed = _dot(mix_w, ybuf[buf])
        out = _layer_norm(DEEPNORM_ALPHA * h + (routed + shared), g2_ref[...], b2_ref[...])
        obuf[half] = out.reshape(tile_t, BATCH, D_MODEL)

        g = 2 * i + half

        @pl.when(g < n_prompt_tiles)
        def _():
            for cp in out_copies(yp_hbm, pl.multiple_of(g * tile_t, tile_t), half):
                cp.start()

        @pl.when(g >= n_prompt_tiles)
        def _():
            for cp in out_copies(ysm_hbm, pl.multiple_of((g - n_prompt_tiles) * tile_t, tile_t), half):
                cp.start()

    cur = 2 * (i % 2)
    nxt = 2 - cur

    @pl.when(i == 0)
    def _():
        gather(0, 0)
        gather(1, 1)

    @pl.when(i + 1 < n_steps)
    def _():
        gather(2 * i + 2, nxt)
        gather(2 * i + 3, nxt + 1)

    @pl.when(i > 0)
    def _():
        out_wait(0)
        out_wait(1)

    prepared = [prepare(half) for half in range(2)]
    for half in range(2):
        finish(half, cur + half, *prepared[half])

    @pl.when(i + 1 == n_steps)
    def _():
        out_wait(0)
        out_wait(1)


def _combine(h1, gate_w, slots, ys, chunk_row, ws_gate, ws_up, ws_down, ln2_g, ln2_b, prompt_shape, sample_shape):
    t_rows = h1.shape[0]
    tc = MOE_TILE
    step = 2 * tc
    n_prompt_tiles = prompt_shape[0] * prompt_shape[1] // tc
    const2 = lambda i, *_: (0, 0)
    any_spec = pl.BlockSpec(memory_space=pl.ANY)
    return pl.pallas_call(
        functools.partial(_combine_kernel, n_prompt_tiles),
        grid_spec=pltpu.PrefetchScalarGridSpec(
            num_scalar_prefetch=1,
            grid=(t_rows // step,),
            in_specs=[pl.BlockSpec((step, D_MODEL), lambda i, *_: (i, 0)),
                      pl.BlockSpec((8, step), lambda i, *_: (0, i)),
                      pl.BlockSpec((8, step), lambda i, *_: (0, i)),
                      any_spec,
                      pl.BlockSpec((D_MODEL, D_EXPERT), const2),
                      pl.BlockSpec((D_MODEL, D_EXPERT), const2),
                      pl.BlockSpec((D_EXPERT, D_MODEL), const2),
                      pl.BlockSpec((1, D_MODEL), const2),
                      pl.BlockSpec((1, D_MODEL), const2)],
            out_specs=(any_spec, any_spec),
            scratch_shapes=[pltpu.VMEM((4, LOCAL_SLOTS, D_MODEL), BF16),
                            pltpu.VMEM((2, tc // BATCH, BATCH, D_MODEL), F32),
                            pltpu.SemaphoreType.DMA((4,)), pltpu.SemaphoreType.DMA((2,))],
        ),
        out_shape=(jax.ShapeDtypeStruct(prompt_shape, F32), jax.ShapeDtypeStruct(sample_shape, F32)),
        compiler_params=pltpu.CompilerParams(dimension_semantics=("arbitrary",), vmem_limit_bytes=V7X_VMEM_LIMIT),
        name="combine",
    )(chunk_row, h1, gate_w, slots, ys, ws_gate.astype(BF16), ws_up.astype(BF16),
      ws_down.astype(BF16), ln2_g.reshape(1, D_MODEL), ln2_b.reshape(1, D_MODEL))


def _time_major(x):
    b, l, d = x.shape
    return jnp.transpose(x, (1, 0, 2)).reshape(l * b, d)


def _batch_major(x, b):
    return jnp.transpose(x.reshape(x.shape[0] // b, b, x.shape[1]), (1, 0, 2))


def _block_diag(blocks):
    n, r, c = blocks.shape
    eye = jnp.eye(n, dtype=blocks.dtype)
    return (blocks[:, :, None, :] * eye[:, None, :, None]).reshape(n * r, n * c)


def kernel(x_prompt, x_sample, state_ssm_re, state_ssm_im, cache_conv, ln_in_g, ln_in_b, w_in, lam_re, lam_im, log_dt, ssm_b_re, ssm_b_im, ssm_c_re, ssm_c_im, ssm_d, w_glu, b_glu, conv_w, beta_ssm, beta_conv, w_out, ln1_g, ln1_b, w_router, router_bias, w_gate, w_up, w_down, ws_gate, ws_up, ws_down, ln2_g, ln2_b):
    bp, lp, _ = x_prompt.shape
    bs, ls, _ = x_sample.shape
    assert bp == BATCH and bs == BATCH and ls == CHUNK_T and lp % CHUNK_T == 0
    assert w_in.shape[0] == 1, "single-layer model"
    n_prompt = bp * lp
    row = lambda a: a.reshape(1, -1)

    a_re, a_im, bb_re, bb_im = _prep(lam_re[0], lam_im[0], log_dt[0], ssm_b_re[0], ssm_b_im[0])
    groups_in = 128 // SSM_GROUP
    groups_out = 256 // SSM_GROUP
    bbd = lambda bb: jnp.stack([_block_diag(bb[j * groups_in:(j + 1) * groups_in])
                                for j in range(N_GROUPS // groups_in)]).astype(BF16)
    ct = lambda cc: jnp.transpose(cc, (0, 2, 1))
    cbd = lambda cc: jnp.stack([_block_diag(ct(cc)[j * groups_out:(j + 1) * groups_out])
                                for j in range(N_GROUPS // groups_out)]).astype(BF16)
    mix_weights = (row(ln_in_g), row(ln_in_b), w_in[0].astype(BF16), row(a_re), row(a_im),
                   bbd(bb_re), bbd(bb_im), cbd(ssm_c_re[0]), cbd(-ssm_c_im[0]), row(ssm_d[0]),
                   w_glu[0].astype(BF16), row(b_glu[0]), conv_w[0], row(beta_ssm[0]), row(beta_conv[0]),
                   w_out[0].astype(BF16), row(ln1_g[0]), row(ln1_b[0]))

    h0r = state_ssm_re[0].reshape(BATCH, D_STATE)
    h0i = state_ssm_im[0].reshape(BATCH, D_STATE)
    cbuf = _time_major(cache_conv[0])
    h1, p_re, p_im, p_conv, s_re, s_im, s_conv = _mix(x_prompt, x_sample, h0r, h0i, cbuf, mix_weights)

    t_rows = h1.shape[0]
    assert t_rows % (2 * MOE_TILE) == 0 and n_prompt % MOE_TILE == 0
    n_tiles = t_rows // MOE_TILE
    max_rows = t_rows * TOP_K + n_tiles * N_EXPERTS * (RUN_CHUNK - 1) + N_EXPERTS * EXPERT_BLOCK
    dummy_blocks = -(-2 * LOCAL_SLOTS // EXPERT_BLOCK)
    n_blocks = -(-max_rows // EXPERT_BLOCK) + dummy_blocks
    n_blocks_pad = -(-n_blocks // 128) * 128
    gate_w, slots, chunk_e, chunk_rel, cnt = _route(h1, w_router[0], router_bias[0])
    block_e, n_used, starts, chunk_row = _pos(cnt, chunk_e[:, 0], chunk_rel[:, 0], n_blocks_pad,
                                               (n_blocks - dummy_blocks) * EXPERT_BLOCK)
    n_used, chunk_row = n_used[0, :1], chunk_row.reshape(-1)
    xs = _scatter(h1, slots, starts[:, 0], cnt[:, 0].astype(I32), n_used, chunk_row, n_blocks * EXPERT_BLOCK)
    ys = _experts(xs, block_e[0, :n_blocks], n_used, w_gate[0], w_up[0], w_down[0])
    y_prompt, y_sample = _combine(h1, gate_w, slots, ys, chunk_row, ws_gate[0], ws_up[0], ws_down[0],
                                  ln2_g[0], ln2_b[0], x_prompt.shape, x_sample.shape)

    st = lambda s: s.reshape(1, BATCH, N_GROUPS, SSM_STATE)
    cv = lambda t: _batch_major(t, BATCH)[None]
    return (y_prompt, y_sample, st(p_re), st(p_im), cv(p_conv), st(s_re), st(s_im), cv(s_conv))
```

```python
import functools

import jax
import jax.numpy as jnp
from jax import lax
from jax.experimental import pallas as pl
from jax.experimental.pallas import tpu as pltpu

F32 = jnp.float32
BF16 = jnp.bfloat16
I32 = jnp.int32

D_MODEL = 1024
D_SSM = 512
D_CONV = 512
SSM_GROUP = 16
N_GROUPS = 32
SSM_STATE = 64
D_STATE = N_GROUPS * SSM_STATE
N_EXPERTS = 64
TOP_K = 6
N_EXPERT_GROUPS = 8
GROUP_SIZE = N_EXPERTS // N_EXPERT_GROUPS
TOPK_GROUPS = 4
D_EXPERT = 256
ROUTED_SCALE = 2.5
DEPTH = 1
DEEPNORM_ALPHA = (2.0 * DEPTH) ** 0.25
LN_EPS = 1e-5
RMS_EPS = 1e-6

BATCH = 8
CHUNK_T = 64
CHUNK_ROWS = CHUNK_T * BATCH
SCAN_COLS = 512
MOE_TILE = 256
TILE_SHIFT = MOE_TILE.bit_length() - 1
RUN_CHUNK = 16
CHUNK_SHIFT = RUN_CHUNK.bit_length() - 1
LOCAL_SLOTS = TOP_K * MOE_TILE + N_EXPERTS * RUN_CHUNK
N_CHUNKS = LOCAL_SLOTS // RUN_CHUNK
EXPERT_BLOCK = 1024
PAD_PIECES = tuple(1 << b for b in range(EXPERT_BLOCK.bit_length() - 2, RUN_CHUNK.bit_length() - 2, -1))
LANES = 128
MXU_WIDTH = 256
V7X_VMEM_LIMIT = 56 * 1024 * 1024
NEG_INF = float("-inf")


def _layer_norm(x, g, b):
    mu = jnp.mean(x, axis=-1, keepdims=True)
    xc = x - mu
    var = jnp.mean(xc * xc, axis=-1, keepdims=True)
    return xc * lax.rsqrt(var + LN_EPS) * g + b


def _rms_norm(x, g):
    return x * lax.rsqrt(jnp.mean(x * x, axis=-1, keepdims=True) + RMS_EPS) * g


def _dot(a, b):
    return jnp.dot(a, b, preferred_element_type=F32)


def _discretise(lr, li, log_dt):
    dt = jnp.exp(log_dt)
    mag = jnp.exp(lr * dt)
    ar = mag * jnp.cos(li * dt)
    ai = mag * jnp.sin(li * dt)
    den = lr * lr + li * li
    qr = ((ar - 1.0) * lr + ai * li) / den
    qi = (ai * lr - (ar - 1.0) * li) / den
    return ar, ai, qr, qi


def _prep_kernel(lr_ref, li_ref, ldt_ref, lrc_ref, lic_ref, ldtc_ref, br_ref, bi_ref,
                 ar_ref, ai_ref, bbr_ref, bbi_ref):
    ar, ai, _, _ = _discretise(lr_ref[...], li_ref[...], ldt_ref[...])
    ar_ref[...] = ar
    ai_ref[...] = ai
    _, _, qr, qi = _discretise(lrc_ref[...], lic_ref[...], ldtc_ref[...])
    br = br_ref[...]
    bi = bi_ref[...]
    bbr_ref[...] = qr * br - qi * bi
    bbi_ref[...] = qr * bi + qi * br


def _prep(lam_re, lam_im, log_dt, b_re, b_im):
    g, p = lam_re.shape
    per_channel = lambda a: jnp.repeat(a, SSM_GROUP, axis=0)
    rows = lambda b: jnp.transpose(b, (0, 2, 1)).reshape(g * SSM_GROUP, p)
    ldt = log_dt.reshape(g, 1)
    ar, ai, bbr, bbi = pl.pallas_call(
        _prep_kernel,
        out_shape=(jax.ShapeDtypeStruct((g, p), F32), jax.ShapeDtypeStruct((g, p), F32),
                   jax.ShapeDtypeStruct((g * SSM_GROUP, p), F32), jax.ShapeDtypeStruct((g * SSM_GROUP, p), F32)),
        name="prep",
    )(lam_re, lam_im, ldt, per_channel(lam_re), per_channel(lam_im), per_channel(ldt), rows(b_re), rows(b_im))
    return ar, ai, bbr.reshape(g, SSM_GROUP, p), bbi.reshape(g, SSM_GROUP, p)


def _chunk_copies(src_hbm, t0, xbuf, slot, sems):
    return [pltpu.make_async_copy(src_hbm.at[b, pl.ds(t0, CHUNK_T), :], xbuf.at[slot, :, b, :], sems.at[slot])
            for b in range(BATCH)]


def _mix_kernel(n_prompt_chunks,
                xp_hbm, xs_hbm, h0r_ref, h0i_ref, cbuf_ref, lng_ref, lnb_ref, win_ref, ar_ref, ai_ref,
                bbr_ref, bbi_ref, cbr_ref, cbi_ref, dsk_ref, wglu_ref, bglu_ref, cw_ref,
                bs_ref, bc_ref, wout_ref, g1_ref, b1_ref,
                h1_ref, pr_ref, pi_ref, pc_ref, sr_ref, si_ref, sc_ref,
                str_ref, sti_ref, hr_ref, hi_ref, cv_ref, xbuf, xsems):
    c = pl.program_id(0)
    n = CHUNK_ROWS
    carry_rows = 2 * BATCH
    slot = c % 2

    @pl.when(c == 0)
    def _():
        for cp in _chunk_copies(xp_hbm, 0, xbuf, 0, xsems):
            cp.start()

    @pl.when(c + 1 < n_prompt_chunks)
    def _():
        for cp in _chunk_copies(xp_hbm, pl.multiple_of((c + 1) * CHUNK_T, CHUNK_T), xbuf, 1 - slot, xsems):
            cp.start()

    @pl.when(c + 1 == n_prompt_chunks)
    def _():
        for cp in _chunk_copies(xs_hbm, 0, xbuf, 1 - slot, xsems):
            cp.start()

    @pl.when(c == 0)
    def _():
        hr_ref[...] = jnp.zeros_like(hr_ref)
        hi_ref[...] = jnp.zeros_like(hi_ref)
        cv_ref[0:carry_rows, :] = jnp.zeros((carry_rows, D_CONV), F32)

    @pl.when(c == n_prompt_chunks)
    def _():
        hr_ref[...] = h0r_ref[...]
        hi_ref[...] = h0i_ref[...]
        cv_ref[0:carry_rows, :] = cbuf_ref[...]

    for cp in _chunk_copies(xp_hbm, 0, xbuf, slot, xsems):
        cp.wait()
    h = _layer_norm(xbuf[slot].reshape(n, D_MODEL), lng_ref[...], lnb_ref[...])
    hb = h.astype(BF16)

    u = _dot(hb, win_ref[:, 0:D_SSM])
    ub = u.astype(BF16)
    u_tile = LANES
    s_tile = u_tile // SSM_GROUP * SSM_STATE
    for j in range(D_SSM // u_tile):
        uj = ub[:, j * u_tile:(j + 1) * u_tile]
        str_ref[:, j * s_tile:(j + 1) * s_tile] = _dot(uj, bbr_ref[j])
        sti_ref[:, j * s_tile:(j + 1) * s_tile] = _dot(uj, bbi_ref[j])

    gate_b = _dot(hb, win_ref[:, D_SSM:D_SSM + D_CONV])
    gate_c = _dot(hb, win_ref[:, D_SSM + D_CONV:D_SSM + 2 * D_CONV])
    v = _dot(hb, win_ref[:, D_SSM + 2 * D_CONV:D_SSM + 3 * D_CONV])
    cv_ref[carry_rows:carry_rows + n, :] = gate_c * v

    for cb in range(D_STATE // SCAN_COLS):
        cols = slice(cb * SCAN_COLS, (cb + 1) * SCAN_COLS)
        ar = jnp.broadcast_to(ar_ref[:, cols], (BATCH, SCAN_COLS))
        ai = jnp.broadcast_to(ai_ref[:, cols], (BATCH, SCAN_COLS))

        def step(t, carry, cols=cols, ar=ar, ai=ai):
            sr, si = carry
            r0 = pl.multiple_of(t * BATCH, BATCH)
            nr = ar * sr - ai * si + str_ref[pl.ds(r0, BATCH), cols]
            ni = ar * si + ai * sr + sti_ref[pl.ds(r0, BATCH), cols]
            str_ref[pl.ds(r0, BATCH), cols] = nr
            sti_ref[pl.ds(r0, BATCH), cols] = ni
            return nr, ni

        sr, si = lax.fori_loop(0, CHUNK_T, step, (hr_ref[:, cols], hi_ref[:, cols]), unroll=True)
        hr_ref[:, cols] = sr
        hi_ref[:, cols] = si

    k_tile = MXU_WIDTH // SSM_GROUP * SSM_STATE
    ys = []
    for j in range(D_STATE // k_tile):
        sl = slice(j * k_tile, (j + 1) * k_tile)
        ys.append(_dot(str_ref[:, sl].astype(BF16), cbr_ref[j]) + _dot(sti_ref[:, sl].astype(BF16), cbi_ref[j]))
    y_ssm = jnp.concatenate(ys, axis=-1) + dsk_ref[...] * u
    g = jax.nn.gelu(y_ssm)
    y_ssm = g * jax.nn.sigmoid(_dot(g.astype(BF16), wglu_ref[...]) + bglu_ref[...])
    mix = _dot(_rms_norm(y_ssm, bs_ref[...]).astype(BF16), wout_ref[0:D_SSM, :])

    y_conv = gate_b * (cv_ref[0:n, :] * cw_ref[0:1, :]
                       + cv_ref[BATCH:BATCH + n, :] * cw_ref[1:2, :]
                       + cv_ref[carry_rows:carry_rows + n, :] * cw_ref[2:3, :])
    tail = cv_ref[n:n + carry_rows, :]
    cv_ref[0:carry_rows, :] = tail
    mix = mix + _dot(_rms_norm(y_conv, bc_ref[...]).astype(BF16), wout_ref[D_SSM:D_SSM + D_CONV, :])

    h1_ref[...] = _layer_norm(DEEPNORM_ALPHA * h + mix, g1_ref[...], b1_ref[...])

    @pl.when(c == n_prompt_chunks - 1)
    def _():
        pr_ref[...] = hr_ref[...]
        pi_ref[...] = hi_ref[...]
        pc_ref[...] = tail

    @pl.when(c == n_prompt_chunks)
    def _():
        sr_ref[...] = hr_ref[...]
        si_ref[...] = hi_ref[...]
        sc_ref[...] = tail


def _mix(x_prompt, x_sample, h0r, h0i, cbuf, weights):
    n_prompt_chunks = x_prompt.shape[1] // CHUNK_T
    n_chunks = n_prompt_chunks + 1
    t_rows = n_chunks * CHUNK_ROWS
    any_spec = pl.BlockSpec(memory_space=pl.ANY)
    const2 = lambda c: (0, 0)
    const3 = lambda c: (0, 0, 0)
    w_specs = [pl.BlockSpec(w.shape, const3 if w.ndim == 3 else const2) for w in weights]
    state = jax.ShapeDtypeStruct((BATCH, D_STATE), F32)
    tail = jax.ShapeDtypeStruct((2 * BATCH, D_CONV), F32)
    state_spec = pl.BlockSpec((BATCH, D_STATE), const2)
    tail_spec = pl.BlockSpec((2 * BATCH, D_CONV), const2)
    return pl.pallas_call(
        functools.partial(_mix_kernel, n_prompt_chunks),
        grid=(n_chunks,),
        in_specs=[any_spec, any_spec, state_spec, state_spec, tail_spec] + w_specs,
        out_specs=(pl.BlockSpec((CHUNK_ROWS, D_MODEL), lambda c: (c, 0)),
                   state_spec, state_spec, tail_spec, state_spec, state_spec, tail_spec),
        out_shape=(jax.ShapeDtypeStruct((t_rows, D_MODEL), F32), state, state, tail, state, state, tail),
        scratch_shapes=[pltpu.VMEM((CHUNK_ROWS, D_STATE), F32), pltpu.VMEM((CHUNK_ROWS, D_STATE), F32),
                        pltpu.VMEM((BATCH, D_STATE), F32), pltpu.VMEM((BATCH, D_STATE), F32),
                        pltpu.VMEM((CHUNK_ROWS + 2 * BATCH, D_CONV), F32),
                        pltpu.VMEM((2, CHUNK_T, BATCH, D_MODEL), F32), pltpu.SemaphoreType.DMA((2,))],
        compiler_params=pltpu.CompilerParams(dimension_semantics=("arbitrary",), vmem_limit_bytes=V7X_VMEM_LIMIT),
        name="mix",
    )(x_prompt, x_sample, h0r, h0i, cbuf, *weights)


def _route_tile(h, w_hi, w_lo, bias, tri, ltri, run_before):
    tr = MOE_TILE

    nt_dot = lambda a, b: lax.dot_general(a, b, (((1,), (1,)), ((), ())), preferred_element_type=F32)
    h_hi = h.astype(BF16)
    h_lo = (h - h_hi.astype(F32)).astype(BF16)
    logits = nt_dot(w_hi, h_hi) + (nt_dot(w_hi, h_lo) + nt_dot(w_lo, h_hi))
    scores = jax.nn.sigmoid(logits)
    sel = scores + bias

    sub = lax.broadcasted_iota(I32, (GROUP_SIZE, tr), 0).astype(F32)
    blocks, gscore = [], []
    for g in range(N_EXPERT_GROUPS):
        blk = sel[g * GROUP_SIZE:(g + 1) * GROUP_SIZE, :]
        m1 = jnp.max(blk, axis=0, keepdims=True)
        first = jnp.min(jnp.where(blk == m1, sub, float(GROUP_SIZE)), axis=0, keepdims=True)
        m2 = jnp.max(jnp.where(sub == first, NEG_INF, blk), axis=0, keepdims=True)
        blocks.append(blk)
        gscore.append(m1 + m2)
    masked = []
    for g in range(N_EXPERT_GROUPS):
        beaten = jnp.zeros((1, tr), F32)
        for o in range(N_EXPERT_GROUPS):
            if o == g:
                continue
            wins = gscore[o] >= gscore[g] if o < g else gscore[o] > gscore[g]
            beaten = beaten + wins.astype(F32)
        masked.append(jnp.where(beaten < float(TOPK_GROUPS), blocks[g], NEG_INF))
    masked = jnp.concatenate(masked, axis=0)

    row = lax.broadcasted_iota(I32, (N_EXPERTS, tr), 0).astype(F32)
    picked = jnp.zeros((N_EXPERTS, tr), F32)
    hots, gates = [], []
    for _ in range(TOP_K):
        m = jnp.max(masked, axis=0, keepdims=True)
        idx = jnp.min(jnp.where(masked == m, row, float(N_EXPERTS)), axis=0, keepdims=True)
        hot = row == idx
        hots.append(hot)
        gates.append(jnp.sum(jnp.where(hot, scores, 0.0), axis=0, keepdims=True))
        masked = jnp.where(hot, NEG_INF, masked)
        picked = picked + hot.astype(F32)
    total = gates[0]
    for k in range(1, TOP_K):
        total = total + gates[k]

    chunk = float(RUN_CHUNK)
    pb = picked.astype(BF16)
    earlier = _dot(pb, tri)
    cnt_col = jnp.sum(picked, axis=1, keepdims=True)
    run_len = jnp.floor((cnt_col + (chunk - 1.0)) * (1.0 / chunk)) * chunk
    lower = _dot(ltri, jnp.broadcast_to(run_len, (N_EXPERTS, LANES)).astype(BF16))[:, 0:1]
    slot_of = earlier + lower
    w_rows = [gates[k] / total * ROUTED_SCALE for k in range(TOP_K)]
    slot_rows = [jnp.sum(jnp.where(hots[k], slot_of, 0.0), axis=0, keepdims=True).astype(I32) for k in range(TOP_K)]

    c_first = lax.broadcasted_iota(I32, (N_EXPERTS, N_CHUNKS), 1).astype(F32) * chunk
    owner = jnp.sum((lower + run_len <= c_first).astype(F32), axis=0, keepdims=True)
    hot_e = lax.broadcasted_iota(I32, (N_EXPERTS, N_CHUNKS), 0).astype(F32) == owner
    rel = jnp.sum(jnp.where(hot_e, run_before - lower + c_first, 0.0), axis=0, keepdims=True)
    return w_rows, slot_rows, owner.astype(I32), rel.astype(I32), run_len


def _route_kernel(h_ref, wrt_ref, bias_ref, tri_ref, ltri_ref, w_ref, slot_ref, ce_ref, crel_ref, cnt_ref, run_col):
    i = pl.program_id(0)
    tr = MOE_TILE

    @pl.when(i == 0)
    def _():
        run_col[...] = jnp.zeros_like(run_col)

    wrt = wrt_ref[...]
    w_hi = wrt.astype(BF16)
    w_lo = (wrt - w_hi.astype(F32)).astype(BF16)
    run = run_col[...]
    w_rows, slot_rows, owner, rel, run_len = _route_tile(
        h_ref[...], w_hi, w_lo, bias_ref[...], tri_ref[...], ltri_ref[...], run)
    for k in range(TOP_K):
        w_ref[k:k + 1, :] = w_rows[k]
        slot_ref[k:k + 1, :] = slot_rows[k]
    w_ref[TOP_K:, :] = jnp.zeros((8 - TOP_K, tr), F32)
    slot_ref[TOP_K:, :] = jnp.full((8 - TOP_K, tr), -1, I32)
    ce_ref[0] = jnp.broadcast_to(owner, (8, N_CHUNKS))
    crel_ref[0] = jnp.broadcast_to(rel, (8, N_CHUNKS))
    run_col[...] = run + run_len
    cnt_ref[...] = jnp.broadcast_to(run + run_len, cnt_ref.shape)


def _route(h1, w_router, router_bias):
    t_rows = h1.shape[0]
    tr = MOE_TILE
    n_tiles = t_rows // tr
    tri = jnp.triu(jnp.ones((tr, tr), BF16), k=1)
    ltri = jnp.tril(jnp.ones((N_EXPERTS, N_EXPERTS), BF16), k=-1)
    tok = lambda i: (0, i)
    const2 = lambda i: (0, 0)
    per_tile = lambda i: (i, 0, 0)
    return pl.pallas_call(
        _route_kernel,
        grid=(n_tiles,),
        in_specs=[pl.BlockSpec((tr, D_MODEL), lambda i: (i, 0)),
                  pl.BlockSpec((N_EXPERTS, D_MODEL), const2),
                  pl.BlockSpec((N_EXPERTS, 1), const2),
                  pl.BlockSpec((tr, tr), const2),
                  pl.BlockSpec((N_EXPERTS, N_EXPERTS), const2)],
        out_specs=(pl.BlockSpec((8, tr), tok), pl.BlockSpec((8, tr), tok),
                   pl.BlockSpec((1, 8, N_CHUNKS), per_tile), pl.BlockSpec((1, 8, N_CHUNKS), per_tile),
                   pl.BlockSpec((N_EXPERTS, LANES), const2)),
        out_shape=(jax.ShapeDtypeStruct((8, t_rows), F32), jax.ShapeDtypeStruct((8, t_rows), I32),
                   jax.ShapeDtypeStruct((n_tiles, 8, N_CHUNKS), I32), jax.ShapeDtypeStruct((n_tiles, 8, N_CHUNKS), I32),
                   jax.ShapeDtypeStruct((N_EXPERTS, LANES), F32)),
        scratch_shapes=[pltpu.VMEM((N_EXPERTS, 1), F32)],
        compiler_params=pltpu.CompilerParams(dimension_semantics=("arbitrary",)),
        name="route",
    )(h1, w_router.T, router_bias.reshape(N_EXPERTS, 1), tri, ltri)


def _pos_kernel(n_blocks_pad, dummy_row, ccol_ref, crow_ref, ce_ref, crel_ref, be_ref, nb_ref, start_ref, row_ref):
    blk = float(EXPERT_BLOCK)
    pad = lambda cnt: jnp.floor((cnt + (blk - 1.0)) * (1.0 / blk)) * blk
    padded_row = pad(crow_ref[0:1, :])
    lane = lax.broadcasted_iota(I32, (N_EXPERTS, LANES), 1)
    sub = lax.broadcasted_iota(I32, (N_EXPERTS, LANES), 0)
    start = jnp.sum(jnp.where(lane < sub, padded_row, 0.0), axis=1, keepdims=True)
    end = start + pad(ccol_ref[:, 0:1])

    first_row = lax.broadcasted_iota(I32, (N_EXPERTS, n_blocks_pad), 1).astype(F32) * blk
    owner = jnp.sum((end <= first_row).astype(F32), axis=0, keepdims=True)
    be_ref[...] = jnp.broadcast_to(jnp.minimum(owner, float(N_EXPERTS - 1)).astype(I32), be_ref.shape)
    used = jnp.max(end, axis=0, keepdims=True) * (1.0 / blk)
    nb_ref[...] = jnp.broadcast_to(used.astype(I32), nb_ref.shape)
    start_ref[...] = jnp.broadcast_to(start.astype(I32), start_ref.shape)

    ce = ce_ref[...]
    tile = lax.broadcasted_iota(I32, ce.shape, 0)
    c = lax.broadcasted_iota(I32, ce.shape, 1)
    base = (dummy_row + ((tile & 1) * N_CHUNKS + c) * RUN_CHUNK).astype(F32)
    for e in range(N_EXPERTS):
        base = jnp.where(ce == e, start[e:e + 1, 0:1], base)
    row_ref[...] = (base.astype(I32) + crel_ref[...]) >> CHUNK_SHIFT


def _pos(cnt, chunk_e, chunk_rel, n_blocks_pad, dummy_row):
    counts = cnt[:, 0]
    ccol = jnp.broadcast_to(counts[:, None], (N_EXPERTS, LANES))
    crow = jnp.broadcast_to(jnp.pad(counts, (0, LANES - N_EXPERTS))[None, :], (8, LANES))
    return pl.pallas_call(
        functools.partial(_pos_kernel, n_blocks_pad, dummy_row),
        out_shape=(jax.ShapeDtypeStruct((8, n_blocks_pad), I32), jax.ShapeDtypeStruct((8, LANES), I32),
                   jax.ShapeDtypeStruct((N_EXPERTS, LANES), I32), jax.ShapeDtypeStruct(chunk_e.shape, I32)),
        name="pos",
    )(ccol, crow, chunk_e, chunk_rel)


def _chunk_copy(src_ref, src_chunk, dst_ref, dst_chunk, sem):
    return pltpu.make_async_copy(src_ref.at[src_chunk], dst_ref.at[dst_chunk], sem)


def _for_each_chunk(tile, place_ref, fn):
    def per_chunk(c, _):
        fn(c, place_ref[tile * N_CHUNKS + c])
        return 0

    lax.fori_loop(0, N_CHUNKS, per_chunk, 0, unroll=8)


def _chunked(x):
    return x.reshape(x.shape[0] // RUN_CHUNK, RUN_CHUNK, D_MODEL)


def _scatter_kernel(n_blocks, start_ref, count_ref, nb_ref, row_ref, h_ref, slot_ref, xs_out,
                    tiles, zeros, sems, zsem):
    i = pl.program_id(0)
    ts = MOE_TILE
    blk_chunks = EXPERT_BLOCK // RUN_CHUNK

    @pl.when(i == 0)
    def _():
        zeros[...] = jnp.zeros_like(zeros)
        zero_block = lambda b: pltpu.make_async_copy(zeros, xs_out.at[pl.ds(b * blk_chunks, blk_chunks)], zsem)

        def for_each_pad_piece(fn):
            def per_expert(e, _):
                first = start_ref[e] + count_ref[e]
                n = ((count_ref[e] + (EXPERT_BLOCK - 1)) & (-EXPERT_BLOCK)) - count_ref[e]
                for piece in PAD_PIECES:
                    done = n & (-2 * piece)

                    @pl.when((n & piece) != 0)
                    def _():
                        dst = pl.ds((first + done) >> CHUNK_SHIFT, piece // RUN_CHUNK)
                        fn(pltpu.make_async_copy(zeros.at[pl.ds(0, piece // RUN_CHUNK)], xs_out.at[dst], zsem))

                return 0

            lax.fori_loop(0, N_EXPERTS, per_expert, 0)

        for_each_pad_piece(lambda cp: cp.start())

        def start_block(b, _):
            zero_block(b).start()
            return 0

        lax.fori_loop(nb_ref[0], n_blocks, start_block, 0)
        for_each_pad_piece(lambda cp: cp.wait())

        def wait_block(b, _):
            zero_block(0).wait()
            return 0

        lax.fori_loop(nb_ref[0], n_blocks, wait_block, 0)

    def drain(half):
        pltpu.make_async_copy(tiles.at[half], xs_out.at[pl.ds(0, N_CHUNKS)], sems.at[half]).wait()

    for half in range(2):
        @pl.when(i > 0)
        def _():
            drain(half)

        h = h_ref[half * ts:(half + 1) * ts, :].astype(BF16)
        slots = slot_ref[:, half * ts:(half + 1) * ts]
        sub = ts // 2
        for jb in range(LOCAL_SLOTS // ts):
            in_block = jnp.where(slots >> TILE_SHIFT == jb, slots & (ts - 1), -1).astype(F32).astype(BF16)
            parts = []
            for r in range(ts // sub):
                j = (lax.broadcasted_iota(I32, (sub, ts), 0) + r * sub).astype(F32).astype(BF16)
                perm = jnp.zeros((sub, ts), BF16)
                for k in range(TOP_K):
                    perm = perm + jnp.where(j == in_block[k:k + 1, :], jnp.ones((), BF16), jnp.zeros((), BF16))
                parts.append(perm)
            tiles[half, jb * (ts // RUN_CHUNK):(jb + 1) * (ts // RUN_CHUNK)] = _chunked(
                _dot(jnp.concatenate(parts, axis=0), h).astype(BF16))

        _for_each_chunk(
            2 * i + half, row_ref,
            lambda local, sorted_row, half=half: _chunk_copy(tiles.at[half], local, xs_out, sorted_row,
                                                             sems.at[half]).start())

    @pl.when(i == pl.num_programs(0) - 1)
    def _():
        drain(0)
        drain(1)


def _scatter(h1, slots, starts, counts, n_used, chunk_row, n_rows):
    t_rows = h1.shape[0]
    step = 2 * MOE_TILE
    n_blocks = n_rows // EXPERT_BLOCK
    return pl.pallas_call(
        functools.partial(_scatter_kernel, n_blocks),
        grid_spec=pltpu.PrefetchScalarGridSpec(
            num_scalar_prefetch=4,
            grid=(t_rows // step,),
            in_specs=[pl.BlockSpec((step, D_MODEL), lambda i, *_: (i, 0)),
                      pl.BlockSpec((8, step), lambda i, *_: (0, i))],
            out_specs=pl.BlockSpec(memory_space=pl.ANY),
            scratch_shapes=[pltpu.VMEM((2, N_CHUNKS, RUN_CHUNK, D_MODEL), BF16),
                            pltpu.VMEM((EXPERT_BLOCK // RUN_CHUNK, RUN_CHUNK, D_MODEL), BF16),
                            pltpu.SemaphoreType.DMA((2,)), pltpu.SemaphoreType.DMA(())],
        ),
        out_shape=jax.ShapeDtypeStruct((n_rows // RUN_CHUNK, RUN_CHUNK, D_MODEL), BF16),
        compiler_params=pltpu.CompilerParams(dimension_semantics=("arbitrary",), vmem_limit_bytes=V7X_VMEM_LIMIT),
        name="scatter",
    )(starts, counts, n_used, chunk_row, h1, slots)


def _experts_kernel(be_ref, nb_ref, x_ref, wg_ref, wu_ref, wd_ref, y_ref, wg_bf, wu_bf, wd_bf):
    b = pl.program_id(0)

    @pl.when((b == 0) | (be_ref[b] != be_ref[jnp.maximum(b - 1, 0)]))
    def _():
        wg_bf[...] = wg_ref[...].astype(BF16)
        wu_bf[...] = wu_ref[...].astype(BF16)
        wd_bf[...] = wd_ref[...].astype(BF16)

    @pl.when(b < nb_ref[0])
    def _():
        xb = x_ref[...].reshape(EXPERT_BLOCK, D_MODEL)
        gate = _dot(xb, wg_bf[...])
        up = _dot(xb, wu_bf[...])
        hid = (gate * jax.nn.sigmoid(gate) * up).astype(BF16)
        y_ref[...] = _chunked(_dot(hid, wd_bf[...]).astype(BF16))

    @pl.when(b >= nb_ref[0])
    def _():
        y_ref[...] = jnp.zeros_like(y_ref)


def _experts(xs, block_e, n_used, w_gate, w_up, w_down):
    blk = (EXPERT_BLOCK // RUN_CHUNK, RUN_CHUNK, D_MODEL)
    return pl.pallas_call(
        _experts_kernel,
        grid_spec=pltpu.PrefetchScalarGridSpec(
            num_scalar_prefetch=2,
            grid=(xs.shape[0] // blk[0],),
            in_specs=[pl.BlockSpec(blk, lambda b, be, nb: (jnp.minimum(b, jnp.maximum(nb[0], 1) - 1), 0, 0)),
                      pl.BlockSpec((None, D_MODEL, D_EXPERT), lambda b, be, nb: (be[b], 0, 0)),
                      pl.BlockSpec((None, D_MODEL, D_EXPERT), lambda b, be, nb: (be[b], 0, 0)),
                      pl.BlockSpec((None, D_EXPERT, D_MODEL), lambda b, be, nb: (be[b], 0, 0))],
            out_specs=pl.BlockSpec(blk, lambda b, be, nb: (b, 0, 0)),
            scratch_shapes=[pltpu.VMEM((D_MODEL, D_EXPERT), BF16), pltpu.VMEM((D_MODEL, D_EXPERT), BF16),
                            pltpu.VMEM((D_EXPERT, D_MODEL), BF16)],
        ),
        out_shape=jax.ShapeDtypeStruct(xs.shape, BF16),
        compiler_params=pltpu.CompilerParams(dimension_semantics=("arbitrary",)),
        name="experts",
    )(block_e, n_used, xs, w_gate, w_up, w_down)


def _combine_kernel(n_prompt_tiles, row_ref, h_ref, w_ref, slot_ref, ys_hbm,
                    wsg_ref, wsu_ref, wsd_ref, g2_ref, b2_ref, yp_hbm, ysm_hbm, ybuf, obuf, gsems, osems):
    i = pl.program_id(0)
    n_steps = pl.num_programs(0)
    tc = MOE_TILE
    tile_t = tc // BATCH

    def gather(tile, slot):
        _for_each_chunk(
            tile, row_ref,
            lambda local, sorted_row: _chunk_copy(ys_hbm, sorted_row, ybuf.at[slot], local, gsems.at[slot]).start())

    def gather_wait(slot):
        pltpu.make_async_copy(ys_hbm.at[pl.ds(0, N_CHUNKS)], ybuf.at[slot], gsems.at[slot]).wait()

    def out_copies(dst_hbm, t0, slot):
        return [pltpu.make_async_copy(obuf.at[slot, :, b, :], dst_hbm.at[b, pl.ds(t0, tile_t), :], osems.at[slot])
                for b in range(BATCH)]

    def out_wait(slot):
        for cp in out_copies(yp_hbm, 0, slot):
            cp.wait()

    def prepare(half):
        rows = slice(half * tc, (half + 1) * tc)
        h = h_ref[rows, :]
        hb = h.astype(BF16)
        gate = _dot(hb, wsg_ref[...])
        up = _dot(hb, wsu_ref[...])
        shared = _dot((gate * jax.nn.sigmoid(gate) * up).astype(BF16), wsd_ref[...])

        eye = (lax.broadcasted_iota(I32, (tc, tc), 0) == lax.broadcasted_iota(I32, (tc, tc), 1)).astype(BF16)
        to_cols = lambda a: lax.dot_general(eye, a.astype(F32).astype(BF16), (((1,), (1,)), ((), ())),
                                            preferred_element_type=F32)
        slots = slot_ref[:, rows]
        w_bf = to_cols(w_ref[:, rows]).astype(BF16)
        s_block = to_cols(slots >> TILE_SHIFT)
        s_offset = to_cols(slots & (tc - 1))

        j = lax.broadcasted_iota(I32, (tc, tc), 1).astype(F32).astype(BF16)
        blocks = []
        for lb in range(LOCAL_SLOTS // tc):
            in_block = jnp.where(s_block == float(lb), s_offset, -1.0).astype(BF16)
            blk = jnp.zeros((tc, tc), BF16)
            for k in range(TOP_K):
                blk = blk + jnp.where(j == in_block[:, k:k + 1], w_bf[:, k:k + 1], jnp.zeros((), BF16))
            blocks.append(blk)
        return h, shared, jnp.concatenate(blocks, axis=-1)

    def finish(half, buf, h, shared, mix_w):
        gather_wait(buf)
        routed = _dot(mix_w, ybuf[buf].reshape(LOCAL_SLOTS, D_MODEL))
        out = _layer_norm(DEEPNORM_ALPHA * h + (routed + shared), g2_ref[...], b2_ref[...])
        obuf[half] = out.reshape(tile_t, BATCH, D_MODEL)

        g = 2 * i + half

        @pl.when(g < n_prompt_tiles)
        def _():
            for cp in out_copies(yp_hbm, pl.multiple_of(g * tile_t, tile_t), half):
                cp.start()

        @pl.when(g >= n_prompt_tiles)
        def _():
            for cp in out_copies(ysm_hbm, pl.multiple_of((g - n_prompt_tiles) * tile_t, tile_t), half):
                cp.start()

    cur = 2 * (i % 2)
    nxt = 2 - cur

    @pl.when(i == 0)
    def _():
        gather(0, 0)
        gather(1, 1)

    @pl.when(i + 1 < n_steps)
    def _():
        gather(2 * i + 2, nxt)
        gather(2 * i + 3, nxt + 1)

    @pl.when(i > 0)
    def _():
        out_wait(0)
        out_wait(1)

    prepared = [prepare(half) for half in range(2)]
    for half in range(2):
        finish(half, cur + half, *prepared[half])

    @pl.when(i + 1 == n_steps)
    def _():
        out_wait(0)
        out_wait(1)


def _combine(h1, gate_w, slots, ys, chunk_row, ws_gate, ws_up, ws_down, ln2_g, ln2_b, prompt_shape, sample_shape):
    t_rows = h1.shape[0]
    tc = MOE_TILE
    step = 2 * tc
    n_prompt_tiles = prompt_shape[0] * prompt_shape[1] // tc
    const2 = lambda i, *_: (0, 0)
    any_spec = pl.BlockSpec(memory_space=pl.ANY)
    return pl.pallas_call(
        functools.partial(_combine_kernel, n_prompt_tiles),
        grid_spec=pltpu.PrefetchScalarGridSpec(
            num_scalar_prefetch=1,
            grid=(t_rows // step,),
            in_specs=[pl.BlockSpec((step, D_MODEL), lambda i, *_: (i, 0)),
                      pl.BlockSpec((8, step), lambda i, *_: (0, i)),
                      pl.BlockSpec((8, step), lambda i, *_: (0, i)),
                      any_spec,
                      pl.BlockSpec((D_MODEL, D_EXPERT), const2),
                      pl.BlockSpec((D_MODEL, D_EXPERT), const2),
                      pl.BlockSpec((D_EXPERT, D_MODEL), const2),
                      pl.BlockSpec((1, D_MODEL), const2),
                      pl.BlockSpec((1, D_MODEL), const2)],
            out_specs=(any_spec, any_spec),
            scratch_shapes=[pltpu.VMEM((4, N_CHUNKS, RUN_CHUNK, D_MODEL), BF16),
                            pltpu.VMEM((2, tc // BATCH, BATCH, D_MODEL), F32),
                            pltpu.SemaphoreType.DMA((4,)), pltpu.SemaphoreType.DMA((2,))],
        ),
        out_shape=(jax.ShapeDtypeStruct(prompt_shape, F32), jax.ShapeDtypeStruct(sample_shape, F32)),
        compiler_params=pltpu.CompilerParams(dimension_semantics=("arbitrary",), vmem_limit_bytes=V7X_VMEM_LIMIT),
        name="combine",
    )(chunk_row, h1, gate_w, slots, ys, ws_gate.astype(BF16), ws_up.astype(BF16),
      ws_down.astype(BF16), ln2_g.reshape(1, D_MODEL), ln2_b.reshape(1, D_MODEL))


def _time_major(x):
    b, l, d = x.shape
    return jnp.transpose(x, (1, 0, 2)).reshape(l * b, d)


def _batch_major(x, b):
    return jnp.transpose(x.reshape(x.shape[0] // b, b, x.shape[1]), (1, 0, 2))


def _block_diag(blocks):
    n, r, c = blocks.shape
    eye = jnp.eye(n, dtype=blocks.dtype)
    return (blocks[:, :, None, :] * eye[:, None, :, None]).reshape(n * r, n * c)


def kernel(x_prompt, x_sample, state_ssm_re, state_ssm_im, cache_conv, ln_in_g, ln_in_b, w_in, lam_re, lam_im, log_dt, ssm_b_re, ssm_b_im, ssm_c_re, ssm_c_im, ssm_d, w_glu, b_glu, conv_w, beta_ssm, beta_conv, w_out, ln1_g, ln1_b, w_router, router_bias, w_gate, w_up, w_down, ws_gate, ws_up, ws_down, ln2_g, ln2_b):
    bp, lp, _ = x_prompt.shape
    bs, ls, _ = x_sample.shape
    assert bp == BATCH and bs == BATCH and ls == CHUNK_T and lp % CHUNK_T == 0
    assert w_in.shape[0] == 1, "single-layer model"
    n_prompt = bp * lp
    row = lambda a: a.reshape(1, -1)

    a_re, a_im, bb_re, bb_im = _prep(lam_re[0], lam_im[0], log_dt[0], ssm_b_re[0], ssm_b_im[0])
    groups_in = LANES // SSM_GROUP
    groups_out = MXU_WIDTH // SSM_GROUP
    bbd = lambda bb: jnp.stack([_block_diag(bb[j * groups_in:(j + 1) * groups_in])
                                for j in range(N_GROUPS // groups_in)]).astype(BF16)
    ct = lambda cc: jnp.transpose(cc, (0, 2, 1))
    cbd = lambda cc: jnp.stack([_block_diag(ct(cc)[j * groups_out:(j + 1) * groups_out])
                                for j in range(N_GROUPS // groups_out)]).astype(BF16)
    mix_weights = (row(ln_in_g), row(ln_in_b), w_in[0].astype(BF16), row(a_re), row(a_im),
                   bbd(bb_re), bbd(bb_im), cbd(ssm_c_re[0]), cbd(-ssm_c_im[0]), row(ssm_d[0]),
                   w_glu[0].astype(BF16), row(b_glu[0]), conv_w[0], row(beta_ssm[0]), row(beta_conv[0]),
                   w_out[0].astype(BF16), row(ln1_g[0]), row(ln1_b[0]))

    h0r = state_ssm_re[0].reshape(BATCH, D_STATE)
    h0i = state_ssm_im[0].reshape(BATCH, D_STATE)
    cbuf = _time_major(cache_conv[0])
    h1, p_re, p_im, p_conv, s_re, s_im, s_conv = _mix(x_prompt, x_sample, h0r, h0i, cbuf, mix_weights)

    t_rows = h1.shape[0]
    assert t_rows % (2 * MOE_TILE) == 0 and n_prompt % MOE_TILE == 0
    n_tiles = t_rows // MOE_TILE
    max_rows = t_rows * TOP_K + n_tiles * N_EXPERTS * (RUN_CHUNK - 1) + N_EXPERTS * EXPERT_BLOCK
    dummy_blocks = -(-2 * LOCAL_SLOTS // EXPERT_BLOCK)
    n_blocks = -(-max_rows // EXPERT_BLOCK) + dummy_blocks
    n_blocks_pad = -(-n_blocks // LANES) * LANES
    gate_w, slots, chunk_e, chunk_rel, cnt = _route(h1, w_router[0], router_bias[0])
    block_e, n_used, starts, chunk_row = _pos(cnt, chunk_e[:, 0], chunk_rel[:, 0], n_blocks_pad,
                                               (n_blocks - dummy_blocks) * EXPERT_BLOCK)
    n_used, chunk_row = n_used[0, :1], chunk_row.reshape(-1)
    xs = _scatter(h1, slots, starts[:, 0], cnt[:, 0].astype(I32), n_used, chunk_row, n_blocks * EXPERT_BLOCK)
    ys = _experts(xs, block_e[0, :n_blocks], n_used, w_gate[0], w_up[0], w_down[0])
    y_prompt, y_sample = _combine(h1, gate_w, slots, ys, chunk_row, ws_gate[0], ws_up[0], ws_down[0],
                                  ln2_g[0], ln2_b[0], x_prompt.shape, x_sample.shape)

    st = lambda s: s.reshape(1, BATCH, N_GROUPS, SSM_STATE)
    cv = lambda t: _batch_major(t, BATCH)[None]
    return (y_prompt, y_sample, st(p_re), st(p_im), cv(p_conv), st(s_re), st(s_im), cv(s_conv))
```

```python
import functools

import jax
import jax.numpy as jnp
from jax import lax
from jax.experimental import pallas as pl
from jax.experimental.pallas import tpu as pltpu

F32 = jnp.float32
BF16 = jnp.bfloat16
I32 = jnp.int32

D_MODEL = 1024
D_SSM = 512
D_CONV = 512
SSM_GROUP = 16
N_GROUPS = 32
SSM_STATE = 64
D_STATE = N_GROUPS * SSM_STATE
N_EXPERTS = 64
TOP_K = 6
N_EXPERT_GROUPS = 8
GROUP_SIZE = N_EXPERTS // N_EXPERT_GROUPS
TOPK_GROUPS = 4
D_EXPERT = 256
ROUTED_SCALE = 2.5
DEPTH = 1
DEEPNORM_ALPHA = (2.0 * DEPTH) ** 0.25
LN_EPS = 1e-5
RMS_EPS = 1e-6

BATCH = 8
CHUNK_T = 64
CHUNK_ROWS = CHUNK_T * BATCH
SCAN_COLS = 512
MOE_TILE = 256
TILE_SHIFT = MOE_TILE.bit_length() - 1
RUN_CHUNK = 16
CHUNK_SHIFT = RUN_CHUNK.bit_length() - 1
LOCAL_SLOTS = TOP_K * MOE_TILE + N_EXPERTS * RUN_CHUNK
N_CHUNKS = LOCAL_SLOTS // RUN_CHUNK
EXPERT_BLOCK = 1024
PAD_PIECES = tuple(1 << b for b in range(EXPERT_BLOCK.bit_length() - 2, RUN_CHUNK.bit_length() - 2, -1))
LANES = 128
MXU_WIDTH = 256
V7X_VMEM_LIMIT = 56 * 1024 * 1024
NEG_INF = float("-inf")


def _layer_norm(x, g, b):
    mu = jnp.mean(x, axis=-1, keepdims=True)
    xc = x - mu
    var = jnp.mean(xc * xc, axis=-1, keepdims=True)
    return xc * lax.rsqrt(var + LN_EPS) * g + b


def _rms_norm(x, g):
    return x * lax.rsqrt(jnp.mean(x * x, axis=-1, keepdims=True) + RMS_EPS) * g


def _dot(a, b):
    return jnp.dot(a, b, preferred_element_type=F32)


def _discretise(lr, li, log_dt):
    dt = jnp.exp(log_dt)
    mag = jnp.exp(lr * dt)
    ar = mag * jnp.cos(li * dt)
    ai = mag * jnp.sin(li * dt)
    den = lr * lr + li * li
    qr = ((ar - 1.0) * lr + ai * li) / den
    qi = (ai * lr - (ar - 1.0) * li) / den
    return ar, ai, qr, qi


def _prep_kernel(lr_ref, li_ref, ldt_ref, lrc_ref, lic_ref, ldtc_ref, br_ref, bi_ref,
                 ar_ref, ai_ref, bbr_ref, bbi_ref):
    ar, ai, _, _ = _discretise(lr_ref[...], li_ref[...], ldt_ref[...])
    ar_ref[...] = ar
    ai_ref[...] = ai
    _, _, qr, qi = _discretise(lrc_ref[...], lic_ref[...], ldtc_ref[...])
    br = br_ref[...]
    bi = bi_ref[...]
    bbr_ref[...] = qr * br - qi * bi
    bbi_ref[...] = qr * bi + qi * br


def _prep(lam_re, lam_im, log_dt, b_re, b_im):
    g, p = lam_re.shape
    per_channel = lambda a: jnp.repeat(a, SSM_GROUP, axis=0)
    rows = lambda b: jnp.transpose(b, (0, 2, 1)).reshape(g * SSM_GROUP, p)
    ldt = log_dt.reshape(g, 1)
    ar, ai, bbr, bbi = pl.pallas_call(
        _prep_kernel,
        out_shape=(jax.ShapeDtypeStruct((g, p), F32), jax.ShapeDtypeStruct((g, p), F32),
                   jax.ShapeDtypeStruct((g * SSM_GROUP, p), F32), jax.ShapeDtypeStruct((g * SSM_GROUP, p), F32)),
        name="prep",
    )(lam_re, lam_im, ldt, per_channel(lam_re), per_channel(lam_im), per_channel(ldt), rows(b_re), rows(b_im))
    return ar, ai, bbr.reshape(g, SSM_GROUP, p), bbi.reshape(g, SSM_GROUP, p)


def _chunk_copies(src_hbm, t0, xbuf, slot, sems):
    return [pltpu.make_async_copy(src_hbm.at[b, pl.ds(t0, CHUNK_T), :], xbuf.at[slot, :, b, :], sems.at[slot])
            for b in range(BATCH)]


def _mix_kernel(n_prompt_chunks,
                xp_hbm, xs_hbm, h0r_ref, h0i_ref, cbuf_ref, lng_ref, lnb_ref, win_ref, ar_ref, ai_ref,
                bbr_ref, bbi_ref, cbr_ref, cbi_ref, dsk_ref, wglu_ref, bglu_ref, cw_ref,
                bs_ref, bc_ref, wout_ref, g1_ref, b1_ref,
                h1_ref, pr_ref, pi_ref, pc_ref, sr_ref, si_ref, sc_ref,
                str_ref, sti_ref, hr_ref, hi_ref, cv_ref, xbuf, xsems):
    c = pl.program_id(0)
    n = CHUNK_ROWS
    carry_rows = 2 * BATCH
    slot = c % 2

    @pl.when(c == 0)
    def _():
        for cp in _chunk_copies(xp_hbm, 0, xbuf, 0, xsems):
            cp.start()

    @pl.when(c + 1 < n_prompt_chunks)
    def _():
        for cp in _chunk_copies(xp_hbm, pl.multiple_of((c + 1) * CHUNK_T, CHUNK_T), xbuf, 1 - slot, xsems):
            cp.start()

    @pl.when(c + 1 == n_prompt_chunks)
    def _():
        for cp in _chunk_copies(xs_hbm, 0, xbuf, 1 - slot, xsems):
            cp.start()

    @pl.when(c == 0)
    def _():
        hr_ref[...] = jnp.zeros_like(hr_ref)
        hi_ref[...] = jnp.zeros_like(hi_ref)
        cv_ref[0:carry_rows, :] = jnp.zeros((carry_rows, D_CONV), F32)

    @pl.when(c == n_prompt_chunks)
    def _():
        hr_ref[...] = h0r_ref[...]
        hi_ref[...] = h0i_ref[...]
        cv_ref[0:carry_rows, :] = cbuf_ref[...]

    for cp in _chunk_copies(xp_hbm, 0, xbuf, slot, xsems):
        cp.wait()
    h = _layer_norm(xbuf[slot].reshape(n, D_MODEL), lng_ref[...], lnb_ref[...])
    hb = h.astype(BF16)

    u = _dot(hb, win_ref[:, 0:D_SSM])
    ub = u.astype(BF16)
    u_tile = LANES
    s_tile = u_tile // SSM_GROUP * SSM_STATE
    for j in range(D_SSM // u_tile):
        uj = ub[:, j * u_tile:(j + 1) * u_tile]
        str_ref[:, j * s_tile:(j + 1) * s_tile] = _dot(uj, bbr_ref[j])
        sti_ref[:, j * s_tile:(j + 1) * s_tile] = _dot(uj, bbi_ref[j])

    gate_b = _dot(hb, win_ref[:, D_SSM:D_SSM + D_CONV])
    gate_c = _dot(hb, win_ref[:, D_SSM + D_CONV:D_SSM + 2 * D_CONV])
    v = _dot(hb, win_ref[:, D_SSM + 2 * D_CONV:D_SSM + 3 * D_CONV])
    cv_ref[carry_rows:carry_rows + n, :] = gate_c * v

    for cb in range(D_STATE // SCAN_COLS):
        cols = slice(cb * SCAN_COLS, (cb + 1) * SCAN_COLS)
        ar = jnp.broadcast_to(ar_ref[:, cols], (BATCH, SCAN_COLS))
        ai = jnp.broadcast_to(ai_ref[:, cols], (BATCH, SCAN_COLS))

        def step(t, carry, cols=cols, ar=ar, ai=ai):
            sr, si = carry
            r0 = pl.multiple_of(t * BATCH, BATCH)
            nr = ar * sr - ai * si + str_ref[pl.ds(r0, BATCH), cols]
            ni = ar * si + ai * sr + sti_ref[pl.ds(r0, BATCH), cols]
            str_ref[pl.ds(r0, BATCH), cols] = nr
            sti_ref[pl.ds(r0, BATCH), cols] = ni
            return nr, ni

        sr, si = lax.fori_loop(0, CHUNK_T, step, (hr_ref[:, cols], hi_ref[:, cols]), unroll=True)
        hr_ref[:, cols] = sr
        hi_ref[:, cols] = si

    k_tile = MXU_WIDTH // SSM_GROUP * SSM_STATE
    ys = []
    for j in range(D_STATE // k_tile):
        sl = slice(j * k_tile, (j + 1) * k_tile)
        ys.append(_dot(str_ref[:, sl].astype(BF16), cbr_ref[j]) + _dot(sti_ref[:, sl].astype(BF16), cbi_ref[j]))
    y_ssm = jnp.concatenate(ys, axis=-1) + dsk_ref[...] * u
    g = jax.nn.gelu(y_ssm)
    y_ssm = g * jax.nn.sigmoid(_dot(g.astype(BF16), wglu_ref[...]) + bglu_ref[...])
    mix = _dot(_rms_norm(y_ssm, bs_ref[...]).astype(BF16), wout_ref[0:D_SSM, :])

    y_conv = gate_b * (cv_ref[0:n, :] * cw_ref[0:1, :]
                       + cv_ref[BATCH:BATCH + n, :] * cw_ref[1:2, :]
                       + cv_ref[carry_rows:carry_rows + n, :] * cw_ref[2:3, :])
    tail = cv_ref[n:n + carry_rows, :]
    cv_ref[0:carry_rows, :] = tail
    mix = mix + _dot(_rms_norm(y_conv, bc_ref[...]).astype(BF16), wout_ref[D_SSM:D_SSM + D_CONV, :])

    h1_ref[...] = _layer_norm(DEEPNORM_ALPHA * h + mix, g1_ref[...], b1_ref[...])

    @pl.when(c == n_prompt_chunks - 1)
    def _():
        pr_ref[...] = hr_ref[...]
        pi_ref[...] = hi_ref[...]
        pc_ref[...] = tail

    @pl.when(c == n_prompt_chunks)
    def _():
        sr_ref[...] = hr_ref[...]
        si_ref[...] = hi_ref[...]
        sc_ref[...] = tail


def _mix(x_prompt, x_sample, h0r, h0i, cbuf, weights):
    n_prompt_chunks = x_prompt.shape[1] // CHUNK_T
    n_chunks = n_prompt_chunks + 1
    t_rows = n_chunks * CHUNK_ROWS
    any_spec = pl.BlockSpec(memory_space=pl.ANY)
    const2 = lambda c: (0, 0)
    const3 = lambda c: (0, 0, 0)
    w_specs = [pl.BlockSpec(w.shape, const3 if w.ndim == 3 else const2) for w in weights]
    state = jax.ShapeDtypeStruct((BATCH, D_STATE), F32)
    tail = jax.ShapeDtypeStruct((2 * BATCH, D_CONV), F32)
    state_spec = pl.BlockSpec((BATCH, D_STATE), const2)
    tail_spec = pl.BlockSpec((2 * BATCH, D_CONV), const2)
    return pl.pallas_call(
        functools.partial(_mix_kernel, n_prompt_chunks),
        grid=(n_chunks,),
        in_specs=[any_spec, any_spec, state_spec, state_spec, tail_spec] + w_specs,
        out_specs=(pl.BlockSpec((CHUNK_ROWS, D_MODEL), lambda c: (c, 0)),
                   state_spec, state_spec, tail_spec, state_spec, state_spec, tail_spec),
        out_shape=(jax.ShapeDtypeStruct((t_rows, D_MODEL), F32), state, state, tail, state, state, tail),
        scratch_shapes=[pltpu.VMEM((CHUNK_ROWS, D_STATE), F32), pltpu.VMEM((CHUNK_ROWS, D_STATE), F32),
                        pltpu.VMEM((BATCH, D_STATE), F32), pltpu.VMEM((BATCH, D_STATE), F32),
                        pltpu.VMEM((CHUNK_ROWS + 2 * BATCH, D_CONV), F32),
                        pltpu.VMEM((2, CHUNK_T, BATCH, D_MODEL), F32), pltpu.SemaphoreType.DMA((2,))],
        compiler_params=pltpu.CompilerParams(dimension_semantics=("arbitrary",), vmem_limit_bytes=V7X_VMEM_LIMIT),
        name="mix",
    )(x_prompt, x_sample, h0r, h0i, cbuf, *weights)


def _route_tile(h, w_hi, w_lo, bias, tri, ltri, run_before):
    tr = MOE_TILE

    nt_dot = lambda a, b: lax.dot_general(a, b, (((1,), (1,)), ((), ())), preferred_element_type=F32)
    h_hi = h.astype(BF16)
    h_lo = (h - h_hi.astype(F32)).astype(BF16)
    logits = nt_dot(w_hi, h_hi) + (nt_dot(w_hi, h_lo) + nt_dot(w_lo, h_hi))
    scores = jax.nn.sigmoid(logits)
    sel = scores + bias

    sub = lax.broadcasted_iota(I32, (GROUP_SIZE, tr), 0).astype(F32)
    blocks, gscore = [], []
    for g in range(N_EXPERT_GROUPS):
        blk = sel[g * GROUP_SIZE:(g + 1) * GROUP_SIZE, :]
        m1 = jnp.max(blk, axis=0, keepdims=True)
        first = jnp.min(jnp.where(blk == m1, sub, float(GROUP_SIZE)), axis=0, keepdims=True)
        m2 = jnp.max(jnp.where(sub == first, NEG_INF, blk), axis=0, keepdims=True)
        blocks.append(blk)
        gscore.append(m1 + m2)
    masked = []
    for g in range(N_EXPERT_GROUPS):
        beaten = jnp.zeros((1, tr), F32)
        for o in range(N_EXPERT_GROUPS):
            if o == g:
                continue
            wins = gscore[o] >= gscore[g] if o < g else gscore[o] > gscore[g]
            beaten = beaten + wins.astype(F32)
        masked.append(jnp.where(beaten < float(TOPK_GROUPS), blocks[g], NEG_INF))
    masked = jnp.concatenate(masked, axis=0)

    row = lax.broadcasted_iota(I32, (N_EXPERTS, tr), 0).astype(F32)
    picked = jnp.zeros((N_EXPERTS, tr), F32)
    hots, gates = [], []
    for _ in range(TOP_K):
        m = jnp.max(masked, axis=0, keepdims=True)
        idx = jnp.min(jnp.where(masked == m, row, float(N_EXPERTS)), axis=0, keepdims=True)
        hot = row == idx
        hots.append(hot)
        gates.append(jnp.sum(jnp.where(hot, scores, 0.0), axis=0, keepdims=True))
        masked = jnp.where(hot, NEG_INF, masked)
        picked = picked + hot.astype(F32)
    total = gates[0]
    for k in range(1, TOP_K):
        total = total + gates[k]

    chunk = float(RUN_CHUNK)
    pb = picked.astype(BF16)
    earlier = _dot(pb, tri)
    cnt_col = jnp.sum(picked, axis=1, keepdims=True)
    run_len = jnp.floor((cnt_col + (chunk - 1.0)) * (1.0 / chunk)) * chunk
    lower = _dot(ltri, jnp.broadcast_to(run_len, (N_EXPERTS, LANES)).astype(BF16))[:, 0:1]
    slot_of = earlier + lower
    w_rows = [gates[k] / total * ROUTED_SCALE for k in range(TOP_K)]
    slot_rows = [jnp.sum(jnp.where(hots[k], slot_of, 0.0), axis=0, keepdims=True).astype(I32) for k in range(TOP_K)]

    c_first = lax.broadcasted_iota(I32, (N_EXPERTS, N_CHUNKS), 1).astype(F32) * chunk
    owner = jnp.sum((lower + run_len <= c_first).astype(F32), axis=0, keepdims=True)
    hot_e = lax.broadcasted_iota(I32, (N_EXPERTS, N_CHUNKS), 0).astype(F32) == owner
    rel = jnp.sum(jnp.where(hot_e, run_before - lower + c_first, 0.0), axis=0, keepdims=True)
    return w_rows, slot_rows, owner.astype(I32), rel.astype(I32), run_len


def _route_kernel(h_ref, wrt_ref, bias_ref, tri_ref, ltri_ref, w_ref, slot_ref, ce_ref, crel_ref, cnt_ref, run_col):
    i = pl.program_id(0)
    tr = MOE_TILE

    @pl.when(i == 0)
    def _():
        run_col[...] = jnp.zeros_like(run_col)

    wrt = wrt_ref[...]
    w_hi = wrt.astype(BF16)
    w_lo = (wrt - w_hi.astype(F32)).astype(BF16)
    run = run_col[...]
    w_rows, slot_rows, owner, rel, run_len = _route_tile(
        h_ref[...], w_hi, w_lo, bias_ref[...], tri_ref[...], ltri_ref[...], run)
    for k in range(TOP_K):
        w_ref[k:k + 1, :] = w_rows[k]
        slot_ref[k:k + 1, :] = slot_rows[k]
    w_ref[TOP_K:, :] = jnp.zeros((8 - TOP_K, tr), F32)
    slot_ref[TOP_K:, :] = jnp.full((8 - TOP_K, tr), -1, I32)
    ce_ref[0] = jnp.broadcast_to(owner, (8, N_CHUNKS))
    crel_ref[0] = jnp.broadcast_to(rel, (8, N_CHUNKS))
    run_col[...] = run + run_len
    cnt_ref[...] = jnp.broadcast_to(run + run_len, cnt_ref.shape)


def _route(h1, w_router, router_bias):
    t_rows = h1.shape[0]
    tr = MOE_TILE
    n_tiles = t_rows // tr
    tri = jnp.triu(jnp.ones((tr, tr), BF16), k=1)
    ltri = jnp.tril(jnp.ones((N_EXPERTS, N_EXPERTS), BF16), k=-1)
    tok = lambda i: (0, i)
    const2 = lambda i: (0, 0)
    per_tile = lambda i: (i, 0, 0)
    return pl.pallas_call(
        _route_kernel,
        grid=(n_tiles,),
        in_specs=[pl.BlockSpec((tr, D_MODEL), lambda i: (i, 0)),
                  pl.BlockSpec((N_EXPERTS, D_MODEL), const2),
                  pl.BlockSpec((N_EXPERTS, 1), const2),
                  pl.BlockSpec((tr, tr), const2),
                  pl.BlockSpec((N_EXPERTS, N_EXPERTS), const2)],
        out_specs=(pl.BlockSpec((8, tr), tok), pl.BlockSpec((8, tr), tok),
                   pl.BlockSpec((1, 8, N_CHUNKS), per_tile), pl.BlockSpec((1, 8, N_CHUNKS), per_tile),
                   pl.BlockSpec((N_EXPERTS, LANES), const2)),
        out_shape=(jax.ShapeDtypeStruct((8, t_rows), F32), jax.ShapeDtypeStruct((8, t_rows), I32),
                   jax.ShapeDtypeStruct((n_tiles, 8, N_CHUNKS), I32), jax.ShapeDtypeStruct((n_tiles, 8, N_CHUNKS), I32),
                   jax.ShapeDtypeStruct((N_EXPERTS, LANES), F32)),
        scratch_shapes=[pltpu.VMEM((N_EXPERTS, 1), F32)],
        compiler_params=pltpu.CompilerParams(dimension_semantics=("arbitrary",)),
        name="route",
    )(h1, w_router.T, router_bias.reshape(N_EXPERTS, 1), tri, ltri)


def _pos_kernel(n_blocks_pad, dummy_row, ccol_ref, crow_ref, ce_ref, crel_ref, be_ref, nb_ref, start_ref, row_ref):
    blk = float(EXPERT_BLOCK)
    pad = lambda cnt: jnp.floor((cnt + (blk - 1.0)) * (1.0 / blk)) * blk
    padded_row = pad(crow_ref[0:1, :])
    lane = lax.broadcasted_iota(I32, (N_EXPERTS, LANES), 1)
    sub = lax.broadcasted_iota(I32, (N_EXPERTS, LANES), 0)
    start = jnp.sum(jnp.where(lane < sub, padded_row, 0.0), axis=1, keepdims=True)
    end = start + pad(ccol_ref[:, 0:1])

    first_row = lax.broadcasted_iota(I32, (N_EXPERTS, n_blocks_pad), 1).astype(F32) * blk
    owner = jnp.sum((end <= first_row).astype(F32), axis=0, keepdims=True)
    be_ref[...] = jnp.broadcast_to(jnp.minimum(owner, float(N_EXPERTS - 1)).astype(I32), be_ref.shape)
    used = jnp.max(end, axis=0, keepdims=True) * (1.0 / blk)
    nb_ref[...] = jnp.broadcast_to(used.astype(I32), nb_ref.shape)
    start_ref[...] = jnp.broadcast_to(start.astype(I32), start_ref.shape)

    ce = ce_ref[...]
    tile = lax.broadcasted_iota(I32, ce.shape, 0)
    c = lax.broadcasted_iota(I32, ce.shape, 1)
    base = (dummy_row + ((tile & 1) * N_CHUNKS + c) * RUN_CHUNK).astype(F32)
    for e in range(N_EXPERTS):
        base = jnp.where(ce == e, start[e:e + 1, 0:1], base)
    row_ref[...] = (base.astype(I32) + crel_ref[...]) >> CHUNK_SHIFT


def _pos(cnt, chunk_e, chunk_rel, n_blocks_pad, dummy_row):
    counts = cnt[:, 0]
    ccol = jnp.broadcast_to(counts[:, None], (N_EXPERTS, LANES))
    crow = jnp.broadcast_to(jnp.pad(counts, (0, LANES - N_EXPERTS))[None, :], (8, LANES))
    return pl.pallas_call(
        functools.partial(_pos_kernel, n_blocks_pad, dummy_row),
        out_shape=(jax.ShapeDtypeStruct((8, n_blocks_pad), I32), jax.ShapeDtypeStruct((8, LANES), I32),
                   jax.ShapeDtypeStruct((N_EXPERTS, LANES), I32), jax.ShapeDtypeStruct(chunk_e.shape, I32)),
        name="pos",
    )(ccol, crow, chunk_e, chunk_rel)


def _chunk_copy(src_ref, src_chunk, dst_ref, dst_chunk, sem):
    return pltpu.make_async_copy(src_ref.at[src_chunk], dst_ref.at[dst_chunk], sem)


def _for_each_chunk(tile, place_ref, fn):
    def per_pair(p, _):
        for priority in range(2):
            c = 2 * p + priority
            fn(c, place_ref[tile * N_CHUNKS + c], priority)
        return 0

    lax.fori_loop(0, N_CHUNKS // 2, per_pair, 0, unroll=4)


def _chunked(x):
    return x.reshape(x.shape[0] // RUN_CHUNK, RUN_CHUNK, D_MODEL)


def _scatter_kernel(n_blocks, start_ref, count_ref, nb_ref, row_ref, h_ref, slot_ref, xs_out,
                    tiles, zeros, sems, zsem):
    i = pl.program_id(0)
    ts = MOE_TILE
    blk_chunks = EXPERT_BLOCK // RUN_CHUNK

    @pl.when(i == 0)
    def _():
        zeros[...] = jnp.zeros_like(zeros)
        zero_block = lambda b: pltpu.make_async_copy(zeros, xs_out.at[pl.ds(b * blk_chunks, blk_chunks)], zsem)

        def for_each_pad_piece(fn):
            def per_expert(e, _):
                first = start_ref[e] + count_ref[e]
                n = ((count_ref[e] + (EXPERT_BLOCK - 1)) & (-EXPERT_BLOCK)) - count_ref[e]
                for piece in PAD_PIECES:
                    done = n & (-2 * piece)

                    @pl.when((n & piece) != 0)
                    def _():
                        dst = pl.ds((first + done) >> CHUNK_SHIFT, piece // RUN_CHUNK)
                        fn(pltpu.make_async_copy(zeros.at[pl.ds(0, piece // RUN_CHUNK)], xs_out.at[dst], zsem))

                return 0

            lax.fori_loop(0, N_EXPERTS, per_expert, 0)

        for_each_pad_piece(lambda cp: cp.start())

        def start_block(b, _):
            zero_block(b).start()
            return 0

        lax.fori_loop(nb_ref[0], n_blocks, start_block, 0)
        for_each_pad_piece(lambda cp: cp.wait())

        def wait_block(b, _):
            zero_block(0).wait()
            return 0

        lax.fori_loop(nb_ref[0], n_blocks, wait_block, 0)

    def drain(half):
        pltpu.make_async_copy(tiles.at[half], xs_out.at[pl.ds(0, N_CHUNKS)], sems.at[half]).wait()

    for half in range(2):
        @pl.when(i > 0)
        def _():
            drain(half)

        h = h_ref[half * ts:(half + 1) * ts, :].astype(BF16)
        slots = slot_ref[:, half * ts:(half + 1) * ts]
        sub = ts // 2
        for jb in range(LOCAL_SLOTS // ts):
            in_block = jnp.where(slots >> TILE_SHIFT == jb, slots & (ts - 1), -1).astype(F32).astype(BF16)
            parts = []
            for r in range(ts // sub):
                j = (lax.broadcasted_iota(I32, (sub, ts), 0) + r * sub).astype(F32).astype(BF16)
                perm = jnp.zeros((sub, ts), BF16)
                for k in range(TOP_K):
                    perm = perm + jnp.where(j == in_block[k:k + 1, :], jnp.ones((), BF16), jnp.zeros((), BF16))
                parts.append(perm)
            tiles[half, jb * (ts // RUN_CHUNK):(jb + 1) * (ts // RUN_CHUNK)] = _chunked(
                _dot(jnp.concatenate(parts, axis=0), h).astype(BF16))

        _for_each_chunk(
            2 * i + half, row_ref,
            lambda local, sorted_row, priority, half=half: _chunk_copy(
                tiles.at[half], local, xs_out, sorted_row, sems.at[half]).start(priority=priority))

    @pl.when(i == pl.num_programs(0) - 1)
    def _():
        drain(0)
        drain(1)


def _scatter(h1, slots, starts, counts, n_used, chunk_row, n_rows):
    t_rows = h1.shape[0]
    step = 2 * MOE_TILE
    n_blocks = n_rows // EXPERT_BLOCK
    return pl.pallas_call(
        functools.partial(_scatter_kernel, n_blocks),
        grid_spec=pltpu.PrefetchScalarGridSpec(
            num_scalar_prefetch=4,
            grid=(t_rows // step,),
            in_specs=[pl.BlockSpec((step, D_MODEL), lambda i, *_: (i, 0)),
                      pl.BlockSpec((8, step), lambda i, *_: (0, i))],
            out_specs=pl.BlockSpec(memory_space=pl.ANY),
            scratch_shapes=[pltpu.VMEM((2, N_CHUNKS, RUN_CHUNK, D_MODEL), BF16),
                            pltpu.VMEM((EXPERT_BLOCK // RUN_CHUNK, RUN_CHUNK, D_MODEL), BF16),
                            pltpu.SemaphoreType.DMA((2,)), pltpu.SemaphoreType.DMA(())],
        ),
        out_shape=jax.ShapeDtypeStruct((n_rows // RUN_CHUNK, RUN_CHUNK, D_MODEL), BF16),
        compiler_params=pltpu.CompilerParams(dimension_semantics=("arbitrary",), vmem_limit_bytes=V7X_VMEM_LIMIT),
        name="scatter",
    )(starts, counts, n_used, chunk_row, h1, slots)


def _experts_kernel(be_ref, nb_ref, x_ref, wg_ref, wu_ref, wd_ref, y_ref, wg_bf, wu_bf, wd_bf):
    b = pl.program_id(0)

    @pl.when((b == 0) | (be_ref[b] != be_ref[jnp.maximum(b - 1, 0)]))
    def _():
        wg_bf[...] = wg_ref[...].astype(BF16)
        wu_bf[...] = wu_ref[...].astype(BF16)
        wd_bf[...] = wd_ref[...].astype(BF16)

    @pl.when(b < nb_ref[0])
    def _():
        xb = x_ref[...].reshape(EXPERT_BLOCK, D_MODEL)
        gate = _dot(xb, wg_bf[...])
        up = _dot(xb, wu_bf[...])
        hid = (gate * jax.nn.sigmoid(gate) * up).astype(BF16)
        y_ref[...] = _chunked(_dot(hid, wd_bf[...]).astype(BF16))

    @pl.when(b >= nb_ref[0])
    def _():
        y_ref[...] = jnp.zeros_like(y_ref)


def _experts(xs, block_e, n_used, w_gate, w_up, w_down):
    blk = (EXPERT_BLOCK // RUN_CHUNK, RUN_CHUNK, D_MODEL)
    return pl.pallas_call(
        _experts_kernel,
        grid_spec=pltpu.PrefetchScalarGridSpec(
            num_scalar_prefetch=2,
            grid=(xs.shape[0] // blk[0],),
            in_specs=[pl.BlockSpec(blk, lambda b, be, nb: (jnp.minimum(b, jnp.maximum(nb[0], 1) - 1), 0, 0)),
                      pl.BlockSpec((None, D_MODEL, D_EXPERT), lambda b, be, nb: (be[b], 0, 0)),
                      pl.BlockSpec((None, D_MODEL, D_EXPERT), lambda b, be, nb: (be[b], 0, 0)),
                      pl.BlockSpec((None, D_EXPERT, D_MODEL), lambda b, be, nb: (be[b], 0, 0))],
            out_specs=pl.BlockSpec(blk, lambda b, be, nb: (b, 0, 0)),
            scratch_shapes=[pltpu.VMEM((D_MODEL, D_EXPERT), BF16), pltpu.VMEM((D_MODEL, D_EXPERT), BF16),
                            pltpu.VMEM((D_EXPERT, D_MODEL), BF16)],
        ),
        out_shape=jax.ShapeDtypeStruct(xs.shape, BF16),
        compiler_params=pltpu.CompilerParams(dimension_semantics=("arbitrary",)),
        name="experts",
    )(block_e, n_used, xs, w_gate, w_up, w_down)


def _combine_kernel(n_prompt_tiles, row_ref, h_ref, w_ref, slot_ref, ys_hbm,
                    wsg_ref, wsu_ref, wsd_ref, g2_ref, b2_ref, yp_hbm, ysm_hbm, ybuf, obuf, gsems, osems):
    i = pl.program_id(0)
    n_steps = pl.num_programs(0)
    tc = MOE_TILE
    tile_t = tc // BATCH

    def gather(tile, slot):
        _for_each_chunk(
            tile, row_ref,
            lambda local, sorted_row, priority: _chunk_copy(
                ys_hbm, sorted_row, ybuf.at[slot], local, gsems.at[slot]).start(priority=priority))

    def gather_wait(slot):
        pltpu.make_async_copy(ys_hbm.at[pl.ds(0, N_CHUNKS)], ybuf.at[slot], gsems.at[slot]).wait()

    def out_copies(dst_hbm, t0, slot):
        return [pltpu.make_async_copy(obuf.at[slot, :, b, :], dst_hbm.at[b, pl.ds(t0, tile_t), :], osems.at[slot])
                for b in range(BATCH)]

    def out_wait(slot):
        for cp in out_copies(yp_hbm, 0, slot):
            cp.wait()

    def prepare(half):
        rows = slice(half * tc, (half + 1) * tc)
        h = h_ref[rows, :]
        hb = h.astype(BF16)
        gate = _dot(hb, wsg_ref[...])
        up = _dot(hb, wsu_ref[...])
        shared = _dot((gate * jax.nn.sigmoid(gate) * up).astype(BF16), wsd_ref[...])

        eye = (lax.broadcasted_iota(I32, (tc, tc), 0) == lax.broadcasted_iota(I32, (tc, tc), 1)).astype(BF16)
        to_cols = lambda a: lax.dot_general(eye, a.astype(F32).astype(BF16), (((1,), (1,)), ((), ())),
                                            preferred_element_type=F32)
        slots = slot_ref[:, rows]
        w_bf = to_cols(w_ref[:, rows]).astype(BF16)
        s_block = to_cols(slots >> TILE_SHIFT)
        s_offset = to_cols(slots & (tc - 1))

        j = lax.broadcasted_iota(I32, (tc, tc), 1).astype(F32).astype(BF16)
        blocks = []
        for lb in range(LOCAL_SLOTS // tc):
            in_block = jnp.where(s_block == float(lb), s_offset, -1.0).astype(BF16)
            blk = jnp.zeros((tc, tc), BF16)
            for k in range(TOP_K):
                blk = blk + jnp.where(j == in_block[:, k:k + 1], w_bf[:, k:k + 1], jnp.zeros((), BF16))
            blocks.append(blk)
        return h, shared, jnp.concatenate(blocks, axis=-1)

    def finish(half, buf, h, shared, mix_w):
        gather_wait(buf)
        routed = _dot(mix_w, ybuf[buf].reshape(LOCAL_SLOTS, D_MODEL))
        out = _layer_norm(DEEPNORM_ALPHA * h + (routed + shared), g2_ref[...], b2_ref[...])
        obuf[half] = out.reshape(tile_t, BATCH, D_MODEL)

        g = 2 * i + half

        @pl.when(g < n_prompt_tiles)
        def _():
            for cp in out_copies(yp_hbm, pl.multiple_of(g * tile_t, tile_t), half):
                cp.start()

        @pl.when(g >= n_prompt_tiles)
        def _():
            for cp in out_copies(ysm_hbm, pl.multiple_of((g - n_prompt_tiles) * tile_t, tile_t), half):
                cp.start()

    cur = 2 * (i % 2)
    nxt = 2 - cur

    @pl.when(i == 0)
    def _():
        gather(0, 0)
        gather(1, 1)

    @pl.when(i + 1 < n_steps)
    def _():
        gather(2 * i + 2, nxt)
        gather(2 * i + 3, nxt + 1)

    @pl.when(i > 0)
    def _():
        out_wait(0)
        out_wait(1)

    prepared = [prepare(half) for half in range(2)]
    for half in range(2):
        finish(half, cur + half, *prepared[half])

    @pl.when(i + 1 == n_steps)
    def _():
        out_wait(0)
        out_wait(1)


def _combine(h1, gate_w, slots, ys, chunk_row, ws_gate, ws_up, ws_down, ln2_g, ln2_b, prompt_shape, sample_shape):
    t_rows = h1.shape[0]
    tc = MOE_TILE
    step = 2 * tc
    n_prompt_tiles = prompt_shape[0] * prompt_shape[1] // tc
    const2 = lambda i, *_: (0, 0)
    any_spec = pl.BlockSpec(memory_space=pl.ANY)
    return pl.pallas_call(
        functools.partial(_combine_kernel, n_prompt_tiles),
        grid_spec=pltpu.PrefetchScalarGridSpec(
            num_scalar_prefetch=1,
            grid=(t_rows // step,),
            in_specs=[pl.BlockSpec((step, D_MODEL), lambda i, *_: (i, 0)),
                      pl.BlockSpec((8, step), lambda i, *_: (0, i)),
                      pl.BlockSpec((8, step), lambda i, *_: (0, i)),
                      any_spec,
                      pl.BlockSpec((D_MODEL, D_EXPERT), const2),
                      pl.BlockSpec((D_MODEL, D_EXPERT), const2),
                      pl.BlockSpec((D_EXPERT, D_MODEL), const2),
                      pl.BlockSpec((1, D_MODEL), const2),
                      pl.BlockSpec((1, D_MODEL), const2)],
            out_specs=(any_spec, any_spec),
            scratch_shapes=[pltpu.VMEM((4, N_CHUNKS, RUN_CHUNK, D_MODEL), BF16),
                            pltpu.VMEM((2, tc // BATCH, BATCH, D_MODEL), F32),
                            pltpu.SemaphoreType.DMA((4,)), pltpu.SemaphoreType.DMA((2,))],
        ),
        out_shape=(jax.ShapeDtypeStruct(prompt_shape, F32), jax.ShapeDtypeStruct(sample_shape, F32)),
        compiler_params=pltpu.CompilerParams(dimension_semantics=("arbitrary",), vmem_limit_bytes=V7X_VMEM_LIMIT),
        name="combine",
    )(chunk_row, h1, gate_w, slots, ys, ws_gate.astype(BF16), ws_up.astype(BF16),
      ws_down.astype(BF16), ln2_g.reshape(1, D_MODEL), ln2_b.reshape(1, D_MODEL))


def _time_major(x):
    b, l, d = x.shape
    return jnp.transpose(x, (1, 0, 2)).reshape(l * b, d)


def _batch_major(x, b):
    return jnp.transpose(x.reshape(x.shape[0] // b, b, x.shape[1]), (1, 0, 2))


def _block_diag(blocks):
    n, r, c = blocks.shape
    eye = jnp.eye(n, dtype=blocks.dtype)
    return (blocks[:, :, None, :] * eye[:, None, :, None]).reshape(n * r, n * c)


def kernel(x_prompt, x_sample, state_ssm_re, state_ssm_im, cache_conv, ln_in_g, ln_in_b, w_in, lam_re, lam_im, log_dt, ssm_b_re, ssm_b_im, ssm_c_re, ssm_c_im, ssm_d, w_glu, b_glu, conv_w, beta_ssm, beta_conv, w_out, ln1_g, ln1_b, w_router, router_bias, w_gate, w_up, w_down, ws_gate, ws_up, ws_down, ln2_g, ln2_b):
    bp, lp, _ = x_prompt.shape
    bs, ls, _ = x_sample.shape
    assert bp == BATCH and bs == BATCH and ls == CHUNK_T and lp % CHUNK_T == 0
    assert w_in.shape[0] == 1, "single-layer model"
    n_prompt = bp * lp
    row = lambda a: a.reshape(1, -1)

    a_re, a_im, bb_re, bb_im = _prep(lam_re[0], lam_im[0], log_dt[0], ssm_b_re[0], ssm_b_im[0])
    groups_in = LANES // SSM_GROUP
    groups_out = MXU_WIDTH // SSM_GROUP
    bbd = lambda bb: jnp.stack([_block_diag(bb[j * groups_in:(j + 1) * groups_in])
                                for j in range(N_GROUPS // groups_in)]).astype(BF16)
    ct = lambda cc: jnp.transpose(cc, (0, 2, 1))
    cbd = lambda cc: jnp.stack([_block_diag(ct(cc)[j * groups_out:(j + 1) * groups_out])
                                for j in range(N_GROUPS // groups_out)]).astype(BF16)
    mix_weights = (row(ln_in_g), row(ln_in_b), w_in[0].astype(BF16), row(a_re), row(a_im),
                   bbd(bb_re), bbd(bb_im), cbd(ssm_c_re[0]), cbd(-ssm_c_im[0]), row(ssm_d[0]),
                   w_glu[0].astype(BF16), row(b_glu[0]), conv_w[0], row(beta_ssm[0]), row(beta_conv[0]),
                   w_out[0].astype(BF16), row(ln1_g[0]), row(ln1_b[0]))

    h0r = state_ssm_re[0].reshape(BATCH, D_STATE)
    h0i = state_ssm_im[0].reshape(BATCH, D_STATE)
    cbuf = _time_major(cache_conv[0])
    h1, p_re, p_im, p_conv, s_re, s_im, s_conv = _mix(x_prompt, x_sample, h0r, h0i, cbuf, mix_weights)

    t_rows = h1.shape[0]
    assert t_rows % (2 * MOE_TILE) == 0 and n_prompt % MOE_TILE == 0
    n_tiles = t_rows // MOE_TILE
    max_rows = t_rows * TOP_K + n_tiles * N_EXPERTS * (RUN_CHUNK - 1) + N_EXPERTS * EXPERT_BLOCK
    dummy_blocks = -(-2 * LOCAL_SLOTS // EXPERT_BLOCK)
    n_blocks = -(-max_rows // EXPERT_BLOCK) + dummy_blocks
    n_blocks_pad = -(-n_blocks // LANES) * LANES
    gate_w, slots, chunk_e, chunk_rel, cnt = _route(h1, w_router[0], router_bias[0])
    block_e, n_used, starts, chunk_row = _pos(cnt, chunk_e[:, 0], chunk_rel[:, 0], n_blocks_pad,
                                               (n_blocks - dummy_blocks) * EXPERT_BLOCK)
    n_used, chunk_row = n_used[0, :1], chunk_row.reshape(-1)
    xs = _scatter(h1, slots, starts[:, 0], cnt[:, 0].astype(I32), n_used, chunk_row, n_blocks * EXPERT_BLOCK)
    ys = _experts(xs, block_e[0, :n_blocks], n_used, w_gate[0], w_up[0], w_down[0])
    y_prompt, y_sample = _combine(h1, gate_w, slots, ys, chunk_row, ws_gate[0], ws_up[0], ws_down[0],
                                  ln2_g[0], ln2_b[0], x_prompt.shape, x_sample.shape)

    st = lambda s: s.reshape(1, BATCH, N_GROUPS, SSM_STATE)
    cv = lambda t: _batch_major(t, BATCH)[None]
    return (y_prompt, y_sample, st(p_re), st(p_im), cv(p_conv), st(s_re), st(s_im), cv(s_conv))
```
